```python
import jax, jax.numpy as jnp
from jax import lax
import numpy as np

D_MODEL = 2048
BATCH = 4
SEQ = 2048
DEPTH = 4
DEC_BATCH = 8
DEC_SEQ = 1
PAST_LEN = 16384
PAGE_SIZE = 128

N_A = DEPTH // 2
N_B = DEPTH - N_A
MEM_TOKENS = 256
N_MEM_HEADS = 4
MEM_HEAD_DIM = 128
MEM_W = N_MEM_HEADS * MEM_HEAD_DIM
MIX_W = D_MODEL
MAIN_W = MIX_W - MEM_W
POOL_WINDOWS = (2, 4, 8, 16)
POOL_GC = MAIN_W // len(POOL_WINDOWS)
POOL_BUF = max(POOL_WINDOWS) - 1
HEAD_DIM = 128
N_HEADS = MAIN_W // HEAD_DIM
N_KV = 4
GQA = N_HEADS // N_KV
N_BRANCH = 3
GATE_W = N_HEADS * N_BRANCH
CMP_BLOCK = 32
CMP_STRIDE = 16
CMP_HIDDEN = HEAD_DIM
SLC_BLOCK = 64
N_SELECT = 16
WINDOW = 512
Q_BLOCK = 128
D_FF = -(-(8 * D_MODEL) // (3 * 256)) * 256
ROPE_THETA = 10000.0
EPS = 1e-6
NEG = -1e30
F32 = jnp.float32
SCALE = HEAD_DIM ** -0.5
MEM_SCALE = MEM_HEAD_DIM ** -0.5

kernel_name = 'yoco_pool_nsa_memory_decoder_step'


def rmsnorm(x, g):
    xf = x.astype(F32)
    y = xf * lax.rsqrt(jnp.mean(xf * xf, -1, keepdims=True) + EPS)
    return (y * g.astype(F32)).astype(x.dtype)


def rope(x, pos):
    half = HEAD_DIM // 2
    inv = ROPE_THETA ** (-jnp.arange(half, dtype=F32) / half)
    ang = pos.astype(F32)[:, None] * inv[None, :]
    cos = jnp.cos(ang)[None, :, None, :]
    sin = jnp.sin(ang)[None, :, None, :]
    xf = x.astype(F32)
    x1, x2 = xf[..., :half], xf[..., half:]
    return jnp.concatenate([x1 * cos - x2 * sin, x2 * cos + x1 * sin], -1).astype(x.dtype)


def pad_seq(x, mult):
    pad = (-x.shape[1]) % mult
    return jnp.pad(x, [(0, 0), (0, pad)] + [(0, 0)] * (x.ndim - 2))


def swiglu(h, w_gu, w_down):
    a, b = jnp.split(h @ w_gu, 2, axis=-1)
    return (jax.nn.silu(a) * b) @ w_down


def causal_pool_mix(p, buf, pos, w_grp, scale):
    S = p.shape[1]
    full = jnp.concatenate([buf.astype(p.dtype), p], 1)
    ff = full.astype(F32)
    cs = jnp.concatenate([jnp.zeros_like(ff[:, :1]), lax.cumsum(ff, axis=1)], 1)
    outs = []
    for g, w in enumerate(POOL_WINDOWS):
        sl = slice(g * POOL_GC, (g + 1) * POOL_GC)
        win_sum = cs[:, POOL_BUF + 1:POOL_BUF + 1 + S, sl] - cs[:, POOL_BUF + 1 - w:POOL_BUF + 1 - w + S, sl]
        cnt = jnp.minimum(pos + 1, w).astype(F32)[None, :, None]
        d = win_sum / cnt - ff[:, POOL_BUF:, sl]
        outs.append(jnp.einsum('bsc,cd->bsd', d, w_grp[g].astype(F32)))
    z = jnp.concatenate(outs, -1) * scale.astype(F32)
    return z.astype(p.dtype), full[:, -POOL_BUF:]


def mem_kv_proj(mem, g, w):
    B, M = mem.shape[:2]
    return (rmsnorm(mem, g) @ w).reshape(B, M, 2, N_MEM_HEADS, MEM_HEAD_DIM)


def mem_attend(qm, mkv):
    s = jnp.einsum('bshd,bmhd->bhsm', qm.astype(F32), mkv[:, :, 0].astype(F32)) * MEM_SCALE
    p = jax.nn.softmax(s, axis=-1)
    o = jnp.einsum('bhsm,bmhd->bshd', p, mkv[:, :, 1].astype(F32))
    return o.reshape(qm.shape[0], qm.shape[1], MEM_W).astype(qm.dtype)


def shared_kv(x, pos, g_kv, w_kv):
    B, S, _ = x.shape
    kv = (rmsnorm(x, g_kv) @ w_kv).reshape(B, S, N_BRANCH, 2, N_KV, HEAD_DIM)
    return [jnp.stack([rope(kv[:, :, br, 0], pos), kv[:, :, br, 1]], 2) for br in range(N_BRANCH)]


def compress(kv, cmp_pe, cmp_w1, cmp_w2):
    B, T = kv.shape[:2]
    n_ch = T // CMP_STRIDE
    ch = kv.reshape(B, n_ch, CMP_STRIDE, 2, N_KV, HEAD_DIM)
    w1 = cmp_w1.reshape(2, 2, CMP_STRIDE, HEAD_DIM, CMP_HIDDEN)
    pe = cmp_pe.reshape(2, 2, CMP_STRIDE, HEAD_DIM)

    def half(i):
        return (jnp.einsum('bnrchd,crdf->bnchf', ch, w1[:, i], preferred_element_type=F32)
                + jnp.einsum('crd,crdf->cf', pe[:, i], w1[:, i]).astype(F32)[None, None, :, None, :])

    hid = jax.nn.gelu(half(0)[:, :-1] + half(1)[:, 1:])
    out = jnp.einsum('bnchf,cfd->bnchd', hid, cmp_w2.astype(F32))
    ends = jnp.arange(n_ch - 1) * CMP_STRIDE + CMP_BLOCK - 1
    return out, ends


def cmp_attend(q, ckv, ends, qpos):
    s = jnp.einsum('bqhgd,bnhd->bqhgn', q.astype(F32), ckv[:, :, 0]) * SCALE
    vis = ends[None, :] <= qpos[:, None]
    s = jnp.where(vis[None, :, None, None, :], s, NEG)
    p = jax.nn.softmax(s, axis=-1) * jnp.any(vis, -1).astype(F32)[None, :, None, None, None]
    o = jnp.einsum('bqhgn,bnhd->bqhgd', p, ckv[:, :, 1])
    return o, p


def select_blocks(p_cmp, ends, qpos, n_slc):
    starts = ends - (CMP_BLOCK - 1)
    jst = jnp.arange(n_slc) * SLC_BLOCK
    cover = ((starts[:, None] < jst[None, :] + SLC_BLOCK) & (ends[:, None] >= jst[None, :])).astype(F32)
    score = jnp.einsum('bqhgn,nj->bqhj', p_cmp, cover)
    qb = (qpos // SLC_BLOCK)[:, None]
    jidx = jnp.arange(n_slc)[None, :]
    valid = jidx <= qb
    forced = (jidx == 0) | (jidx == qb) | (jidx == qb - 1)
    score = jnp.where(forced[None, :, None, :], jnp.inf,
                      jnp.where(valid[None, :, None, :], score, -jnp.inf))
    _, idx = lax.top_k(score, min(N_SELECT, n_slc))
    return idx.astype(jnp.int32)


def gathered_attend(q, kg, vg, kpos, qpos):
    s = jnp.einsum('...hgd,...hkd->...hgk', q.astype(F32), kg.astype(F32)) * SCALE
    vis = kpos[..., :, None, :] <= qpos[..., None, None, None]
    p = jax.nn.softmax(jnp.where(vis, s, NEG), axis=-1)
    return jnp.einsum('...hgk,...hkd->...hgd', p, vg.astype(F32))


def slc_prompt(q, slc_kv, sel, pos):
    B, S = q.shape[:2]
    nb = slc_kv.shape[1] // SLC_BLOCK
    n_sel = sel.shape[-1]
    blocks = slc_kv.reshape(B, nb, SLC_BLOCK, 2, N_KV, HEAD_DIM).transpose(0, 3, 4, 1, 2, 5)
    nqb = S // Q_BLOCK
    qs = q.reshape(B * nqb, Q_BLOCK, N_KV, GQA, HEAD_DIM)
    ss = sel.reshape(B * nqb, Q_BLOCK, N_KV, n_sel)
    ps = jnp.broadcast_to(pos.reshape(1, nqb, Q_BLOCK), (B, nqb, Q_BLOCK)).reshape(B * nqb, Q_BLOCK)
    bs = jnp.repeat(jnp.arange(B), nqb)
    h_i = jnp.arange(N_KV)[None, :, None]
    offs = jnp.arange(SLC_BLOCK)

    def one(args):
        qb, sb, pb, b = args
        blk = blocks[b]
        kg = blk[0][h_i, sb].reshape(Q_BLOCK, N_KV, n_sel * SLC_BLOCK, HEAD_DIM)
        vg = blk[1][h_i, sb].reshape(Q_BLOCK, N_KV, n_sel * SLC_BLOCK, HEAD_DIM)
        kpos = (sb[..., None] * SLC_BLOCK + offs).reshape(Q_BLOCK, N_KV, n_sel * SLC_BLOCK)
        return gathered_attend(qb, kg, vg, kpos, pb)

    out = lax.map(one, (qs, ss, ps, bs))
    return out.reshape(B, S, N_KV, GQA, HEAD_DIM)


def slc_sample(q, sel, cache_slc_kv, page_table, slc_new, pos):
    DB, DS = q.shape[:2]
    kpos = sel[..., None] * SLC_BLOCK + jnp.arange(SLC_BLOCK)
    in_past = kpos < PAST_LEN
    pp = jnp.minimum(kpos, PAST_LEN - 1)
    b_i = jnp.arange(DB)[:, None, None, None, None]
    h_i = jnp.arange(N_KV)[None, None, :, None, None]
    phys = page_table[b_i, pp // PAGE_SIZE]
    off = pp % PAGE_SIZE
    npos = jnp.clip(kpos - PAST_LEN, 0, DS - 1)

    def rows(c):
        past = cache_slc_kv[phys, off, c, h_i]
        new = slc_new[b_i, npos, c, h_i].astype(past.dtype)
        return jnp.where(in_past[..., None], past, new).reshape(DB, DS, N_KV, -1, HEAD_DIM)

    qpos = jnp.broadcast_to(pos, (DB, DS))
    return gathered_attend(q, rows(0), rows(1), kpos.reshape(DB, DS, N_KV, -1), qpos)


def win_prompt(q, win_kv, pos):
    B, S = q.shape[:2]
    nqb = S // Q_BLOCK
    span = WINDOW + Q_BLOCK
    kp = jnp.pad(win_kv, ((0, 0), (WINDOW, 0), (0, 0), (0, 0), (0, 0)))
    idx = jnp.arange(nqb)[:, None] * Q_BLOCK + jnp.arange(span)[None, :]
    kb = kp[:, idx]
    kpos = (idx - WINDOW)[:, None, :]
    qpos = pos.reshape(nqb, Q_BLOCK)[:, :, None]
    vis = (kpos >= 0) & (kpos <= qpos) & (qpos - kpos < WINDOW)
    qb = q.reshape(B, nqb, Q_BLOCK, N_KV, GQA, HEAD_DIM).astype(F32)
    s = jnp.einsum('bnqhgd,bnkhd->bnqhgk', qb, kb[:, :, :, 0].astype(F32)) * SCALE
    p = jax.nn.softmax(jnp.where(vis[None, :, :, None, None, :], s, NEG), axis=-1)
    o = jnp.einsum('bnqhgk,bnkhd->bnqhgd', p, kb[:, :, :, 1].astype(F32))
    return o.reshape(B, S, N_KV, GQA, HEAD_DIM)


def win_sample(q, keys, kpos, pos):
    s = jnp.einsum('bqhgd,bkhd->bqhgk', q.astype(F32), keys[:, :, 0].astype(F32)) * SCALE
    vis = (kpos[None, :] <= pos[:, None]) & (pos[:, None] - kpos[None, :] < WINDOW)
    p = jax.nn.softmax(jnp.where(vis[None, :, None, None, :], s, NEG), axis=-1)
    return jnp.einsum('bqhgk,bkhd->bqhgd', p, keys[:, :, 1].astype(F32))


def prompt_ctx(x, pos, g_kv, w_kv, cmp_pe, cmp_w1, cmp_w2):
    cmp_kv, slc_kv, win_kv = shared_kv(x, pos, g_kv, w_kv)
    ckv, ends = compress(pad_seq(cmp_kv, SLC_BLOCK), cmp_pe, cmp_w1, cmp_w2)
    slc_pad = pad_seq(slc_kv, SLC_BLOCK)
    n_slc = slc_pad.shape[1] // SLC_BLOCK

    def branches(q):
        o_c, p = cmp_attend(q, ckv, ends, pos)
        sel = select_blocks(p, ends, pos, n_slc)
        return o_c, slc_prompt(q, slc_pad, sel, pos), win_prompt(q, win_kv, pos)

    n_keep = min(WINDOW, x.shape[1])
    return branches, (cmp_kv, slc_kv, win_kv[:, -n_keep:])


def sample_ctx(x, pos, cache_cmp_kv, cache_slc_kv, cache_win_kv, page_table, g_kv, w_kv, cmp_pe, cmp_w1, cmp_w2):
    cmp_new, slc_new, win_new = shared_kv(x, pos, g_kv, w_kv)
    DB = x.shape[0]
    past = cache_cmp_kv[page_table].reshape(DB, -1, 2, N_KV, HEAD_DIM)
    cmp_full = pad_seq(jnp.concatenate([past, cmp_new.astype(past.dtype)], 1), SLC_BLOCK)
    ckv, ends = compress(cmp_full, cmp_pe, cmp_w1, cmp_w2)
    n_slc = cmp_full.shape[1] // SLC_BLOCK
    lb = cache_win_kv.shape[1]
    keys_w = jnp.concatenate([cache_win_kv, win_new.astype(cache_win_kv.dtype)], 1)
    kpos_w = jnp.concatenate([PAST_LEN - lb + jnp.arange(lb, dtype=jnp.int32), pos])

    def branches(q):
        o_c, p = cmp_attend(q, ckv, ends, pos)
        sel = select_blocks(p, ends, pos, n_slc)
        o_s = slc_sample(q, sel, cache_slc_kv, page_table, slc_new, pos)
        return o_c, o_s, win_sample(q, keys_w, kpos_w, pos)

    new_win = keys_w[:, -min(WINDOW, keys_w.shape[1]):]
    return branches, (cmp_new, slc_new, new_win)


def run_trunk(x, pos, pool_bufs, mem_kv, make_ctx, g_mix, w_in_a, pool_grp_w, pool_scale,
              w_in_b, b_gate, w_out, g_ffn, w_gu, w_down, g_final):
    B, S, _ = x.shape
    new_bufs = []
    branches, ctx_state = None, None
    for l in range(DEPTH):
        if l == N_A:
            branches, ctx_state = make_ctx(x)
        h = rmsnorm(x, g_mix[l])
        if l < N_A:
            u = h @ w_in_a[l]
            z, nb = causal_pool_mix(u[..., :MAIN_W], pool_bufs[l], pos, pool_grp_w[l], pool_scale[l])
            new_bufs.append(nb)
            qm = u[..., MAIN_W:]
        else:
            j = l - N_A
            u = h @ w_in_b[j]
            q = rope(u[..., :MAIN_W].reshape(B, S, N_HEADS, HEAD_DIM), pos).reshape(B, S, N_KV, GQA, HEAD_DIM)
            gates = jax.nn.sigmoid((u[..., MAIN_W:MAIN_W + GATE_W] + b_gate[j]).astype(F32))
            gates = gates.reshape(B, S, N_KV, GQA, N_BRANCH)
            o_c, o_s, o_w = branches(q)
            z = (gates[..., 0:1] * o_c + gates[..., 1:2] * o_s + gates[..., 2:3] * o_w)
            z = z.reshape(B, S, MAIN_W).astype(x.dtype)
            qm = u[..., MAIN_W + GATE_W:]
        mo = mem_attend(qm.reshape(B, S, N_MEM_HEADS, MEM_HEAD_DIM), mem_kv[l])
        x = x + jnp.concatenate([z, mo], -1) @ w_out[l]
        x = x + swiglu(rmsnorm(x, g_ffn[l]), w_gu[l], w_down[l])
    return rmsnorm(x, g_final), jnp.stack(new_bufs), ctx_state


def setup_inputs(seed: int = 0) -> dict:
    key = jax.random.key(seed)
    ks = jax.random.split(key, 32)

    def nrm(i, shape, scale):
        return jax.random.normal(ks[i], shape, F32) * scale

    def gain(i, shape):
        return 1.0 + 0.05 * jax.random.normal(ks[i], shape, F32)

    n_pages = PAST_LEN // PAGE_SIZE
    n_used = DEC_BATCH * n_pages
    n_phys = n_used + max(1, n_used // 4)
    perm = jax.random.permutation(ks[0], n_phys)
    page_table = perm[:n_used].reshape(DEC_BATCH, n_pages).astype(jnp.int32)
    win_len = min(WINDOW, PAST_LEN)
    return {
        'x_prompt': nrm(1, (BATCH, SEQ, D_MODEL), 1.0),
        'x_sample': nrm(2, (DEC_BATCH, DEC_SEQ, D_MODEL), 1.0),
        'mem_prompt': nrm(3, (BATCH, MEM_TOKENS, D_MODEL), 1.0),
        'state_pool': nrm(4, (N_A, DEC_BATCH, POOL_BUF, MAIN_W), 1.0),
        'cache_cmp_kv': nrm(5, (n_phys, PAGE_SIZE, 2, N_KV, HEAD_DIM), 1.0),
        'cache_slc_kv': nrm(6, (n_phys, PAGE_SIZE, 2, N_KV, HEAD_DIM), 1.0),
        'cache_win_kv': nrm(7, (DEC_BATCH, win_len, 2, N_KV, HEAD_DIM), 1.0),
        'cache_mem_kv': nrm(8, (DEPTH, DEC_BATCH, MEM_TOKENS, 2, N_MEM_HEADS, MEM_HEAD_DIM), 1.0),
        'page_table': page_table,
        'g_mix': gain(9, (DEPTH, D_MODEL)),
        'w_in_a': nrm(10, (N_A, D_MODEL, MAIN_W + MEM_W), D_MODEL ** -0.5),
        'pool_grp_w': nrm(11, (N_A, len(POOL_WINDOWS), POOL_GC, POOL_GC), POOL_GC ** -0.5),
        'pool_scale': gain(12, (N_A, MAIN_W)),
        'w_in_b': nrm(13, (N_B, D_MODEL, MAIN_W + GATE_W + MEM_W), D_MODEL ** -0.5),
        'b_gate': nrm(14, (N_B, GATE_W), 0.1),
        'g_kv': gain(15, (D_MODEL,)),
        'w_kv': nrm(16, (D_MODEL, N_BRANCH * 2 * N_KV * HEAD_DIM), D_MODEL ** -0.5),
        'cmp_pe': nrm(17, (2, CMP_BLOCK, HEAD_DIM), 0.1),
        'cmp_w1': nrm(18, (2, CMP_BLOCK, HEAD_DIM, CMP_HIDDEN), (CMP_BLOCK * HEAD_DIM) ** -0.5),
        'cmp_w2': nrm(19, (2, CMP_HIDDEN, HEAD_DIM), CMP_HIDDEN ** -0.5),
        'g_mem': gain(20, (DEPTH, D_MODEL)),
        'w_mem_kv': nrm(21, (DEPTH, D_MODEL, 2 * MEM_W), D_MODEL ** -0.5),
        'w_out': nrm(22, (DEPTH, MIX_W, D_MODEL), MIX_W ** -0.5),
        'g_ffn': gain(23, (DEPTH, D_MODEL)),
        'w_gu': nrm(24, (DEPTH, D_MODEL, 2 * D_FF), D_MODEL ** -0.5),
        'w_down': nrm(25, (DEPTH, D_FF, D_MODEL), D_FF ** -0.5),
        'g_final': gain(26, (D_MODEL,)),
    }


def reference(x_prompt, x_sample, mem_prompt, state_pool, cache_cmp_kv, cache_slc_kv, cache_win_kv,
              cache_mem_kv, page_table, g_mix, w_in_a, pool_grp_w, pool_scale, w_in_b, b_gate, g_kv, w_kv,
              cmp_pe, cmp_w1, cmp_w2, g_mem, w_mem_kv, w_out, g_ffn, w_gu, w_down, g_final):
    pos_p = jnp.arange(x_prompt.shape[1], dtype=jnp.int32)
    pos_s = PAST_LEN + jnp.arange(x_sample.shape[1], dtype=jnp.int32)

    mem_kv_p = jnp.stack([mem_kv_proj(mem_prompt, g_mem[l], w_mem_kv[l]) for l in range(DEPTH)])
    bufs0 = jnp.zeros((N_A, x_prompt.shape[0], POOL_BUF, MAIN_W), x_prompt.dtype)
    y_prompt, pool_p, (cmp_p, slc_p, win_p) = run_trunk(
        x_prompt, pos_p, bufs0, mem_kv_p,
        lambda h: prompt_ctx(h, pos_p, g_kv, w_kv, cmp_pe, cmp_w1, cmp_w2),
        g_mix, w_in_a, pool_grp_w, pool_scale, w_in_b, b_gate, w_out, g_ffn, w_gu, w_down, g_final)

    y_sample, pool_s, (cmp_s, slc_s, win_s) = run_trunk(
        x_sample, pos_s, state_pool, cache_mem_kv,
        lambda h: sample_ctx(h, pos_s, cache_cmp_kv, cache_slc_kv, cache_win_kv, page_table,
                             g_kv, w_kv, cmp_pe, cmp_w1, cmp_w2),
        g_mix, w_in_a, pool_grp_w, pool_scale, w_in_b, b_gate, w_out, g_ffn, w_gu, w_down, g_final)

    return (y_prompt, y_sample, pool_p, cmp_p, slc_p, win_p, mem_kv_p, pool_s, cmp_s, slc_s, win_s)
```

```python
import functools

import jax
import jax.numpy as jnp
from jax import lax
from jax.experimental import pallas as pl
from jax.experimental.pallas import tpu as pltpu

F32 = jnp.float32
BF16 = jnp.bfloat16

D_MODEL = 2048
DEPTH = 4
N_A = 2
PAST_LEN = 16384
PAGE_SIZE = 128
MEM_TOKENS = 256
N_MEM_HEADS = 4
MEM_HEAD_DIM = 128
MEM_W = 512
MAIN_W = 1536
POOL_WINDOWS = (2, 4, 8, 16)
POOL_GC = 384
POOL_BUF = 15
HEAD_DIM = 128
N_HEADS = 12
N_KV = 4
GQA = 3
N_BRANCH = 3
GATE_W = 36
GATE_PAD = 256
CMP_BLOCK = 32
CMP_STRIDE = 16
SLC_BLOCK = 64
SLC_SHIFT = 6
N_SELECT = 16
WINDOW = 512
Q_BLOCK = 128
D_FF = 5632
ROPE_THETA = 10000.0
EPS = 1e-6
NEG = -1e30
SCALE = HEAD_DIM ** -0.5
MEM_SCALE = MEM_HEAD_DIM ** -0.5
KV_SLOTS = 2 * N_KV
KV_W = KV_SLOTS * HEAD_DIM
W_IN_B = MAIN_W + MEM_W + GATE_PAD

VMEM_LIMIT = 56 * 1024 * 1024
PAGES_PER_STEP = 8
CH_PER_PAGE = PAGE_SIZE // CMP_STRIDE
CMP_STEP = PAGES_PER_STEP * CH_PER_PAGE


def _cparams(*sem):
    return pltpu.CompilerParams(dimension_semantics=sem, vmem_limit_bytes=VMEM_LIMIT)


def _rms(x, g):
    return x * lax.rsqrt(jnp.mean(x * x, axis=-1, keepdims=True) + EPS) * g


def _dot(a, b):
    return jnp.dot(a.astype(BF16), b.astype(BF16), preferred_element_type=F32)


def _dot_t(a, b):
    return lax.dot_general(a.astype(BF16), b.astype(BF16), (((1,), (1,)), ((), ())),
                           preferred_element_type=F32)


def _lhs_dtype(rows):
    return BF16 if rows % 16 == 0 else F32


def _rope(x, cos2, sin2):
    return x * cos2 + pltpu.roll(x, HEAD_DIM // 2, axis=1) * sin2


def _rms_matmul_kernel(x_ref, g_ref, w_ref, o_ref, xn_ref):
    @pl.when(pl.program_id(1) == 0)
    def _():
        xn_ref[...] = _rms(x_ref[...], g_ref[...]).astype(xn_ref.dtype)

    o_ref[...] = _dot(xn_ref[...], w_ref[...])


def rms_matmul(x, g, w, tm, tn):
    m, k = x.shape
    n = w.shape[1]
    return pl.pallas_call(
        _rms_matmul_kernel,
        grid=(m // tm, n // tn),
        in_specs=[pl.BlockSpec((tm, k), lambda i, j: (i, 0)),
                  pl.BlockSpec((1, k), lambda i, j: (0, 0)),
                  pl.BlockSpec((k, tn), lambda i, j: (0, j))],
        out_specs=pl.BlockSpec((tm, tn), lambda i, j: (i, j)),
        out_shape=jax.ShapeDtypeStruct((m, n), F32),
        scratch_shapes=[pltpu.VMEM((tm, k), _lhs_dtype(tm))],
        compiler_params=_cparams("parallel", "arbitrary"),
        name="rms_matmul",
    )(x, g.reshape(1, k), w)


def _kv_proj_kernel(x_ref, g_ref, w_ref, cos_ref, sin_ref, cmp_ref, slc_ref, win_ref, xn_ref):
    j = pl.program_id(1)

    @pl.when(j == 0)
    def _():
        xn_ref[...] = _rms(x_ref[...], g_ref[...]).astype(xn_ref.dtype)

    y = _dot(xn_ref[...], w_ref[...])
    cos2 = cos_ref[...]
    sin2 = sin_ref[...]
    keys = [_rope(y[:, h * HEAD_DIM:(h + 1) * HEAD_DIM], cos2, sin2) for h in range(N_KV)]
    y = jnp.concatenate(keys + [y[:, N_KV * HEAD_DIM:]], axis=1)
    for br, ref in enumerate((cmp_ref, slc_ref, win_ref)):
        @pl.when(j == br)
        def _(ref=ref):
            ref[...] = y


def kv_proj(x, g, w, cos2, sin2, tm):
    m, k = x.shape
    nt = cos2.shape[0] // tm
    out = jax.ShapeDtypeStruct((m, KV_W), F32)
    ospec = pl.BlockSpec((tm, KV_W), lambda i, j: (i, 0))
    return pl.pallas_call(
        _kv_proj_kernel,
        grid=(m // tm, N_BRANCH),
        in_specs=[pl.BlockSpec((tm, k), lambda i, j: (i, 0)),
                  pl.BlockSpec((1, k), lambda i, j: (0, 0)),
                  pl.BlockSpec((k, KV_W), lambda i, j: (0, j)),
                  pl.BlockSpec((tm, HEAD_DIM), lambda i, j: (i % nt, 0)),
                  pl.BlockSpec((tm, HEAD_DIM), lambda i, j: (i % nt, 0))],
        out_specs=[ospec, ospec, ospec],
        out_shape=[out, out, out],
        scratch_shapes=[pltpu.VMEM((tm, k), _lhs_dtype(tm))],
        compiler_params=_cparams("parallel", "arbitrary"),
        name="kv_proj",
    )(x, g.reshape(1, k), w, cos2, sin2)


def _pool_prompt_kernel(cur_ref, prev_ref, w_ref, sc_ref, o_ref, *, ts):
    i = pl.program_id(1)
    keep = (i > 0).astype(F32)
    pos = (i * ts + lax.broadcasted_iota(jnp.int32, (ts, 1), 0) + 1).astype(F32)
    for g, w in enumerate(POOL_WINDOWS):
        sl = slice(g * POOL_GC, (g + 1) * POOL_GC)
        x = cur_ref[0, :, sl]
        ext = jnp.concatenate([prev_ref[0, :, sl] * keep, x], axis=0)
        span = 1
        while span < w:
            ext = ext[span:] + ext[:-span]
            span *= 2
        win = ext[17 - w:17 - w + ts]
        d = win / jnp.minimum(pos, float(w)) - x
        o_ref[0, :, sl] = _dot(d, w_ref[g]) * sc_ref[:, sl]


def pool_prompt(u3, w_grp, scale, ts=256):
    b, s, n = u3.shape
    r = ts // 16
    return pl.pallas_call(
        functools.partial(_pool_prompt_kernel, ts=ts),
        grid=(b, s // ts),
        in_specs=[pl.BlockSpec((1, ts, MAIN_W), lambda bi, i: (bi, i, 0)),
                  pl.BlockSpec((1, 16, MAIN_W), lambda bi, i: (bi, jnp.maximum(i * r - 1, 0), 0)),
                  pl.BlockSpec((4, POOL_GC, POOL_GC), lambda bi, i: (0, 0, 0)),
                  pl.BlockSpec((1, MAIN_W), lambda bi, i: (0, 0))],
        out_specs=pl.BlockSpec((1, ts, MAIN_W), lambda bi, i: (bi, i, 0)),
        out_shape=jax.ShapeDtypeStruct((b, s, MAIN_W), F32),
        compiler_params=_cparams("parallel", "arbitrary"),
        name="pool_prompt",
    )(u3, u3, w_grp, scale.reshape(1, MAIN_W))


def _pool_sample_kernel(cur_ref, buf_ref, w_ref, sc_ref, o_ref):
    for g, w in enumerate(POOL_WINDOWS):
        sl = slice(g * POOL_GC, (g + 1) * POOL_GC)
        x = cur_ref[:, sl]
        win = x
        for r in range(POOL_BUF - (w - 1), POOL_BUF):
            win = win + buf_ref[r, :, sl]
        d = win / float(w) - x
        o_ref[:, sl] = _dot(d, w_ref[g]) * sc_ref[:, sl]


def pool_sample(u, buf_t, w_grp, scale):
    db = u.shape[0]
    return pl.pallas_call(
        _pool_sample_kernel,
        grid=(1,),
        in_specs=[pl.BlockSpec((db, MAIN_W), lambda i: (0, 0)),
                  pl.BlockSpec((POOL_BUF, db, MAIN_W), lambda i: (0, 0, 0)),
                  pl.BlockSpec((4, POOL_GC, POOL_GC), lambda i: (0, 0, 0)),
                  pl.BlockSpec((1, MAIN_W), lambda i: (0, 0))],
        out_specs=pl.BlockSpec((db, MAIN_W), lambda i: (0, 0)),
        out_shape=jax.ShapeDtypeStruct((db, MAIN_W), F32),
        compiler_params=_cparams("arbitrary"),
        name="pool_sample",
    )(u, buf_t, w_grp, scale.reshape(1, MAIN_W))


def _mem_attend_kernel(q_ref, kv_ref, o_ref, *, tq):
    rows = max(tq, 8)
    for h in range(N_MEM_HEADS):
        sl = slice(h * MEM_HEAD_DIM, (h + 1) * MEM_HEAD_DIM)
        q = q_ref[0, :, sl]
        if tq < rows:
            q = jnp.broadcast_to(q[0:1], (rows, MEM_HEAD_DIM))
        k = kv_ref[:, sl]
        v = kv_ref[:, MEM_W + h * MEM_HEAD_DIM:MEM_W + (h + 1) * MEM_HEAD_DIM]
        s = _dot_t(q, k) * MEM_SCALE
        e = jnp.exp(s - jnp.max(s, axis=-1, keepdims=True))
        o = _dot(e, v) / jnp.sum(e, axis=-1, keepdims=True)
        o_ref[0, :, sl] = o[0:tq]


def mem_attend(u3, mkv2, tq):
    b, s, _ = u3.shape
    return pl.pallas_call(
        functools.partial(_mem_attend_kernel, tq=tq),
        grid=(b, s // tq),
        in_specs=[pl.BlockSpec((1, tq, MEM_W), lambda bi, i: (bi, i, MAIN_W // MEM_W)),
                  pl.BlockSpec((MEM_TOKENS, 2 * MEM_W), lambda bi, i: (bi, 0))],
        out_specs=pl.BlockSpec((1, tq, MEM_W), lambda bi, i: (bi, i, 0)),
        out_shape=jax.ShapeDtypeStruct((b, s, MEM_W), F32),
        compiler_params=_cparams("parallel", "arbitrary"),
        name="mem_attend",
    )(u3, mkv2)


def _out_proj_kernel(z_ref, mo_ref, wz_ref, wm_ref, x_ref, o_ref):
    o_ref[...] = x_ref[...] + _dot(z_ref[...], wz_ref[...]) + _dot(mo_ref[...], wm_ref[...])


def out_proj(z, mo, w, x, tm, tn):
    m = x.shape[0]
    return pl.pallas_call(
        _out_proj_kernel,
        grid=(m // tm, D_MODEL // tn),
        in_specs=[pl.BlockSpec((tm, MAIN_W), lambda i, j: (i, 0)),
                  pl.BlockSpec((tm, MEM_W), lambda i, j: (i, 0)),
                  pl.BlockSpec((MAIN_W, tn), lambda i, j: (0, j)),
                  pl.BlockSpec((MEM_W, tn), lambda i, j: (MAIN_W // MEM_W, j)),
                  pl.BlockSpec((tm, tn), lambda i, j: (i, j))],
        out_specs=pl.BlockSpec((tm, tn), lambda i, j: (i, j)),
        out_shape=jax.ShapeDtypeStruct((m, D_MODEL), F32),
        compiler_params=_cparams("parallel", "arbitrary"),
        name="out_proj",
    )(z, mo, w, w, x)


def _ffn_kernel(x_ref, g_ref, wa_ref, wb_ref, wd_ref, gf_ref, o_ref, xn_ref, *, final):
    j = pl.program_id(1)

    @pl.when(j == 0)
    def _():
        x = x_ref[...]
        xn_ref[...] = _rms(x, g_ref[...]).astype(xn_ref.dtype)
        o_ref[...] = x

    xn = xn_ref[...]
    a = _dot(xn, wa_ref[...])
    b = _dot(xn, wb_ref[...])
    o_ref[...] += _dot(jax.nn.silu(a) * b, wd_ref[...])

    if final:
        @pl.when(j == pl.num_programs(1) - 1)
        def _():
            o_ref[...] = _rms(o_ref[...], gf_ref[...])


def ffn(x, g, w_gu, w_down, g_final, final, tm, tf):
    m, k = x.shape
    nf = D_FF // tf
    return pl.pallas_call(
        functools.partial(_ffn_kernel, final=final),
        grid=(m // tm, nf),
        in_specs=[pl.BlockSpec((tm, k), lambda i, j: (i, 0)),
                  pl.BlockSpec((1, k), lambda i, j: (0, 0)),
                  pl.BlockSpec((k, tf), lambda i, j: (0, j)),
                  pl.BlockSpec((k, tf), lambda i, j: (0, nf + j)),
                  pl.BlockSpec((tf, k), lambda i, j: (j, 0)),
                  pl.BlockSpec((1, k), lambda i, j: (0, 0))],
        out_specs=pl.BlockSpec((tm, k), lambda i, j: (i, 0)),
        out_shape=jax.ShapeDtypeStruct((m, k), F32),
        scratch_shapes=[pltpu.VMEM((tm, k), _lhs_dtype(tm))],
        compiler_params=_cparams("parallel", "arbitrary"),
        name="ffn",
    )(x, g.reshape(1, k), w_gu, w_gu, w_down, g_final.reshape(1, k))


def _compress_kernel(tbl_ref, *refs, n_steps):
    pages = refs[:PAGES_PER_STEP]
    nxt_ref, extra_ref, w1_ref, pe_ref, w2_ref, o_ref = refs[PAGES_PER_STEP:]
    p = pl.program_id(1)
    on_extra = p == n_steps
    nxt_extra = p == n_steps - 1
    nb = CMP_STEP * KV_SLOTS
    width = CMP_STRIDE * HEAD_DIM

    def chunk(ref, n):
        return jnp.concatenate([ref[0, n * CMP_STRIDE + r] for r in range(CMP_STRIDE)], axis=1)

    blocks = []
    for k in range(PAGES_PER_STEP):
        for n in range(CH_PER_PAGE):
            x = chunk(pages[k], n)
            blocks.append(jnp.where(on_extra, chunk(extra_ref, n) if k == 0 else 0.0, x))
    blocks.append(jnp.where(on_extra, 0.0, jnp.where(nxt_extra, chunk(extra_ref, 0), chunk(nxt_ref, 0))))
    blocks += [pe_ref[0, 0], pe_ref[0, 1], pe_ref[1, 0], pe_ref[1, 1], jnp.zeros((4, width), F32)]
    lhs = jnp.concatenate(blocks, axis=0)
    w1 = jnp.concatenate([w1_ref[0, 0], w1_ref[0, 1], w1_ref[1, 0], w1_ref[1, 1]], axis=1)
    hc = _dot(lhs, w1)
    is_v = (lax.broadcasted_iota(jnp.int32, (nb + KV_SLOTS, 1), 0) & N_KV) != 0
    d = HEAD_DIM
    hsel = jnp.where(is_v, hc[:nb + KV_SLOTS, 2 * d:], hc[:nb + KV_SLOTS, :2 * d])
    pe_k = hc[nb + 8:nb + 9, 0:d] + hc[nb + 9:nb + 10, d:2 * d]
    pe_v = hc[nb + 10:nb + 11, 2 * d:3 * d] + hc[nb + 11:nb + 12, 3 * d:]
    hid = jax.nn.gelu(hsel[:nb, :d] + hsel[KV_SLOTS:, d:] + jnp.where(is_v[:nb], pe_v, pe_k))
    o2 = _dot(hid, jnp.concatenate([w2_ref[0], w2_ref[1]], axis=1))
    out = jnp.where(is_v[:nb], o2[:, d:], o2[:, :d])
    o_ref[0] = out.reshape(CMP_STEP, KV_SLOTS, d)


def compress(pages, table, extra, cmp_pe, cmp_w1, cmp_w2):
    nb, n_pages = table.shape
    n_steps = n_pages // PAGES_PER_STEP
    w1 = cmp_w1.reshape(2, 2, CMP_STRIDE * HEAD_DIM, HEAD_DIM)
    pe = cmp_pe.reshape(2, 2, 1, CMP_STRIDE * HEAD_DIM)

    def page_map(k):
        return lambda b, p, tbl: (tbl[b, jnp.minimum(p * PAGES_PER_STEP + k, n_pages - 1)], 0, 0, 0)

    in_specs = [pl.BlockSpec((1, PAGE_SIZE, KV_SLOTS, HEAD_DIM), page_map(k)) for k in range(PAGES_PER_STEP)]
    in_specs += [pl.BlockSpec((1, CMP_STRIDE, KV_SLOTS, HEAD_DIM), page_map(PAGES_PER_STEP)),
                 pl.BlockSpec((1, PAGE_SIZE, KV_SLOTS, HEAD_DIM), lambda b, p, tbl: (b, 0, 0, 0)),
                 pl.BlockSpec(w1.shape, lambda b, p, tbl: (0, 0, 0, 0)),
                 pl.BlockSpec(pe.shape, lambda b, p, tbl: (0, 0, 0, 0)),
                 pl.BlockSpec(cmp_w2.shape, lambda b, p, tbl: (0, 0, 0))]
    n_out = CMP_STEP * (n_steps + 1)
    out = pl.pallas_call(
        functools.partial(_compress_kernel, n_steps=n_steps),
        grid_spec=pltpu.PrefetchScalarGridSpec(
            num_scalar_prefetch=1,
            grid=(nb, n_steps + 1),
            in_specs=in_specs,
            out_specs=pl.BlockSpec((1, CMP_STEP, KV_SLOTS, HEAD_DIM), lambda b, p, tbl: (b, p, 0, 0)),
        ),
        out_shape=jax.ShapeDtypeStruct((nb, n_out, KV_SLOTS, HEAD_DIM), F32),
        compiler_params=_cparams("parallel", "arbitrary"),
        name="compress",
    )(table, *([pages] * (PAGES_PER_STEP + 1)), extra, w1, pe, cmp_w2)
    return jnp.transpose(out, (0, 2, 1, 3)).reshape(nb, 2, N_KV, n_out, HEAD_DIM)


def _cover(n_pad, j_pad, n_cmp):
    n = lax.broadcasted_iota(jnp.int32, (n_pad, j_pad), 0)
    j = lax.broadcasted_iota(jnp.int32, (n_pad, j_pad), 1)
    hit = (n * CMP_STRIDE < j * SLC_BLOCK + SLC_BLOCK) & (n * CMP_STRIDE + CMP_BLOCK - 1 >= j * SLC_BLOCK)
    return (hit & (n < n_cmp)).astype(F32)


def _softmax_rows(s):
    e = jnp.exp(s - jnp.max(s, axis=-1, keepdims=True))
    return e / jnp.sum(e, axis=-1, keepdims=True)


def _nsa_prompt_kernel(q_ref, gl_ref, bg_ref, cos_ref, sin_ref, ckv_ref, ks_ref, vs_ref, kw_ref, vw_ref,
                       o_ref, *, n_cmp, n_slc, kt):
    qb = pl.program_id(1)
    tq = Q_BLOCK
    cos2 = cos_ref[...]
    sin2 = sin_ref[...]
    n_pad = ckv_ref.shape[3]
    qpos1 = qb * tq + lax.broadcasted_iota(jnp.int32, (tq, 1), 0)
    qpos = jnp.concatenate([qpos1] * GQA, axis=0)
    gates = jax.nn.sigmoid(gl_ref[...] + bg_ref[...])

    n_i = lax.broadcasted_iota(jnp.int32, (1, n_pad), 1)
    vis_c = (n_i * CMP_STRIDE + CMP_BLOCK - 1 <= qpos) & (n_i < n_cmp)
    any_c = (qpos >= CMP_BLOCK - 1).astype(F32)
    cover = _cover(n_pad, n_slc, n_cmp)
    j_i = lax.broadcasted_iota(jnp.int32, (1, n_slc), 1)
    qblk = jnp.right_shift(qpos1, SLC_SHIFT)
    forced = (j_i == 0) | (j_i == qblk) | (j_i == qblk - 1)
    valid = j_i <= qblk
    n_sel = min(N_SELECT, n_slc)

    w_lo = jnp.maximum(qb * tq - WINDOW, 0)
    w_lo = pl.multiple_of(w_lo, tq)
    kpos_w = w_lo + lax.broadcasted_iota(jnp.int32, (1, WINDOW + tq), 1)
    vis_w = (kpos_w <= qpos) & (qpos - kpos_w < WINDOW)

    for h in range(N_KV):
        q = jnp.concatenate(
            [_rope(q_ref[:, (h * GQA + g) * HEAD_DIM:(h * GQA + g + 1) * HEAD_DIM], cos2, sin2)
             for g in range(GQA)], axis=0)
        qs = (q * SCALE).astype(BF16)
        hs = slice(h * HEAD_DIM, (h + 1) * HEAD_DIM)

        s = jnp.where(vis_c, _dot_t(qs, ckv_ref[0, 0, h]), NEG)
        p = _softmax_rows(s) * any_c
        o_c = _dot(p, ckv_ref[0, 1, h])

        p_sum = p[0:tq] + p[tq:2 * tq] + p[2 * tq:3 * tq]
        score = jnp.dot(p_sum, cover, preferred_element_type=F32, precision=lax.Precision.HIGHEST)
        score = jnp.where(forced, jnp.inf, jnp.where(valid, score, -jnp.inf))
        rank = jnp.zeros((tq, n_slc), jnp.int32)
        for i in range(n_slc):
            col = score[:, i:i + 1]
            beats = (col > score) | ((col == score) & (i < j_i))
            rank = rank + beats.astype(jnp.int32)
        sel = (rank < n_sel).astype(BF16)

        def slc_step(t, carry):
            m, l, acc = carry
            k0 = pl.multiple_of(t * kt, kt)
            kpos = k0 + lax.broadcasted_iota(jnp.int32, (1, kt), 1)
            expand = (jnp.right_shift(kpos, SLC_SHIFT) == lax.broadcasted_iota(jnp.int32, (n_slc, kt), 0)).astype(BF16)
            chosen = jnp.dot(sel, expand, preferred_element_type=F32)
            chosen = jnp.concatenate([chosen] * GQA, axis=0)
            vis = (chosen > 0.5) & (kpos <= qpos)
            s = jnp.where(vis, _dot_t(qs, ks_ref[pl.ds(k0, kt), hs]), NEG)
            m_new = jnp.maximum(m, jnp.max(s, axis=-1, keepdims=True))
            a = jnp.exp(m - m_new)
            e = jnp.exp(s - m_new)
            l = l * a + jnp.sum(e, axis=-1, keepdims=True)
            acc = acc * a + _dot(e, vs_ref[pl.ds(k0, kt), hs])
            return m_new, l, acc

        n_t = (qb * tq + tq + kt - 1) // kt
        m0 = jnp.full((GQA * tq, 1), NEG, F32)
        l0 = jnp.zeros((GQA * tq, 1), F32)
        a0 = jnp.zeros((GQA * tq, HEAD_DIM), F32)
        _, l, acc = lax.fori_loop(0, n_t, slc_step, (m0, l0, a0))
        o_s = acc / l

        s = jnp.where(vis_w, _dot_t(qs, kw_ref[pl.ds(w_lo, WINDOW + tq), hs]), NEG)
        e = jnp.exp(s - jnp.max(s, axis=-1, keepdims=True))
        o_w = _dot(e, vw_ref[pl.ds(w_lo, WINDOW + tq), hs]) / jnp.sum(e, axis=-1, keepdims=True)

        for g in range(GQA):
            hd = h * GQA + g
            r = slice(g * tq, (g + 1) * tq)
            gc = gates[:, hd * N_BRANCH + 0:hd * N_BRANCH + 1]
            gs = gates[:, hd * N_BRANCH + 1:hd * N_BRANCH + 2]
            gw = gates[:, hd * N_BRANCH + 2:hd * N_BRANCH + 3]
            o_ref[:, hd * HEAD_DIM:(hd + 1) * HEAD_DIM] = gc * o_c[r] + gs * o_s[r] + gw * o_w[r]


def nsa_prompt(u, b_gate_pad, cos2, sin2, ckv, slc_k, slc_v, win_k, win_v, batch, seq):
    nqb = seq // Q_BLOCK
    n_pad = ckv.shape[3]
    kspec = pl.BlockSpec((seq, N_KV * HEAD_DIM), lambda b, i: (b, 0))
    return pl.pallas_call(
        functools.partial(_nsa_prompt_kernel, n_cmp=seq // CMP_STRIDE - 1, n_slc=seq // SLC_BLOCK, kt=512),
        grid=(batch, nqb),
        in_specs=[pl.BlockSpec((Q_BLOCK, MAIN_W), lambda b, i: (b * nqb + i, 0)),
                  pl.BlockSpec((Q_BLOCK, GATE_PAD), lambda b, i: (b * nqb + i, (MAIN_W + MEM_W) // GATE_PAD)),
                  pl.BlockSpec((1, GATE_PAD), lambda b, i: (0, 0)),
                  pl.BlockSpec((Q_BLOCK, HEAD_DIM), lambda b, i: (i, 0)),
                  pl.BlockSpec((Q_BLOCK, HEAD_DIM), lambda b, i: (i, 0)),
                  pl.BlockSpec((1, 2, N_KV, n_pad, HEAD_DIM), lambda b, i: (b, 0, 0, 0, 0)),
                  kspec, kspec, kspec, kspec],
        out_specs=pl.BlockSpec((Q_BLOCK, MAIN_W), lambda b, i: (b * nqb + i, 0)),
        out_shape=jax.ShapeDtypeStruct((batch * seq, MAIN_W), F32),
        compiler_params=_cparams("parallel", "arbitrary"),
        name="nsa_prompt",
    )(u, u, b_gate_pad, cos2, sin2, ckv, slc_k, slc_v, win_k, win_v)


HEAD_ROWS = 16


def _head_mask(h):
    r = lax.broadcasted_iota(jnp.int32, (HEAD_ROWS, 1), 0)
    return (r >= h * GQA) & (r < (h + 1) * GQA)


def _nsa_sample_a_kernel(q_ref, cos_ref, sin_ref, ckv_ref, wkv_ref, wnew_ref, qr_ref, oc_ref, ow_ref, sel_ref,
                         *, n_cmp, n_slc, j_pad):
    n_pad = ckv_ref.shape[3]
    q = _rope(q_ref[0], cos_ref[...], sin_ref[...])
    qr_ref[0] = q
    qs = q * SCALE
    n_i = lax.broadcasted_iota(jnp.int32, (1, n_pad), 1)
    vis_c = (n_i * CMP_STRIDE + CMP_BLOCK - 1 <= PAST_LEN) & (n_i < n_cmp)
    cover = _cover(n_pad, j_pad, n_cmp)
    j_row = lax.broadcasted_iota(jnp.int32, (1, j_pad), 1)
    j_col = lax.broadcasted_iota(jnp.int32, (j_pad, 1), 0)
    qblk = PAST_LEN // SLC_BLOCK
    lane = lax.broadcasted_iota(jnp.int32, (N_SELECT, HEAD_DIM), 1)
    r_col = lax.broadcasted_iota(jnp.int32, (N_SELECT, 1), 0)
    n_win = wkv_ref.shape[1]
    kpos_w = PAST_LEN - n_win + lax.broadcasted_iota(jnp.int32, (1, n_win), 1)
    vis_w = (kpos_w <= PAST_LEN) & (PAST_LEN - kpos_w < WINDOW)

    o_c = jnp.zeros((HEAD_ROWS, HEAD_DIM), F32)
    o_w = jnp.zeros((HEAD_ROWS, HEAD_DIM), F32)
    sel_out = jnp.zeros((N_SELECT, HEAD_DIM), jnp.int32)
    for h in range(N_KV):
        mine = _head_mask(h)
        hs = slice(h * HEAD_DIM, (h + 1) * HEAD_DIM)
        vsl = slice((N_KV + h) * HEAD_DIM, (N_KV + h + 1) * HEAD_DIM)

        s = jnp.where(vis_c, _dot_t(qs, ckv_ref[0, 0, h]), NEG)
        p = _softmax_rows(s) * float(PAST_LEN >= CMP_BLOCK - 1)
        o_c = jnp.where(mine, _dot(p, ckv_ref[0, 1, h]), o_c)

        p_sum = jnp.sum(jnp.where(mine, p, 0.0), axis=0, keepdims=True)
        score = jnp.dot(jnp.broadcast_to(p_sum, (8, n_pad)), cover, preferred_element_type=F32,
                        precision=lax.Precision.HIGHEST)
        forced = (j_row == 0) | (j_row == qblk) | (j_row == qblk - 1)
        score = jnp.where(forced, jnp.inf, jnp.where(j_row <= qblk, score, -jnp.inf))
        score = jnp.where(j_row < n_slc, score, -jnp.inf)
        s_col = jnp.transpose(score)[:, 0:1]
        beats = ((s_col > score[0:1]) | ((s_col == score[0:1]) & (j_col < j_row))) & (j_col < n_slc)
        rank = jnp.sum(beats.astype(F32), axis=0, keepdims=True)
        hit = (rank == r_col.astype(F32)) & (j_row < n_slc)
        idx = jnp.sum(jnp.where(hit, j_row.astype(F32), 0.0), axis=1, keepdims=True)
        sel_out = jnp.where(lane == h, idx.astype(jnp.int32), sel_out)

        s = jnp.where(vis_w, _dot_t(qs, wkv_ref[0, :, hs]), NEG)
        s_new = jnp.sum(qs * wnew_ref[0, :, hs], axis=-1, keepdims=True)
        m = jnp.maximum(jnp.max(s, axis=-1, keepdims=True), s_new)
        e = jnp.exp(s - m)
        e_new = jnp.exp(s_new - m)
        o = (_dot(e, wkv_ref[0, :, vsl]) + e_new * wnew_ref[0, :, vsl]) / (jnp.sum(e, axis=-1, keepdims=True) + e_new)
        o_w = jnp.where(mine, o, o_w)
    oc_ref[0] = o_c
    ow_ref[0] = o_w
    sel_ref[0] = sel_out


def nsa_sample_a(q16, cos2, sin2, ckv, win_cache, win_new, n_cmp, n_slc):
    db = q16.shape[0]
    n_pad = ckv.shape[3]
    n_win = win_cache.shape[1]
    j_pad = -(-n_slc // 128) * 128
    hspec = pl.BlockSpec((1, HEAD_ROWS, HEAD_DIM), lambda b: (b, 0, 0))
    hout = jax.ShapeDtypeStruct((db, HEAD_ROWS, HEAD_DIM), F32)
    return pl.pallas_call(
        functools.partial(_nsa_sample_a_kernel, n_cmp=n_cmp, n_slc=n_slc, j_pad=j_pad),
        grid=(db,),
        in_specs=[hspec,
                  pl.BlockSpec((1, HEAD_DIM), lambda b: (0, 0)),
                  pl.BlockSpec((1, HEAD_DIM), lambda b: (0, 0)),
                  pl.BlockSpec((1, 2, N_KV, n_pad, HEAD_DIM), lambda b: (b, 0, 0, 0, 0)),
                  pl.BlockSpec((1, n_win, KV_W), lambda b: (b, 0, 0)),
                  pl.BlockSpec((1, 1, KV_W), lambda b: (b, 0, 0))],
        out_specs=[hspec, hspec, hspec, pl.BlockSpec((1, N_SELECT, HEAD_DIM), lambda b: (b, 0, 0))],
        out_shape=[hout, hout, hout, jax.ShapeDtypeStruct((db, N_SELECT, HEAD_DIM), jnp.int32)],
        compiler_params=_cparams("arbitrary"),
        name="nsa_sample_a",
    )(q16, cos2, sin2, ckv, win_cache, win_new)


def _nsa_sample_b_kernel(sel_ref, tbl_ref, q_ref, k_ref, v_ref, new_k_ref, new_v_ref, oc_ref, ow_ref, gl_ref, bg_ref,
                         o_ref, m_ref, l_ref, acc_ref, os_ref):
    b = pl.program_id(0)
    h = pl.program_id(1)
    r = pl.program_id(2)
    blk = sel_ref[b, h, r]

    @pl.when(r == 0)
    def _():
        m_ref[...] = jnp.full(m_ref.shape, NEG, F32)
        l_ref[...] = jnp.zeros(l_ref.shape, F32)
        acc_ref[...] = jnp.zeros(acc_ref.shape, F32)

    in_past = blk * SLC_BLOCK < PAST_LEN
    k = jnp.where(in_past, k_ref[0], new_k_ref[0])
    v = jnp.where(in_past, v_ref[0], new_v_ref[0])
    kpos = blk * SLC_BLOCK + lax.broadcasted_iota(jnp.int32, (1, SLC_BLOCK), 1)
    s = jnp.where(kpos <= PAST_LEN, _dot_t(q_ref[0] * SCALE, k), NEG)
    m_old = m_ref[...]
    m_new = jnp.maximum(m_old, jnp.max(s, axis=-1, keepdims=True))
    a = jnp.exp(m_old - m_new)
    e = jnp.exp(s - m_new)
    l_ref[...] = l_ref[...] * a + jnp.sum(e, axis=-1, keepdims=True)
    acc_ref[...] = acc_ref[...] * a + _dot(e, v)
    m_ref[...] = m_new

    last = r == pl.num_programs(2) - 1

    @pl.when((r == 0) & (h == 0))
    def _():
        os_ref[...] = jnp.zeros(os_ref.shape, F32)

    @pl.when(last)
    def _():
        row = lax.broadcasted_iota(jnp.int32, (HEAD_ROWS, 1), 0)
        mine = (row >= h * GQA) & (row < (h + 1) * GQA)
        os_ref[...] = jnp.where(mine, acc_ref[...] / l_ref[...], os_ref[...])

    @pl.when(last & (h == pl.num_programs(1) - 1))
    def _():
        gates = jax.nn.sigmoid(gl_ref[0] + bg_ref[...])
        o_ref[0] = gates[:, 0:1] * oc_ref[0] + gates[:, 1:2] * os_ref[...] + gates[:, 2:3] * ow_ref[0]


def nsa_sample_b(sel, page_table, q_rope, slc_cache2, slc_new, o_c, o_w, gate_logits, b_gate16):
    db = q_rope.shape[0]
    last_blk = PAST_LEN // SLC_BLOCK - 1
    halves = PAGE_SIZE // SLC_BLOCK

    def cache_map(c):
        def index(b, h, r, sel_ref, tbl_ref):
            j = jnp.minimum(sel_ref[b, h, r], last_blk)
            return (tbl_ref[b, j // halves] * halves + j % halves, 0, c * N_KV + h)
        return index

    hspec = pl.BlockSpec((1, HEAD_ROWS, HEAD_DIM), lambda b, h, r, s, t: (b, 0, 0))
    return pl.pallas_call(
        _nsa_sample_b_kernel,
        grid_spec=pltpu.PrefetchScalarGridSpec(
            num_scalar_prefetch=2,
            grid=(db, N_KV, N_SELECT),
            in_specs=[hspec,
                      pl.BlockSpec((1, SLC_BLOCK, HEAD_DIM), cache_map(0)),
                      pl.BlockSpec((1, SLC_BLOCK, HEAD_DIM), cache_map(1)),
                      pl.BlockSpec((1, 1, HEAD_DIM), lambda b, h, r, s, t: (b, 0, h)),
                      pl.BlockSpec((1, 1, HEAD_DIM), lambda b, h, r, s, t: (b, 0, N_KV + h)),
                      hspec, hspec, hspec,
                      pl.BlockSpec((HEAD_ROWS, HEAD_DIM), lambda b, h, r, s, t: (0, 0))],
            out_specs=hspec,
            scratch_shapes=[pltpu.VMEM((HEAD_ROWS, 1), F32), pltpu.VMEM((HEAD_ROWS, 1), F32),
                            pltpu.VMEM((HEAD_ROWS, HEAD_DIM), F32), pltpu.VMEM((HEAD_ROWS, HEAD_DIM), F32)],
        ),
        out_shape=jax.ShapeDtypeStruct((db, HEAD_ROWS, HEAD_DIM), F32),
        compiler_params=_cparams("parallel", "arbitrary", "arbitrary"),
        name="nsa_sample_b",
    )(sel, page_table, q_rope, slc_cache2, slc_cache2, slc_new, slc_new, o_c, o_w, gate_logits, b_gate16)


def _rope_tables(pos):
    half = HEAD_DIM // 2
    inv = ROPE_THETA ** (-jnp.arange(half, dtype=F32) / half)
    ang = pos.astype(F32)[:, None] * inv[None, :]
    cos, sin = jnp.cos(ang), jnp.sin(ang)
    return jnp.concatenate([cos, cos], -1), jnp.concatenate([-sin, sin], -1)


def _pad_w_in_b(w):
    return jnp.concatenate([w[:, :MAIN_W], w[:, MAIN_W + GATE_W:],
                            jnp.pad(w[:, MAIN_W:MAIN_W + GATE_W], ((0, 0), (0, GATE_PAD - GATE_W)))], axis=1)


def _prompt_trunk(x_prompt, mem_prompt, g_mix, w_in_a, pool_grp_w, pool_scale, w_in_b_pad, b_gate, g_kv, w_kv,
                  cmp_pe, cmp_w1, cmp_w2, g_mem, w_mem_kv, w_out, g_ffn, w_gu, w_down, g_final):
    batch, seq, _ = x_prompt.shape
    m = batch * seq
    tm = 512
    x = x_prompt.reshape(m, D_MODEL)
    mem2 = mem_prompt.reshape(batch * MEM_TOKENS, D_MODEL)
    mem_kv = [rms_matmul(mem2, g_mem[l], w_mem_kv[l], 512, 512) for l in range(DEPTH)]
    cos2, sin2 = _rope_tables(jnp.arange(seq, dtype=jnp.int32))
    pools = []
    nsa = None
    for l in range(DEPTH):
        if l == N_A:
            cmp_kv, slc_kv, win_kv = kv_proj(x, g_kv, w_kv, cos2, sin2, tm)
            table = jnp.arange(batch * seq // PAGE_SIZE, dtype=jnp.int32).reshape(batch, seq // PAGE_SIZE)
            ckv = compress(cmp_kv.reshape(-1, PAGE_SIZE, KV_SLOTS, HEAD_DIM), table,
                           jnp.zeros((batch, PAGE_SIZE, KV_SLOTS, HEAD_DIM), F32), cmp_pe, cmp_w1, cmp_w2)
            half = N_KV * HEAD_DIM
            nsa = (ckv, slc_kv[:, :half].astype(BF16), slc_kv[:, half:].astype(BF16),
                   win_kv[:, :half].astype(BF16), win_kv[:, half:].astype(BF16))
        if l < N_A:
            u = rms_matmul(x, g_mix[l], w_in_a[l], tm, 512)
            u3 = u.reshape(batch, seq, -1)
            z = pool_prompt(u3, pool_grp_w[l], pool_scale[l]).reshape(m, MAIN_W)
            pools.append(u3[:, seq - POOL_BUF:, :MAIN_W])
        else:
            j = l - N_A
            u = rms_matmul(x, g_mix[l], w_in_b_pad[j], tm, 768)
            u3 = u.reshape(batch, seq, -1)
            bg = jnp.pad(b_gate[j], (0, GATE_PAD - GATE_W)).reshape(1, GATE_PAD)
            z = nsa_prompt(u, bg, cos2, sin2, *nsa, batch, seq)
        mo = mem_attend(u3, mem_kv[l], 512).reshape(m, MEM_W)
        x = out_proj(z, mo, w_out[l], x, tm, 512)
        x = ffn(x, g_ffn[l], w_gu[l], w_down[l], g_final, l == DEPTH - 1, tm, 512)
    kv5 = (batch, seq, 2, N_KV, HEAD_DIM)
    n_keep = min(WINDOW, seq)
    return (x.reshape(batch, seq, D_MODEL), jnp.stack(pools), cmp_kv.reshape(kv5), slc_kv.reshape(kv5),
            win_kv.reshape(kv5)[:, seq - n_keep:],
            jnp.stack(mem_kv).reshape(DEPTH, batch, MEM_TOKENS, 2, N_MEM_HEADS, MEM_HEAD_DIM))


def _sample_trunk(x_sample, state_pool, cache_cmp_kv, cache_slc_kv, cache_win_kv, cache_mem_kv, page_table,
                  g_mix, w_in_a, pool_grp_w, pool_scale, w_in_b_pad, b_gate, g_kv, w_kv,
                  cmp_pe, cmp_w1, cmp_w2, w_out, g_ffn, w_gu, w_down, g_final):
    db = x_sample.shape[0]
    x = x_sample.reshape(db, D_MODEL)
    cos2, sin2 = _rope_tables(jnp.full((db,), PAST_LEN, jnp.int32))
    n_phys = cache_cmp_kv.shape[0]
    pools = []
    for l in range(DEPTH):
        mkv = cache_mem_kv[l].reshape(db * MEM_TOKENS, 2 * MEM_W)
        if l == N_A:
            cmp_new, slc_new, win_new = kv_proj(x, g_kv, w_kv, cos2, sin2, db)
            extra = jnp.pad(cmp_new.reshape(db, 1, KV_SLOTS, HEAD_DIM), ((0, 0), (0, PAGE_SIZE - 1), (0, 0), (0, 0)))
            ckv = compress(cache_cmp_kv.reshape(n_phys, PAGE_SIZE, KV_SLOTS, HEAD_DIM), page_table, extra,
                           cmp_pe, cmp_w1, cmp_w2)
            t_full = -(-(PAST_LEN + 1) // SLC_BLOCK) * SLC_BLOCK
            n_cmp = t_full // CMP_STRIDE - 1
            n_slc = t_full // SLC_BLOCK
        if l < N_A:
            u = rms_matmul(x, g_mix[l], w_in_a[l], db, 512)
            z = pool_sample(u, jnp.transpose(state_pool[l], (1, 0, 2)), pool_grp_w[l], pool_scale[l])
            pools.append(jnp.concatenate([state_pool[l][:, 1:], u[:, None, :MAIN_W]], axis=1))
        else:
            j = l - N_A
            u = rms_matmul(x, g_mix[l], w_in_b_pad[j], db, 768)
            q16 = jnp.pad(u[:, :MAIN_W].reshape(db, N_HEADS, HEAD_DIM), ((0, 0), (0, HEAD_ROWS - N_HEADS), (0, 0)))
            q_rope, o_c, o_w, sel = nsa_sample_a(
                q16, cos2[:1], sin2[:1], ckv, cache_win_kv.reshape(db, -1, KV_W), win_new.reshape(db, 1, KV_W),
                n_cmp, n_slc)
            sel = jnp.transpose(sel[:, :, :N_KV], (0, 2, 1))
            gl = u[:, MAIN_W + MEM_W:MAIN_W + MEM_W + GATE_W].reshape(db, N_HEADS, N_BRANCH)
            gl = jnp.pad(gl, ((0, 0), (0, HEAD_ROWS - N_HEADS), (0, HEAD_DIM - N_BRANCH)))
            bg = jnp.pad(b_gate[j].reshape(N_HEADS, N_BRANCH), ((0, HEAD_ROWS - N_HEADS), (0, HEAD_DIM - N_BRANCH)))
            z16 = nsa_sample_b(sel, page_table, q_rope, cache_slc_kv.reshape(n_phys * 2, SLC_BLOCK, KV_W),
                               slc_new.reshape(db, 1, KV_W), o_c, o_w, gl, bg)
            z = z16[:, :N_HEADS].reshape(db, MAIN_W)
        mo = mem_attend(u.reshape(db, 1, -1), mkv, 1).reshape(db, MEM_W)
        x = out_proj(z, mo, w_out[l], x, db, 512)
        x = ffn(x, g_ffn[l], w_gu[l], w_down[l], g_final, l == DEPTH - 1, db, 512)
    kv5 = (db, 1, 2, N_KV, HEAD_DIM)
    win_s = jnp.concatenate([cache_win_kv, win_new.reshape(kv5)], axis=1)
    win_s = win_s[:, win_s.shape[1] - min(WINDOW, win_s.shape[1]):]
    return (x.reshape(db, 1, D_MODEL), jnp.stack(pools), cmp_new.reshape(kv5), slc_new.reshape(kv5), win_s)


def kernel(x_prompt, x_sample, mem_prompt, state_pool, cache_cmp_kv, cache_slc_kv, cache_win_kv, cache_mem_kv, page_table, g_mix, w_in_a, pool_grp_w, pool_scale, w_in_b, b_gate, g_kv, w_kv, cmp_pe, cmp_w1, cmp_w2, g_mem, w_mem_kv, w_out, g_ffn, w_gu, w_down, g_final):
    w_in_b_pad = [_pad_w_in_b(w_in_b[j]) for j in range(DEPTH - N_A)]
    y_p, pool_p, cmp_p, slc_p, win_p, mem_kv_p = _prompt_trunk(
        x_prompt, mem_prompt, g_mix, w_in_a, pool_grp_w, pool_scale, w_in_b_pad, b_gate, g_kv, w_kv,
        cmp_pe, cmp_w1, cmp_w2, g_mem, w_mem_kv, w_out, g_ffn, w_gu, w_down, g_final)
    y_s, pool_s, cmp_s, slc_s, win_s = _sample_trunk(
        x_sample, state_pool, cache_cmp_kv, cache_slc_kv, cache_win_kv, cache_mem_kv, page_table,
        g_mix, w_in_a, pool_grp_w, pool_scale, w_in_b_pad, b_gate, g_kv, w_kv,
        cmp_pe, cmp_w1, cmp_w2, w_out, g_ffn, w_gu, w_down, g_final)
    return (y_p, y_s, pool_p, cmp_p, slc_p, win_p, mem_kv_p, pool_s, cmp_s, slc_s, win_s)
```

```python
import functools

import jax
import jax.numpy as jnp
from jax import lax
from jax.experimental import pallas as pl
from jax.experimental.pallas import tpu as pltpu

F32 = jnp.float32
BF16 = jnp.bfloat16

D_MODEL = 2048
DEPTH = 4
N_A = 2
PAST_LEN = 16384
PAGE_SIZE = 128
MEM_TOKENS = 256
N_MEM_HEADS = 4
MEM_HEAD_DIM = 128
MEM_W = 512
MAIN_W = 1536
POOL_WINDOWS = (2, 4, 8, 16)
POOL_GC = 384
POOL_BUF = 15
HEAD_DIM = 128
N_HEADS = 12
N_KV = 4
GQA = 3
N_BRANCH = 3
GATE_W = 36
GATE_PAD = 256
CMP_BLOCK = 32
CMP_STRIDE = 16
SLC_BLOCK = 64
SLC_SHIFT = 6
N_SELECT = 16
WINDOW = 512
Q_BLOCK = 128
D_FF = 5632
ROPE_THETA = 10000.0
EPS = 1e-6
NEG = -1e30
SCALE = HEAD_DIM ** -0.5
MEM_SCALE = MEM_HEAD_DIM ** -0.5
KV_SLOTS = 2 * N_KV
KV_W = KV_SLOTS * HEAD_DIM
W_IN_B = MAIN_W + MEM_W + GATE_PAD

VMEM_LIMIT = 56 * 1024 * 1024
FFN_TM = 1024
FFN_TF = 256
PAGES_PER_STEP = 8
CH_PER_PAGE = PAGE_SIZE // CMP_STRIDE
CMP_STEP = PAGES_PER_STEP * CH_PER_PAGE


def _cparams(*sem):
    return pltpu.CompilerParams(dimension_semantics=sem, vmem_limit_bytes=VMEM_LIMIT)


def _rms(x, g):
    return x * lax.rsqrt(jnp.mean(x * x, axis=-1, keepdims=True) + EPS) * g


def _dot(a, b):
    return jnp.dot(a.astype(BF16), b.astype(BF16), preferred_element_type=F32)


def _dot_t(a, b):
    return lax.dot_general(a.astype(BF16), b.astype(BF16), (((1,), (1,)), ((), ())),
                           preferred_element_type=F32)


def _lhs_dtype(rows):
    return BF16 if rows % 16 == 0 else F32


def _rope(x, cos2, sin2):
    return x * cos2 + pltpu.roll(x, HEAD_DIM // 2, axis=1) * sin2


def _rms_matmul_kernel(x_ref, g_ref, w_ref, o_ref, xn_ref, wb_ref):
    i = pl.program_id(0)
    j = pl.program_id(1)

    @pl.when(j == 0)
    def _():
        xn_ref[...] = _rms(x_ref[...], g_ref[...]).astype(xn_ref.dtype)

    @pl.when(i == 0)
    def _():
        wb_ref[j] = w_ref[...].astype(BF16)

    o_ref[...] = _dot(xn_ref[...], wb_ref[j])


def _once_per_column(nj):
    return lambda i, j: jnp.where(i == 0, j, nj - 1)


def rms_matmul(x, g, w, layer, tm, tn):
    m, k = x.shape
    n = w.shape[2]
    nj = n // tn
    col = _once_per_column(nj)
    return pl.pallas_call(
        _rms_matmul_kernel,
        grid=(m // tm, nj),
        in_specs=[pl.BlockSpec((tm, k), lambda i, j: (i, 0)),
                  pl.BlockSpec((1, k), lambda i, j: (0, 0)),
                  pl.BlockSpec((None, k, tn), lambda i, j: (layer, 0, col(i, j)))],
        out_specs=pl.BlockSpec((tm, tn), lambda i, j: (i, j)),
        out_shape=jax.ShapeDtypeStruct((m, n), F32),
        scratch_shapes=[pltpu.VMEM((tm, k), _lhs_dtype(tm)), pltpu.VMEM((nj, k, tn), BF16)],
        compiler_params=_cparams("arbitrary", "arbitrary"),
        name="rms_matmul",
    )(x, g.reshape(1, k), w)


def _kv_proj_kernel(x_ref, g_ref, w_ref, cos_ref, sin_ref, cmp_ref, slc_ref, win_ref,
                    slc_k_ref, slc_v_ref, win_k_ref, win_v_ref, xn_ref, wb_ref):
    i = pl.program_id(0)
    j = pl.program_id(1)
    half = N_KV * HEAD_DIM

    @pl.when(j == 0)
    def _():
        xn_ref[...] = _rms(x_ref[...], g_ref[...]).astype(xn_ref.dtype)

    @pl.when(i == 0)
    def _():
        wb_ref[j] = w_ref[...].astype(BF16)

    y = _dot(xn_ref[...], wb_ref[j])
    cos2 = cos_ref[...]
    sin2 = sin_ref[...]
    roped = jnp.concatenate(
        [_rope(y[:, h * HEAD_DIM:(h + 1) * HEAD_DIM], cos2, sin2) for h in range(N_KV)], axis=1)
    y = jnp.where(j % 2 == 0, roped, y)
    copies = ((slc_k_ref, slc_v_ref), (win_k_ref, win_v_ref))
    for br, ref in enumerate((cmp_ref, slc_ref, win_ref)):
        for c in range(2):
            @pl.when(j == 2 * br + c)
            def _(ref=ref, br=br, c=c):
                ref[:, c * half:(c + 1) * half] = y
                if br > 0:
                    copies[br - 1][c][...] = y.astype(BF16)


def kv_proj(x, g, w, cos2, sin2, tm):
    m, k = x.shape
    half = N_KV * HEAD_DIM
    nt = cos2.shape[0] // tm
    nj = 2 * N_BRANCH
    col = _once_per_column(nj)
    out = jax.ShapeDtypeStruct((m, KV_W), F32)
    out16 = jax.ShapeDtypeStruct((m, half), BF16)
    ospec = pl.BlockSpec((tm, KV_W), lambda i, j: (i, 0))
    ospec16 = pl.BlockSpec((tm, half), lambda i, j: (i, 0))
    return pl.pallas_call(
        _kv_proj_kernel,
        grid=(m // tm, nj),
        in_specs=[pl.BlockSpec((tm, k), lambda i, j: (i, 0)),
                  pl.BlockSpec((1, k), lambda i, j: (0, 0)),
                  pl.BlockSpec((k, half), lambda i, j: (0, col(i, j))),
                  pl.BlockSpec((tm, HEAD_DIM), lambda i, j: (i % nt, 0)),
                  pl.BlockSpec((tm, HEAD_DIM), lambda i, j: (i % nt, 0))],
        out_specs=[ospec, ospec, ospec, ospec16, ospec16, ospec16, ospec16],
        out_shape=[out, out, out, out16, out16, out16, out16],
        scratch_shapes=[pltpu.VMEM((tm, k), _lhs_dtype(tm)), pltpu.VMEM((nj, k, half), BF16)],
        compiler_params=_cparams("arbitrary", "arbitrary"),
        name="kv_proj",
    )(x, g.reshape(1, k), w, cos2, sin2)


def _pool_prompt_kernel(cur_ref, prev_ref, w_ref, sc_ref, o_ref, *, ts):
    i = pl.program_id(1)
    keep = (i > 0).astype(F32)
    pos = (i * ts + lax.broadcasted_iota(jnp.int32, (ts, 1), 0) + 1).astype(F32)
    for g, w in enumerate(POOL_WINDOWS):
        sl = slice(g * POOL_GC, (g + 1) * POOL_GC)
        x = cur_ref[0, :, sl]
        ext = jnp.concatenate([prev_ref[0, :, sl] * keep, x], axis=0)
        span = 1
        while span < w:
            ext = ext[span:] + ext[:-span]
            span *= 2
        win = ext[17 - w:17 - w + ts]
        d = win / jnp.minimum(pos, float(w)) - x
        o_ref[0, :, sl] = _dot(d, w_ref[g]) * sc_ref[:, sl]


def pool_prompt(u3, w_grp, scale, ts=256):
    b, s, n = u3.shape
    r = ts // 16
    return pl.pallas_call(
        functools.partial(_pool_prompt_kernel, ts=ts),
        grid=(b, s // ts),
        in_specs=[pl.BlockSpec((1, ts, MAIN_W), lambda bi, i: (bi, i, 0)),
                  pl.BlockSpec((1, 16, MAIN_W), lambda bi, i: (bi, jnp.maximum(i * r - 1, 0), 0)),
                  pl.BlockSpec((4, POOL_GC, POOL_GC), lambda bi, i: (0, 0, 0)),
                  pl.BlockSpec((1, MAIN_W), lambda bi, i: (0, 0))],
        out_specs=pl.BlockSpec((1, ts, MAIN_W), lambda bi, i: (bi, i, 0)),
        out_shape=jax.ShapeDtypeStruct((b, s, MAIN_W), F32),
        compiler_params=_cparams("parallel", "arbitrary"),
        name="pool_prompt",
    )(u3, u3, w_grp, scale.reshape(1, MAIN_W))


def _pool_sample_kernel(cur_ref, buf_ref, w_ref, sc_ref, o_ref):
    for g, w in enumerate(POOL_WINDOWS):
        sl = slice(g * POOL_GC, (g + 1) * POOL_GC)
        x = cur_ref[:, sl]
        win = x
        for r in range(POOL_BUF - (w - 1), POOL_BUF):
            win = win + buf_ref[r, :, sl]
        d = win / float(w) - x
        o_ref[:, sl] = _dot(d, w_ref[g]) * sc_ref[:, sl]


def pool_sample(u, buf_t, w_grp, scale):
    db = u.shape[0]
    return pl.pallas_call(
        _pool_sample_kernel,
        grid=(1,),
        in_specs=[pl.BlockSpec((db, MAIN_W), lambda i: (0, 0)),
                  pl.BlockSpec((POOL_BUF, db, MAIN_W), lambda i: (0, 0, 0)),
                  pl.BlockSpec((4, POOL_GC, POOL_GC), lambda i: (0, 0, 0)),
                  pl.BlockSpec((1, MAIN_W), lambda i: (0, 0))],
        out_specs=pl.BlockSpec((db, MAIN_W), lambda i: (0, 0)),
        out_shape=jax.ShapeDtypeStruct((db, MAIN_W), F32),
        compiler_params=_cparams("arbitrary"),
        name="pool_sample",
    )(u, buf_t, w_grp, scale.reshape(1, MAIN_W))


def _mem_attend_kernel(q_ref, kv_ref, o_ref, *, tq):
    rows = max(tq, 8)
    for h in range(N_MEM_HEADS):
        sl = slice(h * MEM_HEAD_DIM, (h + 1) * MEM_HEAD_DIM)
        q = q_ref[0, :, sl]
        if tq < rows:
            q = jnp.broadcast_to(q[0:1], (rows, MEM_HEAD_DIM))
        k = kv_ref[:, sl]
        v = kv_ref[:, MEM_W + h * MEM_HEAD_DIM:MEM_W + (h + 1) * MEM_HEAD_DIM]
        s = _dot_t(q, k) * MEM_SCALE
        e = jnp.exp(s - jnp.max(s, axis=-1, keepdims=True))
        o = _dot(e, v) / jnp.sum(e, axis=-1, keepdims=True)
        o_ref[0, :, sl] = o[0:tq]


def mem_attend(u3, mkv2, tq):
    b, s, _ = u3.shape
    return pl.pallas_call(
        functools.partial(_mem_attend_kernel, tq=tq),
        grid=(b, s // tq),
        in_specs=[pl.BlockSpec((1, tq, MEM_W), lambda bi, i: (bi, i, MAIN_W // MEM_W)),
                  pl.BlockSpec((MEM_TOKENS, 2 * MEM_W), lambda bi, i: (bi, 0))],
        out_specs=pl.BlockSpec((1, tq, MEM_W), lambda bi, i: (bi, i, 0)),
        out_shape=jax.ShapeDtypeStruct((b, s, MEM_W), F32),
        compiler_params=_cparams("parallel", "arbitrary"),
        name="mem_attend",
    )(u3, mkv2)


def _out_proj_kernel(z_ref, mo_ref, w_ref, x_ref, o_ref, lhs_ref, wb_ref):
    i = pl.program_id(0)
    j = pl.program_id(1)

    @pl.when(j == 0)
    def _():
        lhs_ref[:, :MAIN_W] = z_ref[...].astype(lhs_ref.dtype)
        lhs_ref[:, MAIN_W:] = mo_ref[...].astype(lhs_ref.dtype)

    @pl.when(i == 0)
    def _():
        wb_ref[j] = w_ref[...].astype(BF16)

    o_ref[...] = x_ref[...] + _dot(lhs_ref[...], wb_ref[j])


def out_proj(z, mo, w, layer, x, tm, tn):
    m = x.shape[0]
    k = MAIN_W + MEM_W
    nj = D_MODEL // tn
    col = _once_per_column(nj)
    return pl.pallas_call(
        _out_proj_kernel,
        grid=(m // tm, nj),
        in_specs=[pl.BlockSpec((tm, MAIN_W), lambda i, j: (i, 0)),
                  pl.BlockSpec((tm, MEM_W), lambda i, j: (i, 0)),
                  pl.BlockSpec((None, k, tn), lambda i, j: (layer, 0, col(i, j))),
                  pl.BlockSpec((tm, tn), lambda i, j: (i, j))],
        out_specs=pl.BlockSpec((tm, tn), lambda i, j: (i, j)),
        out_shape=jax.ShapeDtypeStruct((m, D_MODEL), F32),
        scratch_shapes=[pltpu.VMEM((tm, k), _lhs_dtype(tm)), pltpu.VMEM((nj, k, tn), BF16)],
        compiler_params=_cparams("arbitrary", "arbitrary"),
        name="out_proj",
    )(z, mo, w, x)


def _ffn_kernel(x_ref, g_ref, wa_ref, wb_ref, wd_ref, gf_ref, o_ref, xn_ref, *, final):
    j = pl.program_id(1)

    @pl.when(j == 0)
    def _():
        x = x_ref[...]
        xn_ref[...] = _rms(x, g_ref[...]).astype(xn_ref.dtype)
        o_ref[...] = x

    xn = xn_ref[...]
    a = _dot(xn, wa_ref[...])
    b = _dot(xn, wb_ref[...])
    o_ref[...] += _dot(jax.nn.silu(a) * b, wd_ref[...])

    if final:
        @pl.when(j == pl.num_programs(1) - 1)
        def _():
            o_ref[...] = _rms(o_ref[...], gf_ref[...])


def ffn(x, g, w_gu, w_down, layer, g_final, final, tm, tf):
    m, k = x.shape
    nf = D_FF // tf
    return pl.pallas_call(
        functools.partial(_ffn_kernel, final=final),
        grid=(m // tm, nf),
        in_specs=[pl.BlockSpec((tm, k), lambda i, j: (i, 0), pipeline_mode=pl.Buffered(1)),
                  pl.BlockSpec((1, k), lambda i, j: (0, 0)),
                  pl.BlockSpec((None, k, tf), lambda i, j: (layer, 0, j)),
                  pl.BlockSpec((None, k, tf), lambda i, j: (layer, 0, nf + j)),
                  pl.BlockSpec((None, tf, k), lambda i, j: (layer, j, 0)),
                  pl.BlockSpec((1, k), lambda i, j: (0, 0))],
        out_specs=pl.BlockSpec((tm, k), lambda i, j: (i, 0)),
        out_shape=jax.ShapeDtypeStruct((m, k), F32),
        scratch_shapes=[pltpu.VMEM((tm, k), _lhs_dtype(tm))],
        compiler_params=_cparams("parallel", "arbitrary"),
        name="ffn",
    )(x, g.reshape(1, k), w_gu, w_gu, w_down, g_final.reshape(1, k))


def _compress_kernel(tbl_ref, *refs, n_steps):
    pages = refs[:PAGES_PER_STEP]
    nxt_ref, extra_ref, w1_ref, pe_ref, w2_ref, o_ref = refs[PAGES_PER_STEP:]
    p = pl.program_id(1)
    on_extra = p == n_steps
    nxt_extra = p == n_steps - 1
    nb = CMP_STEP * KV_SLOTS
    width = CMP_STRIDE * HEAD_DIM

    def chunk(ref, n):
        return jnp.concatenate([ref[0, n * CMP_STRIDE + r] for r in range(CMP_STRIDE)], axis=1)

    blocks = []
    for k in range(PAGES_PER_STEP):
        for n in range(CH_PER_PAGE):
            x = chunk(pages[k], n)
            blocks.append(jnp.where(on_extra, chunk(extra_ref, n) if k == 0 else 0.0, x))
    blocks.append(jnp.where(on_extra, 0.0, jnp.where(nxt_extra, chunk(extra_ref, 0), chunk(nxt_ref, 0))))
    blocks += [pe_ref[0, 0], pe_ref[0, 1], pe_ref[1, 0], pe_ref[1, 1], jnp.zeros((4, width), F32)]
    lhs = jnp.concatenate(blocks, axis=0)
    w1 = jnp.concatenate([w1_ref[0, 0], w1_ref[0, 1], w1_ref[1, 0], w1_ref[1, 1]], axis=1)
    hc = _dot(lhs, w1)
    is_v = (lax.broadcasted_iota(jnp.int32, (nb + KV_SLOTS, 1), 0) & N_KV) != 0
    d = HEAD_DIM
    hsel = jnp.where(is_v, hc[:nb + KV_SLOTS, 2 * d:], hc[:nb + KV_SLOTS, :2 * d])
    pe_k = hc[nb + 8:nb + 9, 0:d] + hc[nb + 9:nb + 10, d:2 * d]
    pe_v = hc[nb + 10:nb + 11, 2 * d:3 * d] + hc[nb + 11:nb + 12, 3 * d:]
    hid = jax.nn.gelu(hsel[:nb, :d] + hsel[KV_SLOTS:, d:] + jnp.where(is_v[:nb], pe_v, pe_k))
    o2 = _dot(hid, jnp.concatenate([w2_ref[0], w2_ref[1]], axis=1))
    out = jnp.where(is_v[:nb], o2[:, d:], o2[:, :d])
    o_ref[0] = out.reshape(CMP_STEP, KV_SLOTS, d)


def compress(pages, table, extra, cmp_pe, cmp_w1, cmp_w2):
    nb, n_pages = table.shape
    n_steps = n_pages // PAGES_PER_STEP
    w1 = cmp_w1.reshape(2, 2, CMP_STRIDE * HEAD_DIM, HEAD_DIM)
    pe = cmp_pe.reshape(2, 2, 1, CMP_STRIDE * HEAD_DIM)

    def page_map(k):
        return lambda b, p, tbl: (tbl[b, jnp.minimum(p * PAGES_PER_STEP + k, n_pages - 1)], 0, 0, 0)

    in_specs = [pl.BlockSpec((1, PAGE_SIZE, KV_SLOTS, HEAD_DIM), page_map(k)) for k in range(PAGES_PER_STEP)]
    in_specs += [pl.BlockSpec((1, CMP_STRIDE, KV_SLOTS, HEAD_DIM), page_map(PAGES_PER_STEP)),
                 pl.BlockSpec((1, PAGE_SIZE, KV_SLOTS, HEAD_DIM), lambda b, p, tbl: (b, 0, 0, 0)),
                 pl.BlockSpec(w1.shape, lambda b, p, tbl: (0, 0, 0, 0)),
                 pl.BlockSpec(pe.shape, lambda b, p, tbl: (0, 0, 0, 0)),
                 pl.BlockSpec(cmp_w2.shape, lambda b, p, tbl: (0, 0, 0))]
    n_out = CMP_STEP * (n_steps + 1)
    out = pl.pallas_call(
        functools.partial(_compress_kernel, n_steps=n_steps),
        grid_spec=pltpu.PrefetchScalarGridSpec(
            num_scalar_prefetch=1,
            grid=(nb, n_steps + 1),
            in_specs=in_specs,
            out_specs=pl.BlockSpec((1, CMP_STEP, KV_SLOTS, HEAD_DIM), lambda b, p, tbl: (b, p, 0, 0)),
        ),
        out_shape=jax.ShapeDtypeStruct((nb, n_out, KV_SLOTS, HEAD_DIM), F32),
        compiler_params=_cparams("parallel", "arbitrary"),
        name="compress",
    )(table, *([pages] * (PAGES_PER_STEP + 1)), extra, w1, pe, cmp_w2)
    return jnp.transpose(out, (0, 2, 1, 3)).reshape(nb, 2, N_KV, n_out, HEAD_DIM)


def _cover(n_pad, j_pad, n_cmp):
    n = lax.broadcasted_iota(jnp.int32, (n_pad, j_pad), 0)
    j = lax.broadcasted_iota(jnp.int32, (n_pad, j_pad), 1)
    hit = (n * CMP_STRIDE < j * SLC_BLOCK + SLC_BLOCK) & (n * CMP_STRIDE + CMP_BLOCK - 1 >= j * SLC_BLOCK)
    return (hit & (n < n_cmp)).astype(F32)


def _softmax_rows(s):
    e = jnp.exp(s - jnp.max(s, axis=-1, keepdims=True))
    return e / jnp.sum(e, axis=-1, keepdims=True)


def _nsa_prompt_kernel(q_ref, gl_ref, bg_ref, cos_ref, sin_ref, ckv_ref, ks_ref, vs_ref, kw_ref, vw_ref,
                       o_ref, *, n_cmp, n_slc, kt):
    qb = pl.program_id(1)
    tq = Q_BLOCK
    cos2 = cos_ref[...]
    sin2 = sin_ref[...]
    n_pad = ckv_ref.shape[3]
    qpos1 = qb * tq + lax.broadcasted_iota(jnp.int32, (tq, 1), 0)
    qpos = jnp.concatenate([qpos1] * GQA, axis=0)
    gates = jax.nn.sigmoid(gl_ref[...] + bg_ref[...])

    n_i = lax.broadcasted_iota(jnp.int32, (1, n_pad), 1)
    vis_c = (n_i * CMP_STRIDE + CMP_BLOCK - 1 <= qpos) & (n_i < n_cmp)
    any_c = (qpos >= CMP_BLOCK - 1).astype(F32)
    cover = _cover(n_pad, n_slc, n_cmp)
    j_i = lax.broadcasted_iota(jnp.int32, (1, n_slc), 1)
    qblk = jnp.right_shift(qpos1, SLC_SHIFT)
    forced = (j_i == 0) | (j_i == qblk) | (j_i == qblk - 1)
    valid = j_i <= qblk
    n_sel = min(N_SELECT, n_slc)

    w_lo = jnp.maximum(qb * tq - WINDOW, 0)
    w_lo = pl.multiple_of(w_lo, tq)
    kpos_w = w_lo + lax.broadcasted_iota(jnp.int32, (1, WINDOW + tq), 1)
    vis_w = (kpos_w <= qpos) & (qpos - kpos_w < WINDOW)

    for h in range(N_KV):
        q = jnp.concatenate(
            [_rope(q_ref[:, (h * GQA + g) * HEAD_DIM:(h * GQA + g + 1) * HEAD_DIM], cos2, sin2)
             for g in range(GQA)], axis=0)
        qs = (q * SCALE).astype(BF16)
        hs = slice(h * HEAD_DIM, (h + 1) * HEAD_DIM)

        s = jnp.where(vis_c, _dot_t(qs, ckv_ref[0, 0, h]), NEG)
        p = _softmax_rows(s) * any_c
        o_c = _dot(p, ckv_ref[0, 1, h])

        p_sum = p[0:tq] + p[tq:2 * tq] + p[2 * tq:3 * tq]
        score = jnp.dot(p_sum, cover, preferred_element_type=F32, precision=lax.Precision.HIGHEST)
        score = jnp.where(forced, jnp.inf, jnp.where(valid, score, -jnp.inf))
        rank = jnp.zeros((tq, n_slc), jnp.int32)
        for i in range(n_slc):
            col = score[:, i:i + 1]
            beats = (col > score) | ((col == score) & (i < j_i))
            rank = rank + beats.astype(jnp.int32)
        sel = (rank < n_sel).astype(BF16)

        def slc_step(t, carry):
            m, l, acc = carry
            k0 = pl.multiple_of(t * kt, kt)
            kpos = k0 + lax.broadcasted_iota(jnp.int32, (1, kt), 1)
            expand = (jnp.right_shift(kpos, SLC_SHIFT) == lax.broadcasted_iota(jnp.int32, (n_slc, kt), 0)).astype(BF16)
            chosen = jnp.dot(sel, expand, preferred_element_type=F32)
            chosen = jnp.concatenate([chosen] * GQA, axis=0)
            vis = (chosen > 0.5) & (kpos <= qpos)
            s = jnp.where(vis, _dot_t(qs, ks_ref[pl.ds(k0, kt), hs]), NEG)
            m_new = jnp.maximum(m, jnp.max(s, axis=-1, keepdims=True))
            a = jnp.exp(m - m_new)
            e = jnp.exp(s - m_new)
            l = l * a + jnp.sum(e, axis=-1, keepdims=True)
            acc = acc * a + _dot(e, vs_ref[pl.ds(k0, kt), hs])
            return m_new, l, acc

        n_t = (qb * tq + tq + kt - 1) // kt
        m0 = jnp.full((GQA * tq, 1), NEG, F32)
        l0 = jnp.zeros((GQA * tq, 1), F32)
        a0 = jnp.zeros((GQA * tq, HEAD_DIM), F32)
        _, l, acc = lax.fori_loop(0, n_t, slc_step, (m0, l0, a0))
        o_s = acc / l

        s = jnp.where(vis_w, _dot_t(qs, kw_ref[pl.ds(w_lo, WINDOW + tq), hs]), NEG)
        e = jnp.exp(s - jnp.max(s, axis=-1, keepdims=True))
        o_w = _dot(e, vw_ref[pl.ds(w_lo, WINDOW + tq), hs]) / jnp.sum(e, axis=-1, keepdims=True)

        for g in range(GQA):
            hd = h * GQA + g
            r = slice(g * tq, (g + 1) * tq)
            gc = gates[:, hd * N_BRANCH + 0:hd * N_BRANCH + 1]
            gs = gates[:, hd * N_BRANCH + 1:hd * N_BRANCH + 2]
            gw = gates[:, hd * N_BRANCH + 2:hd * N_BRANCH + 3]
            o_ref[:, hd * HEAD_DIM:(hd + 1) * HEAD_DIM] = gc * o_c[r] + gs * o_s[r] + gw * o_w[r]


def nsa_prompt(u, b_gate_pad, cos2, sin2, ckv, slc_k, slc_v, win_k, win_v, batch, seq):
    nqb = seq // Q_BLOCK
    n_pad = ckv.shape[3]
    kspec = pl.BlockSpec((seq, N_KV * HEAD_DIM), lambda b, i: (b, 0))
    return pl.pallas_call(
        functools.partial(_nsa_prompt_kernel, n_cmp=seq // CMP_STRIDE - 1, n_slc=seq // SLC_BLOCK, kt=512),
        grid=(batch, nqb),
        in_specs=[pl.BlockSpec((Q_BLOCK, MAIN_W), lambda b, i: (b * nqb + i, 0)),
                  pl.BlockSpec((Q_BLOCK, GATE_PAD), lambda b, i: (b * nqb + i, (MAIN_W + MEM_W) // GATE_PAD)),
                  pl.BlockSpec((1, GATE_PAD), lambda b, i: (0, 0)),
                  pl.BlockSpec((Q_BLOCK, HEAD_DIM), lambda b, i: (i, 0)),
                  pl.BlockSpec((Q_BLOCK, HEAD_DIM), lambda b, i: (i, 0)),
                  pl.BlockSpec((1, 2, N_KV, n_pad, HEAD_DIM), lambda b, i: (b, 0, 0, 0, 0)),
                  kspec, kspec, kspec, kspec],
        out_specs=pl.BlockSpec((Q_BLOCK, MAIN_W), lambda b, i: (b * nqb + i, 0)),
        out_shape=jax.ShapeDtypeStruct((batch * seq, MAIN_W), F32),
        compiler_params=_cparams("parallel", "arbitrary"),
        name="nsa_prompt",
    )(u, u, b_gate_pad, cos2, sin2, ckv, slc_k, slc_v, win_k, win_v)


HEAD_ROWS = 16


def _head_mask(h):
    r = lax.broadcasted_iota(jnp.int32, (HEAD_ROWS, 1), 0)
    return (r >= h * GQA) & (r < (h + 1) * GQA)


def _nsa_sample_a_kernel(q_ref, cos_ref, sin_ref, ckv_ref, wkv_ref, wnew_ref, qr_ref, oc_ref, ow_ref, sel_ref,
                         *, n_cmp, n_slc, j_pad):
    n_pad = ckv_ref.shape[3]
    q = _rope(q_ref[0], cos_ref[...], sin_ref[...])
    qr_ref[0] = q
    qs = q * SCALE
    n_i = lax.broadcasted_iota(jnp.int32, (1, n_pad), 1)
    vis_c = (n_i * CMP_STRIDE + CMP_BLOCK - 1 <= PAST_LEN) & (n_i < n_cmp)
    cover = _cover(n_pad, j_pad, n_cmp)
    j_row = lax.broadcasted_iota(jnp.int32, (1, j_pad), 1)
    j_col = lax.broadcasted_iota(jnp.int32, (j_pad, 1), 0)
    qblk = PAST_LEN // SLC_BLOCK
    lane = lax.broadcasted_iota(jnp.int32, (N_SELECT, HEAD_DIM), 1)
    r_col = lax.broadcasted_iota(jnp.int32, (N_SELECT, 1), 0)
    n_win = wkv_ref.shape[1]
    kpos_w = PAST_LEN - n_win + lax.broadcasted_iota(jnp.int32, (1, n_win), 1)
    vis_w = (kpos_w <= PAST_LEN) & (PAST_LEN - kpos_w < WINDOW)

    o_c = jnp.zeros((HEAD_ROWS, HEAD_DIM), F32)
    o_w = jnp.zeros((HEAD_ROWS, HEAD_DIM), F32)
    sel_out = jnp.zeros((N_SELECT, HEAD_DIM), jnp.int32)
    for h in range(N_KV):
        mine = _head_mask(h)
        hs = slice(h * HEAD_DIM, (h + 1) * HEAD_DIM)
        vsl = slice((N_KV + h) * HEAD_DIM, (N_KV + h + 1) * HEAD_DIM)

        s = jnp.where(vis_c, _dot_t(qs, ckv_ref[0, 0, h]), NEG)
        p = _softmax_rows(s) * float(PAST_LEN >= CMP_BLOCK - 1)
        o_c = jnp.where(mine, _dot(p, ckv_ref[0, 1, h]), o_c)

        p_sum = jnp.sum(jnp.where(mine, p, 0.0), axis=0, keepdims=True)
        score = jnp.dot(jnp.broadcast_to(p_sum, (8, n_pad)), cover, preferred_element_type=F32,
                        precision=lax.Precision.HIGHEST)
        forced = (j_row == 0) | (j_row == qblk) | (j_row == qblk - 1)
        score = jnp.where(forced, jnp.inf, jnp.where(j_row <= qblk, score, -jnp.inf))
        score = jnp.where(j_row < n_slc, score, -jnp.inf)
        s_col = jnp.transpose(score)[:, 0:1]
        beats = ((s_col > score[0:1]) | ((s_col == score[0:1]) & (j_col < j_row))) & (j_col < n_slc)
        rank = jnp.sum(beats.astype(F32), axis=0, keepdims=True)
        hit = (rank == r_col.astype(F32)) & (j_row < n_slc)
        idx = jnp.sum(jnp.where(hit, j_row.astype(F32), 0.0), axis=1, keepdims=True)
        sel_out = jnp.where(lane == h, idx.astype(jnp.int32), sel_out)

        s = jnp.where(vis_w, _dot_t(qs, wkv_ref[0, :, hs]), NEG)
        s_new = jnp.sum(qs * wnew_ref[0, :, hs], axis=-1, keepdims=True)
        m = jnp.maximum(jnp.max(s, axis=-1, keepdims=True), s_new)
        e = jnp.exp(s - m)
        e_new = jnp.exp(s_new - m)
        o = (_dot(e, wkv_ref[0, :, vsl]) + e_new * wnew_ref[0, :, vsl]) / (jnp.sum(e, axis=-1, keepdims=True) + e_new)
        o_w = jnp.where(mine, o, o_w)
    oc_ref[0] = o_c
    ow_ref[0] = o_w
    sel_ref[0] = sel_out


def nsa_sample_a(q16, cos2, sin2, ckv, win_cache, win_new, n_cmp, n_slc):
    db = q16.shape[0]
    n_pad = ckv.shape[3]
    n_win = win_cache.shape[1]
    j_pad = -(-n_slc // 128) * 128
    hspec = pl.BlockSpec((1, HEAD_ROWS, HEAD_DIM), lambda b: (b, 0, 0))
    hout = jax.ShapeDtypeStruct((db, HEAD_ROWS, HEAD_DIM), F32)
    return pl.pallas_call(
        functools.partial(_nsa_sample_a_kernel, n_cmp=n_cmp, n_slc=n_slc, j_pad=j_pad),
        grid=(db,),
        in_specs=[hspec,
                  pl.BlockSpec((1, HEAD_DIM), lambda b: (0, 0)),
                  pl.BlockSpec((1, HEAD_DIM), lambda b: (0, 0)),
                  pl.BlockSpec((1, 2, N_KV, n_pad, HEAD_DIM), lambda b: (b, 0, 0, 0, 0)),
                  pl.BlockSpec((1, n_win, KV_W), lambda b: (b, 0, 0)),
                  pl.BlockSpec((1, 1, KV_W), lambda b: (b, 0, 0))],
        out_specs=[hspec, hspec, hspec, pl.BlockSpec((1, N_SELECT, HEAD_DIM), lambda b: (b, 0, 0))],
        out_shape=[hout, hout, hout, jax.ShapeDtypeStruct((db, N_SELECT, HEAD_DIM), jnp.int32)],
        compiler_params=_cparams("arbitrary"),
        name="nsa_sample_a",
    )(q16, cos2, sin2, ckv, win_cache, win_new)


def _nsa_sample_b_kernel(sel_ref, tbl_ref, q_ref, *refs):
    blocks = refs[:N_SELECT]
    new_ref, oc_ref, ow_ref, gl_ref, bg_ref, o_ref, os_ref = refs[N_SELECT:]
    b = pl.program_id(0)
    h = pl.program_id(1)
    n_keys = N_SELECT * SLC_BLOCK
    lane_blk = jnp.right_shift(lax.broadcasted_iota(jnp.int32, (1, n_keys), 1), SLC_SHIFT)
    new_k = new_ref[0, pl.ds(h, 1), :]
    new_v = new_ref[0, pl.ds(h + N_KV, 1), :]
    ks, vs = [], []
    blk_of = jnp.zeros((1, n_keys), jnp.int32)
    for r in range(N_SELECT):
        blk = sel_ref[b, h, r]
        in_past = blk * SLC_BLOCK < PAST_LEN
        ks.append(jnp.where(in_past, blocks[r][0, pl.ds(h, SLC_BLOCK, stride=KV_SLOTS), :], new_k))
        vs.append(jnp.where(in_past, blocks[r][0, pl.ds(h + N_KV, SLC_BLOCK, stride=KV_SLOTS), :], new_v))
        blk_of = jnp.where(lane_blk == r, blk, blk_of)
    kpos = blk_of * SLC_BLOCK + (lax.broadcasted_iota(jnp.int32, (1, n_keys), 1) & (SLC_BLOCK - 1))
    s = jnp.where(kpos <= PAST_LEN, _dot_t(q_ref[0] * SCALE, jnp.concatenate(ks, axis=0)), NEG)
    e = jnp.exp(s - jnp.max(s, axis=-1, keepdims=True))
    o_s = _dot(e, jnp.concatenate(vs, axis=0)) / jnp.sum(e, axis=-1, keepdims=True)

    @pl.when(h == 0)
    def _():
        os_ref[...] = jnp.zeros(os_ref.shape, F32)

    row = lax.broadcasted_iota(jnp.int32, (HEAD_ROWS, 1), 0)
    mine = (row >= h * GQA) & (row < (h + 1) * GQA)
    os_ref[...] = jnp.where(mine, o_s, os_ref[...])

    @pl.when(h == pl.num_programs(1) - 1)
    def _():
        gates = jax.nn.sigmoid(gl_ref[0] + bg_ref[...])
        o_ref[0] = gates[:, 0:1] * oc_ref[0] + gates[:, 1:2] * os_ref[...] + gates[:, 2:3] * ow_ref[0]


def nsa_sample_b(sel, page_table, q_rope, slc_halves, slc_new, o_c, o_w, gate_logits, b_gate16):
    db = q_rope.shape[0]
    last_blk = PAST_LEN // SLC_BLOCK - 1
    halves = PAGE_SIZE // SLC_BLOCK

    def cache_map(r):
        def index(b, h, sel_ref, tbl_ref):
            j = jnp.minimum(sel_ref[b, h, r], last_blk)
            return (tbl_ref[b, j // halves] * halves + j % halves, 0, 0)
        return index

    hspec = pl.BlockSpec((1, HEAD_ROWS, HEAD_DIM), lambda b, h, s, t: (b, 0, 0))
    return pl.pallas_call(
        _nsa_sample_b_kernel,
        grid_spec=pltpu.PrefetchScalarGridSpec(
            num_scalar_prefetch=2,
            grid=(db, N_KV),
            in_specs=[hspec]
            + [pl.BlockSpec((1, SLC_BLOCK * KV_SLOTS, HEAD_DIM), cache_map(r)) for r in range(N_SELECT)]
            + [pl.BlockSpec((1, KV_SLOTS, HEAD_DIM), lambda b, h, s, t: (b, 0, 0)),
               hspec, hspec, hspec,
               pl.BlockSpec((HEAD_ROWS, HEAD_DIM), lambda b, h, s, t: (0, 0))],
            out_specs=hspec,
            scratch_shapes=[pltpu.VMEM((HEAD_ROWS, HEAD_DIM), F32)],
        ),
        out_shape=jax.ShapeDtypeStruct((db, HEAD_ROWS, HEAD_DIM), F32),
        compiler_params=_cparams("arbitrary", "arbitrary"),
        name="nsa_sample_b",
    )(sel, page_table, q_rope, *([slc_halves] * N_SELECT), slc_new, o_c, o_w, gate_logits, b_gate16)


def _rope_tables(pos):
    half = HEAD_DIM // 2
    inv = ROPE_THETA ** (-jnp.arange(half, dtype=F32) / half)
    ang = pos.astype(F32)[:, None] * inv[None, :]
    cos, sin = jnp.cos(ang), jnp.sin(ang)
    return jnp.concatenate([cos, cos], -1), jnp.concatenate([-sin, sin], -1)


def _pad_w_in_b(w):
    return jnp.concatenate([w[..., :MAIN_W], w[..., MAIN_W + GATE_W:],
                            jnp.pad(w[..., MAIN_W:MAIN_W + GATE_W], ((0, 0), (0, 0), (0, GATE_PAD - GATE_W)))], axis=-1)


def _prompt_trunk(x_prompt, mem_prompt, g_mix, w_in_a, pool_grp_w, pool_scale, w_in_b_pad, b_gate, g_kv, w_kv,
                  cmp_pe, cmp_w1, cmp_w2, g_mem, w_mem_kv, w_out, g_ffn, w_gu, w_down, g_final):
    batch, seq, _ = x_prompt.shape
    m = batch * seq
    tm = 512
    x = x_prompt.reshape(m, D_MODEL)
    mem2 = mem_prompt.reshape(batch * MEM_TOKENS, D_MODEL)
    mem_kv = [rms_matmul(mem2, g_mem[l], w_mem_kv, l, 512, 512) for l in range(DEPTH)]
    cos2, sin2 = _rope_tables(jnp.arange(seq, dtype=jnp.int32))
    pools = []
    nsa = None
    for l in range(DEPTH):
        if l == N_A:
            cmp_kv, slc_kv, win_kv, slc_k, slc_v, win_k, win_v = kv_proj(x, g_kv, w_kv, cos2, sin2, tm)
            table = jnp.arange(batch * seq // PAGE_SIZE, dtype=jnp.int32).reshape(batch, seq // PAGE_SIZE)
            ckv = compress(cmp_kv.reshape(-1, PAGE_SIZE, KV_SLOTS, HEAD_DIM), table,
                           jnp.zeros((batch, PAGE_SIZE, KV_SLOTS, HEAD_DIM), F32), cmp_pe, cmp_w1, cmp_w2)
            nsa = (ckv, slc_k, slc_v, win_k, win_v)
        if l < N_A:
            u = rms_matmul(x, g_mix[l], w_in_a, l, tm, 512)
            u3 = u.reshape(batch, seq, -1)
            z = pool_prompt(u3, pool_grp_w[l], pool_scale[l]).reshape(m, MAIN_W)
            pools.append(u3[:, seq - POOL_BUF:, :MAIN_W])
        else:
            j = l - N_A
            u = rms_matmul(x, g_mix[l], w_in_b_pad, j, tm, 768)
            u3 = u.reshape(batch, seq, -1)
            bg = jnp.pad(b_gate[j], (0, GATE_PAD - GATE_W)).reshape(1, GATE_PAD)
            z = nsa_prompt(u, bg, cos2, sin2, *nsa, batch, seq)
        mo = mem_attend(u3, mem_kv[l], 512).reshape(m, MEM_W)
        x = out_proj(z, mo, w_out, l, x, tm, 512)
        x = ffn(x, g_ffn[l], w_gu, w_down, l, g_final, l == DEPTH - 1, FFN_TM, FFN_TF)
    kv5 = (batch, seq, 2, N_KV, HEAD_DIM)
    n_keep = min(WINDOW, seq)
    return (x.reshape(batch, seq, D_MODEL), jnp.stack(pools), cmp_kv.reshape(kv5), slc_kv.reshape(kv5),
            win_kv.reshape(kv5)[:, seq - n_keep:],
            jnp.stack(mem_kv).reshape(DEPTH, batch, MEM_TOKENS, 2, N_MEM_HEADS, MEM_HEAD_DIM))


def _sample_trunk(x_sample, state_pool, cache_cmp_kv, cache_slc_kv, cache_win_kv, cache_mem_kv, page_table,
                  g_mix, w_in_a, pool_grp_w, pool_scale, w_in_b_pad, b_gate, g_kv, w_kv,
                  cmp_pe, cmp_w1, cmp_w2, w_out, g_ffn, w_gu, w_down, g_final):
    db = x_sample.shape[0]
    x = x_sample.reshape(db, D_MODEL)
    cos2, sin2 = _rope_tables(jnp.full((db,), PAST_LEN, jnp.int32))
    n_phys = cache_cmp_kv.shape[0]
    pools = []
    for l in range(DEPTH):
        mkv = cache_mem_kv[l].reshape(db * MEM_TOKENS, 2 * MEM_W)
        if l == N_A:
            cmp_new, slc_new, win_new = kv_proj(x, g_kv, w_kv, cos2, sin2, db)[:N_BRANCH]
            extra = jnp.pad(cmp_new.reshape(db, 1, KV_SLOTS, HEAD_DIM), ((0, 0), (0, PAGE_SIZE - 1), (0, 0), (0, 0)))
            ckv = compress(cache_cmp_kv.reshape(n_phys, PAGE_SIZE, KV_SLOTS, HEAD_DIM), page_table, extra,
                           cmp_pe, cmp_w1, cmp_w2)
            t_full = -(-(PAST_LEN + 1) // SLC_BLOCK) * SLC_BLOCK
            n_cmp = t_full // CMP_STRIDE - 1
            n_slc = t_full // SLC_BLOCK
        if l < N_A:
            u = rms_matmul(x, g_mix[l], w_in_a, l, db, 512)
            z = pool_sample(u, jnp.transpose(state_pool[l], (1, 0, 2)), pool_grp_w[l], pool_scale[l])
            pools.append(jnp.concatenate([state_pool[l][:, 1:], u[:, None, :MAIN_W]], axis=1))
        else:
            j = l - N_A
            u = rms_matmul(x, g_mix[l], w_in_b_pad, j, db, 768)
            q16 = jnp.pad(u[:, :MAIN_W].reshape(db, N_HEADS, HEAD_DIM), ((0, 0), (0, HEAD_ROWS - N_HEADS), (0, 0)))
            q_rope, o_c, o_w, sel = nsa_sample_a(
                q16, cos2[:1], sin2[:1], ckv, cache_win_kv.reshape(db, -1, KV_W), win_new.reshape(db, 1, KV_W),
                n_cmp, n_slc)
            sel = jnp.transpose(sel[:, :, :N_KV], (0, 2, 1))
            gl = u[:, MAIN_W + MEM_W:MAIN_W + MEM_W + GATE_W].reshape(db, N_HEADS, N_BRANCH)
            gl = jnp.pad(gl, ((0, 0), (0, HEAD_ROWS - N_HEADS), (0, HEAD_DIM - N_BRANCH)))
            bg = jnp.pad(b_gate[j].reshape(N_HEADS, N_BRANCH), ((0, HEAD_ROWS - N_HEADS), (0, HEAD_DIM - N_BRANCH)))
            z16 = nsa_sample_b(sel, page_table, q_rope,
                               cache_slc_kv.reshape(n_phys * 2, SLC_BLOCK * KV_SLOTS, HEAD_DIM),
                               slc_new.reshape(db, KV_SLOTS, HEAD_DIM), o_c, o_w, gl, bg)
            z = z16[:, :N_HEADS].reshape(db, MAIN_W)
        mo = mem_attend(u.reshape(db, 1, -1), mkv, 1).reshape(db, MEM_W)
        x = out_proj(z, mo, w_out, l, x, db, 512)
        x = ffn(x, g_ffn[l], w_gu, w_down, l, g_final, l == DEPTH - 1, db, 512)
    kv5 = (db, 1, 2, N_KV, HEAD_DIM)
    win_s = jnp.concatenate([cache_win_kv, win_new.reshape(kv5)], axis=1)
    win_s = win_s[:, win_s.shape[1] - min(WINDOW, win_s.shape[1]):]
    return (x.reshape(db, 1, D_MODEL), jnp.stack(pools), cmp_new.reshape(kv5), slc_new.reshape(kv5), win_s)


def kernel(x_prompt, x_sample, mem_prompt, state_pool, cache_cmp_kv, cache_slc_kv, cache_win_kv, cache_mem_kv, page_table, g_mix, w_in_a, pool_grp_w, pool_scale, w_in_b, b_gate, g_kv, w_kv, cmp_pe, cmp_w1, cmp_w2, g_mem, w_mem_kv, w_out, g_ffn, w_gu, w_down, g_final):
    w_in_b_pad = _pad_w_in_b(w_in_b)
    y_p, pool_p, cmp_p, slc_p, win_p, mem_kv_p = _prompt_trunk(
        x_prompt, mem_prompt, g_mix, w_in_a, pool_grp_w, pool_scale, w_in_b_pad, b_gate, g_kv, w_kv,
        cmp_pe, cmp_w1, cmp_w2, g_mem, w_mem_kv, w_out, g_ffn, w_gu, w_down, g_final)
    y_s, pool_s, cmp_s, slc_s, win_s = _sample_trunk(
        x_sample, state_pool, cache_cmp_kv, cache_slc_kv, cache_win_kv, cache_mem_kv, page_table,
        g_mix, w_in_a, pool_grp_w, pool_scale, w_in_b_pad, b_gate, g_kv, w_kv,
        cmp_pe, cmp_w1, cmp_w2, w_out, g_ffn, w_gu, w_down, g_final)
    return (y_p, y_s, pool_p, cmp_p, slc_p, win_p, mem_kv_p, pool_s, cmp_s, slc_s, win_s)
```

```python
import functools

import jax
import jax.numpy as jnp
from jax import lax
from jax.experimental import pallas as pl
from jax.experimental.pallas import tpu as pltpu

F32 = jnp.float32
BF16 = jnp.bfloat16

D_MODEL = 2048
DEPTH = 4
N_A = 2
PAST_LEN = 16384
PAGE_SIZE = 128
MEM_TOKENS = 256
N_MEM_HEADS = 4
MEM_HEAD_DIM = 128
MEM_W = 512
MAIN_W = 1536
POOL_WINDOWS = (2, 4, 8, 16)
POOL_GC = 384
POOL_BUF = 15
HEAD_DIM = 128
N_HEADS = 12
N_KV = 4
GQA = 3
N_BRANCH = 3
GATE_W = 36
GATE_PAD = 256
CMP_BLOCK = 32
CMP_STRIDE = 16
SLC_BLOCK = 64
SLC_SHIFT = 6
N_SELECT = 16
WINDOW = 512
Q_BLOCK = 128
D_FF = 5632
ROPE_THETA = 10000.0
EPS = 1e-6
NEG = -1e30
SCALE = HEAD_DIM ** -0.5
MEM_SCALE = MEM_HEAD_DIM ** -0.5
KV_SLOTS = 2 * N_KV
KV_W = KV_SLOTS * HEAD_DIM
W_IN_B = MAIN_W + MEM_W + GATE_PAD

VMEM_LIMIT = 56 * 1024 * 1024
PROJ_TM = 1024
FFN_TM = 1024
FFN_TF = 256
PAGES_PER_STEP = 8
CH_PER_PAGE = PAGE_SIZE // CMP_STRIDE
CMP_STEP = PAGES_PER_STEP * CH_PER_PAGE


def _cparams(*sem):
    return pltpu.CompilerParams(dimension_semantics=sem, vmem_limit_bytes=VMEM_LIMIT)


def _rms(x, g):
    return x * lax.rsqrt(jnp.mean(x * x, axis=-1, keepdims=True) + EPS) * g


def _dot(a, b):
    return jnp.dot(a.astype(BF16), b.astype(BF16), preferred_element_type=F32)


def _dot_t(a, b):
    return lax.dot_general(a.astype(BF16), b.astype(BF16), (((1,), (1,)), ((), ())),
                           preferred_element_type=F32)


def _lhs_dtype(rows):
    return BF16 if rows % 16 == 0 else F32


def _rope(x, cos2, sin2):
    return x * cos2 + pltpu.roll(x, HEAD_DIM // 2, axis=1) * sin2


def _rms_matmul_kernel(x_ref, g_ref, w_ref, o_ref, xn_ref, wb_ref):
    i = pl.program_id(0)
    j = pl.program_id(1)

    @pl.when(j == 0)
    def _():
        xn_ref[...] = _rms(x_ref[...], g_ref[...]).astype(xn_ref.dtype)

    @pl.when(i == 0)
    def _():
        wb_ref[j] = w_ref[...].astype(BF16)

    o_ref[...] = _dot(xn_ref[...], wb_ref[j])


def _once_per_column(nj):
    return lambda i, j: jnp.where(i == 0, j, nj - 1)


def rms_matmul(x, g, w, layer, tm, tn):
    m, k = x.shape
    n = w.shape[2]
    nj = n // tn
    col = _once_per_column(nj)
    return pl.pallas_call(
        _rms_matmul_kernel,
        grid=(m // tm, nj),
        in_specs=[pl.BlockSpec((tm, k), lambda i, j: (i, 0)),
                  pl.BlockSpec((1, k), lambda i, j: (0, 0)),
                  pl.BlockSpec((None, k, tn), lambda i, j: (layer, 0, col(i, j)))],
        out_specs=pl.BlockSpec((tm, tn), lambda i, j: (i, j)),
        out_shape=jax.ShapeDtypeStruct((m, n), F32),
        scratch_shapes=[pltpu.VMEM((tm, k), _lhs_dtype(tm)), pltpu.VMEM((nj, k, tn), BF16)],
        compiler_params=_cparams("arbitrary", "arbitrary"),
        name="rms_matmul",
    )(x, g.reshape(1, k), w)


def _kv_proj_kernel(x_ref, g_ref, w_ref, cos_ref, sin_ref, cmp_ref, slc_ref, win_ref,
                    slc_k_ref, slc_v_ref, win_k_ref, win_v_ref, xn_ref, wb_ref):
    i = pl.program_id(0)
    j = pl.program_id(1)
    half = N_KV * HEAD_DIM

    @pl.when(j == 0)
    def _():
        xn_ref[...] = _rms(x_ref[...], g_ref[...]).astype(xn_ref.dtype)

    @pl.when(i == 0)
    def _():
        wb_ref[j] = w_ref[...].astype(BF16)

    y = _dot(xn_ref[...], wb_ref[j])
    cos2 = cos_ref[...]
    sin2 = sin_ref[...]
    roped = jnp.concatenate(
        [_rope(y[:, h * HEAD_DIM:(h + 1) * HEAD_DIM], cos2, sin2) for h in range(N_KV)], axis=1)
    y = jnp.where(j % 2 == 0, roped, y)
    copies = ((slc_k_ref, slc_v_ref), (win_k_ref, win_v_ref))
    for br, ref in enumerate((cmp_ref, slc_ref, win_ref)):
        for c in range(2):
            @pl.when(j == 2 * br + c)
            def _(ref=ref, br=br, c=c):
                ref[:, c * half:(c + 1) * half] = y
                if br > 0:
                    copies[br - 1][c][...] = y.astype(BF16)


def kv_proj(x, g, w, cos2, sin2, tm):
    m, k = x.shape
    half = N_KV * HEAD_DIM
    nt = cos2.shape[0] // tm
    nj = 2 * N_BRANCH
    col = _once_per_column(nj)
    out = jax.ShapeDtypeStruct((m, KV_W), F32)
    out16 = jax.ShapeDtypeStruct((m, half), BF16)
    ospec = pl.BlockSpec((tm, KV_W), lambda i, j: (i, 0))
    ospec16 = pl.BlockSpec((tm, half), lambda i, j: (i, 0))
    return pl.pallas_call(
        _kv_proj_kernel,
        grid=(m // tm, nj),
        in_specs=[pl.BlockSpec((tm, k), lambda i, j: (i, 0)),
                  pl.BlockSpec((1, k), lambda i, j: (0, 0)),
                  pl.BlockSpec((k, half), lambda i, j: (0, col(i, j))),
                  pl.BlockSpec((tm, HEAD_DIM), lambda i, j: (i % nt, 0)),
                  pl.BlockSpec((tm, HEAD_DIM), lambda i, j: (i % nt, 0))],
        out_specs=[ospec, ospec, ospec, ospec16, ospec16, ospec16, ospec16],
        out_shape=[out, out, out, out16, out16, out16, out16],
        scratch_shapes=[pltpu.VMEM((tm, k), _lhs_dtype(tm)), pltpu.VMEM((nj, k, half), BF16)],
        compiler_params=_cparams("arbitrary", "arbitrary"),
        name="kv_proj",
    )(x, g.reshape(1, k), w, cos2, sin2)


def _pool_prompt_kernel(cur_ref, prev_ref, w_ref, sc_ref, o_ref, *, ts):
    i = pl.program_id(1)
    keep = (i > 0).astype(F32)
    pos = (i * ts + lax.broadcasted_iota(jnp.int32, (ts, 1), 0) + 1).astype(F32)
    for g, w in enumerate(POOL_WINDOWS):
        sl = slice(g * POOL_GC, (g + 1) * POOL_GC)
        x = cur_ref[0, :, sl]
        ext = jnp.concatenate([prev_ref[0, :, sl] * keep, x], axis=0)
        span = 1
        while span < w:
            ext = ext[span:] + ext[:-span]
            span *= 2
        win = ext[17 - w:17 - w + ts]
        d = win / jnp.minimum(pos, float(w)) - x
        o_ref[0, :, sl] = _dot(d, w_ref[g]) * sc_ref[:, sl]


def pool_prompt(u3, w_grp, scale, ts=256):
    b, s, n = u3.shape
    r = ts // 16
    return pl.pallas_call(
        functools.partial(_pool_prompt_kernel, ts=ts),
        grid=(b, s // ts),
        in_specs=[pl.BlockSpec((1, ts, MAIN_W), lambda bi, i: (bi, i, 0)),
                  pl.BlockSpec((1, 16, MAIN_W), lambda bi, i: (bi, jnp.maximum(i * r - 1, 0), 0)),
                  pl.BlockSpec((4, POOL_GC, POOL_GC), lambda bi, i: (0, 0, 0)),
                  pl.BlockSpec((1, MAIN_W), lambda bi, i: (0, 0))],
        out_specs=pl.BlockSpec((1, ts, MAIN_W), lambda bi, i: (bi, i, 0)),
        out_shape=jax.ShapeDtypeStruct((b, s, MAIN_W), F32),
        compiler_params=_cparams("parallel", "arbitrary"),
        name="pool_prompt",
    )(u3, u3, w_grp, scale.reshape(1, MAIN_W))


def _pool_sample_kernel(cur_ref, buf_ref, w_ref, sc_ref, o_ref):
    for g, w in enumerate(POOL_WINDOWS):
        sl = slice(g * POOL_GC, (g + 1) * POOL_GC)
        x = cur_ref[:, sl]
        win = x
        for r in range(POOL_BUF - (w - 1), POOL_BUF):
            win = win + buf_ref[r, :, sl]
        d = win / float(w) - x
        o_ref[:, sl] = _dot(d, w_ref[g]) * sc_ref[:, sl]


def pool_sample(u, buf_t, w_grp, scale):
    db = u.shape[0]
    return pl.pallas_call(
        _pool_sample_kernel,
        grid=(1,),
        in_specs=[pl.BlockSpec((db, MAIN_W), lambda i: (0, 0)),
                  pl.BlockSpec((POOL_BUF, db, MAIN_W), lambda i: (0, 0, 0)),
                  pl.BlockSpec((4, POOL_GC, POOL_GC), lambda i: (0, 0, 0)),
                  pl.BlockSpec((1, MAIN_W), lambda i: (0, 0))],
        out_specs=pl.BlockSpec((db, MAIN_W), lambda i: (0, 0)),
        out_shape=jax.ShapeDtypeStruct((db, MAIN_W), F32),
        compiler_params=_cparams("arbitrary"),
        name="pool_sample",
    )(u, buf_t, w_grp, scale.reshape(1, MAIN_W))


def _mem_attend_kernel(q_ref, kv_ref, o_ref, *, tq):
    rows = max(tq, 8)
    for h in range(N_MEM_HEADS):
        sl = slice(h * MEM_HEAD_DIM, (h + 1) * MEM_HEAD_DIM)
        q = q_ref[0, :, sl]
        if tq < rows:
            q = jnp.broadcast_to(q[0:1], (rows, MEM_HEAD_DIM))
        k = kv_ref[:, sl]
        v = kv_ref[:, MEM_W + h * MEM_HEAD_DIM:MEM_W + (h + 1) * MEM_HEAD_DIM]
        s = _dot_t(q, k) * MEM_SCALE
        e = jnp.exp(s - jnp.max(s, axis=-1, keepdims=True))
        o = _dot(e, v) / jnp.sum(e, axis=-1, keepdims=True)
        o_ref[0, :, sl] = o[0:tq]


def mem_attend(u3, mkv2, tq):
    b, s, _ = u3.shape
    return pl.pallas_call(
        functools.partial(_mem_attend_kernel, tq=tq),
        grid=(b, s // tq),
        in_specs=[pl.BlockSpec((1, tq, MEM_W), lambda bi, i: (bi, i, MAIN_W // MEM_W)),
                  pl.BlockSpec((MEM_TOKENS, 2 * MEM_W), lambda bi, i: (bi, 0))],
        out_specs=pl.BlockSpec((1, tq, MEM_W), lambda bi, i: (bi, i, 0)),
        out_shape=jax.ShapeDtypeStruct((b, s, MEM_W), F32),
        compiler_params=_cparams("parallel", "arbitrary"),
        name="mem_attend",
    )(u3, mkv2)


def _out_proj_kernel(z_ref, mo_ref, w_ref, x_ref, o_ref, lhs_ref, wb_ref):
    i = pl.program_id(0)
    j = pl.program_id(1)

    @pl.when(j == 0)
    def _():
        lhs_ref[:, :MAIN_W] = z_ref[...].astype(lhs_ref.dtype)
        lhs_ref[:, MAIN_W:] = mo_ref[...].astype(lhs_ref.dtype)

    @pl.when(i == 0)
    def _():
        wb_ref[j] = w_ref[...].astype(BF16)

    o_ref[...] = x_ref[...] + _dot(lhs_ref[...], wb_ref[j])


def out_proj(z, mo, w, layer, x, tm, tn):
    m = x.shape[0]
    k = MAIN_W + MEM_W
    nj = D_MODEL // tn
    col = _once_per_column(nj)
    return pl.pallas_call(
        _out_proj_kernel,
        grid=(m // tm, nj),
        in_specs=[pl.BlockSpec((tm, MAIN_W), lambda i, j: (i, 0)),
                  pl.BlockSpec((tm, MEM_W), lambda i, j: (i, 0)),
                  pl.BlockSpec((None, k, tn), lambda i, j: (layer, 0, col(i, j))),
                  pl.BlockSpec((tm, tn), lambda i, j: (i, j))],
        out_specs=pl.BlockSpec((tm, tn), lambda i, j: (i, j)),
        out_shape=jax.ShapeDtypeStruct((m, D_MODEL), F32),
        scratch_shapes=[pltpu.VMEM((tm, k), _lhs_dtype(tm)), pltpu.VMEM((nj, k, tn), BF16)],
        compiler_params=_cparams("arbitrary", "arbitrary"),
        name="out_proj",
    )(z, mo, w, x)


def _ffn_kernel(x_ref, g_ref, wa_ref, wb_ref, wd_ref, gf_ref, o_ref, xn_ref, *, final):
    j = pl.program_id(1)

    @pl.when(j == 0)
    def _():
        x = x_ref[...]
        xn_ref[...] = _rms(x, g_ref[...]).astype(xn_ref.dtype)
        o_ref[...] = x

    xn = xn_ref[...]
    a = _dot(xn, wa_ref[...])
    b = _dot(xn, wb_ref[...])
    o_ref[...] += _dot(jax.nn.silu(a) * b, wd_ref[...])

    if final:
        @pl.when(j == pl.num_programs(1) - 1)
        def _():
            o_ref[...] = _rms(o_ref[...], gf_ref[...])


def ffn(x, g, w_gu, w_down, layer, g_final, final, tm, tf):
    m, k = x.shape
    nf = D_FF // tf
    return pl.pallas_call(
        functools.partial(_ffn_kernel, final=final),
        grid=(m // tm, nf),
        in_specs=[pl.BlockSpec((tm, k), lambda i, j: (i, 0), pipeline_mode=pl.Buffered(1)),
                  pl.BlockSpec((1, k), lambda i, j: (0, 0)),
                  pl.BlockSpec((None, k, tf), lambda i, j: (layer, 0, j)),
                  pl.BlockSpec((None, k, tf), lambda i, j: (layer, 0, nf + j)),
                  pl.BlockSpec((None, tf, k), lambda i, j: (layer, j, 0)),
                  pl.BlockSpec((1, k), lambda i, j: (0, 0))],
        out_specs=pl.BlockSpec((tm, k), lambda i, j: (i, 0)),
        out_shape=jax.ShapeDtypeStruct((m, k), F32),
        scratch_shapes=[pltpu.VMEM((tm, k), _lhs_dtype(tm))],
        compiler_params=_cparams("parallel", "arbitrary"),
        name="ffn",
    )(x, g.reshape(1, k), w_gu, w_gu, w_down, g_final.reshape(1, k))


def _compress_kernel(tbl_ref, *refs, n_steps):
    pages = refs[:PAGES_PER_STEP]
    nxt_ref, extra_ref, w1_ref, pe_ref, w2_ref, o_ref = refs[PAGES_PER_STEP:]
    p = pl.program_id(1)
    on_extra = p == n_steps
    nxt_extra = p == n_steps - 1
    nb = CMP_STEP * KV_SLOTS
    width = CMP_STRIDE * HEAD_DIM

    def chunk(ref, n):
        return jnp.concatenate([ref[0, n * CMP_STRIDE + r] for r in range(CMP_STRIDE)], axis=1)

    blocks = []
    for k in range(PAGES_PER_STEP):
        for n in range(CH_PER_PAGE):
            x = chunk(pages[k], n)
            blocks.append(jnp.where(on_extra, chunk(extra_ref, n) if k == 0 else 0.0, x))
    blocks.append(jnp.where(on_extra, 0.0, jnp.where(nxt_extra, chunk(extra_ref, 0), chunk(nxt_ref, 0))))
    blocks += [pe_ref[0, 0], pe_ref[0, 1], pe_ref[1, 0], pe_ref[1, 1], jnp.zeros((4, width), F32)]
    lhs = jnp.concatenate(blocks, axis=0)
    w1 = jnp.concatenate([w1_ref[0, 0], w1_ref[0, 1], w1_ref[1, 0], w1_ref[1, 1]], axis=1)
    hc = _dot(lhs, w1)
    is_v = (lax.broadcasted_iota(jnp.int32, (nb + KV_SLOTS, 1), 0) & N_KV) != 0
    d = HEAD_DIM
    hsel = jnp.where(is_v, hc[:nb + KV_SLOTS, 2 * d:], hc[:nb + KV_SLOTS, :2 * d])
    pe_k = hc[nb + 8:nb + 9, 0:d] + hc[nb + 9:nb + 10, d:2 * d]
    pe_v = hc[nb + 10:nb + 11, 2 * d:3 * d] + hc[nb + 11:nb + 12, 3 * d:]
    hid = jax.nn.gelu(hsel[:nb, :d] + hsel[KV_SLOTS:, d:] + jnp.where(is_v[:nb], pe_v, pe_k))
    o2 = _dot(hid, jnp.concatenate([w2_ref[0], w2_ref[1]], axis=1))
    out = jnp.where(is_v[:nb], o2[:, d:], o2[:, :d])
    o_ref[0] = out.reshape(CMP_STEP, KV_SLOTS, d)


def compress(pages, table, extra, cmp_pe, cmp_w1, cmp_w2):
    nb, n_pages = table.shape
    n_steps = n_pages // PAGES_PER_STEP
    w1 = cmp_w1.reshape(2, 2, CMP_STRIDE * HEAD_DIM, HEAD_DIM)
    pe = cmp_pe.reshape(2, 2, 1, CMP_STRIDE * HEAD_DIM)

    def page_map(k):
        return lambda b, p, tbl: (tbl[b, jnp.minimum(p * PAGES_PER_STEP + k, n_pages - 1)], 0, 0, 0)

    in_specs = [pl.BlockSpec((1, PAGE_SIZE, KV_SLOTS, HEAD_DIM), page_map(k)) for k in range(PAGES_PER_STEP)]
    in_specs += [pl.BlockSpec((1, CMP_STRIDE, KV_SLOTS, HEAD_DIM), page_map(PAGES_PER_STEP)),
                 pl.BlockSpec((1, PAGE_SIZE, KV_SLOTS, HEAD_DIM), lambda b, p, tbl: (b, 0, 0, 0)),
                 pl.BlockSpec(w1.shape, lambda b, p, tbl: (0, 0, 0, 0)),
                 pl.BlockSpec(pe.shape, lambda b, p, tbl: (0, 0, 0, 0)),
                 pl.BlockSpec(cmp_w2.shape, lambda b, p, tbl: (0, 0, 0))]
    n_out = CMP_STEP * (n_steps + 1)
    out = pl.pallas_call(
        functools.partial(_compress_kernel, n_steps=n_steps),
        grid_spec=pltpu.PrefetchScalarGridSpec(
            num_scalar_prefetch=1,
            grid=(nb, n_steps + 1),
            in_specs=in_specs,
            out_specs=pl.BlockSpec((1, CMP_STEP, KV_SLOTS, HEAD_DIM), lambda b, p, tbl: (b, p, 0, 0)),
        ),
        out_shape=jax.ShapeDtypeStruct((nb, n_out, KV_SLOTS, HEAD_DIM), F32),
        compiler_params=_cparams("parallel", "arbitrary"),
        name="compress",
    )(table, *([pages] * (PAGES_PER_STEP + 1)), extra, w1, pe, cmp_w2)
    return jnp.transpose(out, (0, 2, 1, 3)).reshape(nb, 2, N_KV, n_out, HEAD_DIM)


def _cover(n_pad, j_pad, n_cmp):
    n = lax.broadcasted_iota(jnp.int32, (n_pad, j_pad), 0)
    j = lax.broadcasted_iota(jnp.int32, (n_pad, j_pad), 1)
    hit = (n * CMP_STRIDE < j * SLC_BLOCK + SLC_BLOCK) & (n * CMP_STRIDE + CMP_BLOCK - 1 >= j * SLC_BLOCK)
    return (hit & (n < n_cmp)).astype(F32)


def _cover_t(j_pad, n_pad, n_cmp):
    j = lax.broadcasted_iota(jnp.int32, (j_pad, n_pad), 0)
    n = lax.broadcasted_iota(jnp.int32, (j_pad, n_pad), 1)
    hit = (n * CMP_STRIDE < j * SLC_BLOCK + SLC_BLOCK) & (n * CMP_STRIDE + CMP_BLOCK - 1 >= j * SLC_BLOCK)
    return (hit & (n < n_cmp)).astype(F32)


def _softmax_rows(s):
    e = jnp.exp(s - jnp.max(s, axis=-1, keepdims=True))
    return e / jnp.sum(e, axis=-1, keepdims=True)


def _nsa_prompt_kernel(q_ref, gl_ref, bg_ref, cos_ref, sin_ref, ckv_ref, ks_ref, vs_ref, kw_ref, vw_ref,
                       o_ref, qs_ref, drop_ref, *, n_cmp, n_slc, kt):
    qb = pl.program_id(1)
    tq = Q_BLOCK
    cos2 = cos_ref[...]
    sin2 = sin_ref[...]
    n_pad = ckv_ref.shape[3]
    qpos1 = qb * tq + lax.broadcasted_iota(jnp.int32, (tq, 1), 0)
    qpos = jnp.concatenate([qpos1] * GQA, axis=0)
    gates = jax.nn.sigmoid(gl_ref[...] + bg_ref[...])

    n_i = lax.broadcasted_iota(jnp.int32, (1, n_pad), 1)
    vis_c = (n_i * CMP_STRIDE + CMP_BLOCK - 1 <= qpos) & (n_i < n_cmp)
    any_c = (qpos >= CMP_BLOCK - 1).astype(F32)
    cover_t = _cover_t(n_slc, n_pad, n_cmp)
    j_col = lax.broadcasted_iota(jnp.int32, (n_slc, 1), 0)
    qblk = jnp.right_shift(qb * tq + lax.broadcasted_iota(jnp.int32, (1, tq), 1), SLC_SHIFT)
    forced = (j_col == 0) | (j_col == qblk) | (j_col == qblk - 1)
    valid = j_col <= qblk
    n_sel = min(N_SELECT, n_slc)
    j_rows = lax.broadcasted_iota(jnp.int32, (HEAD_DIM, kt), 0)

    w_lo = jnp.maximum(qb * tq - WINDOW, 0)
    w_lo = pl.multiple_of(w_lo, tq)
    kpos_w = w_lo + lax.broadcasted_iota(jnp.int32, (1, WINDOW + tq), 1)
    vis_w = (kpos_w <= qpos) & (qpos - kpos_w < WINDOW)

    def gate_col(hd, br):
        return gates[:, hd * N_BRANCH + br:hd * N_BRANCH + br + 1]

    for h in range(N_KV):
        q = jnp.concatenate(
            [_rope(q_ref[:, (h * GQA + g) * HEAD_DIM:(h * GQA + g + 1) * HEAD_DIM], cos2, sin2)
             for g in range(GQA)], axis=0)
        qs = (q * SCALE).astype(BF16)
        qs_ref[h] = qs
        hs = slice(h * HEAD_DIM, (h + 1) * HEAD_DIM)

        s = jnp.where(vis_c, _dot_t(qs, ckv_ref[0, 0, h]), NEG)
        p = _softmax_rows(s) * any_c
        o_c = _dot(p, ckv_ref[0, 1, h])

        p_sum = p[0:tq] + p[tq:2 * tq] + p[2 * tq:3 * tq]
        score = lax.dot_general(cover_t, p_sum, (((1,), (1,)), ((), ())), preferred_element_type=F32,
                                precision=lax.Precision.HIGHEST)
        score = jnp.where(forced, jnp.inf, jnp.where(valid, score, -jnp.inf))
        rank = jnp.zeros((n_slc, tq), F32)
        for i in range(n_slc):
            row = score[i:i + 1, :]
            beats = (row > score) | ((row == score) & (i < j_col))
            rank = rank + jnp.where(beats, 1.0, 0.0)
        drop = jnp.where(rank >= n_sel, 1.0, 0.0)
        drop = jnp.transpose(jnp.concatenate([drop, jnp.zeros((HEAD_DIM - n_slc, tq), F32)], axis=0))
        drop_ref[h] = drop.astype(BF16)

        s = jnp.where(vis_w, _dot_t(qs, kw_ref[pl.ds(w_lo, WINDOW + tq), hs]), NEG)
        e = jnp.exp(s - jnp.max(s, axis=-1, keepdims=True))
        o_w = _dot(e, vw_ref[pl.ds(w_lo, WINDOW + tq), hs]) / jnp.sum(e, axis=-1, keepdims=True)

        for g in range(GQA):
            hd = h * GQA + g
            r = slice(g * tq, (g + 1) * tq)
            o_ref[:, hd * HEAD_DIM:(hd + 1) * HEAD_DIM] = gate_col(hd, 0) * o_c[r] + gate_col(hd, 2) * o_w[r]

    def slc_step(t, carry):
        k0 = pl.multiple_of(t * kt, kt)
        kpos = k0 + lax.broadcasted_iota(jnp.int32, (1, kt), 1)
        expand = jnp.where(jnp.right_shift(kpos, SLC_SHIFT) == j_rows, NEG, 0.0).astype(BF16)
        causal = jnp.where(kpos <= qpos1, 0.0, NEG)
        out = []
        for h in range(N_KV):
            m, l, acc = carry[h]
            hs = slice(h * HEAD_DIM, (h + 1) * HEAD_DIM)
            bias = jnp.dot(drop_ref[h], expand, preferred_element_type=F32) + causal
            s = _dot_t(qs_ref[h], ks_ref[pl.ds(k0, kt), hs]) + jnp.concatenate([bias] * GQA, axis=0)
            m_new = jnp.maximum(m, jnp.max(s, axis=-1, keepdims=True))
            a = jnp.exp(m - m_new)
            e = jnp.exp(s - m_new)
            l = l * a + jnp.sum(e, axis=-1, keepdims=True)
            acc = acc * a + _dot(e, vs_ref[pl.ds(k0, kt), hs])
            out.append((m_new, l, acc))
        return tuple(out)

    n_t = (qb * tq + tq + kt - 1) // kt
    init = (jnp.full((GQA * tq, 1), NEG, F32), jnp.zeros((GQA * tq, 1), F32), jnp.zeros((GQA * tq, HEAD_DIM), F32))
    done = lax.fori_loop(0, n_t, slc_step, (init,) * N_KV)
    for h in range(N_KV):
        _, l, acc = done[h]
        o_s = acc / l
        for g in range(GQA):
            hd = h * GQA + g
            r = slice(g * tq, (g + 1) * tq)
            o_ref[:, hd * HEAD_DIM:(hd + 1) * HEAD_DIM] += gate_col(hd, 1) * o_s[r]


def nsa_prompt(u, b_gate_pad, cos2, sin2, ckv, slc_k, slc_v, win_k, win_v, batch, seq):
    nqb = seq // Q_BLOCK
    n_pad = ckv.shape[3]
    kspec = pl.BlockSpec((seq, N_KV * HEAD_DIM), lambda b, i: (b, 0))
    return pl.pallas_call(
        functools.partial(_nsa_prompt_kernel, n_cmp=seq // CMP_STRIDE - 1, n_slc=seq // SLC_BLOCK, kt=512),
        grid=(batch, nqb),
        in_specs=[pl.BlockSpec((Q_BLOCK, MAIN_W), lambda b, i: (b * nqb + i, 0)),
                  pl.BlockSpec((Q_BLOCK, GATE_PAD), lambda b, i: (b * nqb + i, (MAIN_W + MEM_W) // GATE_PAD)),
                  pl.BlockSpec((1, GATE_PAD), lambda b, i: (0, 0)),
                  pl.BlockSpec((Q_BLOCK, HEAD_DIM), lambda b, i: (i, 0)),
                  pl.BlockSpec((Q_BLOCK, HEAD_DIM), lambda b, i: (i, 0)),
                  pl.BlockSpec((1, 2, N_KV, n_pad, HEAD_DIM), lambda b, i: (b, 0, 0, 0, 0)),
                  kspec, kspec, kspec, kspec],
        out_specs=pl.BlockSpec((Q_BLOCK, MAIN_W), lambda b, i: (b * nqb + i, 0)),
        out_shape=jax.ShapeDtypeStruct((batch * seq, MAIN_W), F32),
        scratch_shapes=[pltpu.VMEM((N_KV, GQA * Q_BLOCK, HEAD_DIM), BF16),
                        pltpu.VMEM((N_KV, Q_BLOCK, HEAD_DIM), BF16)],
        compiler_params=_cparams("parallel", "arbitrary"),
        name="nsa_prompt",
    )(u, u, b_gate_pad, cos2, sin2, ckv, slc_k, slc_v, win_k, win_v)


HEAD_ROWS = 16


def _head_mask(h):
    r = lax.broadcasted_iota(jnp.int32, (HEAD_ROWS, 1), 0)
    return (r >= h * GQA) & (r < (h + 1) * GQA)


def _nsa_sample_a_kernel(q_ref, cos_ref, sin_ref, ckv_ref, wkv_ref, wnew_ref, qr_ref, oc_ref, ow_ref, sel_ref,
                         *, n_cmp, n_slc, j_pad):
    n_pad = ckv_ref.shape[3]
    q = _rope(q_ref[0], cos_ref[...], sin_ref[...])
    qr_ref[0] = q
    qs = q * SCALE
    n_i = lax.broadcasted_iota(jnp.int32, (1, n_pad), 1)
    vis_c = (n_i * CMP_STRIDE + CMP_BLOCK - 1 <= PAST_LEN) & (n_i < n_cmp)
    cover = _cover(n_pad, j_pad, n_cmp)
    j_row = lax.broadcasted_iota(jnp.int32, (1, j_pad), 1)
    j_col = lax.broadcasted_iota(jnp.int32, (j_pad, 1), 0)
    qblk = PAST_LEN // SLC_BLOCK
    lane = lax.broadcasted_iota(jnp.int32, (N_SELECT, HEAD_DIM), 1)
    r_col = lax.broadcasted_iota(jnp.int32, (N_SELECT, 1), 0)
    n_win = wkv_ref.shape[1]
    kpos_w = PAST_LEN - n_win + lax.broadcasted_iota(jnp.int32, (1, n_win), 1)
    vis_w = (kpos_w <= PAST_LEN) & (PAST_LEN - kpos_w < WINDOW)

    o_c = jnp.zeros((HEAD_ROWS, HEAD_DIM), F32)
    o_w = jnp.zeros((HEAD_ROWS, HEAD_DIM), F32)
    sel_out = jnp.zeros((N_SELECT, HEAD_DIM), jnp.int32)
    for h in range(N_KV):
        mine = _head_mask(h)
        hs = slice(h * HEAD_DIM, (h + 1) * HEAD_DIM)
        vsl = slice((N_KV + h) * HEAD_DIM, (N_KV + h + 1) * HEAD_DIM)

        s = jnp.where(vis_c, _dot_t(qs, ckv_ref[0, 0, h]), NEG)
        p = _softmax_rows(s) * float(PAST_LEN >= CMP_BLOCK - 1)
        o_c = jnp.where(mine, _dot(p, ckv_ref[0, 1, h]), o_c)

        p_sum = jnp.sum(jnp.where(mine, p, 0.0), axis=0, keepdims=True)
        score = jnp.dot(jnp.broadcast_to(p_sum, (8, n_pad)), cover, preferred_element_type=F32,
                        precision=lax.Precision.HIGHEST)
        forced = (j_row == 0) | (j_row == qblk) | (j_row == qblk - 1)
        score = jnp.where(forced, jnp.inf, jnp.where(j_row <= qblk, score, -jnp.inf))
        score = jnp.where(j_row < n_slc, score, -jnp.inf)
        s_col = jnp.transpose(score)[:, 0:1]
        beats = ((s_col > score[0:1]) | ((s_col == score[0:1]) & (j_col < j_row))) & (j_col < n_slc)
        rank = jnp.sum(beats.astype(F32), axis=0, keepdims=True)
        hit = (rank == r_col.astype(F32)) & (j_row < n_slc)
        idx = jnp.sum(jnp.where(hit, j_row.astype(F32), 0.0), axis=1, keepdims=True)
        sel_out = jnp.where(lane == h, idx.astype(jnp.int32), sel_out)

        s = jnp.where(vis_w, _dot_t(qs, wkv_ref[0, :, hs]), NEG)
        s_new = jnp.sum(qs * wnew_ref[0, :, hs], axis=-1, keepdims=True)
        m = jnp.maximum(jnp.max(s, axis=-1, keepdims=True), s_new)
        e = jnp.exp(s - m)
        e_new = jnp.exp(s_new - m)
        o = (_dot(e, wkv_ref[0, :, vsl]) + e_new * wnew_ref[0, :, vsl]) / (jnp.sum(e, axis=-1, keepdims=True) + e_new)
        o_w = jnp.where(mine, o, o_w)
    oc_ref[0] = o_c
    ow_ref[0] = o_w
    sel_ref[0] = sel_out


def nsa_sample_a(q16, cos2, sin2, ckv, win_cache, win_new, n_cmp, n_slc):
    db = q16.shape[0]
    n_pad = ckv.shape[3]
    n_win = win_cache.shape[1]
    j_pad = -(-n_slc // 128) * 128
    hspec = pl.BlockSpec((1, HEAD_ROWS, HEAD_DIM), lambda b: (b, 0, 0))
    hout = jax.ShapeDtypeStruct((db, HEAD_ROWS, HEAD_DIM), F32)
    return pl.pallas_call(
        functools.partial(_nsa_sample_a_kernel, n_cmp=n_cmp, n_slc=n_slc, j_pad=j_pad),
        grid=(db,),
        in_specs=[hspec,
                  pl.BlockSpec((1, HEAD_DIM), lambda b: (0, 0)),
                  pl.BlockSpec((1, HEAD_DIM), lambda b: (0, 0)),
                  pl.BlockSpec((1, 2, N_KV, n_pad, HEAD_DIM), lambda b: (b, 0, 0, 0, 0)),
                  pl.BlockSpec((1, n_win, KV_W), lambda b: (b, 0, 0)),
                  pl.BlockSpec((1, 1, KV_W), lambda b: (b, 0, 0))],
        out_specs=[hspec, hspec, hspec, pl.BlockSpec((1, N_SELECT, HEAD_DIM), lambda b: (b, 0, 0))],
        out_shape=[hout, hout, hout, jax.ShapeDtypeStruct((db, N_SELECT, HEAD_DIM), jnp.int32)],
        compiler_params=_cparams("arbitrary"),
        name="nsa_sample_a",
    )(q16, cos2, sin2, ckv, win_cache, win_new)


def _nsa_sample_b_kernel(sel_ref, tbl_ref, q_ref, *refs):
    blocks = refs[:N_SELECT]
    new_ref, oc_ref, ow_ref, gl_ref, bg_ref, o_ref, os_ref = refs[N_SELECT:]
    b = pl.program_id(0)
    h = pl.program_id(1)
    n_keys = N_SELECT * SLC_BLOCK
    lane_blk = jnp.right_shift(lax.broadcasted_iota(jnp.int32, (1, n_keys), 1), SLC_SHIFT)
    new_k = new_ref[0, pl.ds(h, 1), :]
    new_v = new_ref[0, pl.ds(h + N_KV, 1), :]
    ks, vs = [], []
    blk_of = jnp.zeros((1, n_keys), jnp.int32)
    for r in range(N_SELECT):
        blk = sel_ref[b, h, r]
        in_past = blk * SLC_BLOCK < PAST_LEN
        ks.append(jnp.where(in_past, blocks[r][0, pl.ds(h, SLC_BLOCK, stride=KV_SLOTS), :], new_k))
        vs.append(jnp.where(in_past, blocks[r][0, pl.ds(h + N_KV, SLC_BLOCK, stride=KV_SLOTS), :], new_v))
        blk_of = jnp.where(lane_blk == r, blk, blk_of)
    kpos = blk_of * SLC_BLOCK + (lax.broadcasted_iota(jnp.int32, (1, n_keys), 1) & (SLC_BLOCK - 1))
    s = jnp.where(kpos <= PAST_LEN, _dot_t(q_ref[0] * SCALE, jnp.concatenate(ks, axis=0)), NEG)
    e = jnp.exp(s - jnp.max(s, axis=-1, keepdims=True))
    o_s = _dot(e, jnp.concatenate(vs, axis=0)) / jnp.sum(e, axis=-1, keepdims=True)

    @pl.when(h == 0)
    def _():
        os_ref[...] = jnp.zeros(os_ref.shape, F32)

    row = lax.broadcasted_iota(jnp.int32, (HEAD_ROWS, 1), 0)
    mine = (row >= h * GQA) & (row < (h + 1) * GQA)
    os_ref[...] = jnp.where(mine, o_s, os_ref[...])

    @pl.when(h == pl.num_programs(1) - 1)
    def _():
        gates = jax.nn.sigmoid(gl_ref[0] + bg_ref[...])
        o_ref[0] = gates[:, 0:1] * oc_ref[0] + gates[:, 1:2] * os_ref[...] + gates[:, 2:3] * ow_ref[0]


def nsa_sample_b(sel, page_table, q_rope, slc_halves, slc_new, o_c, o_w, gate_logits, b_gate16):
    db = q_rope.shape[0]
    last_blk = PAST_LEN // SLC_BLOCK - 1
    halves = PAGE_SIZE // SLC_BLOCK

    def cache_map(r):
        def index(b, h, sel_ref, tbl_ref):
            j = jnp.minimum(sel_ref[b, h, r], last_blk)
            return (tbl_ref[b, j // halves] * halves + j % halves, 0, 0)
        return index

    hspec = pl.BlockSpec((1, HEAD_ROWS, HEAD_DIM), lambda b, h, s, t: (b, 0, 0))
    return pl.pallas_call(
        _nsa_sample_b_kernel,
        grid_spec=pltpu.PrefetchScalarGridSpec(
            num_scalar_prefetch=2,
            grid=(db, N_KV),
            in_specs=[hspec]
            + [pl.BlockSpec((1, SLC_BLOCK * KV_SLOTS, HEAD_DIM), cache_map(r)) for r in range(N_SELECT)]
            + [pl.BlockSpec((1, KV_SLOTS, HEAD_DIM), lambda b, h, s, t: (b, 0, 0)),
               hspec, hspec, hspec,
               pl.BlockSpec((HEAD_ROWS, HEAD_DIM), lambda b, h, s, t: (0, 0))],
            out_specs=hspec,
            scratch_shapes=[pltpu.VMEM((HEAD_ROWS, HEAD_DIM), F32)],
        ),
        out_shape=jax.ShapeDtypeStruct((db, HEAD_ROWS, HEAD_DIM), F32),
        compiler_params=_cparams("arbitrary", "arbitrary"),
        name="nsa_sample_b",
    )(sel, page_table, q_rope, *([slc_halves] * N_SELECT), slc_new, o_c, o_w, gate_logits, b_gate16)


def _rope_tables(pos):
    half = HEAD_DIM // 2
    inv = ROPE_THETA ** (-jnp.arange(half, dtype=F32) / half)
    ang = pos.astype(F32)[:, None] * inv[None, :]
    cos, sin = jnp.cos(ang), jnp.sin(ang)
    return jnp.concatenate([cos, cos], -1), jnp.concatenate([-sin, sin], -1)


def _pad_w_in_b(w):
    return jnp.concatenate([w[..., :MAIN_W], w[..., MAIN_W + GATE_W:],
                            jnp.pad(w[..., MAIN_W:MAIN_W + GATE_W], ((0, 0), (0, 0), (0, GATE_PAD - GATE_W)))], axis=-1)


def _prompt_trunk(x_prompt, mem_prompt, g_mix, w_in_a, pool_grp_w, pool_scale, w_in_b_pad, b_gate, g_kv, w_kv,
                  cmp_pe, cmp_w1, cmp_w2, g_mem, w_mem_kv, w_out, g_ffn, w_gu, w_down, g_final):
    batch, seq, _ = x_prompt.shape
    m = batch * seq
    tm = PROJ_TM
    x = x_prompt.reshape(m, D_MODEL)
    mem2 = mem_prompt.reshape(batch * MEM_TOKENS, D_MODEL)
    mem_kv = [rms_matmul(mem2, g_mem[l], w_mem_kv, l, 512, 512) for l in range(DEPTH)]
    cos2, sin2 = _rope_tables(jnp.arange(seq, dtype=jnp.int32))
    pools = []
    nsa = None
    for l in range(DEPTH):
        if l == N_A:
            cmp_kv, slc_kv, win_kv, slc_k, slc_v, win_k, win_v = kv_proj(x, g_kv, w_kv, cos2, sin2, 512)
            table = jnp.arange(batch * seq // PAGE_SIZE, dtype=jnp.int32).reshape(batch, seq // PAGE_SIZE)
            ckv = compress(cmp_kv.reshape(-1, PAGE_SIZE, KV_SLOTS, HEAD_DIM), table,
                           jnp.zeros((batch, PAGE_SIZE, KV_SLOTS, HEAD_DIM), F32), cmp_pe, cmp_w1, cmp_w2)
            nsa = (ckv, slc_k, slc_v, win_k, win_v)
        if l < N_A:
            u = rms_matmul(x, g_mix[l], w_in_a, l, tm, 512)
            u3 = u.reshape(batch, seq, -1)
            z = pool_prompt(u3, pool_grp_w[l], pool_scale[l]).reshape(m, MAIN_W)
            pools.append(u3[:, seq - POOL_BUF:, :MAIN_W])
        else:
            j = l - N_A
            u = rms_matmul(x, g_mix[l], w_in_b_pad, j, tm, 768)
            u3 = u.reshape(batch, seq, -1)
            bg = jnp.pad(b_gate[j], (0, GATE_PAD - GATE_W)).reshape(1, GATE_PAD)
            z = nsa_prompt(u, bg, cos2, sin2, *nsa, batch, seq)
        mo = mem_attend(u3, mem_kv[l], 512).reshape(m, MEM_W)
        x = out_proj(z, mo, w_out, l, x, tm, 512)
        x = ffn(x, g_ffn[l], w_gu, w_down, l, g_final, l == DEPTH - 1, FFN_TM, FFN_TF)
    kv5 = (batch, seq, 2, N_KV, HEAD_DIM)
    n_keep = min(WINDOW, seq)
    return (x.reshape(batch, seq, D_MODEL), jnp.stack(pools), cmp_kv.reshape(kv5), slc_kv.reshape(kv5),
            win_kv.reshape(kv5)[:, seq - n_keep:],
            jnp.stack(mem_kv).reshape(DEPTH, batch, MEM_TOKENS, 2, N_MEM_HEADS, MEM_HEAD_DIM))


def _sample_trunk(x_sample, state_pool, cache_cmp_kv, cache_slc_kv, cache_win_kv, cache_mem_kv, page_table,
                  g_mix, w_in_a, pool_grp_w, pool_scale, w_in_b_pad, b_gate, g_kv, w_kv,
                  cmp_pe, cmp_w1, cmp_w2, w_out, g_ffn, w_gu, w_down, g_final):
    db = x_sample.shape[0]
    x = x_sample.reshape(db, D_MODEL)
    cos2, sin2 = _rope_tables(jnp.full((db,), PAST_LEN, jnp.int32))
    n_phys = cache_cmp_kv.shape[0]
    pools = []
    for l in range(DEPTH):
        mkv = cache_mem_kv[l].reshape(db * MEM_TOKENS, 2 * MEM_W)
        if l == N_A:
            cmp_new, slc_new, win_new = kv_proj(x, g_kv, w_kv, cos2, sin2, db)[:N_BRANCH]
            extra = jnp.pad(cmp_new.reshape(db, 1, KV_SLOTS, HEAD_DIM), ((0, 0), (0, PAGE_SIZE - 1), (0, 0), (0, 0)))
            ckv = compress(cache_cmp_kv.reshape(n_phys, PAGE_SIZE, KV_SLOTS, HEAD_DIM), page_table, extra,
                           cmp_pe, cmp_w1, cmp_w2)
            t_full = -(-(PAST_LEN + 1) // SLC_BLOCK) * SLC_BLOCK
            n_cmp = t_full // CMP_STRIDE - 1
            n_slc = t_full // SLC_BLOCK
        if l < N_A:
            u = rms_matmul(x, g_mix[l], w_in_a, l, db, 512)
            z = pool_sample(u, jnp.transpose(state_pool[l], (1, 0, 2)), pool_grp_w[l], pool_scale[l])
            pools.append(jnp.concatenate([state_pool[l][:, 1:], u[:, None, :MAIN_W]], axis=1))
        else:
            j = l - N_A
            u = rms_matmul(x, g_mix[l], w_in_b_pad, j, db, 768)
            q16 = jnp.pad(u[:, :MAIN_W].reshape(db, N_HEADS, HEAD_DIM), ((0, 0), (0, HEAD_ROWS - N_HEADS), (0, 0)))
            q_rope, o_c, o_w, sel = nsa_sample_a(
                q16, cos2[:1], sin2[:1], ckv, cache_win_kv.reshape(db, -1, KV_W), win_new.reshape(db, 1, KV_W),
                n_cmp, n_slc)
            sel = jnp.transpose(sel[:, :, :N_KV], (0, 2, 1))
            gl = u[:, MAIN_W + MEM_W:MAIN_W + MEM_W + GATE_W].reshape(db, N_HEADS, N_BRANCH)
            gl = jnp.pad(gl, ((0, 0), (0, HEAD_ROWS - N_HEADS), (0, HEAD_DIM - N_BRANCH)))
            bg = jnp.pad(b_gate[j].reshape(N_HEADS, N_BRANCH), ((0, HEAD_ROWS - N_HEADS), (0, HEAD_DIM - N_BRANCH)))
            z16 = nsa_sample_b(sel, page_table, q_rope,
                               cache_slc_kv.reshape(n_phys * 2, SLC_BLOCK * KV_SLOTS, HEAD_DIM),
                               slc_new.reshape(db, KV_SLOTS, HEAD_DIM), o_c, o_w, gl, bg)
            z = z16[:, :N_HEADS].reshape(db, MAIN_W)
        mo = mem_attend(u.reshape(db, 1, -1), mkv, 1).reshape(db, MEM_W)
        x = out_proj(z, mo, w_out, l, x, db, 512)
        x = ffn(x, g_ffn[l], w_gu, w_down, l, g_final, l == DEPTH - 1, db, 512)
    kv5 = (db, 1, 2, N_KV, HEAD_DIM)
    win_s = jnp.concatenate([cache_win_kv, win_new.reshape(kv5)], axis=1)
    win_s = win_s[:, win_s.shape[1] - min(WINDOW, win_s.shape[1]):]
    return (x.reshape(db, 1, D_MODEL), jnp.stack(pools), cmp_new.reshape(kv5), slc_new.reshape(kv5), win_s)


def kernel(x_prompt, x_sample, mem_prompt, state_pool, cache_cmp_kv, cache_slc_kv, cache_win_kv, cache_mem_kv, page_table, g_mix, w_in_a, pool_grp_w, pool_scale, w_in_b, b_gate, g_kv, w_kv, cmp_pe, cmp_w1, cmp_w2, g_mem, w_mem_kv, w_out, g_ffn, w_gu, w_down, g_final):
    w_in_b_pad = _pad_w_in_b(w_in_b)
    y_p, pool_p, cmp_p, slc_p, win_p, mem_kv_p = _prompt_trunk(
        x_prompt, mem_prompt, g_mix, w_in_a, pool_grp_w, pool_scale, w_in_b_pad, b_gate, g_kv, w_kv,
        cmp_pe, cmp_w1, cmp_w2, g_mem, w_mem_kv, w_out, g_ffn, w_gu, w_down, g_final)
    y_s, pool_s, cmp_s, slc_s, win_s = _sample_trunk(
        x_sample, state_pool, cache_cmp_kv, cache_slc_kv, cache_win_kv, cache_mem_kv, page_table,
        g_mix, w_in_a, pool_grp_w, pool_scale, w_in_b_pad, b_gate, g_kv, w_kv,
        cmp_pe, cmp_w1, cmp_w2, w_out, g_ffn, w_gu, w_down, g_final)
    return (y_p, y_s, pool_p, cmp_p, slc_p, win_p, mem_kv_p, pool_s, cmp_s, slc_s, win_s)
```

```python
import functools

import jax
import jax.numpy as jnp
from jax import lax
from jax.experimental import pallas as pl
from jax.experimental.pallas import tpu as pltpu

F32 = jnp.float32
BF16 = jnp.bfloat16

D_MODEL = 2048
DEPTH = 4
N_A = 2
PAST_LEN = 16384
PAGE_SIZE = 128
MEM_TOKENS = 256
N_MEM_HEADS = 4
MEM_HEAD_DIM = 128
MEM_W = 512
MAIN_W = 1536
POOL_WINDOWS = (2, 4, 8, 16)
POOL_GC = 384
POOL_BUF = 15
HEAD_DIM = 128
N_HEADS = 12
N_KV = 4
GQA = 3
N_BRANCH = 3
GATE_W = 36
GATE_PAD = 256
CMP_BLOCK = 32
CMP_STRIDE = 16
SLC_BLOCK = 64
SLC_SHIFT = 6
N_SELECT = 16
WINDOW = 512
Q_BLOCK = 128
D_FF = 5632
ROPE_THETA = 10000.0
EPS = 1e-6
NEG = -1e30
SCALE = HEAD_DIM ** -0.5
MEM_SCALE = MEM_HEAD_DIM ** -0.5
KV_SLOTS = 2 * N_KV
KV_W = KV_SLOTS * HEAD_DIM
W_IN_B = MAIN_W + MEM_W + GATE_PAD

VMEM_LIMIT = 56 * 1024 * 1024
PROJ_TM = 1024
FFN_TM = 1024
FFN_TF = 256
PAGES_PER_STEP = 16
CH_PER_PAGE = PAGE_SIZE // CMP_STRIDE
CMP_STEP = PAGES_PER_STEP * CH_PER_PAGE


def _cparams(*sem):
    return pltpu.CompilerParams(dimension_semantics=sem, vmem_limit_bytes=VMEM_LIMIT)


def _rms(x, g):
    return x * lax.rsqrt(jnp.mean(x * x, axis=-1, keepdims=True) + EPS) * g


def _dot(a, b):
    return jnp.dot(a.astype(BF16), b.astype(BF16), preferred_element_type=F32)


def _dot_t(a, b):
    return lax.dot_general(a.astype(BF16), b.astype(BF16), (((1,), (1,)), ((), ())),
                           preferred_element_type=F32)


def _lhs_dtype(rows):
    return BF16 if rows % 16 == 0 else F32


def _rope(x, cos2, sin2):
    return x * cos2 + pltpu.roll(x, HEAD_DIM // 2, axis=1) * sin2


def _rms_matmul_kernel(x_ref, g_ref, w_ref, o_ref, xn_ref, wb_ref):
    i = pl.program_id(0)
    j = pl.program_id(1)

    @pl.when(j == 0)
    def _():
        xn_ref[...] = _rms(x_ref[...], g_ref[...]).astype(xn_ref.dtype)

    @pl.when(i == 0)
    def _():
        wb_ref[j] = w_ref[...].astype(BF16)

    o_ref[...] = _dot(xn_ref[...], wb_ref[j])


def _rms_matmul2_kernel(x_ref, xs_ref, g_ref, w_ref, o_ref, os_ref, xn_ref, xsn_ref, wb_ref):
    _rms_matmul_kernel(x_ref, g_ref, w_ref, o_ref, xn_ref, wb_ref)
    i = pl.program_id(0)
    j = pl.program_id(1)

    @pl.when((i == 0) & (j == 0))
    def _():
        xsn_ref[...] = _rms(xs_ref[...], g_ref[...])

    @pl.when(i == 0)
    def _():
        os_ref[...] = _dot(xsn_ref[...], wb_ref[j])


def _once_per_column(nj):
    return lambda i, j: jnp.where(i == 0, j, nj - 1)


def rms_matmul(x, g, w, layer, tm, tn, xs=None):
    m, k = x.shape
    n = w.shape[2]
    nj = n // tn
    col = _once_per_column(nj)
    x_spec = pl.BlockSpec((tm, k), lambda i, j: (i, 0))
    g_spec = pl.BlockSpec((1, k), lambda i, j: (0, 0))
    w_spec = pl.BlockSpec((None, k, tn), lambda i, j: (layer, 0, col(i, j)))
    o_spec = pl.BlockSpec((tm, tn), lambda i, j: (i, j))
    scratch = [pltpu.VMEM((tm, k), _lhs_dtype(tm)), pltpu.VMEM((nj, k, tn), BF16)]
    if xs is None:
        return pl.pallas_call(
            _rms_matmul_kernel,
            grid=(m // tm, nj),
            in_specs=[x_spec, g_spec, w_spec],
            out_specs=o_spec,
            out_shape=jax.ShapeDtypeStruct((m, n), F32),
            scratch_shapes=scratch,
            compiler_params=_cparams("arbitrary", "arbitrary"),
            name="rms_matmul",
        )(x, g.reshape(1, k), w)
    ms = xs.shape[0]
    return pl.pallas_call(
        _rms_matmul2_kernel,
        grid=(m // tm, nj),
        in_specs=[x_spec, pl.BlockSpec((ms, k), lambda i, j: (0, 0)), g_spec, w_spec],
        out_specs=[o_spec, pl.BlockSpec((ms, tn), lambda i, j: (0, col(i, j)))],
        out_shape=[jax.ShapeDtypeStruct((m, n), F32), jax.ShapeDtypeStruct((ms, n), F32)],
        scratch_shapes=[scratch[0], pltpu.VMEM((ms, k), F32), scratch[1]],
        compiler_params=_cparams("arbitrary", "arbitrary"),
        name="rms_matmul2",
    )(x, xs, g.reshape(1, k), w)


def _kv_proj_kernel(x_ref, g_ref, w_ref, cos_ref, sin_ref, cmp_ref, slc_ref, win_ref,
                    slc_k_ref, slc_v_ref, win_k_ref, win_v_ref, xn_ref, wb_ref):
    i = pl.program_id(0)
    j = pl.program_id(1)
    half = N_KV * HEAD_DIM

    @pl.when(j == 0)
    def _():
        xn_ref[...] = _rms(x_ref[...], g_ref[...]).astype(xn_ref.dtype)

    @pl.when(i == 0)
    def _():
        wb_ref[j] = w_ref[...].astype(BF16)

    y = _dot(xn_ref[...], wb_ref[j])
    copies = ((slc_k_ref, slc_v_ref), (win_k_ref, win_v_ref))
    for br, ref in enumerate((cmp_ref, slc_ref, win_ref)):
        for c in range(2):
            @pl.when(j == 2 * br + c)
            def _(ref=ref, br=br, c=c):
                val = y
                if c == 0:
                    cos2 = cos_ref[...]
                    sin2 = sin_ref[...]
                    val = jnp.concatenate(
                        [_rope(y[:, h * HEAD_DIM:(h + 1) * HEAD_DIM], cos2, sin2) for h in range(N_KV)], axis=1)
                ref[:, c * half:(c + 1) * half] = val
                if br > 0:
                    copies[br - 1][c][...] = val.astype(BF16)


def kv_proj(x, g, w, cos2, sin2, tm):
    m, k = x.shape
    half = N_KV * HEAD_DIM
    nt = cos2.shape[0] // tm
    nj = 2 * N_BRANCH
    col = _once_per_column(nj)
    out = jax.ShapeDtypeStruct((m, KV_W), F32)
    out16 = jax.ShapeDtypeStruct((m, half), BF16)
    ospec = pl.BlockSpec((tm, KV_W), lambda i, j: (i, 0))
    ospec16 = pl.BlockSpec((tm, half), lambda i, j: (i, 0))
    return pl.pallas_call(
        _kv_proj_kernel,
        grid=(m // tm, nj),
        in_specs=[pl.BlockSpec((tm, k), lambda i, j: (i, 0)),
                  pl.BlockSpec((1, k), lambda i, j: (0, 0)),
                  pl.BlockSpec((k, half), lambda i, j: (0, col(i, j))),
                  pl.BlockSpec((tm, HEAD_DIM), lambda i, j: (i % nt, 0)),
                  pl.BlockSpec((tm, HEAD_DIM), lambda i, j: (i % nt, 0))],
        out_specs=[ospec, ospec, ospec, ospec16, ospec16, ospec16, ospec16],
        out_shape=[out, out, out, out16, out16, out16, out16],
        scratch_shapes=[pltpu.VMEM((tm, k), _lhs_dtype(tm)), pltpu.VMEM((nj, k, half), BF16)],
        compiler_params=_cparams("arbitrary", "arbitrary"),
        name="kv_proj",
    )(x, g.reshape(1, k), w, cos2, sin2)


def _pool_prompt_kernel(cur_ref, prev_ref, w_ref, sc_ref, o_ref, *, ts):
    i = pl.program_id(1)
    keep = (i > 0).astype(F32)
    pos = (i * ts + lax.broadcasted_iota(jnp.int32, (ts, 1), 0) + 1).astype(F32)
    for g, w in enumerate(POOL_WINDOWS):
        sl = slice(g * POOL_GC, (g + 1) * POOL_GC)
        x = cur_ref[0, :, sl]
        ext = jnp.concatenate([prev_ref[0, :, sl] * keep, x], axis=0)
        span = 1
        while span < w:
            ext = ext[span:] + ext[:-span]
            span *= 2
        win = ext[17 - w:17 - w + ts]
        d = win / jnp.minimum(pos, float(w)) - x
        o_ref[0, :, sl] = _dot(d, w_ref[g]) * sc_ref[:, sl]


def pool_prompt(u3, w_grp, scale, ts=256):
    b, s, n = u3.shape
    r = ts // 16
    return pl.pallas_call(
        functools.partial(_pool_prompt_kernel, ts=ts),
        grid=(b, s // ts),
        in_specs=[pl.BlockSpec((1, ts, MAIN_W), lambda bi, i: (bi, i, 0)),
                  pl.BlockSpec((1, 16, MAIN_W), lambda bi, i: (bi, jnp.maximum(i * r - 1, 0), 0)),
                  pl.BlockSpec((4, POOL_GC, POOL_GC), lambda bi, i: (0, 0, 0)),
                  pl.BlockSpec((1, MAIN_W), lambda bi, i: (0, 0))],
        out_specs=pl.BlockSpec((1, ts, MAIN_W), lambda bi, i: (bi, i, 0)),
        out_shape=jax.ShapeDtypeStruct((b, s, MAIN_W), F32),
        compiler_params=_cparams("parallel", "arbitrary"),
        name="pool_prompt",
    )(u3, u3, w_grp, scale.reshape(1, MAIN_W))


def _pool_sample_kernel(cur_ref, buf_ref, w_ref, sc_ref, o_ref):
    for g, w in enumerate(POOL_WINDOWS):
        sl = slice(g * POOL_GC, (g + 1) * POOL_GC)
        x = cur_ref[:, sl]
        win = x
        for r in range(POOL_BUF - (w - 1), POOL_BUF):
            win = win + buf_ref[r, :, sl]
        d = win / float(w) - x
        o_ref[:, sl] = _dot(d, w_ref[g]) * sc_ref[:, sl]


def pool_sample(u, buf_t, w_grp, scale):
    db = u.shape[0]
    return pl.pallas_call(
        _pool_sample_kernel,
        grid=(1,),
        in_specs=[pl.BlockSpec((db, MAIN_W), lambda i: (0, 0)),
                  pl.BlockSpec((POOL_BUF, db, MAIN_W), lambda i: (0, 0, 0)),
                  pl.BlockSpec((4, POOL_GC, POOL_GC), lambda i: (0, 0, 0)),
                  pl.BlockSpec((1, MAIN_W), lambda i: (0, 0))],
        out_specs=pl.BlockSpec((db, MAIN_W), lambda i: (0, 0)),
        out_shape=jax.ShapeDtypeStruct((db, MAIN_W), F32),
        compiler_params=_cparams("arbitrary"),
        name="pool_sample",
    )(u, buf_t, w_grp, scale.reshape(1, MAIN_W))


def _mem_attend_kernel(q_ref, kv_ref, o_ref, *, tq, slots):
    rows = max(tq, 8)
    for h in range(N_MEM_HEADS):
        sl = slice(h * MEM_HEAD_DIM, (h + 1) * MEM_HEAD_DIM)
        q = q_ref[0, :, sl]
        if tq < rows:
            q = jnp.broadcast_to(q[0:1], (rows, MEM_HEAD_DIM))
        if slots:
            k = kv_ref[pl.ds(h, MEM_TOKENS, stride=2 * N_MEM_HEADS), :]
            v = kv_ref[pl.ds(N_MEM_HEADS + h, MEM_TOKENS, stride=2 * N_MEM_HEADS), :]
        else:
            k = kv_ref[:, sl]
            v = kv_ref[:, MEM_W + h * MEM_HEAD_DIM:MEM_W + (h + 1) * MEM_HEAD_DIM]
        s = _dot_t(q, k) * MEM_SCALE
        e = jnp.exp(s - jnp.max(s, axis=-1, keepdims=True))
        o = _dot(e, v) / jnp.sum(e, axis=-1, keepdims=True)
        o_ref[0, :, sl] = o[0:tq]


def mem_attend(u3, mkv, tq, layer=None):
    b, s, _ = u3.shape
    if layer is None:
        kv_spec = pl.BlockSpec((MEM_TOKENS, 2 * MEM_W), lambda bi, i: (bi, 0))
    else:
        kv_spec = pl.BlockSpec((MEM_TOKENS * 2 * N_MEM_HEADS, MEM_HEAD_DIM), lambda bi, i: (layer * b + bi, 0))
    return pl.pallas_call(
        functools.partial(_mem_attend_kernel, tq=tq, slots=layer is not None),
        grid=(b, s // tq),
        in_specs=[pl.BlockSpec((1, tq, MEM_W), lambda bi, i: (bi, i, MAIN_W // MEM_W)), kv_spec],
        out_specs=pl.BlockSpec((1, tq, MEM_W), lambda bi, i: (bi, i, 0)),
        out_shape=jax.ShapeDtypeStruct((b, s, MEM_W), F32),
        compiler_params=_cparams("parallel", "arbitrary"),
        name="mem_attend",
    )(u3, mkv)


def _out_proj_kernel(z_ref, mo_ref, w_ref, x_ref, o_ref, lhs_ref, wb_ref):
    i = pl.program_id(0)
    j = pl.program_id(1)

    @pl.when(j == 0)
    def _():
        lhs_ref[:, :MAIN_W] = z_ref[...].astype(lhs_ref.dtype)
        lhs_ref[:, MAIN_W:] = mo_ref[...].astype(lhs_ref.dtype)

    @pl.when(i == 0)
    def _():
        wb_ref[j] = w_ref[...].astype(BF16)

    o_ref[...] = x_ref[...] + _dot(lhs_ref[...], wb_ref[j])


def _out_proj2_kernel(z_ref, mo_ref, w_ref, x_ref, zs_ref, mos_ref, xs_ref, o_ref, os_ref, lhs_ref, wb_ref):
    _out_proj_kernel(z_ref, mo_ref, w_ref, x_ref, o_ref, lhs_ref, wb_ref)
    j = pl.program_id(1)

    @pl.when(pl.program_id(0) == 0)
    def _():
        lhs = jnp.concatenate([zs_ref[...], mos_ref[...]], axis=1)
        os_ref[...] = xs_ref[...] + _dot(lhs, wb_ref[j])


def out_proj(z, mo, w, layer, x, tm, tn, sample=None):
    m = x.shape[0]
    k = MAIN_W + MEM_W
    nj = D_MODEL // tn
    col = _once_per_column(nj)
    in_specs = [pl.BlockSpec((tm, MAIN_W), lambda i, j: (i, 0)),
                pl.BlockSpec((tm, MEM_W), lambda i, j: (i, 0)),
                pl.BlockSpec((None, k, tn), lambda i, j: (layer, 0, col(i, j))),
                pl.BlockSpec((tm, tn), lambda i, j: (i, j))]
    o_spec = pl.BlockSpec((tm, tn), lambda i, j: (i, j))
    scratch = [pltpu.VMEM((tm, k), _lhs_dtype(tm)), pltpu.VMEM((nj, k, tn), BF16)]
    if sample is None:
        return pl.pallas_call(
            _out_proj_kernel,
            grid=(m // tm, nj),
            in_specs=in_specs,
            out_specs=o_spec,
            out_shape=jax.ShapeDtypeStruct((m, D_MODEL), F32),
            scratch_shapes=scratch,
            compiler_params=_cparams("arbitrary", "arbitrary"),
            name="out_proj",
        )(z, mo, w, x)
    zs, mos, xs = sample
    ms = xs.shape[0]
    s_spec = pl.BlockSpec((ms, tn), lambda i, j: (0, col(i, j)))
    return pl.pallas_call(
        _out_proj2_kernel,
        grid=(m // tm, nj),
        in_specs=in_specs + [pl.BlockSpec((ms, MAIN_W), lambda i, j: (0, 0)),
                             pl.BlockSpec((ms, MEM_W), lambda i, j: (0, 0)), s_spec],
        out_specs=[o_spec, s_spec],
        out_shape=[jax.ShapeDtypeStruct((m, D_MODEL), F32), jax.ShapeDtypeStruct((ms, D_MODEL), F32)],
        scratch_shapes=scratch,
        compiler_params=_cparams("arbitrary", "arbitrary"),
        name="out_proj2",
    )(z, mo, w, x, zs, mos, xs)


def _ffn_kernel(x_ref, g_ref, wa_ref, wb_ref, wd_ref, gf_ref, o_ref, xn_ref, *, final):
    j = pl.program_id(1)

    @pl.when(j == 0)
    def _():
        x = x_ref[...]
        xn_ref[...] = _rms(x, g_ref[...]).astype(xn_ref.dtype)
        o_ref[...] = x

    xn = xn_ref[...]
    a = _dot(xn, wa_ref[...])
    b = _dot(xn, wb_ref[...])
    o_ref[...] += _dot(jax.nn.silu(a) * b, wd_ref[...])

    if final:
        @pl.when(j == pl.num_programs(1) - 1)
        def _():
            o_ref[...] = _rms(o_ref[...], gf_ref[...])


def _ffn2_kernel(x_ref, g_ref, wa_ref, wb_ref, wd_ref, gf_ref, xs_ref, o_ref, os_ref, xn_ref, xsn_ref, *, final):
    _ffn_kernel(x_ref, g_ref, wa_ref, wb_ref, wd_ref, gf_ref, o_ref, xn_ref, final=final)
    j = pl.program_id(1)

    @pl.when(pl.program_id(0) == 0)
    def _():
        @pl.when(j == 0)
        def _():
            xs = xs_ref[...]
            xsn_ref[...] = _rms(xs, g_ref[...])
            os_ref[...] = xs

        xsn = xsn_ref[...]
        a = _dot(xsn, wa_ref[...])
        b = _dot(xsn, wb_ref[...])
        os_ref[...] += _dot(jax.nn.silu(a) * b, wd_ref[...])

        if final:
            @pl.when(j == pl.num_programs(1) - 1)
            def _():
                os_ref[...] = _rms(os_ref[...], gf_ref[...])


def ffn(x, g, w_gu, w_down, layer, g_final, final, tm, tf, xs=None):
    m, k = x.shape
    nf = D_FF // tf
    in_specs = [pl.BlockSpec((tm, k), lambda i, j: (i, 0), pipeline_mode=pl.Buffered(1)),
                pl.BlockSpec((1, k), lambda i, j: (0, 0)),
                pl.BlockSpec((None, k, tf), lambda i, j: (layer, 0, j)),
                pl.BlockSpec((None, k, tf), lambda i, j: (layer, 0, nf + j)),
                pl.BlockSpec((None, tf, k), lambda i, j: (layer, j, 0)),
                pl.BlockSpec((1, k), lambda i, j: (0, 0))]
    o_spec = pl.BlockSpec((tm, k), lambda i, j: (i, 0))
    scratch = [pltpu.VMEM((tm, k), _lhs_dtype(tm))]
    args = (x, g.reshape(1, k), w_gu, w_gu, w_down, g_final.reshape(1, k))
    if xs is None:
        return pl.pallas_call(
            functools.partial(_ffn_kernel, final=final),
            grid=(m // tm, nf),
            in_specs=in_specs,
            out_specs=o_spec,
            out_shape=jax.ShapeDtypeStruct((m, k), F32),
            scratch_shapes=scratch,
            compiler_params=_cparams("parallel", "arbitrary"),
            name="ffn",
        )(*args)
    ms = xs.shape[0]
    s_spec = pl.BlockSpec((ms, k), lambda i, j: (0, 0))
    return pl.pallas_call(
        functools.partial(_ffn2_kernel, final=final),
        grid=(m // tm, nf),
        in_specs=in_specs + [s_spec],
        out_specs=[o_spec, s_spec],
        out_shape=[jax.ShapeDtypeStruct((m, k), F32), jax.ShapeDtypeStruct((ms, k), F32)],
        scratch_shapes=scratch + [pltpu.VMEM((ms, k), F32)],
        compiler_params=_cparams("arbitrary", "arbitrary"),
        name="ffn2",
    )(*args, xs)


def _compress_kernel(tbl_ref, *refs, n_steps):
    pages = refs[:PAGES_PER_STEP]
    nxt_ref, extra_ref, w1_ref, pe_ref, w2_ref, o_ref = refs[PAGES_PER_STEP:]
    p = pl.program_id(1)
    on_extra = p == n_steps
    nxt_extra = p == n_steps - 1
    nh = CMP_STEP * N_KV
    width = CMP_STRIDE * HEAD_DIM
    d = HEAD_DIM

    def chunk(ref, n):
        return jnp.concatenate([ref[0, n * CMP_STRIDE + r] for r in range(CMP_STRIDE)], axis=1)

    chunks = []
    for k in range(PAGES_PER_STEP):
        for n in range(CH_PER_PAGE):
            x = chunk(pages[k], n)
            chunks.append(jnp.where(on_extra, chunk(extra_ref, n) if k == 0 else 0.0, x))
    chunks.append(jnp.where(on_extra, 0.0, jnp.where(nxt_extra, chunk(extra_ref, 0), chunk(nxt_ref, 0))))
    chunks.append(jnp.zeros((KV_SLOTS, width), F32))
    for c in range(2):
        sl = slice(c * N_KV, (c + 1) * N_KV)
        rows = [jnp.concatenate([chunks[i][sl], chunks[i + 1][sl]], axis=0) for i in range(0, len(chunks), 2)]
        tail = [pe_ref[c, 0], pe_ref[c, 1], jnp.zeros((6, width), F32)]
        lhs = jnp.concatenate(rows + tail, axis=0)
        n_rows = N_KV * len(chunks)
        hc = _dot(lhs, jnp.concatenate([w1_ref[c, 0], w1_ref[c, 1]], axis=1))
        pe_term = hc[n_rows:n_rows + 1, :d] + hc[n_rows + 1:n_rows + 2, d:]
        hid = jax.nn.gelu(hc[:nh, :d] + hc[N_KV:nh + N_KV, d:] + pe_term)
        o_ref[0, c] = _dot(hid, w2_ref[c])


def compress(pages, table, extra, cmp_pe, cmp_w1, cmp_w2):
    nb, n_pages = table.shape
    n_steps = n_pages // PAGES_PER_STEP
    w1 = cmp_w1.reshape(2, 2, CMP_STRIDE * HEAD_DIM, HEAD_DIM)
    pe = cmp_pe.reshape(2, 2, 1, CMP_STRIDE * HEAD_DIM)

    def page_map(k):
        return lambda b, p, tbl: (tbl[b, jnp.minimum(p * PAGES_PER_STEP + k, n_pages - 1)], 0, 0, 0)

    in_specs = [pl.BlockSpec((1, PAGE_SIZE, KV_SLOTS, HEAD_DIM), page_map(k)) for k in range(PAGES_PER_STEP)]
    in_specs += [pl.BlockSpec((1, CMP_STRIDE, KV_SLOTS, HEAD_DIM), page_map(PAGES_PER_STEP)),
                 pl.BlockSpec((1, PAGE_SIZE, KV_SLOTS, HEAD_DIM), lambda b, p, tbl: (b, 0, 0, 0)),
                 pl.BlockSpec(w1.shape, lambda b, p, tbl: (0, 0, 0, 0)),
                 pl.BlockSpec(pe.shape, lambda b, p, tbl: (0, 0, 0, 0)),
                 pl.BlockSpec(cmp_w2.shape, lambda b, p, tbl: (0, 0, 0))]
    n_out = CMP_STEP * (n_steps + 1)
    out = pl.pallas_call(
        functools.partial(_compress_kernel, n_steps=n_steps),
        grid_spec=pltpu.PrefetchScalarGridSpec(
            num_scalar_prefetch=1,
            grid=(nb, n_steps + 1),
            in_specs=in_specs,
            out_specs=pl.BlockSpec((1, 2, CMP_STEP * N_KV, HEAD_DIM), lambda b, p, tbl: (b, 0, p, 0)),
        ),
        out_shape=jax.ShapeDtypeStruct((nb, 2, n_out * N_KV, HEAD_DIM), F32),
        compiler_params=_cparams("parallel", "arbitrary"),
        name="compress",
    )(table, *([pages] * (PAGES_PER_STEP + 1)), extra, w1, pe, cmp_w2)
    return jnp.transpose(out.reshape(nb, 2, n_out, N_KV, HEAD_DIM), (0, 1, 3, 2, 4))


def _cover(n_pad, j_pad, n_cmp):
    n = lax.broadcasted_iota(jnp.int32, (n_pad, j_pad), 0)
    j = lax.broadcasted_iota(jnp.int32, (n_pad, j_pad), 1)
    hit = (n * CMP_STRIDE < j * SLC_BLOCK + SLC_BLOCK) & (n * CMP_STRIDE + CMP_BLOCK - 1 >= j * SLC_BLOCK)
    return (hit & (n < n_cmp)).astype(F32)


def _cover_t(j_pad, n_pad, n_cmp):
    j = lax.broadcasted_iota(jnp.int32, (j_pad, n_pad), 0)
    n = lax.broadcasted_iota(jnp.int32, (j_pad, n_pad), 1)
    hit = (n * CMP_STRIDE < j * SLC_BLOCK + SLC_BLOCK) & (n * CMP_STRIDE + CMP_BLOCK - 1 >= j * SLC_BLOCK)
    return (hit & (n < n_cmp)).astype(F32)


def _softmax_rows(s):
    e = jnp.exp(s - jnp.max(s, axis=-1, keepdims=True))
    return e / jnp.sum(e, axis=-1, keepdims=True)


def _nsa_prompt_kernel(q_ref, gl_ref, bg_ref, cos_ref, sin_ref, ckv_ref, ks_ref, vs_ref, kw_ref, vw_ref,
                       o_ref, qs_ref, drop_ref, *, n_cmp, n_slc, kt):
    qb = pl.program_id(1)
    tq = Q_BLOCK
    cos2 = cos_ref[...]
    sin2 = sin_ref[...]
    n_pad = ckv_ref.shape[3]
    qpos1 = qb * tq + lax.broadcasted_iota(jnp.int32, (tq, 1), 0)
    qpos = jnp.concatenate([qpos1] * GQA, axis=0)
    gates = jax.nn.sigmoid(gl_ref[...] + bg_ref[...])

    n_i = lax.broadcasted_iota(jnp.int32, (1, n_pad), 1)
    vis_c = (n_i * CMP_STRIDE + CMP_BLOCK - 1 <= qpos) & (n_i < n_cmp)
    any_c = (qpos >= CMP_BLOCK - 1).astype(F32)
    cover_t = _cover_t(n_slc, n_pad, n_cmp)
    j_col = lax.broadcasted_iota(jnp.int32, (n_slc, 1), 0)
    qblk = jnp.right_shift(qb * tq + lax.broadcasted_iota(jnp.int32, (1, tq), 1), SLC_SHIFT)
    forced = (j_col == 0) | (j_col == qblk) | (j_col == qblk - 1)
    valid = j_col <= qblk
    n_sel = min(N_SELECT, n_slc)
    j_rows = lax.broadcasted_iota(jnp.int32, (HEAD_DIM, kt), 0)

    w_lo = jnp.maximum(qb * tq - WINDOW, 0)
    w_lo = pl.multiple_of(w_lo, tq)
    kpos_w = w_lo + lax.broadcasted_iota(jnp.int32, (1, WINDOW + tq), 1)
    vis_w = (kpos_w <= qpos) & (qpos - kpos_w < WINDOW)

    def gate_col(hd, br):
        return gates[:, hd * N_BRANCH + br:hd * N_BRANCH + br + 1]

    for h in range(N_KV):
        q = jnp.concatenate(
            [_rope(q_ref[:, (h * GQA + g) * HEAD_DIM:(h * GQA + g + 1) * HEAD_DIM], cos2, sin2)
             for g in range(GQA)], axis=0)
        qs = (q * SCALE).astype(BF16)
        qs_ref[h] = qs
        hs = slice(h * HEAD_DIM, (h + 1) * HEAD_DIM)

        s = jnp.where(vis_c, _dot_t(qs, ckv_ref[0, 0, h]), NEG)
        p = _softmax_rows(s) * any_c
        o_c = _dot(p, ckv_ref[0, 1, h])

        p_sum = p[0:tq] + p[tq:2 * tq] + p[2 * tq:3 * tq]
        score = lax.dot_general(cover_t, p_sum, (((1,), (1,)), ((), ())), preferred_element_type=F32,
                                precision=lax.Precision.HIGHEST)
        score = jnp.where(forced, jnp.inf, jnp.where(valid, score, -jnp.inf))
        rank = jnp.zeros((n_slc, tq), F32)
        for i in range(n_slc):
            row = score[i:i + 1, :]
            beats = (row > score) | ((row == score) & (i < j_col))
            rank = rank + jnp.where(beats, 1.0, 0.0)
        drop = jnp.where(rank >= n_sel, 1.0, 0.0)
        drop = jnp.transpose(jnp.concatenate([drop, jnp.zeros((HEAD_DIM - n_slc, tq), F32)], axis=0))
        drop_ref[h] = drop.astype(BF16)

        s = jnp.where(vis_w, _dot_t(qs, kw_ref[pl.ds(w_lo, WINDOW + tq), hs]), NEG)
        e = jnp.exp(s - jnp.max(s, axis=-1, keepdims=True))
        o_w = _dot(e, vw_ref[pl.ds(w_lo, WINDOW + tq), hs]) / jnp.sum(e, axis=-1, keepdims=True)

        for g in range(GQA):
            hd = h * GQA + g
            r = slice(g * tq, (g + 1) * tq)
            o_ref[:, hd * HEAD_DIM:(hd + 1) * HEAD_DIM] = gate_col(hd, 0) * o_c[r] + gate_col(hd, 2) * o_w[r]

    def slc_step(t, carry):
        k0 = pl.multiple_of(t * kt, kt)
        kpos = k0 + lax.broadcasted_iota(jnp.int32, (1, kt), 1)
        expand = jnp.where(jnp.right_shift(kpos, SLC_SHIFT) == j_rows, NEG, 0.0).astype(BF16)
        causal = jnp.where(kpos <= qpos1, 0.0, NEG)
        out = []
        for h in range(N_KV):
            m, l, acc = carry[h]
            hs = slice(h * HEAD_DIM, (h + 1) * HEAD_DIM)
            bias = jnp.dot(drop_ref[h], expand, preferred_element_type=F32) + causal
            s = _dot_t(qs_ref[h], ks_ref[pl.ds(k0, kt), hs]) + jnp.concatenate([bias] * GQA, axis=0)
            m_new = jnp.maximum(m, jnp.max(s, axis=-1, keepdims=True))
            a = jnp.exp(m - m_new)
            e = jnp.exp(s - m_new)
            l = l * a + jnp.sum(e, axis=-1, keepdims=True)
            acc = acc * a + _dot(e, vs_ref[pl.ds(k0, kt), hs])
            out.append((m_new, l, acc))
        return tuple(out)

    n_t = (qb * tq + tq + kt - 1) // kt
    init = (jnp.full((GQA * tq, 1), NEG, F32), jnp.zeros((GQA * tq, 1), F32), jnp.zeros((GQA * tq, HEAD_DIM), F32))
    done = lax.fori_loop(0, n_t, slc_step, (init,) * N_KV)
    for h in range(N_KV):
        _, l, acc = done[h]
        o_s = acc / l
        for g in range(GQA):
            hd = h * GQA + g
            r = slice(g * tq, (g + 1) * tq)
            o_ref[:, hd * HEAD_DIM:(hd + 1) * HEAD_DIM] += gate_col(hd, 1) * o_s[r]


def nsa_prompt(u, b_gate_pad, cos2, sin2, ckv, slc_k, slc_v, win_k, win_v, batch, seq):
    nqb = seq // Q_BLOCK
    n_pad = ckv.shape[3]
    kspec = pl.BlockSpec((seq, N_KV * HEAD_DIM), lambda b, i: (b, 0))
    return pl.pallas_call(
        functools.partial(_nsa_prompt_kernel, n_cmp=seq // CMP_STRIDE - 1, n_slc=seq // SLC_BLOCK, kt=512),
        grid=(batch, nqb),
        in_specs=[pl.BlockSpec((Q_BLOCK, MAIN_W), lambda b, i: (b * nqb + i, 0)),
                  pl.BlockSpec((Q_BLOCK, GATE_PAD), lambda b, i: (b * nqb + i, (MAIN_W + MEM_W) // GATE_PAD)),
                  pl.BlockSpec((1, GATE_PAD), lambda b, i: (0, 0)),
                  pl.BlockSpec((Q_BLOCK, HEAD_DIM), lambda b, i: (i, 0)),
                  pl.BlockSpec((Q_BLOCK, HEAD_DIM), lambda b, i: (i, 0)),
                  pl.BlockSpec((1, 2, N_KV, n_pad, HEAD_DIM), lambda b, i: (b, 0, 0, 0, 0)),
                  kspec, kspec, kspec, kspec],
        out_specs=pl.BlockSpec((Q_BLOCK, MAIN_W), lambda b, i: (b * nqb + i, 0)),
        out_shape=jax.ShapeDtypeStruct((batch * seq, MAIN_W), F32),
        scratch_shapes=[pltpu.VMEM((N_KV, GQA * Q_BLOCK, HEAD_DIM), BF16),
                        pltpu.VMEM((N_KV, Q_BLOCK, HEAD_DIM), BF16)],
        compiler_params=_cparams("parallel", "arbitrary"),
        name="nsa_prompt",
    )(u, u, b_gate_pad, cos2, sin2, ckv, slc_k, slc_v, win_k, win_v)


HEAD_ROWS = 16


def _head_mask(h):
    r = lax.broadcasted_iota(jnp.int32, (HEAD_ROWS, 1), 0)
    return (r >= h * GQA) & (r < (h + 1) * GQA)


def _nsa_sample_a_kernel(q_ref, cos_ref, sin_ref, ckv_ref, wkv_ref, wnew_ref, qr_ref, oc_ref, ow_ref, sel_ref,
                         *, n_cmp, n_slc, j_pad):
    n_pad = ckv_ref.shape[3]
    q = _rope(q_ref[0], cos_ref[...], sin_ref[...])
    qr_ref[0] = q
    qs = q * SCALE
    n_i = lax.broadcasted_iota(jnp.int32, (1, n_pad), 1)
    vis_c = (n_i * CMP_STRIDE + CMP_BLOCK - 1 <= PAST_LEN) & (n_i < n_cmp)
    cover = _cover(n_pad, j_pad, n_cmp)
    j_row = lax.broadcasted_iota(jnp.int32, (1, j_pad), 1)
    j_col = lax.broadcasted_iota(jnp.int32, (j_pad, 1), 0)
    qblk = PAST_LEN // SLC_BLOCK
    lane = lax.broadcasted_iota(jnp.int32, (N_SELECT, HEAD_DIM), 1)
    r_col = lax.broadcasted_iota(jnp.int32, (N_SELECT, 1), 0)
    n_win = wkv_ref.shape[1] // KV_SLOTS
    kpos_w = PAST_LEN - n_win + lax.broadcasted_iota(jnp.int32, (1, n_win), 1)
    vis_w = (kpos_w <= PAST_LEN) & (PAST_LEN - kpos_w < WINDOW)

    o_c = jnp.zeros((HEAD_ROWS, HEAD_DIM), F32)
    o_w = jnp.zeros((HEAD_ROWS, HEAD_DIM), F32)
    sel_out = jnp.zeros((N_SELECT, HEAD_DIM), jnp.int32)
    for h in range(N_KV):
        mine = _head_mask(h)
        hs = slice(h * HEAD_DIM, (h + 1) * HEAD_DIM)
        vsl = slice((N_KV + h) * HEAD_DIM, (N_KV + h + 1) * HEAD_DIM)

        s = jnp.where(vis_c, _dot_t(qs, ckv_ref[0, 0, h]), NEG)
        p = _softmax_rows(s) * float(PAST_LEN >= CMP_BLOCK - 1)
        o_c = jnp.where(mine, _dot(p, ckv_ref[0, 1, h]), o_c)

        p_sum = jnp.sum(jnp.where(mine, p, 0.0), axis=0, keepdims=True)
        score = jnp.dot(jnp.broadcast_to(p_sum, (8, n_pad)), cover, preferred_element_type=F32,
                        precision=lax.Precision.HIGHEST)
        forced = (j_row == 0) | (j_row == qblk) | (j_row == qblk - 1)
        score = jnp.where(forced, jnp.inf, jnp.where(j_row <= qblk, score, -jnp.inf))
        score = jnp.where(j_row < n_slc, score, -jnp.inf)
        s_col = jnp.transpose(score)[:, 0:1]
        beats = ((s_col > score[0:1]) | ((s_col == score[0:1]) & (j_col < j_row))) & (j_col < n_slc)
        rank = jnp.sum(beats.astype(F32), axis=0, keepdims=True)
        hit = (rank == r_col.astype(F32)) & (j_row < n_slc)
        idx = jnp.sum(jnp.where(hit, j_row.astype(F32), 0.0), axis=1, keepdims=True)
        sel_out = jnp.where(lane == h, idx.astype(jnp.int32), sel_out)

        k_win = wkv_ref[0, pl.ds(h, n_win, stride=KV_SLOTS), :]
        v_win = wkv_ref[0, pl.ds(N_KV + h, n_win, stride=KV_SLOTS), :]
        s = jnp.where(vis_w, _dot_t(qs, k_win), NEG)
        s_new = jnp.sum(qs * wnew_ref[0, :, hs], axis=-1, keepdims=True)
        m = jnp.maximum(jnp.max(s, axis=-1, keepdims=True), s_new)
        e = jnp.exp(s - m)
        e_new = jnp.exp(s_new - m)
        o = (_dot(e, v_win) + e_new * wnew_ref[0, :, vsl]) / (jnp.sum(e, axis=-1, keepdims=True) + e_new)
        o_w = jnp.where(mine, o, o_w)
    oc_ref[0] = o_c
    ow_ref[0] = o_w
    sel_ref[0] = sel_out


def nsa_sample_a(q16, cos2, sin2, ckv, win_cache, win_new, n_cmp, n_slc):
    db = q16.shape[0]
    n_pad = ckv.shape[3]
    win_rows = win_cache.shape[1]
    j_pad = -(-n_slc // 128) * 128
    hspec = pl.BlockSpec((1, HEAD_ROWS, HEAD_DIM), lambda b: (b, 0, 0))
    hout = jax.ShapeDtypeStruct((db, HEAD_ROWS, HEAD_DIM), F32)
    return pl.pallas_call(
        functools.partial(_nsa_sample_a_kernel, n_cmp=n_cmp, n_slc=n_slc, j_pad=j_pad),
        grid=(db,),
        in_specs=[hspec,
                  pl.BlockSpec((1, HEAD_DIM), lambda b: (0, 0)),
                  pl.BlockSpec((1, HEAD_DIM), lambda b: (0, 0)),
                  pl.BlockSpec((1, 2, N_KV, n_pad, HEAD_DIM), lambda b: (b, 0, 0, 0, 0)),
                  pl.BlockSpec((1, win_rows, HEAD_DIM), lambda b: (b, 0, 0)),
                  pl.BlockSpec((1, 1, KV_W), lambda b: (b, 0, 0))],
        out_specs=[hspec, hspec, hspec, pl.BlockSpec((1, N_SELECT, HEAD_DIM), lambda b: (b, 0, 0))],
        out_shape=[hout, hout, hout, jax.ShapeDtypeStruct((db, N_SELECT, HEAD_DIM), jnp.int32)],
        compiler_params=_cparams("arbitrary"),
        name="nsa_sample_a",
    )(q16, cos2, sin2, ckv, win_cache, win_new)


def _nsa_sample_b_kernel(sel_ref, tbl_ref, q_ref, *refs):
    blocks = refs[:N_SELECT]
    new_ref, oc_ref, ow_ref, gl_ref, bg_ref, o_ref, os_ref = refs[N_SELECT:]
    b = pl.program_id(0)
    h = pl.program_id(1)
    n_keys = N_SELECT * SLC_BLOCK
    lane_blk = jnp.right_shift(lax.broadcasted_iota(jnp.int32, (1, n_keys), 1), SLC_SHIFT)
    new_k = new_ref[0, pl.ds(h, 1), :]
    new_v = new_ref[0, pl.ds(h + N_KV, 1), :]
    ks, vs = [], []
    blk_of = jnp.zeros((1, n_keys), jnp.int32)
    for r in range(N_SELECT):
        blk = sel_ref[b, h, r]
        in_past = blk * SLC_BLOCK < PAST_LEN
        ks.append(jnp.where(in_past, blocks[r][0, pl.ds(h, SLC_BLOCK, stride=KV_SLOTS), :], new_k))
        vs.append(jnp.where(in_past, blocks[r][0, pl.ds(h + N_KV, SLC_BLOCK, stride=KV_SLOTS), :], new_v))
        blk_of = jnp.where(lane_blk == r, blk, blk_of)
    kpos = blk_of * SLC_BLOCK + (lax.broadcasted_iota(jnp.int32, (1, n_keys), 1) & (SLC_BLOCK - 1))
    s = jnp.where(kpos <= PAST_LEN, _dot_t(q_ref[0] * SCALE, jnp.concatenate(ks, axis=0)), NEG)
    e = jnp.exp(s - jnp.max(s, axis=-1, keepdims=True))
    o_s = _dot(e, jnp.concatenate(vs, axis=0)) / jnp.sum(e, axis=-1, keepdims=True)

    @pl.when(h == 0)
    def _():
        os_ref[...] = jnp.zeros(os_ref.shape, F32)

    row = lax.broadcasted_iota(jnp.int32, (HEAD_ROWS, 1), 0)
    mine = (row >= h * GQA) & (row < (h + 1) * GQA)
    os_ref[...] = jnp.where(mine, o_s, os_ref[...])

    @pl.when(h == pl.num_programs(1) - 1)
    def _():
        gates = jax.nn.sigmoid(gl_ref[0] + bg_ref[...])
        o_ref[0] = gates[:, 0:1] * oc_ref[0] + gates[:, 1:2] * os_ref[...] + gates[:, 2:3] * ow_ref[0]


def nsa_sample_b(sel, page_table, q_rope, slc_halves, slc_new, o_c, o_w, gate_logits, b_gate16):
    db = q_rope.shape[0]
    last_blk = PAST_LEN // SLC_BLOCK - 1
    halves = PAGE_SIZE // SLC_BLOCK

    def cache_map(r):
        def index(b, h, sel_ref, tbl_ref):
            j = jnp.minimum(sel_ref[b, h, r], last_blk)
            return (tbl_ref[b, j // halves] * halves + j % halves, 0, 0)
        return index

    hspec = pl.BlockSpec((1, HEAD_ROWS, HEAD_DIM), lambda b, h, s, t: (b, 0, 0))
    return pl.pallas_call(
        _nsa_sample_b_kernel,
        grid_spec=pltpu.PrefetchScalarGridSpec(
            num_scalar_prefetch=2,
            grid=(db, N_KV),
            in_specs=[hspec]
            + [pl.BlockSpec((1, SLC_BLOCK * KV_SLOTS, HEAD_DIM), cache_map(r)) for r in range(N_SELECT)]
            + [pl.BlockSpec((1, KV_SLOTS, HEAD_DIM), lambda b, h, s, t: (b, 0, 0)),
               hspec, hspec, hspec,
               pl.BlockSpec((HEAD_ROWS, HEAD_DIM), lambda b, h, s, t: (0, 0))],
            out_specs=hspec,
            scratch_shapes=[pltpu.VMEM((HEAD_ROWS, HEAD_DIM), F32)],
        ),
        out_shape=jax.ShapeDtypeStruct((db, HEAD_ROWS, HEAD_DIM), F32),
        compiler_params=_cparams("arbitrary", "arbitrary"),
        name="nsa_sample_b",
    )(sel, page_table, q_rope, *([slc_halves] * N_SELECT), slc_new, o_c, o_w, gate_logits, b_gate16)


def _rope_tables(pos):
    half = HEAD_DIM // 2
    inv = ROPE_THETA ** (-jnp.arange(half, dtype=F32) / half)
    ang = pos.astype(F32)[:, None] * inv[None, :]
    cos, sin = jnp.cos(ang), jnp.sin(ang)
    return jnp.concatenate([cos, cos], -1), jnp.concatenate([-sin, sin], -1)


def _pad_w_in_b(w):
    return jnp.concatenate([w[..., :MAIN_W], w[..., MAIN_W + GATE_W:],
                            jnp.pad(w[..., MAIN_W:MAIN_W + GATE_W], ((0, 0), (0, 0), (0, GATE_PAD - GATE_W)))], axis=-1)


def kernel(x_prompt, x_sample, mem_prompt, state_pool, cache_cmp_kv, cache_slc_kv, cache_win_kv, cache_mem_kv, page_table, g_mix, w_in_a, pool_grp_w, pool_scale, w_in_b, b_gate, g_kv, w_kv, cmp_pe, cmp_w1, cmp_w2, g_mem, w_mem_kv, w_out, g_ffn, w_gu, w_down, g_final):
    batch, seq, _ = x_prompt.shape
    db = x_sample.shape[0]
    m = batch * seq
    tm = PROJ_TM
    n_phys = cache_cmp_kv.shape[0]
    w_in_b_pad = _pad_w_in_b(w_in_b)
    xp = x_prompt.reshape(m, D_MODEL)
    xs = x_sample.reshape(db, D_MODEL)
    mem2 = mem_prompt.reshape(batch * MEM_TOKENS, D_MODEL)
    mem_kv = [rms_matmul(mem2, g_mem[l], w_mem_kv, l, 512, 512) for l in range(DEPTH)]
    mem_cache = cache_mem_kv.reshape(-1, MEM_HEAD_DIM)
    cos_p, sin_p = _rope_tables(jnp.arange(seq, dtype=jnp.int32))
    cos_s, sin_s = _rope_tables(jnp.full((db,), PAST_LEN, jnp.int32))
    pools_p, pools_s = [], []
    for l in range(DEPTH):
        if l == N_A:
            cmp_kv, slc_kv, win_kv, slc_k, slc_v, win_k, win_v = kv_proj(xp, g_kv, w_kv, cos_p, sin_p, 512)
            table = jnp.arange(batch * seq // PAGE_SIZE, dtype=jnp.int32).reshape(batch, seq // PAGE_SIZE)
            ckv_p = compress(cmp_kv.reshape(-1, PAGE_SIZE, KV_SLOTS, HEAD_DIM), table,
                             jnp.zeros((batch, PAGE_SIZE, KV_SLOTS, HEAD_DIM), F32), cmp_pe, cmp_w1, cmp_w2)
            cmp_new, slc_new, win_new = kv_proj(xs, g_kv, w_kv, cos_s, sin_s, db)[:N_BRANCH]
            extra = jnp.pad(cmp_new.reshape(db, 1, KV_SLOTS, HEAD_DIM), ((0, 0), (0, PAGE_SIZE - 1), (0, 0), (0, 0)))
            ckv_s = compress(cache_cmp_kv.reshape(n_phys, PAGE_SIZE, KV_SLOTS, HEAD_DIM), page_table, extra,
                             cmp_pe, cmp_w1, cmp_w2)
            t_full = -(-(PAST_LEN + 1) // SLC_BLOCK) * SLC_BLOCK
            n_cmp = t_full // CMP_STRIDE - 1
            n_slc = t_full // SLC_BLOCK
        if l < N_A:
            up, us = rms_matmul(xp, g_mix[l], w_in_a, l, tm, 512, xs=xs)
            u3 = up.reshape(batch, seq, -1)
            zp = pool_prompt(u3, pool_grp_w[l], pool_scale[l]).reshape(m, MAIN_W)
            pools_p.append(u3[:, seq - POOL_BUF:, :MAIN_W])
            zs = pool_sample(us, jnp.transpose(state_pool[l], (1, 0, 2)), pool_grp_w[l], pool_scale[l])
            pools_s.append(jnp.concatenate([state_pool[l][:, 1:], us[:, None, :MAIN_W]], axis=1))
        else:
            j = l - N_A
            up, us = rms_matmul(xp, g_mix[l], w_in_b_pad, j, tm, 768, xs=xs)
            u3 = up.reshape(batch, seq, -1)
            bg = jnp.pad(b_gate[j], (0, GATE_PAD - GATE_W)).reshape(1, GATE_PAD)
            zp = nsa_prompt(up, bg, cos_p, sin_p, ckv_p, slc_k, slc_v, win_k, win_v, batch, seq)
            q16 = jnp.pad(us[:, :MAIN_W].reshape(db, N_HEADS, HEAD_DIM), ((0, 0), (0, HEAD_ROWS - N_HEADS), (0, 0)))
            q_rope, o_c, o_w, sel = nsa_sample_a(
                q16, cos_s[:1], sin_s[:1], ckv_s, cache_win_kv.reshape(db, -1, HEAD_DIM),
                win_new.reshape(db, 1, KV_W), n_cmp, n_slc)
            sel = jnp.transpose(sel[:, :, :N_KV], (0, 2, 1))
            gl = us[:, MAIN_W + MEM_W:MAIN_W + MEM_W + GATE_W].reshape(db, N_HEADS, N_BRANCH)
            gl = jnp.pad(gl, ((0, 0), (0, HEAD_ROWS - N_HEADS), (0, HEAD_DIM - N_BRANCH)))
            bg16 = jnp.pad(b_gate[j].reshape(N_HEADS, N_BRANCH), ((0, HEAD_ROWS - N_HEADS), (0, HEAD_DIM - N_BRANCH)))
            z16 = nsa_sample_b(sel, page_table, q_rope,
                               cache_slc_kv.reshape(n_phys * 2, SLC_BLOCK * KV_SLOTS, HEAD_DIM),
                               slc_new.reshape(db, KV_SLOTS, HEAD_DIM), o_c, o_w, gl, bg16)
            zs = z16[:, :N_HEADS].reshape(db, MAIN_W)
        mo_p = mem_attend(u3, mem_kv[l], 512).reshape(m, MEM_W)
        mo_s = mem_attend(us.reshape(db, 1, -1), mem_cache, 1, layer=l).reshape(db, MEM_W)
        xp, xs = out_proj(zp, mo_p, w_out, l, xp, tm, 512, sample=(zs, mo_s, xs))
        xp, xs = ffn(xp, g_ffn[l], w_gu, w_down, l, g_final, l == DEPTH - 1, FFN_TM, FFN_TF, xs=xs)
    kv5 = (batch, seq, 2, N_KV, HEAD_DIM)
    kv5s = (db, 1, 2, N_KV, HEAD_DIM)
    n_keep = min(WINDOW, seq)
    win_s = jnp.concatenate([cache_win_kv, win_new.reshape(kv5s)], axis=1)
    win_s = win_s[:, win_s.shape[1] - min(WINDOW, win_s.shape[1]):]
    return (xp.reshape(batch, seq, D_MODEL), xs.reshape(db, 1, D_MODEL), jnp.stack(pools_p),
            cmp_kv.reshape(kv5), slc_kv.reshape(kv5), win_kv.reshape(kv5)[:, seq - n_keep:],
            jnp.stack(mem_kv).reshape(DEPTH, batch, MEM_TOKENS, 2, N_MEM_HEADS, MEM_HEAD_DIM),
            jnp.stack(pools_s), cmp_new.reshape(kv5s), slc_new.reshape(kv5s), win_s)
```

```python
import functools

import jax
import jax.numpy as jnp
from jax import lax
from jax.experimental import pallas as pl
from jax.experimental.pallas import tpu as pltpu

F32 = jnp.float32
BF16 = jnp.bfloat16

D_MODEL = 2048
DEPTH = 4
N_A = 2
PAST_LEN = 16384
PAGE_SIZE = 128
MEM_TOKENS = 256
N_MEM_HEADS = 4
MEM_HEAD_DIM = 128
MEM_W = 512
MAIN_W = 1536
POOL_WINDOWS = (2, 4, 8, 16)
POOL_GC = 384
POOL_BUF = 15
HEAD_DIM = 128
N_HEADS = 12
N_KV = 4
GQA = 3
N_BRANCH = 3
GATE_W = 36
GATE_PAD = 256
CMP_BLOCK = 32
CMP_STRIDE = 16
SLC_BLOCK = 64
SLC_SHIFT = 6
N_SELECT = 16
WINDOW = 512
Q_BLOCK = 128
D_FF = 5632
ROPE_THETA = 10000.0
EPS = 1e-6
NEG = -1e30
SCALE = HEAD_DIM ** -0.5
MEM_SCALE = MEM_HEAD_DIM ** -0.5
KV_SLOTS = 2 * N_KV
KV_W = KV_SLOTS * HEAD_DIM
W_IN_B = MAIN_W + MEM_W + GATE_PAD

VMEM_LIMIT = 56 * 1024 * 1024
SLC_KT = 512
LOG2E = 1.4426950408889634
PROJ_TM = 1024
FFN_TM = 1024
FFN_TF = 256
PAGES_PER_STEP = 16
CH_PER_PAGE = PAGE_SIZE // CMP_STRIDE
CMP_STEP = PAGES_PER_STEP * CH_PER_PAGE


def _cparams(*sem):
    return pltpu.CompilerParams(dimension_semantics=sem, vmem_limit_bytes=VMEM_LIMIT)


def _rms(x, g):
    return x * lax.rsqrt(jnp.mean(x * x, axis=-1, keepdims=True) + EPS) * g


def _dot(a, b):
    return jnp.dot(a.astype(BF16), b.astype(BF16), preferred_element_type=F32)


def _dot_t(a, b):
    return lax.dot_general(a.astype(BF16), b.astype(BF16), (((1,), (1,)), ((), ())),
                           preferred_element_type=F32)


def _lhs_dtype(rows):
    return BF16 if rows % 16 == 0 else F32


def _rope(x, cos2, sin2):
    return x * cos2 + pltpu.roll(x, HEAD_DIM // 2, axis=1) * sin2


def _rms_matmul_kernel(x_ref, g_ref, w_ref, o_ref, xn_ref, wb_ref):
    i = pl.program_id(0)
    j = pl.program_id(1)

    @pl.when(j == 0)
    def _():
        xn_ref[...] = _rms(x_ref[...], g_ref[...]).astype(xn_ref.dtype)

    @pl.when(i == 0)
    def _():
        wb_ref[j] = w_ref[...].astype(BF16)

    o_ref[...] = _dot(xn_ref[...], wb_ref[j])


def _rms_matmul2_kernel(x_ref, xs_ref, g_ref, w_ref, o_ref, os_ref, xn_ref, xsn_ref, wb_ref):
    _rms_matmul_kernel(x_ref, g_ref, w_ref, o_ref, xn_ref, wb_ref)
    i = pl.program_id(0)
    j = pl.program_id(1)

    @pl.when((i == 0) & (j == 0))
    def _():
        xsn_ref[...] = _rms(xs_ref[...], g_ref[...])

    @pl.when(i == 0)
    def _():
        os_ref[...] = _dot(xsn_ref[...], wb_ref[j])


def _once_per_column(nj):
    return lambda i, j: jnp.where(i == 0, j, nj - 1)


def rms_matmul(x, g, w, layer, tm, tn, xs=None):
    m, k = x.shape
    n = w.shape[2]
    nj = n // tn
    col = _once_per_column(nj)
    x_spec = pl.BlockSpec((tm, k), lambda i, j: (i, 0))
    g_spec = pl.BlockSpec((1, k), lambda i, j: (0, 0))
    w_spec = pl.BlockSpec((None, k, tn), lambda i, j: (layer, 0, col(i, j)))
    o_spec = pl.BlockSpec((tm, tn), lambda i, j: (i, j))
    scratch = [pltpu.VMEM((tm, k), _lhs_dtype(tm)), pltpu.VMEM((nj, k, tn), BF16)]
    if xs is None:
        return pl.pallas_call(
            _rms_matmul_kernel,
            grid=(m // tm, nj),
            in_specs=[x_spec, g_spec, w_spec],
            out_specs=o_spec,
            out_shape=jax.ShapeDtypeStruct((m, n), F32),
            scratch_shapes=scratch,
            compiler_params=_cparams("arbitrary", "arbitrary"),
            name="rms_matmul",
        )(x, g.reshape(1, k), w)
    ms = xs.shape[0]
    return pl.pallas_call(
        _rms_matmul2_kernel,
        grid=(m // tm, nj),
        in_specs=[x_spec, pl.BlockSpec((ms, k), lambda i, j: (0, 0)), g_spec, w_spec],
        out_specs=[o_spec, pl.BlockSpec((ms, tn), lambda i, j: (0, col(i, j)))],
        out_shape=[jax.ShapeDtypeStruct((m, n), F32), jax.ShapeDtypeStruct((ms, n), F32)],
        scratch_shapes=[scratch[0], pltpu.VMEM((ms, k), F32), scratch[1]],
        compiler_params=_cparams("arbitrary", "arbitrary"),
        name="rms_matmul2",
    )(x, xs, g.reshape(1, k), w)


def _kv_proj_kernel(x_ref, g_ref, w_ref, cos_ref, sin_ref, cmp_ref, slc_ref, win_ref, *rest):
    xn_ref, wb_ref = rest[-2:]
    attn = rest[:-2]
    i = pl.program_id(0)
    j = pl.program_id(1)
    half = N_KV * HEAD_DIM

    @pl.when(j == 0)
    def _():
        xn_ref[...] = _rms(x_ref[...], g_ref[...]).astype(xn_ref.dtype)

    @pl.when(i == 0)
    def _():
        wb_ref[j] = w_ref[...].astype(BF16)

    y = _dot(xn_ref[...], wb_ref[j])
    for br, ref in enumerate((cmp_ref, slc_ref, win_ref)):
        for c in range(2):
            @pl.when(j == 2 * br + c)
            def _(ref=ref, br=br, c=c):
                val = y
                if c == 0:
                    cos2 = cos_ref[...]
                    sin2 = sin_ref[...]
                    val = jnp.concatenate(
                        [_rope(y[:, h * HEAD_DIM:(h + 1) * HEAD_DIM], cos2, sin2) for h in range(N_KV)], axis=1)
                ref[:, c * half:(c + 1) * half] = val
                if attn and br == 1 and c == 1:
                    attn[1][0] = jnp.transpose(val).astype(BF16)
                elif attn and br > 0:
                    attn[2 * (br - 1) + c][...] = val.astype(BF16)


def kv_proj(x, g, w, cos2, sin2, tm, attn_copies):
    m, k = x.shape
    half = N_KV * HEAD_DIM
    nt = cos2.shape[0] // tm
    nj = 2 * N_BRANCH
    col = _once_per_column(nj)
    out = jax.ShapeDtypeStruct((m, KV_W), F32)
    ospec = pl.BlockSpec((tm, KV_W), lambda i, j: (i, 0))
    out_shape, out_specs = [out, out, out], [ospec, ospec, ospec]
    if attn_copies:
        out16 = jax.ShapeDtypeStruct((m, half), BF16)
        ospec16 = pl.BlockSpec((tm, half), lambda i, j: (i, 0))
        out_shape += [out16, jax.ShapeDtypeStruct((m // tm, half, tm), BF16), out16, out16]
        out_specs += [ospec16, pl.BlockSpec((1, half, tm), lambda i, j: (i, 0, 0)), ospec16, ospec16]
    return pl.pallas_call(
        _kv_proj_kernel,
        grid=(m // tm, nj),
        in_specs=[pl.BlockSpec((tm, k), lambda i, j: (i, 0)),
                  pl.BlockSpec((1, k), lambda i, j: (0, 0)),
                  pl.BlockSpec((k, half), lambda i, j: (0, col(i, j))),
                  pl.BlockSpec((tm, HEAD_DIM), lambda i, j: (i % nt, 0)),
                  pl.BlockSpec((tm, HEAD_DIM), lambda i, j: (i % nt, 0))],
        out_specs=out_specs,
        out_shape=out_shape,
        scratch_shapes=[pltpu.VMEM((tm, k), _lhs_dtype(tm)), pltpu.VMEM((nj, k, half), BF16)],
        compiler_params=_cparams("arbitrary", "arbitrary"),
        name="kv_proj",
    )(x, g.reshape(1, k), w, cos2, sin2)


def _pool_prompt_kernel(cur_ref, prev_ref, w_ref, sc_ref, o_ref, *, ts):
    i = pl.program_id(1)
    keep = (i > 0).astype(F32)
    pos = (i * ts + lax.broadcasted_iota(jnp.int32, (ts, 1), 0) + 1).astype(F32)
    for g, w in enumerate(POOL_WINDOWS):
        sl = slice(g * POOL_GC, (g + 1) * POOL_GC)
        x = cur_ref[0, :, sl]
        ext = jnp.concatenate([prev_ref[0, :, sl] * keep, x], axis=0)
        span = 1
        while span < w:
            ext = ext[span:] + ext[:-span]
            span *= 2
        win = ext[17 - w:17 - w + ts]
        d = win / jnp.minimum(pos, float(w)) - x
        o_ref[0, :, sl] = _dot(d, w_ref[g]) * sc_ref[:, sl]


def pool_prompt(u3, w_grp, scale, ts=256):
    b, s, n = u3.shape
    r = ts // 16
    return pl.pallas_call(
        functools.partial(_pool_prompt_kernel, ts=ts),
        grid=(b, s // ts),
        in_specs=[pl.BlockSpec((1, ts, MAIN_W), lambda bi, i: (bi, i, 0)),
                  pl.BlockSpec((1, 16, MAIN_W), lambda bi, i: (bi, jnp.maximum(i * r - 1, 0), 0)),
                  pl.BlockSpec((4, POOL_GC, POOL_GC), lambda bi, i: (0, 0, 0)),
                  pl.BlockSpec((1, MAIN_W), lambda bi, i: (0, 0))],
        out_specs=pl.BlockSpec((1, ts, MAIN_W), lambda bi, i: (bi, i, 0)),
        out_shape=jax.ShapeDtypeStruct((b, s, MAIN_W), F32),
        compiler_params=_cparams("parallel", "arbitrary"),
        name="pool_prompt",
    )(u3, u3, w_grp, scale.reshape(1, MAIN_W))


def _pool_sample_kernel(cur_ref, buf_ref, w_ref, sc_ref, o_ref):
    for g, w in enumerate(POOL_WINDOWS):
        sl = slice(g * POOL_GC, (g + 1) * POOL_GC)
        x = cur_ref[:, sl]
        win = x
        for r in range(POOL_BUF - (w - 1), POOL_BUF):
            win = win + buf_ref[r, :, sl]
        d = win / float(w) - x
        o_ref[:, sl] = _dot(d, w_ref[g]) * sc_ref[:, sl]


def pool_sample(u, buf_t, w_grp, scale):
    db = u.shape[0]
    return pl.pallas_call(
        _pool_sample_kernel,
        grid=(1,),
        in_specs=[pl.BlockSpec((db, MAIN_W), lambda i: (0, 0)),
                  pl.BlockSpec((POOL_BUF, db, MAIN_W), lambda i: (0, 0, 0)),
                  pl.BlockSpec((4, POOL_GC, POOL_GC), lambda i: (0, 0, 0)),
                  pl.BlockSpec((1, MAIN_W), lambda i: (0, 0))],
        out_specs=pl.BlockSpec((db, MAIN_W), lambda i: (0, 0)),
        out_shape=jax.ShapeDtypeStruct((db, MAIN_W), F32),
        compiler_params=_cparams("arbitrary"),
        name="pool_sample",
    )(u, buf_t, w_grp, scale.reshape(1, MAIN_W))


def _mem_attend_kernel(q_ref, kv_ref, o_ref, *, tq, slots):
    rows = max(tq, 8)
    logits, values = [], []
    for h in range(N_MEM_HEADS):
        sl = slice(h * MEM_HEAD_DIM, (h + 1) * MEM_HEAD_DIM)
        q = q_ref[0, :, sl]
        if tq < rows:
            q = jnp.broadcast_to(q[0:1], (rows, MEM_HEAD_DIM))
        if slots:
            k = kv_ref[pl.ds(h, MEM_TOKENS, stride=2 * N_MEM_HEADS), :]
            v = kv_ref[pl.ds(N_MEM_HEADS + h, MEM_TOKENS, stride=2 * N_MEM_HEADS), :]
        else:
            k = kv_ref[:, sl]
            v = kv_ref[:, MEM_W + h * MEM_HEAD_DIM:MEM_W + (h + 1) * MEM_HEAD_DIM]
        logits.append(_dot_t(q, k) * MEM_SCALE)
        values.append(v)
    for h in range(N_MEM_HEADS):
        sl = slice(h * MEM_HEAD_DIM, (h + 1) * MEM_HEAD_DIM)
        s, v = logits[h], values[h]
        e = jnp.exp(s - jnp.max(s, axis=-1, keepdims=True))
        o = _dot(e, v) / jnp.sum(e, axis=-1, keepdims=True)
        o_ref[0, :, sl] = o[0:tq]


def mem_attend(u3, mkv, tq, layer=None):
    b, s, _ = u3.shape
    if layer is None:
        kv_spec = pl.BlockSpec((MEM_TOKENS, 2 * MEM_W), lambda bi, i: (bi, 0))
    else:
        kv_spec = pl.BlockSpec((MEM_TOKENS * 2 * N_MEM_HEADS, MEM_HEAD_DIM), lambda bi, i: (layer * b + bi, 0))
    return pl.pallas_call(
        functools.partial(_mem_attend_kernel, tq=tq, slots=layer is not None),
        grid=(b, s // tq),
        in_specs=[pl.BlockSpec((1, tq, MEM_W), lambda bi, i: (bi, i, MAIN_W // MEM_W)), kv_spec],
        out_specs=pl.BlockSpec((1, tq, MEM_W), lambda bi, i: (bi, i, 0)),
        out_shape=jax.ShapeDtypeStruct((b, s, MEM_W), F32),
        compiler_params=_cparams("parallel", "arbitrary"),
        name="mem_attend",
    )(u3, mkv)


def _out_proj_kernel(z_ref, mo_ref, w_ref, x_ref, o_ref, lhs_ref, wb_ref):
    i = pl.program_id(0)
    j = pl.program_id(1)

    @pl.when(j == 0)
    def _():
        lhs_ref[:, :MAIN_W] = z_ref[...].astype(lhs_ref.dtype)
        lhs_ref[:, MAIN_W:] = mo_ref[...].astype(lhs_ref.dtype)

    @pl.when(i == 0)
    def _():
        wb_ref[j] = w_ref[...].astype(BF16)

    o_ref[...] = x_ref[...] + _dot(lhs_ref[...], wb_ref[j])


def _out_proj2_kernel(z_ref, mo_ref, w_ref, x_ref, zs_ref, mos_ref, xs_ref, o_ref, os_ref, lhs_ref, wb_ref):
    _out_proj_kernel(z_ref, mo_ref, w_ref, x_ref, o_ref, lhs_ref, wb_ref)
    j = pl.program_id(1)

    @pl.when(pl.program_id(0) == 0)
    def _():
        lhs = jnp.concatenate([zs_ref[...], mos_ref[...]], axis=1)
        os_ref[...] = xs_ref[...] + _dot(lhs, wb_ref[j])


def out_proj(z, mo, w, layer, x, tm, tn, sample=None):
    m = x.shape[0]
    k = MAIN_W + MEM_W
    nj = D_MODEL // tn
    col = _once_per_column(nj)
    in_specs = [pl.BlockSpec((tm, MAIN_W), lambda i, j: (i, 0)),
                pl.BlockSpec((tm, MEM_W), lambda i, j: (i, 0)),
                pl.BlockSpec((None, k, tn), lambda i, j: (layer, 0, col(i, j))),
                pl.BlockSpec((tm, tn), lambda i, j: (i, j))]
    o_spec = pl.BlockSpec((tm, tn), lambda i, j: (i, j))
    scratch = [pltpu.VMEM((tm, k), _lhs_dtype(tm)), pltpu.VMEM((nj, k, tn), BF16)]
    if sample is None:
        return pl.pallas_call(
            _out_proj_kernel,
            grid=(m // tm, nj),
            in_specs=in_specs,
            out_specs=o_spec,
            out_shape=jax.ShapeDtypeStruct((m, D_MODEL), F32),
            scratch_shapes=scratch,
            compiler_params=_cparams("arbitrary", "arbitrary"),
            name="out_proj",
        )(z, mo, w, x)
    zs, mos, xs = sample
    ms = xs.shape[0]
    s_spec = pl.BlockSpec((ms, tn), lambda i, j: (0, col(i, j)))
    return pl.pallas_call(
        _out_proj2_kernel,
        grid=(m // tm, nj),
        in_specs=in_specs + [pl.BlockSpec((ms, MAIN_W), lambda i, j: (0, 0)),
                             pl.BlockSpec((ms, MEM_W), lambda i, j: (0, 0)), s_spec],
        out_specs=[o_spec, s_spec],
        out_shape=[jax.ShapeDtypeStruct((m, D_MODEL), F32), jax.ShapeDtypeStruct((ms, D_MODEL), F32)],
        scratch_shapes=scratch,
        compiler_params=_cparams("arbitrary", "arbitrary"),
        name="out_proj2",
    )(z, mo, w, x, zs, mos, xs)


def _ffn_kernel(x_ref, g_ref, wa_ref, wb_ref, wd_ref, gf_ref, o_ref, xn_ref, *, final):
    j = pl.program_id(1)

    @pl.when(j == 0)
    def _():
        x = x_ref[...]
        xn_ref[...] = _rms(x, g_ref[...]).astype(xn_ref.dtype)
        o_ref[...] = x

    xn = xn_ref[...]
    a = _dot(xn, wa_ref[...])
    b = _dot(xn, wb_ref[...])
    o_ref[...] += _dot(jax.nn.silu(a) * b, wd_ref[...])

    if final:
        @pl.when(j == pl.num_programs(1) - 1)
        def _():
            o_ref[...] = _rms(o_ref[...], gf_ref[...])


def _ffn2_kernel(x_ref, g_ref, wa_ref, wb_ref, wd_ref, gf_ref, xs_ref, o_ref, os_ref, xn_ref, xsn_ref, *, final):
    _ffn_kernel(x_ref, g_ref, wa_ref, wb_ref, wd_ref, gf_ref, o_ref, xn_ref, final=final)
    j = pl.program_id(1)

    @pl.when(pl.program_id(0) == 0)
    def _():
        @pl.when(j == 0)
        def _():
            xs = xs_ref[...]
            xsn_ref[...] = _rms(xs, g_ref[...])
            os_ref[...] = xs

        xsn = xsn_ref[...]
        a = _dot(xsn, wa_ref[...])
        b = _dot(xsn, wb_ref[...])
        os_ref[...] += _dot(jax.nn.silu(a) * b, wd_ref[...])

        if final:
            @pl.when(j == pl.num_programs(1) - 1)
            def _():
                os_ref[...] = _rms(os_ref[...], gf_ref[...])


def ffn(x, g, w_gu, w_down, layer, g_final, final, tm, tf, xs=None):
    m, k = x.shape
    nf = D_FF // tf
    in_specs = [pl.BlockSpec((tm, k), lambda i, j: (i, 0), pipeline_mode=pl.Buffered(1)),
                pl.BlockSpec((1, k), lambda i, j: (0, 0)),
                pl.BlockSpec((None, k, tf), lambda i, j: (layer, 0, j)),
                pl.BlockSpec((None, k, tf), lambda i, j: (layer, 0, nf + j)),
                pl.BlockSpec((None, tf, k), lambda i, j: (layer, j, 0)),
                pl.BlockSpec((1, k), lambda i, j: (0, 0))]
    o_spec = pl.BlockSpec((tm, k), lambda i, j: (i, 0))
    scratch = [pltpu.VMEM((tm, k), _lhs_dtype(tm))]
    args = (x, g.reshape(1, k), w_gu, w_gu, w_down, g_final.reshape(1, k))
    if xs is None:
        return pl.pallas_call(
            functools.partial(_ffn_kernel, final=final),
            grid=(m // tm, nf),
            in_specs=in_specs,
            out_specs=o_spec,
            out_shape=jax.ShapeDtypeStruct((m, k), F32),
            scratch_shapes=scratch,
            compiler_params=_cparams("parallel", "arbitrary"),
            name="ffn",
        )(*args)
    ms = xs.shape[0]
    s_spec = pl.BlockSpec((ms, k), lambda i, j: (0, 0))
    return pl.pallas_call(
        functools.partial(_ffn2_kernel, final=final),
        grid=(m // tm, nf),
        in_specs=in_specs + [s_spec],
        out_specs=[o_spec, s_spec],
        out_shape=[jax.ShapeDtypeStruct((m, k), F32), jax.ShapeDtypeStruct((ms, k), F32)],
        scratch_shapes=scratch + [pltpu.VMEM((ms, k), F32)],
        compiler_params=_cparams("arbitrary", "arbitrary"),
        name="ffn2",
    )(*args, xs)


def _compress_kernel(tbl_ref, *refs, n_steps):
    pages = refs[:PAGES_PER_STEP]
    nxt_ref, extra_ref, w1_ref, pe_ref, w2_ref, o_ref = refs[PAGES_PER_STEP:]
    p = pl.program_id(1)
    on_extra = p == n_steps
    nxt_extra = p == n_steps - 1
    nh = CMP_STEP * N_KV
    width = CMP_STRIDE * HEAD_DIM
    d = HEAD_DIM

    def chunk(ref, n):
        return jnp.concatenate([ref[0, n * CMP_STRIDE + r] for r in range(CMP_STRIDE)], axis=1)

    chunks = []
    for k in range(PAGES_PER_STEP):
        for n in range(CH_PER_PAGE):
            x = chunk(pages[k], n)
            chunks.append(jnp.where(on_extra, chunk(extra_ref, n) if k == 0 else 0.0, x))
    chunks.append(jnp.where(on_extra, 0.0, jnp.where(nxt_extra, chunk(extra_ref, 0), chunk(nxt_ref, 0))))
    chunks.append(jnp.zeros((KV_SLOTS, width), F32))
    for c in range(2):
        sl = slice(c * N_KV, (c + 1) * N_KV)
        rows = [jnp.concatenate([chunks[i][sl], chunks[i + 1][sl]], axis=0) for i in range(0, len(chunks), 2)]
        tail = [pe_ref[c, 0], pe_ref[c, 1], jnp.zeros((6, width), F32)]
        lhs = jnp.concatenate(rows + tail, axis=0)
        n_rows = N_KV * len(chunks)
        hc = _dot(lhs, jnp.concatenate([w1_ref[c, 0], w1_ref[c, 1]], axis=1))
        pe_term = hc[n_rows:n_rows + 1, :d] + hc[n_rows + 1:n_rows + 2, d:]
        hid = jax.nn.gelu(hc[:nh, :d] + hc[N_KV:nh + N_KV, d:] + pe_term)
        o_ref[0, c] = _dot(hid, w2_ref[c])


def compress(pages, table, extra, cmp_pe, cmp_w1, cmp_w2):
    nb, n_pages = table.shape
    n_steps = n_pages // PAGES_PER_STEP
    w1 = cmp_w1.reshape(2, 2, CMP_STRIDE * HEAD_DIM, HEAD_DIM)
    pe = cmp_pe.reshape(2, 2, 1, CMP_STRIDE * HEAD_DIM)

    def page_map(k):
        return lambda b, p, tbl: (tbl[b, jnp.minimum(p * PAGES_PER_STEP + k, n_pages - 1)], 0, 0, 0)

    in_specs = [pl.BlockSpec((1, PAGE_SIZE, KV_SLOTS, HEAD_DIM), page_map(k)) for k in range(PAGES_PER_STEP)]
    in_specs += [pl.BlockSpec((1, CMP_STRIDE, KV_SLOTS, HEAD_DIM), page_map(PAGES_PER_STEP)),
                 pl.BlockSpec((1, PAGE_SIZE, KV_SLOTS, HEAD_DIM), lambda b, p, tbl: (b, 0, 0, 0)),
                 pl.BlockSpec(w1.shape, lambda b, p, tbl: (0, 0, 0, 0)),
                 pl.BlockSpec(pe.shape, lambda b, p, tbl: (0, 0, 0, 0)),
                 pl.BlockSpec(cmp_w2.shape, lambda b, p, tbl: (0, 0, 0))]
    n_out = CMP_STEP * (n_steps + 1)
    out = pl.pallas_call(
        functools.partial(_compress_kernel, n_steps=n_steps),
        grid_spec=pltpu.PrefetchScalarGridSpec(
            num_scalar_prefetch=1,
            grid=(nb, n_steps + 1),
            in_specs=in_specs,
            out_specs=pl.BlockSpec((1, 2, CMP_STEP * N_KV, HEAD_DIM), lambda b, p, tbl: (b, 0, p, 0)),
        ),
        out_shape=jax.ShapeDtypeStruct((nb, 2, n_out * N_KV, HEAD_DIM), F32),
        compiler_params=_cparams("parallel", "arbitrary"),
        name="compress",
    )(table, *([pages] * (PAGES_PER_STEP + 1)), extra, w1, pe, cmp_w2)
    return jnp.transpose(out.reshape(nb, 2, n_out, N_KV, HEAD_DIM), (0, 1, 3, 2, 4))


def _cover(n_pad, j_pad, n_cmp):
    n = lax.broadcasted_iota(jnp.int32, (n_pad, j_pad), 0)
    j = lax.broadcasted_iota(jnp.int32, (n_pad, j_pad), 1)
    hit = (n * CMP_STRIDE < j * SLC_BLOCK + SLC_BLOCK) & (n * CMP_STRIDE + CMP_BLOCK - 1 >= j * SLC_BLOCK)
    return (hit & (n < n_cmp)).astype(F32)


def _cover_t(j_pad, n_pad, n_cmp):
    j = lax.broadcasted_iota(jnp.int32, (j_pad, n_pad), 0)
    n = lax.broadcasted_iota(jnp.int32, (j_pad, n_pad), 1)
    hit = (n * CMP_STRIDE < j * SLC_BLOCK + SLC_BLOCK) & (n * CMP_STRIDE + CMP_BLOCK - 1 >= j * SLC_BLOCK)
    return (hit & (n < n_cmp)).astype(F32)


def _softmax_rows(s):
    e = jnp.exp(s - jnp.max(s, axis=-1, keepdims=True))
    return e / jnp.sum(e, axis=-1, keepdims=True)


def _nsa_prompt_kernel(q_ref, gl_ref, bg_ref, cos_ref, sin_ref, ckv_ref, ks_ref, vt_ref, kw_ref, vw_ref,
                       o_ref, qt_ref, drop_ref, *, n_cmp, n_slc, kt):
    qb = pl.program_id(1)
    tq = Q_BLOCK
    cos2 = cos_ref[...]
    sin2 = sin_ref[...]
    n_pad = ckv_ref.shape[3]
    qpos1 = qb * tq + lax.broadcasted_iota(jnp.int32, (tq, 1), 0)
    qpos = jnp.concatenate([qpos1] * GQA, axis=0)
    gates = jax.nn.sigmoid(gl_ref[...] + bg_ref[...])

    n_i = lax.broadcasted_iota(jnp.int32, (1, n_pad), 1)
    vis_c = (n_i * CMP_STRIDE + CMP_BLOCK - 1 <= qpos) & (n_i < n_cmp)
    any_c = (qpos >= CMP_BLOCK - 1).astype(F32)
    cover_t = _cover_t(n_slc, n_pad, n_cmp)
    j_col = lax.broadcasted_iota(jnp.int32, (n_slc, 1), 0)
    qblk = jnp.right_shift(qb * tq + lax.broadcasted_iota(jnp.int32, (1, tq), 1), SLC_SHIFT)
    forced = (j_col == 0) | (j_col == qblk) | (j_col == qblk - 1)
    valid = j_col <= qblk
    n_sel = min(N_SELECT, n_slc)
    qpos_row = qb * tq + lax.broadcasted_iota(jnp.int32, (1, tq), 1)

    w_lo = jnp.maximum(qb * tq - WINDOW, 0)
    w_lo = pl.multiple_of(w_lo, tq)
    kpos_w = w_lo + lax.broadcasted_iota(jnp.int32, (1, WINDOW + tq), 1)
    vis_w = (kpos_w <= qpos) & (qpos - kpos_w < WINDOW)

    def gate_col(hd, br):
        return gates[:, hd * N_BRANCH + br:hd * N_BRANCH + br + 1]

    heads = range(N_KV)
    s_cmp, s_win = [], []
    for h in heads:
        q = jnp.concatenate(
            [_rope(q_ref[:, (h * GQA + g) * HEAD_DIM:(h * GQA + g + 1) * HEAD_DIM], cos2, sin2)
             for g in range(GQA)], axis=0)
        qs = (q * SCALE).astype(BF16)
        qt_ref[h] = jnp.transpose(q * (SCALE * LOG2E)).astype(BF16)
        hs = slice(h * HEAD_DIM, (h + 1) * HEAD_DIM)
        s_cmp.append(jnp.where(vis_c, _dot_t(qs, ckv_ref[0, 0, h]), NEG))
        s_win.append(jnp.where(vis_w, _dot_t(qs, kw_ref[pl.ds(w_lo, WINDOW + tq), hs]), NEG))

    o_cmp, scores = [], []
    for h in heads:
        p = _softmax_rows(s_cmp[h]) * any_c
        o_cmp.append(_dot(p, ckv_ref[0, 1, h]))
        p_sum = p[0:tq] + p[tq:2 * tq] + p[2 * tq:3 * tq]
        score = lax.dot_general(cover_t, p_sum, (((1,), (1,)), ((), ())), preferred_element_type=F32,
                                precision=lax.Precision.HIGHEST)
        scores.append(jnp.where(forced, jnp.inf, jnp.where(valid, score, -jnp.inf)))

    for h in heads:
        hs = slice(h * HEAD_DIM, (h + 1) * HEAD_DIM)
        e = jnp.exp(s_win[h] - jnp.max(s_win[h], axis=-1, keepdims=True))
        o_w = _dot(e, vw_ref[pl.ds(w_lo, WINDOW + tq), hs]) / jnp.sum(e, axis=-1, keepdims=True)
        for g in range(GQA):
            hd = h * GQA + g
            r = slice(g * tq, (g + 1) * tq)
            o_ref[:, hd * HEAD_DIM:(hd + 1) * HEAD_DIM] = gate_col(hd, 0) * o_cmp[h][r] + gate_col(hd, 2) * o_w[r]

    for h in heads:
        score = scores[h]
        rank = jnp.zeros((n_slc, tq), F32)
        for i in range(n_slc):
            row = score[i:i + 1, :]
            beats = (row > score) | ((row == score) & (i < j_col))
            rank = rank + jnp.where(beats, 1.0, 0.0)
        drop_ref[h] = jnp.where(rank >= n_sel, NEG, 0.0)

    blocks_per_tile = kt // SLC_BLOCK

    def slc_step(t, carry):
        k0 = pl.multiple_of(t * kt, kt)
        kpos = k0 + lax.broadcasted_iota(jnp.int32, (kt, 1), 0)
        causal = jnp.where(kpos <= qpos_row, 0.0, NEG)
        def logits(h):
            hs = slice(h * HEAD_DIM, (h + 1) * HEAD_DIM)
            dropped = jnp.concatenate(
                [jnp.broadcast_to(drop_ref[h, pl.ds(t * blocks_per_tile + jb, 1), :], (SLC_BLOCK, tq))
                 for jb in range(blocks_per_tile)], axis=0)
            bias = dropped + causal
            return _dot(ks_ref[pl.ds(k0, kt), hs], qt_ref[h]) + jnp.concatenate([bias] * GQA, axis=1)

        scores = [logits(h) for h in range(N_KV)]
        out = []
        for h in range(N_KV):
            m, l, acc = carry[h]
            hs = slice(h * HEAD_DIM, (h + 1) * HEAD_DIM)
            s = scores[h]
            m_new = jnp.maximum(m, jnp.max(s, axis=0, keepdims=True))
            a = jnp.exp2(m - m_new)
            e = jnp.exp2(s - m_new)
            l = l * a + jnp.sum(e, axis=0, keepdims=True)
            acc = acc * a + _dot(vt_ref[t, hs, :], e)
            out.append((m_new, l, acc))
        return tuple(out)

    n_t = (qb * tq + tq + kt - 1) // kt
    init = (jnp.full((1, GQA * tq), NEG, F32), jnp.zeros((1, GQA * tq), F32), jnp.zeros((HEAD_DIM, GQA * tq), F32))
    done = lax.fori_loop(0, n_t, slc_step, (init,) * N_KV)
    for h in range(N_KV):
        _, l, acc = done[h]
        o_s = jnp.transpose(acc / l)
        for g in range(GQA):
            hd = h * GQA + g
            r = slice(g * tq, (g + 1) * tq)
            o_ref[:, hd * HEAD_DIM:(hd + 1) * HEAD_DIM] += gate_col(hd, 1) * o_s[r]


def nsa_prompt(u, b_gate_pad, cos2, sin2, ckv, slc_k, slc_vt, win_k, win_v, batch, seq):
    nqb = seq // Q_BLOCK
    n_pad = ckv.shape[3]
    half = N_KV * HEAD_DIM
    kspec = pl.BlockSpec((seq, half), lambda b, i: (b, 0))
    vt_spec = pl.BlockSpec((seq // SLC_KT, half, SLC_KT), lambda b, i: (b, 0, 0))
    return pl.pallas_call(
        functools.partial(_nsa_prompt_kernel, n_cmp=seq // CMP_STRIDE - 1, n_slc=seq // SLC_BLOCK, kt=SLC_KT),
        grid=(batch, nqb),
        in_specs=[pl.BlockSpec((Q_BLOCK, MAIN_W), lambda b, i: (b * nqb + i, 0)),
                  pl.BlockSpec((Q_BLOCK, GATE_PAD), lambda b, i: (b * nqb + i, (MAIN_W + MEM_W) // GATE_PAD)),
                  pl.BlockSpec((1, GATE_PAD), lambda b, i: (0, 0)),
                  pl.BlockSpec((Q_BLOCK, HEAD_DIM), lambda b, i: (i, 0)),
                  pl.BlockSpec((Q_BLOCK, HEAD_DIM), lambda b, i: (i, 0)),
                  pl.BlockSpec((1, 2, N_KV, n_pad, HEAD_DIM), lambda b, i: (b, 0, 0, 0, 0)),
                  kspec, vt_spec, kspec, kspec],
        out_specs=pl.BlockSpec((Q_BLOCK, MAIN_W), lambda b, i: (b * nqb + i, 0)),
        out_shape=jax.ShapeDtypeStruct((batch * seq, MAIN_W), F32),
        scratch_shapes=[pltpu.VMEM((N_KV, HEAD_DIM, GQA * Q_BLOCK), BF16),
                        pltpu.VMEM((N_KV, seq // SLC_BLOCK, Q_BLOCK), F32)],
        compiler_params=_cparams("parallel", "arbitrary"),
        name="nsa_prompt",
    )(u, u, b_gate_pad, cos2, sin2, ckv, slc_k, slc_vt, win_k, win_v)


HEAD_ROWS = 16


def _head_mask(h):
    r = lax.broadcasted_iota(jnp.int32, (HEAD_ROWS, 1), 0)
    return (r >= h * GQA) & (r < (h + 1) * GQA)


def _nsa_sample_a_kernel(q_ref, cos_ref, sin_ref, ckv_ref, wkv_ref, wnew_ref, qr_ref, oc_ref, ow_ref, sel_ref,
                         *, n_cmp, n_slc, j_pad):
    n_pad = ckv_ref.shape[3]
    q = _rope(q_ref[0], cos_ref[...], sin_ref[...])
    qr_ref[0] = q
    qs = q * SCALE
    n_i = lax.broadcasted_iota(jnp.int32, (1, n_pad), 1)
    vis_c = (n_i * CMP_STRIDE + CMP_BLOCK - 1 <= PAST_LEN) & (n_i < n_cmp)
    cover = _cover(n_pad, j_pad, n_cmp)
    j_row = lax.broadcasted_iota(jnp.int32, (1, j_pad), 1)
    j_col = lax.broadcasted_iota(jnp.int32, (j_pad, 1), 0)
    qblk = PAST_LEN // SLC_BLOCK
    lane = lax.broadcasted_iota(jnp.int32, (N_SELECT, HEAD_DIM), 1)
    r_col = lax.broadcasted_iota(jnp.int32, (N_SELECT, 1), 0)
    n_win = wkv_ref.shape[1] // KV_SLOTS
    kpos_w = PAST_LEN - n_win + lax.broadcasted_iota(jnp.int32, (1, n_win), 1)
    vis_w = (kpos_w <= PAST_LEN) & (PAST_LEN - kpos_w < WINDOW)

    heads = range(N_KV)
    forced = (j_row == 0) | (j_row == qblk) | (j_row == qblk - 1)
    s_cmp = [jnp.where(vis_c, _dot_t(qs, ckv_ref[0, 0, h]), NEG) for h in heads]
    s_win = [jnp.where(vis_w, _dot_t(qs, wkv_ref[0, pl.ds(h, n_win, stride=KV_SLOTS), :]), NEG) for h in heads]

    o_c = jnp.zeros((HEAD_ROWS, HEAD_DIM), F32)
    scores = []
    for h in heads:
        mine = _head_mask(h)
        p = _softmax_rows(s_cmp[h]) * float(PAST_LEN >= CMP_BLOCK - 1)
        o_c = jnp.where(mine, _dot(p, ckv_ref[0, 1, h]), o_c)
        p_sum = jnp.sum(jnp.where(mine, p, 0.0), axis=0, keepdims=True)
        score = jnp.dot(jnp.broadcast_to(p_sum, (8, n_pad)), cover, preferred_element_type=F32,
                        precision=lax.Precision.HIGHEST)
        score = jnp.where(forced, jnp.inf, jnp.where(j_row <= qblk, score, -jnp.inf))
        scores.append(jnp.where(j_row < n_slc, score, -jnp.inf))
    oc_ref[0] = o_c

    o_w = jnp.zeros((HEAD_ROWS, HEAD_DIM), F32)
    for h in heads:
        hs = slice(h * HEAD_DIM, (h + 1) * HEAD_DIM)
        vsl = slice((N_KV + h) * HEAD_DIM, (N_KV + h + 1) * HEAD_DIM)
        v_win = wkv_ref[0, pl.ds(N_KV + h, n_win, stride=KV_SLOTS), :]
        s = s_win[h]
        s_new = jnp.sum(qs * wnew_ref[0, :, hs], axis=-1, keepdims=True)
        m = jnp.maximum(jnp.max(s, axis=-1, keepdims=True), s_new)
        e = jnp.exp(s - m)
        e_new = jnp.exp(s_new - m)
        o = (_dot(e, v_win) + e_new * wnew_ref[0, :, vsl]) / (jnp.sum(e, axis=-1, keepdims=True) + e_new)
        o_w = jnp.where(_head_mask(h), o, o_w)
    ow_ref[0] = o_w

    sel_out = jnp.zeros((N_SELECT, HEAD_DIM), jnp.int32)
    for h in heads:
        score = scores[h]
        s_col = jnp.transpose(score)[:, 0:1]
        beats = ((s_col > score[0:1]) | ((s_col == score[0:1]) & (j_col < j_row))) & (j_col < n_slc)
        rank = jnp.sum(beats.astype(F32), axis=0, keepdims=True)
        hit = (rank == r_col.astype(F32)) & (j_row < n_slc)
        idx = jnp.sum(jnp.where(hit, j_row.astype(F32), 0.0), axis=1, keepdims=True)
        sel_out = jnp.where(lane == h, idx.astype(jnp.int32), sel_out)
    sel_ref[0] = sel_out


def nsa_sample_a(q16, cos2, sin2, ckv, win_cache, win_new, n_cmp, n_slc):
    db = q16.shape[0]
    n_pad = ckv.shape[3]
    win_rows = win_cache.shape[1]
    j_pad = -(-n_slc // 128) * 128
    hspec = pl.BlockSpec((1, HEAD_ROWS, HEAD_DIM), lambda b: (b, 0, 0))
    hout = jax.ShapeDtypeStruct((db, HEAD_ROWS, HEAD_DIM), F32)
    return pl.pallas_call(
        functools.partial(_nsa_sample_a_kernel, n_cmp=n_cmp, n_slc=n_slc, j_pad=j_pad),
        grid=(db,),
        in_specs=[hspec,
                  pl.BlockSpec((1, HEAD_DIM), lambda b: (0, 0)),
                  pl.BlockSpec((1, HEAD_DIM), lambda b: (0, 0)),
                  pl.BlockSpec((1, 2, N_KV, n_pad, HEAD_DIM), lambda b: (b, 0, 0, 0, 0)),
                  pl.BlockSpec((1, win_rows, HEAD_DIM), lambda b: (b, 0, 0)),
                  pl.BlockSpec((1, 1, KV_W), lambda b: (b, 0, 0))],
        out_specs=[hspec, hspec, hspec, pl.BlockSpec((1, N_SELECT, HEAD_DIM), lambda b: (b, 0, 0))],
        out_shape=[hout, hout, hout, jax.ShapeDtypeStruct((db, N_SELECT, HEAD_DIM), jnp.int32)],
        compiler_params=_cparams("arbitrary"),
        name="nsa_sample_a",
    )(q16, cos2, sin2, ckv, win_cache, win_new)


def _nsa_sample_b_kernel(sel_ref, tbl_ref, q_ref, *refs):
    blocks = refs[:N_SELECT]
    new_ref, oc_ref, ow_ref, gl_ref, bg_ref, o_ref, os_ref = refs[N_SELECT:]
    b = pl.program_id(0)
    h = pl.program_id(1)
    n_keys = N_SELECT * SLC_BLOCK
    lane_blk = jnp.right_shift(lax.broadcasted_iota(jnp.int32, (1, n_keys), 1), SLC_SHIFT)
    new_k = new_ref[0, pl.ds(h, 1), :]
    new_v = new_ref[0, pl.ds(h + N_KV, 1), :]
    ks, vs = [], []
    blk_of = jnp.zeros((1, n_keys), jnp.int32)
    for r in range(N_SELECT):
        blk = sel_ref[b, h, r]
        in_past = blk * SLC_BLOCK < PAST_LEN
        ks.append(jnp.where(in_past, blocks[r][0, pl.ds(h, SLC_BLOCK, stride=KV_SLOTS), :], new_k))
        vs.append(jnp.where(in_past, blocks[r][0, pl.ds(h + N_KV, SLC_BLOCK, stride=KV_SLOTS), :], new_v))
        blk_of = jnp.where(lane_blk == r, blk, blk_of)
    kpos = blk_of * SLC_BLOCK + (lax.broadcasted_iota(jnp.int32, (1, n_keys), 1) & (SLC_BLOCK - 1))
    s = jnp.where(kpos <= PAST_LEN, _dot_t(q_ref[0] * SCALE, jnp.concatenate(ks, axis=0)), NEG)
    e = jnp.exp(s - jnp.max(s, axis=-1, keepdims=True))
    o_s = _dot(e, jnp.concatenate(vs, axis=0)) / jnp.sum(e, axis=-1, keepdims=True)

    @pl.when(h == 0)
    def _():
        os_ref[...] = jnp.zeros(os_ref.shape, F32)

    row = lax.broadcasted_iota(jnp.int32, (HEAD_ROWS, 1), 0)
    mine = (row >= h * GQA) & (row < (h + 1) * GQA)
    os_ref[...] = jnp.where(mine, o_s, os_ref[...])

    @pl.when(h == pl.num_programs(1) - 1)
    def _():
        gates = jax.nn.sigmoid(gl_ref[0] + bg_ref[...])
        o_ref[0] = gates[:, 0:1] * oc_ref[0] + gates[:, 1:2] * os_ref[...] + gates[:, 2:3] * ow_ref[0]


def nsa_sample_b(sel, page_table, q_rope, slc_halves, slc_new, o_c, o_w, gate_logits, b_gate16):
    db = q_rope.shape[0]
    last_blk = PAST_LEN // SLC_BLOCK - 1
    halves = PAGE_SIZE // SLC_BLOCK

    def cache_map(r):
        def index(b, h, sel_ref, tbl_ref):
            j = jnp.minimum(sel_ref[b, h, r], last_blk)
            return (tbl_ref[b, j // halves] * halves + j % halves, 0, 0)
        return index

    hspec = pl.BlockSpec((1, HEAD_ROWS, HEAD_DIM), lambda b, h, s, t: (b, 0, 0))
    return pl.pallas_call(
        _nsa_sample_b_kernel,
        grid_spec=pltpu.PrefetchScalarGridSpec(
            num_scalar_prefetch=2,
            grid=(db, N_KV),
            in_specs=[hspec]
            + [pl.BlockSpec((1, SLC_BLOCK * KV_SLOTS, HEAD_DIM), cache_map(r)) for r in range(N_SELECT)]
            + [pl.BlockSpec((1, KV_SLOTS, HEAD_DIM), lambda b, h, s, t: (b, 0, 0)),
               hspec, hspec, hspec,
               pl.BlockSpec((HEAD_ROWS, HEAD_DIM), lambda b, h, s, t: (0, 0))],
            out_specs=hspec,
            scratch_shapes=[pltpu.VMEM((HEAD_ROWS, HEAD_DIM), F32)],
        ),
        out_shape=jax.ShapeDtypeStruct((db, HEAD_ROWS, HEAD_DIM), F32),
        compiler_params=_cparams("arbitrary", "arbitrary"),
        name="nsa_sample_b",
    )(sel, page_table, q_rope, *([slc_halves] * N_SELECT), slc_new, o_c, o_w, gate_logits, b_gate16)


def _rope_tables(pos):
    half = HEAD_DIM // 2
    inv = ROPE_THETA ** (-jnp.arange(half, dtype=F32) / half)
    ang = pos.astype(F32)[:, None] * inv[None, :]
    cos, sin = jnp.cos(ang), jnp.sin(ang)
    return jnp.concatenate([cos, cos], -1), jnp.concatenate([-sin, sin], -1)


def _pad_w_in_b(w):
    return jnp.concatenate([w[..., :MAIN_W], w[..., MAIN_W + GATE_W:],
                            jnp.pad(w[..., MAIN_W:MAIN_W + GATE_W], ((0, 0), (0, 0), (0, GATE_PAD - GATE_W)))], axis=-1)


def kernel(x_prompt, x_sample, mem_prompt, state_pool, cache_cmp_kv, cache_slc_kv, cache_win_kv, cache_mem_kv, page_table, g_mix, w_in_a, pool_grp_w, pool_scale, w_in_b, b_gate, g_kv, w_kv, cmp_pe, cmp_w1, cmp_w2, g_mem, w_mem_kv, w_out, g_ffn, w_gu, w_down, g_final):
    batch, seq, _ = x_prompt.shape
    db = x_sample.shape[0]
    m = batch * seq
    tm = PROJ_TM
    n_phys = cache_cmp_kv.shape[0]
    w_in_b_pad = _pad_w_in_b(w_in_b)
    xp = x_prompt.reshape(m, D_MODEL)
    xs = x_sample.reshape(db, D_MODEL)
    mem2 = mem_prompt.reshape(batch * MEM_TOKENS, D_MODEL)
    mem_kv = [rms_matmul(mem2, g_mem[l], w_mem_kv, l, 512, 512) for l in range(DEPTH)]
    mem_cache = cache_mem_kv.reshape(-1, MEM_HEAD_DIM)
    cos_p, sin_p = _rope_tables(jnp.arange(seq, dtype=jnp.int32))
    cos_s, sin_s = _rope_tables(jnp.full((db,), PAST_LEN, jnp.int32))
    pools_p, pools_s = [], []
    for l in range(DEPTH):
        if l == N_A:
            cmp_kv, slc_kv, win_kv, slc_k, slc_vt, win_k, win_v = kv_proj(xp, g_kv, w_kv, cos_p, sin_p, SLC_KT, True)
            table = jnp.arange(batch * seq // PAGE_SIZE, dtype=jnp.int32).reshape(batch, seq // PAGE_SIZE)
            ckv_p = compress(cmp_kv.reshape(-1, PAGE_SIZE, KV_SLOTS, HEAD_DIM), table,
                             jnp.zeros((batch, PAGE_SIZE, KV_SLOTS, HEAD_DIM), F32), cmp_pe, cmp_w1, cmp_w2)
            cmp_new, slc_new, win_new = kv_proj(xs, g_kv, w_kv, cos_s, sin_s, db, False)
            extra = jnp.pad(cmp_new.reshape(db, 1, KV_SLOTS, HEAD_DIM), ((0, 0), (0, PAGE_SIZE - 1), (0, 0), (0, 0)))
            ckv_s = compress(cache_cmp_kv.reshape(n_phys, PAGE_SIZE, KV_SLOTS, HEAD_DIM), page_table, extra,
                             cmp_pe, cmp_w1, cmp_w2)
            t_full = -(-(PAST_LEN + 1) // SLC_BLOCK) * SLC_BLOCK
            n_cmp = t_full // CMP_STRIDE - 1
            n_slc = t_full // SLC_BLOCK
        if l < N_A:
            up, us = rms_matmul(xp, g_mix[l], w_in_a, l, tm, 512, xs=xs)
            u3 = up.reshape(batch, seq, -1)
            zp = pool_prompt(u3, pool_grp_w[l], pool_scale[l]).reshape(m, MAIN_W)
            pools_p.append(u3[:, seq - POOL_BUF:, :MAIN_W])
            zs = pool_sample(us, jnp.transpose(state_pool[l], (1, 0, 2)), pool_grp_w[l], pool_scale[l])
            pools_s.append(jnp.concatenate([state_pool[l][:, 1:], us[:, None, :MAIN_W]], axis=1))
        else:
            j = l - N_A
            up, us = rms_matmul(xp, g_mix[l], w_in_b_pad, j, tm, 768, xs=xs)
            u3 = up.reshape(batch, seq, -1)
            bg = jnp.pad(b_gate[j], (0, GATE_PAD - GATE_W)).reshape(1, GATE_PAD)
            zp = nsa_prompt(up, bg, cos_p, sin_p, ckv_p, slc_k, slc_vt, win_k, win_v, batch, seq)
            q16 = jnp.pad(us[:, :MAIN_W].reshape(db, N_HEADS, HEAD_DIM), ((0, 0), (0, HEAD_ROWS - N_HEADS), (0, 0)))
            q_rope, o_c, o_w, sel = nsa_sample_a(
                q16, cos_s[:1], sin_s[:1], ckv_s, cache_win_kv.reshape(db, -1, HEAD_DIM),
                win_new.reshape(db, 1, KV_W), n_cmp, n_slc)
            sel = jnp.transpose(sel[:, :, :N_KV], (0, 2, 1))
            gl = us[:, MAIN_W + MEM_W:MAIN_W + MEM_W + GATE_W].reshape(db, N_HEADS, N_BRANCH)
            gl = jnp.pad(gl, ((0, 0), (0, HEAD_ROWS - N_HEADS), (0, HEAD_DIM - N_BRANCH)))
            bg16 = jnp.pad(b_gate[j].reshape(N_HEADS, N_BRANCH), ((0, HEAD_ROWS - N_HEADS), (0, HEAD_DIM - N_BRANCH)))
            z16 = nsa_sample_b(sel, page_table, q_rope,
                               cache_slc_kv.reshape(n_phys * 2, SLC_BLOCK * KV_SLOTS, HEAD_DIM),
                               slc_new.reshape(db, KV_SLOTS, HEAD_DIM), o_c, o_w, gl, bg16)
            zs = z16[:, :N_HEADS].reshape(db, MAIN_W)
        mo_p = mem_attend(u3, mem_kv[l], 512).reshape(m, MEM_W)
        mo_s = mem_attend(us.reshape(db, 1, -1), mem_cache, 1, layer=l).reshape(db, MEM_W)
        xp, xs = out_proj(zp, mo_p, w_out, l, xp, tm, 512, sample=(zs, mo_s, xs))
        xp, xs = ffn(xp, g_ffn[l], w_gu, w_down, l, g_final, l == DEPTH - 1, FFN_TM, FFN_TF, xs=xs)
    kv5 = (batch, seq, 2, N_KV, HEAD_DIM)
    kv5s = (db, 1, 2, N_KV, HEAD_DIM)
    n_keep = min(WINDOW, seq)
    win_s = jnp.concatenate([cache_win_kv, win_new.reshape(kv5s)], axis=1)
    win_s = win_s[:, win_s.shape[1] - min(WINDOW, win_s.shape[1]):]
    return (xp.reshape(batch, seq, D_MODEL), xs.reshape(db, 1, D_MODEL), jnp.stack(pools_p),
            cmp_kv.reshape(kv5), slc_kv.reshape(kv5), win_kv.reshape(kv5)[:, seq - n_keep:],
            jnp.stack(mem_kv).reshape(DEPTH, batch, MEM_TOKENS, 2, N_MEM_HEADS, MEM_HEAD_DIM),
            jnp.stack(pools_s), cmp_new.reshape(kv5s), slc_new.reshape(kv5s), win_s)
```

```python
import functools

import jax
import jax.numpy as jnp
from jax import lax
from jax.experimental import pallas as pl
from jax.experimental.pallas import tpu as pltpu

F32 = jnp.float32
BF16 = jnp.bfloat16

D_MODEL = 2048
DEPTH = 4
N_A = 2
PAST_LEN = 16384
PAGE_SIZE = 128
MEM_TOKENS = 256
N_MEM_HEADS = 4
MEM_HEAD_DIM = 128
MEM_W = 512
MAIN_W = 1536
POOL_WINDOWS = (2, 4, 8, 16)
POOL_GC = 384
POOL_BUF = 15
HEAD_DIM = 128
N_HEADS = 12
N_KV = 4
GQA = 3
N_BRANCH = 3
GATE_W = 36
GATE_PAD = 256
CMP_BLOCK = 32
CMP_STRIDE = 16
SLC_BLOCK = 64
SLC_SHIFT = 6
N_SELECT = 16
WINDOW = 512
Q_BLOCK = 128
D_FF = 5632
ROPE_THETA = 10000.0
EPS = 1e-6
NEG = -1e30
SCALE = HEAD_DIM ** -0.5
MEM_SCALE = MEM_HEAD_DIM ** -0.5
KV_SLOTS = 2 * N_KV
KV_W = KV_SLOTS * HEAD_DIM
W_IN_B = MAIN_W + MEM_W + GATE_PAD

VMEM_LIMIT = 56 * 1024 * 1024
SLC_KT = 512
LOG2E = 1.4426950408889634
PROJ_TM = 1024
FFN_TM = 1024
FFN_TF = 256
PAGES_PER_STEP = 16
CH_PER_PAGE = PAGE_SIZE // CMP_STRIDE
CMP_STEP = PAGES_PER_STEP * CH_PER_PAGE


def _cparams(*sem):
    return pltpu.CompilerParams(dimension_semantics=sem, vmem_limit_bytes=VMEM_LIMIT)


def _rms(x, g):
    return x * lax.rsqrt(jnp.mean(x * x, axis=-1, keepdims=True) + EPS) * g


def _dot(a, b):
    return jnp.dot(a.astype(BF16), b.astype(BF16), preferred_element_type=F32)


def _dot_t(a, b):
    return lax.dot_general(a.astype(BF16), b.astype(BF16), (((1,), (1,)), ((), ())),
                           preferred_element_type=F32)


def _lhs_dtype(rows):
    return BF16 if rows % 16 == 0 else F32


def _rope(x, cos2, sin2):
    return x * cos2 + pltpu.roll(x, HEAD_DIM // 2, axis=1) * sin2


def _rms_matmul_kernel(x_ref, g_ref, w_ref, o_ref, xn_ref, wb_ref):
    i = pl.program_id(0)
    j = pl.program_id(1)

    @pl.when(j == 0)
    def _():
        xn_ref[...] = _rms(x_ref[...], g_ref[...]).astype(xn_ref.dtype)

    @pl.when(i == 0)
    def _():
        wb_ref[j] = w_ref[...].astype(BF16)

    o_ref[...] = _dot(xn_ref[...], wb_ref[j])


def _rms_matmul2_kernel(x_ref, xs_ref, g_ref, w_ref, o_ref, os_ref, xn_ref, xsn_ref, wb_ref):
    _rms_matmul_kernel(x_ref, g_ref, w_ref, o_ref, xn_ref, wb_ref)
    i = pl.program_id(0)
    j = pl.program_id(1)

    @pl.when((i == 0) & (j == 0))
    def _():
        xsn_ref[...] = _rms(xs_ref[...], g_ref[...])

    @pl.when(i == 0)
    def _():
        os_ref[...] = _dot(xsn_ref[...], wb_ref[j])


def _once_per_column(nj):
    return lambda i, j: jnp.where(i == 0, j, nj - 1)


def rms_matmul(x, g, w, layer, tm, tn, xs=None):
    m, k = x.shape
    n = w.shape[2]
    nj = n // tn
    col = _once_per_column(nj)
    x_spec = pl.BlockSpec((tm, k), lambda i, j: (i, 0))
    g_spec = pl.BlockSpec((1, k), lambda i, j: (0, 0))
    w_spec = pl.BlockSpec((None, k, tn), lambda i, j: (layer, 0, col(i, j)))
    o_spec = pl.BlockSpec((tm, tn), lambda i, j: (i, j))
    scratch = [pltpu.VMEM((tm, k), _lhs_dtype(tm)), pltpu.VMEM((nj, k, tn), BF16)]
    if xs is None:
        return pl.pallas_call(
            _rms_matmul_kernel,
            grid=(m // tm, nj),
            in_specs=[x_spec, g_spec, w_spec],
            out_specs=o_spec,
            out_shape=jax.ShapeDtypeStruct((m, n), F32),
            scratch_shapes=scratch,
            compiler_params=_cparams("arbitrary", "arbitrary"),
            name="rms_matmul",
        )(x, g.reshape(1, k), w)
    ms = xs.shape[0]
    return pl.pallas_call(
        _rms_matmul2_kernel,
        grid=(m // tm, nj),
        in_specs=[x_spec, pl.BlockSpec((ms, k), lambda i, j: (0, 0)), g_spec, w_spec],
        out_specs=[o_spec, pl.BlockSpec((ms, tn), lambda i, j: (0, col(i, j)))],
        out_shape=[jax.ShapeDtypeStruct((m, n), F32), jax.ShapeDtypeStruct((ms, n), F32)],
        scratch_shapes=[scratch[0], pltpu.VMEM((ms, k), F32), scratch[1]],
        compiler_params=_cparams("arbitrary", "arbitrary"),
        name="rms_matmul2",
    )(x, xs, g.reshape(1, k), w)


def _kv_proj_kernel(x_ref, g_ref, w_ref, cos_ref, sin_ref, cmp_ref, slc_ref, win_ref, *rest):
    xn_ref, wb_ref = rest[-2:]
    attn = rest[:-2]
    i = pl.program_id(0)
    j = pl.program_id(1)
    half = N_KV * HEAD_DIM

    @pl.when(j == 0)
    def _():
        xn_ref[...] = _rms(x_ref[...], g_ref[...]).astype(xn_ref.dtype)

    @pl.when(i == 0)
    def _():
        wb_ref[j] = w_ref[...].astype(BF16)

    y = _dot(xn_ref[...], wb_ref[j])
    for br, ref in enumerate((cmp_ref, slc_ref, win_ref)):
        for c in range(2):
            @pl.when(j == 2 * br + c)
            def _(ref=ref, br=br, c=c):
                val = y
                if c == 0:
                    cos2 = cos_ref[...]
                    sin2 = sin_ref[...]
                    val = jnp.concatenate(
                        [_rope(y[:, h * HEAD_DIM:(h + 1) * HEAD_DIM], cos2, sin2) for h in range(N_KV)], axis=1)
                for h in range(N_KV):
                    ref[:, c * N_KV + h, :] = val[:, h * HEAD_DIM:(h + 1) * HEAD_DIM]
                if attn and br == 1 and c == 1:
                    attn[1][0] = jnp.transpose(val).astype(BF16)
                elif attn and br > 0:
                    attn[2 * (br - 1) + c][...] = val.astype(BF16)


def kv_proj(x, g, w, cos2, sin2, tm, attn_copies):
    m, k = x.shape
    half = N_KV * HEAD_DIM
    nt = cos2.shape[0] // tm
    nj = 2 * N_BRANCH
    col = _once_per_column(nj)
    out = jax.ShapeDtypeStruct((m, KV_SLOTS, HEAD_DIM), F32)
    ospec = pl.BlockSpec((tm, KV_SLOTS, HEAD_DIM), lambda i, j: (i, 0, 0))
    out_shape, out_specs = [out, out, out], [ospec, ospec, ospec]
    if attn_copies:
        out16 = jax.ShapeDtypeStruct((m, half), BF16)
        ospec16 = pl.BlockSpec((tm, half), lambda i, j: (i, 0))
        out_shape += [out16, jax.ShapeDtypeStruct((m // tm, half, tm), BF16), out16, out16]
        out_specs += [ospec16, pl.BlockSpec((1, half, tm), lambda i, j: (i, 0, 0)), ospec16, ospec16]
    return pl.pallas_call(
        _kv_proj_kernel,
        grid=(m // tm, nj),
        in_specs=[pl.BlockSpec((tm, k), lambda i, j: (i, 0)),
                  pl.BlockSpec((1, k), lambda i, j: (0, 0)),
                  pl.BlockSpec((k, half), lambda i, j: (0, col(i, j))),
                  pl.BlockSpec((tm, HEAD_DIM), lambda i, j: (i % nt, 0)),
                  pl.BlockSpec((tm, HEAD_DIM), lambda i, j: (i % nt, 0))],
        out_specs=out_specs,
        out_shape=out_shape,
        scratch_shapes=[pltpu.VMEM((tm, k), _lhs_dtype(tm)), pltpu.VMEM((nj, k, half), BF16)],
        compiler_params=_cparams("arbitrary", "arbitrary"),
        name="kv_proj",
    )(x, g.reshape(1, k), w, cos2, sin2)


def _pool_prompt_kernel(cur_ref, prev_ref, w_ref, sc_ref, o_ref, *, ts):
    i = pl.program_id(1)
    keep = (i > 0).astype(F32)
    pos = (i * ts + lax.broadcasted_iota(jnp.int32, (ts, 1), 0) + 1).astype(F32)
    diffs = []
    for g, w in enumerate(POOL_WINDOWS):
        sl = slice(g * POOL_GC, (g + 1) * POOL_GC)
        x = cur_ref[0, :, sl]
        ext = jnp.concatenate([prev_ref[0, :, sl] * keep, x], axis=0)
        span = 1
        while span < w:
            ext = ext[span:] + ext[:-span]
            span *= 2
        win = ext[17 - w:17 - w + ts]
        diffs.append((win / jnp.minimum(pos, float(w)) - x).astype(BF16))
    for g in range(len(POOL_WINDOWS)):
        sl = slice(g * POOL_GC, (g + 1) * POOL_GC)
        o_ref[0, :, sl] = _dot(diffs[g], w_ref[g]) * sc_ref[:, sl]


def pool_prompt(u3, w_grp, scale, ts=256):
    b, s, n = u3.shape
    r = ts // 16
    return pl.pallas_call(
        functools.partial(_pool_prompt_kernel, ts=ts),
        grid=(b, s // ts),
        in_specs=[pl.BlockSpec((1, ts, MAIN_W), lambda bi, i: (bi, i, 0)),
                  pl.BlockSpec((1, 16, MAIN_W), lambda bi, i: (bi, jnp.maximum(i * r - 1, 0), 0)),
                  pl.BlockSpec((4, POOL_GC, POOL_GC), lambda bi, i: (0, 0, 0)),
                  pl.BlockSpec((1, MAIN_W), lambda bi, i: (0, 0))],
        out_specs=pl.BlockSpec((1, ts, MAIN_W), lambda bi, i: (bi, i, 0)),
        out_shape=jax.ShapeDtypeStruct((b, s, MAIN_W), F32),
        compiler_params=_cparams("parallel", "arbitrary"),
        name="pool_prompt",
    )(u3, u3, w_grp, scale.reshape(1, MAIN_W))


def _pool_sample_kernel(cur_ref, buf_ref, w_ref, sc_ref, o_ref):
    for g, w in enumerate(POOL_WINDOWS):
        sl = slice(g * POOL_GC, (g + 1) * POOL_GC)
        x = cur_ref[:, sl]
        win = x
        for r in range(POOL_BUF - (w - 1), POOL_BUF):
            win = win + buf_ref[r, :, sl]
        d = win / float(w) - x
        o_ref[:, sl] = _dot(d, w_ref[g]) * sc_ref[:, sl]


def pool_sample(u, buf_t, w_grp, scale):
    db = u.shape[0]
    return pl.pallas_call(
        _pool_sample_kernel,
        grid=(1,),
        in_specs=[pl.BlockSpec((db, MAIN_W), lambda i: (0, 0)),
                  pl.BlockSpec((POOL_BUF, db, MAIN_W), lambda i: (0, 0, 0)),
                  pl.BlockSpec((4, POOL_GC, POOL_GC), lambda i: (0, 0, 0)),
                  pl.BlockSpec((1, MAIN_W), lambda i: (0, 0))],
        out_specs=pl.BlockSpec((db, MAIN_W), lambda i: (0, 0)),
        out_shape=jax.ShapeDtypeStruct((db, MAIN_W), F32),
        compiler_params=_cparams("arbitrary"),
        name="pool_sample",
    )(u, buf_t, w_grp, scale.reshape(1, MAIN_W))


def _mem_attend_kernel(q_ref, kv_ref, o_ref, *, tq, slots):
    rows = max(tq, 8)
    logits, values = [], []
    for h in range(N_MEM_HEADS):
        sl = slice(h * MEM_HEAD_DIM, (h + 1) * MEM_HEAD_DIM)
        q = q_ref[0, :, sl]
        if tq < rows:
            q = jnp.broadcast_to(q[0:1], (rows, MEM_HEAD_DIM))
        if slots:
            k = kv_ref[pl.ds(h, MEM_TOKENS, stride=2 * N_MEM_HEADS), :]
            v = kv_ref[pl.ds(N_MEM_HEADS + h, MEM_TOKENS, stride=2 * N_MEM_HEADS), :]
        else:
            k = kv_ref[:, sl]
            v = kv_ref[:, MEM_W + h * MEM_HEAD_DIM:MEM_W + (h + 1) * MEM_HEAD_DIM]
        logits.append(_dot_t(q, k) * MEM_SCALE)
        values.append(v)
    for h in range(N_MEM_HEADS):
        sl = slice(h * MEM_HEAD_DIM, (h + 1) * MEM_HEAD_DIM)
        s, v = logits[h], values[h]
        e = jnp.exp(s - jnp.max(s, axis=-1, keepdims=True))
        o = _dot(e, v) / jnp.sum(e, axis=-1, keepdims=True)
        o_ref[0, :, sl] = o[0:tq]


def mem_attend(u3, mkv, tq, layer=None):
    b, s, _ = u3.shape
    if layer is None:
        kv_spec = pl.BlockSpec((MEM_TOKENS, 2 * MEM_W), lambda bi, i: (bi, 0))
    else:
        kv_spec = pl.BlockSpec((MEM_TOKENS * 2 * N_MEM_HEADS, MEM_HEAD_DIM), lambda bi, i: (layer * b + bi, 0))
    return pl.pallas_call(
        functools.partial(_mem_attend_kernel, tq=tq, slots=layer is not None),
        grid=(b, s // tq),
        in_specs=[pl.BlockSpec((1, tq, MEM_W), lambda bi, i: (bi, i, MAIN_W // MEM_W)), kv_spec],
        out_specs=pl.BlockSpec((1, tq, MEM_W), lambda bi, i: (bi, i, 0)),
        out_shape=jax.ShapeDtypeStruct((b, s, MEM_W), F32),
        compiler_params=_cparams("parallel", "arbitrary"),
        name="mem_attend",
    )(u3, mkv)


def _out_proj_kernel(z_ref, mo_ref, w_ref, x_ref, o_ref, lhs_ref, wb_ref):
    i = pl.program_id(0)
    j = pl.program_id(1)

    @pl.when(j == 0)
    def _():
        lhs_ref[:, :MAIN_W] = z_ref[...].astype(lhs_ref.dtype)
        lhs_ref[:, MAIN_W:] = mo_ref[...].astype(lhs_ref.dtype)

    @pl.when(i == 0)
    def _():
        wb_ref[j] = w_ref[...].astype(BF16)

    o_ref[...] = x_ref[...] + _dot(lhs_ref[...], wb_ref[j])


def _out_proj2_kernel(z_ref, mo_ref, w_ref, x_ref, zs_ref, mos_ref, xs_ref, o_ref, os_ref, lhs_ref, wb_ref):
    _out_proj_kernel(z_ref, mo_ref, w_ref, x_ref, o_ref, lhs_ref, wb_ref)
    j = pl.program_id(1)

    @pl.when(pl.program_id(0) == 0)
    def _():
        lhs = jnp.concatenate([zs_ref[...], mos_ref[...]], axis=1)
        os_ref[...] = xs_ref[...] + _dot(lhs, wb_ref[j])


def out_proj(z, mo, w, layer, x, tm, tn, sample=None):
    m = x.shape[0]
    k = MAIN_W + MEM_W
    nj = D_MODEL // tn
    col = _once_per_column(nj)
    in_specs = [pl.BlockSpec((tm, MAIN_W), lambda i, j: (i, 0)),
                pl.BlockSpec((tm, MEM_W), lambda i, j: (i, 0)),
                pl.BlockSpec((None, k, tn), lambda i, j: (layer, 0, col(i, j))),
                pl.BlockSpec((tm, tn), lambda i, j: (i, j))]
    o_spec = pl.BlockSpec((tm, tn), lambda i, j: (i, j))
    scratch = [pltpu.VMEM((tm, k), _lhs_dtype(tm)), pltpu.VMEM((nj, k, tn), BF16)]
    if sample is None:
        return pl.pallas_call(
            _out_proj_kernel,
            grid=(m // tm, nj),
            in_specs=in_specs,
            out_specs=o_spec,
            out_shape=jax.ShapeDtypeStruct((m, D_MODEL), F32),
            scratch_shapes=scratch,
            compiler_params=_cparams("arbitrary", "arbitrary"),
            name="out_proj",
        )(z, mo, w, x)
    zs, mos, xs = sample
    ms = xs.shape[0]
    s_spec = pl.BlockSpec((ms, tn), lambda i, j: (0, col(i, j)))
    return pl.pallas_call(
        _out_proj2_kernel,
        grid=(m // tm, nj),
        in_specs=in_specs + [pl.BlockSpec((ms, MAIN_W), lambda i, j: (0, 0)),
                             pl.BlockSpec((ms, MEM_W), lambda i, j: (0, 0)), s_spec],
        out_specs=[o_spec, s_spec],
        out_shape=[jax.ShapeDtypeStruct((m, D_MODEL), F32), jax.ShapeDtypeStruct((ms, D_MODEL), F32)],
        scratch_shapes=scratch,
        compiler_params=_cparams("arbitrary", "arbitrary"),
        name="out_proj2",
    )(z, mo, w, x, zs, mos, xs)


def _ffn_kernel(x_ref, g_ref, wa_ref, wb_ref, wd_ref, gf_ref, o_ref, xn_ref, *, final):
    j = pl.program_id(1)

    @pl.when(j == 0)
    def _():
        x = x_ref[...]
        xn_ref[...] = _rms(x, g_ref[...]).astype(xn_ref.dtype)
        o_ref[...] = x

    xn = xn_ref[...]
    a = _dot(xn, wa_ref[...])
    b = _dot(xn, wb_ref[...])
    o_ref[...] += _dot(jax.nn.silu(a) * b, wd_ref[...])

    if final:
        @pl.when(j == pl.num_programs(1) - 1)
        def _():
            o_ref[...] = _rms(o_ref[...], gf_ref[...])


def _ffn2_kernel(x_ref, g_ref, wa_ref, wb_ref, wd_ref, gf_ref, xs_ref, o_ref, os_ref, xn_ref, xsn_ref, *, final):
    _ffn_kernel(x_ref, g_ref, wa_ref, wb_ref, wd_ref, gf_ref, o_ref, xn_ref, final=final)
    j = pl.program_id(1)

    @pl.when(pl.program_id(0) == 0)
    def _():
        @pl.when(j == 0)
        def _():
            xs = xs_ref[...]
            xsn_ref[...] = _rms(xs, g_ref[...])
            os_ref[...] = xs

        xsn = xsn_ref[...]
        a = _dot(xsn, wa_ref[...])
        b = _dot(xsn, wb_ref[...])
        os_ref[...] += _dot(jax.nn.silu(a) * b, wd_ref[...])

        if final:
            @pl.when(j == pl.num_programs(1) - 1)
            def _():
                os_ref[...] = _rms(os_ref[...], gf_ref[...])


def ffn(x, g, w_gu, w_down, layer, g_final, final, tm, tf, xs=None):
    m, k = x.shape
    nf = D_FF // tf
    in_specs = [pl.BlockSpec((tm, k), lambda i, j: (i, 0), pipeline_mode=pl.Buffered(1)),
                pl.BlockSpec((1, k), lambda i, j: (0, 0)),
                pl.BlockSpec((None, k, tf), lambda i, j: (layer, 0, j)),
                pl.BlockSpec((None, k, tf), lambda i, j: (layer, 0, nf + j)),
                pl.BlockSpec((None, tf, k), lambda i, j: (layer, j, 0)),
                pl.BlockSpec((1, k), lambda i, j: (0, 0))]
    o_spec = pl.BlockSpec((tm, k), lambda i, j: (i, 0))
    scratch = [pltpu.VMEM((tm, k), _lhs_dtype(tm))]
    args = (x, g.reshape(1, k), w_gu, w_gu, w_down, g_final.reshape(1, k))
    if xs is None:
        return pl.pallas_call(
            functools.partial(_ffn_kernel, final=final),
            grid=(m // tm, nf),
            in_specs=in_specs,
            out_specs=o_spec,
            out_shape=jax.ShapeDtypeStruct((m, k), F32),
            scratch_shapes=scratch,
            compiler_params=_cparams("parallel", "arbitrary"),
            name="ffn",
        )(*args)
    ms = xs.shape[0]
    s_spec = pl.BlockSpec((ms, k), lambda i, j: (0, 0))
    return pl.pallas_call(
        functools.partial(_ffn2_kernel, final=final),
        grid=(m // tm, nf),
        in_specs=in_specs + [s_spec],
        out_specs=[o_spec, s_spec],
        out_shape=[jax.ShapeDtypeStruct((m, k), F32), jax.ShapeDtypeStruct((ms, k), F32)],
        scratch_shapes=scratch + [pltpu.VMEM((ms, k), F32)],
        compiler_params=_cparams("arbitrary", "arbitrary"),
        name="ffn2",
    )(*args, xs)


def _compress_kernel(tbl_ref, *refs, n_steps):
    pages = refs[:PAGES_PER_STEP]
    nxt_ref, extra_ref, w1_ref, pe_ref, w2_ref, o_ref, w1b_ref = refs[PAGES_PER_STEP:]
    p = pl.program_id(1)
    on_extra = p == n_steps
    nxt_extra = p == n_steps - 1
    nh = CMP_STEP * N_KV
    width = CMP_STRIDE * HEAD_DIM
    d = HEAD_DIM

    def chunk(ref, n):
        return jnp.concatenate([ref[0, n * CMP_STRIDE + r] for r in range(CMP_STRIDE)], axis=1)

    chunks = []
    for k in range(PAGES_PER_STEP):
        for n in range(CH_PER_PAGE):
            x = chunk(pages[k], n)
            chunks.append(jnp.where(on_extra, chunk(extra_ref, n) if k == 0 else 0.0, x))
    chunks.append(jnp.where(on_extra, 0.0, jnp.where(nxt_extra, chunk(extra_ref, 0), chunk(nxt_ref, 0))))
    chunks.append(jnp.zeros((KV_SLOTS, width), F32))
    @pl.when((pl.program_id(0) == 0) & (p == 0))
    def _():
        for c in range(2):
            w1b_ref[c] = jnp.concatenate([w1_ref[c, 0], w1_ref[c, 1]], axis=1).astype(BF16)

    n_rows = N_KV * len(chunks)
    lhs = []
    for c in range(2):
        sl = slice(c * N_KV, (c + 1) * N_KV)
        rows = [jnp.concatenate([chunks[i][sl], chunks[i + 1][sl]], axis=0) for i in range(0, len(chunks), 2)]
        tail = [pe_ref[c, 0], pe_ref[c, 1], jnp.zeros((6, width), F32)]
        lhs.append(jnp.concatenate(rows + tail, axis=0).astype(BF16))
    hc = [_dot(lhs[c], w1b_ref[c]) for c in range(2)]
    for c in range(2):
        pe_term = hc[c][n_rows:n_rows + 1, :d] + hc[c][n_rows + 1:n_rows + 2, d:]
        hid = jax.nn.gelu(hc[c][:nh, :d] + hc[c][N_KV:nh + N_KV, d:] + pe_term)
        o_ref[0, c] = _dot(hid, w2_ref[c])


def compress(pages, table, extra, cmp_pe, cmp_w1, cmp_w2):
    nb, n_pages = table.shape
    n_steps = n_pages // PAGES_PER_STEP
    w1 = cmp_w1.reshape(2, 2, CMP_STRIDE * HEAD_DIM, HEAD_DIM)
    pe = cmp_pe.reshape(2, 2, 1, CMP_STRIDE * HEAD_DIM)

    def page_map(k):
        return lambda b, p, tbl: (tbl[b, jnp.minimum(p * PAGES_PER_STEP + k, n_pages - 1)], 0, 0, 0)

    in_specs = [pl.BlockSpec((1, PAGE_SIZE, KV_SLOTS, HEAD_DIM), page_map(k)) for k in range(PAGES_PER_STEP)]
    in_specs += [pl.BlockSpec((1, CMP_STRIDE, KV_SLOTS, HEAD_DIM), page_map(PAGES_PER_STEP)),
                 pl.BlockSpec((1, PAGE_SIZE, KV_SLOTS, HEAD_DIM), lambda b, p, tbl: (b, 0, 0, 0)),
                 pl.BlockSpec(w1.shape, lambda b, p, tbl: (0, 0, 0, 0)),
                 pl.BlockSpec(pe.shape, lambda b, p, tbl: (0, 0, 0, 0)),
                 pl.BlockSpec(cmp_w2.shape, lambda b, p, tbl: (0, 0, 0))]
    n_out = CMP_STEP * (n_steps + 1)
    out = pl.pallas_call(
        functools.partial(_compress_kernel, n_steps=n_steps),
        grid_spec=pltpu.PrefetchScalarGridSpec(
            num_scalar_prefetch=1,
            grid=(nb, n_steps + 1),
            in_specs=in_specs,
            out_specs=pl.BlockSpec((1, 2, CMP_STEP * N_KV, HEAD_DIM), lambda b, p, tbl: (b, 0, p, 0)),
            scratch_shapes=[pltpu.VMEM((2, CMP_STRIDE * HEAD_DIM, 2 * HEAD_DIM), BF16)],
        ),
        out_shape=jax.ShapeDtypeStruct((nb, 2, n_out * N_KV, HEAD_DIM), F32),
        compiler_params=_cparams("arbitrary", "arbitrary"),
        name="compress",
    )(table, *([pages] * (PAGES_PER_STEP + 1)), extra, w1, pe, cmp_w2)
    return jnp.transpose(out.reshape(nb, 2, n_out, N_KV, HEAD_DIM), (0, 1, 3, 2, 4))


def _cover(n_pad, j_pad, n_cmp):
    n = lax.broadcasted_iota(jnp.int32, (n_pad, j_pad), 0)
    j = lax.broadcasted_iota(jnp.int32, (n_pad, j_pad), 1)
    hit = (n * CMP_STRIDE < j * SLC_BLOCK + SLC_BLOCK) & (n * CMP_STRIDE + CMP_BLOCK - 1 >= j * SLC_BLOCK)
    return (hit & (n < n_cmp)).astype(F32)


def _cover_t(j_pad, n_pad, n_cmp):
    j = lax.broadcasted_iota(jnp.int32, (j_pad, n_pad), 0)
    n = lax.broadcasted_iota(jnp.int32, (j_pad, n_pad), 1)
    hit = (n * CMP_STRIDE < j * SLC_BLOCK + SLC_BLOCK) & (n * CMP_STRIDE + CMP_BLOCK - 1 >= j * SLC_BLOCK)
    return (hit & (n < n_cmp)).astype(F32)


def _softmax_rows(s):
    e = jnp.exp(s - jnp.max(s, axis=-1, keepdims=True))
    return e / jnp.sum(e, axis=-1, keepdims=True)


def _nsa_prompt_kernel(q_ref, gl_ref, bg_ref, cos_ref, sin_ref, ckv_ref, ks_ref, vt_ref, kw_ref, vw_ref,
                       o_ref, qt_ref, drop_ref, *, n_cmp, n_slc, kt):
    qb = pl.program_id(1)
    tq = Q_BLOCK
    cos2 = cos_ref[...]
    sin2 = sin_ref[...]
    n_pad = ckv_ref.shape[3]
    qpos1 = qb * tq + lax.broadcasted_iota(jnp.int32, (tq, 1), 0)
    qpos = jnp.concatenate([qpos1] * GQA, axis=0)
    gates = jax.nn.sigmoid(gl_ref[...] + bg_ref[...])

    n_i = lax.broadcasted_iota(jnp.int32, (1, n_pad), 1)
    vis_c = (n_i * CMP_STRIDE + CMP_BLOCK - 1 <= qpos) & (n_i < n_cmp)
    any_c = (qpos >= CMP_BLOCK - 1).astype(F32)
    cover_t = _cover_t(n_slc, n_pad, n_cmp)
    j_col = lax.broadcasted_iota(jnp.int32, (n_slc, 1), 0)
    qblk = jnp.right_shift(qb * tq + lax.broadcasted_iota(jnp.int32, (1, tq), 1), SLC_SHIFT)
    forced = (j_col == 0) | (j_col == qblk) | (j_col == qblk - 1)
    valid = j_col <= qblk
    n_sel = min(N_SELECT, n_slc)
    qpos_row = qb * tq + lax.broadcasted_iota(jnp.int32, (1, tq), 1)

    w_lo = jnp.maximum(qb * tq - WINDOW, 0)
    w_lo = pl.multiple_of(w_lo, tq)
    kpos_w = w_lo + lax.broadcasted_iota(jnp.int32, (1, WINDOW + tq), 1)
    vis_w = (kpos_w <= qpos) & (qpos - kpos_w < WINDOW)

    def gate_col(hd, br):
        return gates[:, hd * N_BRANCH + br:hd * N_BRANCH + br + 1]

    heads = range(N_KV)
    s_cmp, s_win = [], []
    for h in heads:
        q = jnp.concatenate(
            [_rope(q_ref[:, (h * GQA + g) * HEAD_DIM:(h * GQA + g + 1) * HEAD_DIM], cos2, sin2)
             for g in range(GQA)], axis=0)
        q = q * (SCALE * LOG2E)
        qs = q.astype(BF16)
        qt_ref[h] = jnp.transpose(q).astype(BF16)
        hs = slice(h * HEAD_DIM, (h + 1) * HEAD_DIM)
        s_cmp.append(jnp.where(vis_c, _dot_t(qs, ckv_ref[0, 0, h]), NEG))
        s_win.append(jnp.where(vis_w, _dot_t(qs, kw_ref[pl.ds(w_lo, WINDOW + tq), hs]), NEG))

    o_cmp, scores = [], []
    for h in heads:
        e = jnp.exp2(s_cmp[h] - jnp.max(s_cmp[h], axis=-1, keepdims=True))
        p = e / jnp.sum(e, axis=-1, keepdims=True) * any_c
        o_cmp.append(_dot(p, ckv_ref[0, 1, h]))
        p_sum = p[0:tq] + p[tq:2 * tq] + p[2 * tq:3 * tq]
        score = lax.dot_general(cover_t, p_sum, (((1,), (1,)), ((), ())), preferred_element_type=F32,
                                precision=lax.Precision.HIGHEST)
        scores.append(jnp.where(forced, jnp.inf, jnp.where(valid, score, -jnp.inf)))

    for h in heads:
        hs = slice(h * HEAD_DIM, (h + 1) * HEAD_DIM)
        e = jnp.exp2(s_win[h] - jnp.max(s_win[h], axis=-1, keepdims=True))
        o_w = _dot(e, vw_ref[pl.ds(w_lo, WINDOW + tq), hs]) / jnp.sum(e, axis=-1, keepdims=True)
        for g in range(GQA):
            hd = h * GQA + g
            r = slice(g * tq, (g + 1) * tq)
            o_ref[:, hd * HEAD_DIM:(hd + 1) * HEAD_DIM] = gate_col(hd, 0) * o_cmp[h][r] + gate_col(hd, 2) * o_w[r]

    for h in heads:
        score = scores[h]
        rank = jnp.zeros((n_slc, tq), F32)
        for i in range(n_slc):
            row = score[i:i + 1, :]
            beats = (row > score) | ((row == score) & (i < j_col))
            rank = rank + jnp.where(beats, 1.0, 0.0)
        drop_ref[h] = jnp.where(rank >= n_sel, NEG, 0.0)

    blocks_per_tile = kt // SLC_BLOCK

    def slc_step(t, carry):
        k0 = pl.multiple_of(t * kt, kt)
        kpos = k0 + lax.broadcasted_iota(jnp.int32, (kt, 1), 0)
        causal = jnp.where(kpos <= qpos_row, 0.0, NEG)
        def logits(h):
            hs = slice(h * HEAD_DIM, (h + 1) * HEAD_DIM)
            dropped = jnp.concatenate(
                [jnp.broadcast_to(drop_ref[h, pl.ds(t * blocks_per_tile + jb, 1), :], (SLC_BLOCK, tq))
                 for jb in range(blocks_per_tile)], axis=0)
            bias = dropped + causal
            return _dot(ks_ref[pl.ds(k0, kt), hs], qt_ref[h]) + jnp.concatenate([bias] * GQA, axis=1)

        scores = [logits(h) for h in range(N_KV)]
        out = []
        for h in range(N_KV):
            m, l, acc = carry[h]
            hs = slice(h * HEAD_DIM, (h + 1) * HEAD_DIM)
            s = scores[h]
            m_new = jnp.maximum(m, jnp.max(s, axis=0, keepdims=True))
            a = jnp.exp2(m - m_new)
            e = jnp.exp2(s - m_new)
            l = l * a + jnp.sum(e, axis=0, keepdims=True)
            acc = acc * a + _dot(vt_ref[t, hs, :], e)
            out.append((m_new, l, acc))
        return tuple(out)

    n_t = (qb * tq + tq + kt - 1) // kt
    init = (jnp.full((1, GQA * tq), NEG, F32), jnp.zeros((1, GQA * tq), F32), jnp.zeros((HEAD_DIM, GQA * tq), F32))
    done = lax.fori_loop(0, n_t, slc_step, (init,) * N_KV)
    for h in range(N_KV):
        _, l, acc = done[h]
        o_s = jnp.transpose(acc / l)
        for g in range(GQA):
            hd = h * GQA + g
            r = slice(g * tq, (g + 1) * tq)
            o_ref[:, hd * HEAD_DIM:(hd + 1) * HEAD_DIM] += gate_col(hd, 1) * o_s[r]


def nsa_prompt(u, b_gate_pad, cos2, sin2, ckv, slc_k, slc_vt, win_k, win_v, batch, seq):
    nqb = seq // Q_BLOCK
    n_pad = ckv.shape[3]
    half = N_KV * HEAD_DIM
    kspec = pl.BlockSpec((seq, half), lambda b, i: (b, 0))
    vt_spec = pl.BlockSpec((seq // SLC_KT, half, SLC_KT), lambda b, i: (b, 0, 0))
    return pl.pallas_call(
        functools.partial(_nsa_prompt_kernel, n_cmp=seq // CMP_STRIDE - 1, n_slc=seq // SLC_BLOCK, kt=SLC_KT),
        grid=(batch, nqb),
        in_specs=[pl.BlockSpec((Q_BLOCK, MAIN_W), lambda b, i: (b * nqb + i, 0)),
                  pl.BlockSpec((Q_BLOCK, GATE_PAD), lambda b, i: (b * nqb + i, (MAIN_W + MEM_W) // GATE_PAD)),
                  pl.BlockSpec((1, GATE_PAD), lambda b, i: (0, 0)),
                  pl.BlockSpec((Q_BLOCK, HEAD_DIM), lambda b, i: (i, 0)),
                  pl.BlockSpec((Q_BLOCK, HEAD_DIM), lambda b, i: (i, 0)),
                  pl.BlockSpec((1, 2, N_KV, n_pad, HEAD_DIM), lambda b, i: (b, 0, 0, 0, 0)),
                  kspec, vt_spec, kspec, kspec],
        out_specs=pl.BlockSpec((Q_BLOCK, MAIN_W), lambda b, i: (b * nqb + i, 0)),
        out_shape=jax.ShapeDtypeStruct((batch * seq, MAIN_W), F32),
        scratch_shapes=[pltpu.VMEM((N_KV, HEAD_DIM, GQA * Q_BLOCK), BF16),
                        pltpu.VMEM((N_KV, seq // SLC_BLOCK, Q_BLOCK), F32)],
        compiler_params=_cparams("parallel", "arbitrary"),
        name="nsa_prompt",
    )(u, u, b_gate_pad, cos2, sin2, ckv, slc_k, slc_vt, win_k, win_v)


HEAD_ROWS = 16


def _head_mask(h):
    r = lax.broadcasted_iota(jnp.int32, (HEAD_ROWS, 1), 0)
    return (r >= h * GQA) & (r < (h + 1) * GQA)


def _nsa_sample_a_kernel(q_ref, cos_ref, sin_ref, ckv_ref, wkv_ref, wnew_ref, qr_ref, oc_ref, ow_ref, sel_ref,
                         *, n_cmp, n_slc, j_pad):
    n_pad = ckv_ref.shape[3]
    q = _rope(q_ref[0], cos_ref[...], sin_ref[...])
    qr_ref[0] = q
    qs = q * SCALE
    n_i = lax.broadcasted_iota(jnp.int32, (1, n_pad), 1)
    vis_c = (n_i * CMP_STRIDE + CMP_BLOCK - 1 <= PAST_LEN) & (n_i < n_cmp)
    cover = _cover(n_pad, j_pad, n_cmp)
    j_row = lax.broadcasted_iota(jnp.int32, (1, j_pad), 1)
    j_col = lax.broadcasted_iota(jnp.int32, (j_pad, 1), 0)
    qblk = PAST_LEN // SLC_BLOCK
    lane = lax.broadcasted_iota(jnp.int32, (N_SELECT, HEAD_DIM), 1)
    r_col = lax.broadcasted_iota(jnp.int32, (N_SELECT, 1), 0)
    n_win = wkv_ref.shape[1] // KV_SLOTS
    kpos_w = PAST_LEN - n_win + lax.broadcasted_iota(jnp.int32, (1, n_win), 1)
    vis_w = (kpos_w <= PAST_LEN) & (PAST_LEN - kpos_w < WINDOW)

    heads = range(N_KV)
    forced = (j_row == 0) | (j_row == qblk) | (j_row == qblk - 1)
    s_cmp = [jnp.where(vis_c, _dot_t(qs, ckv_ref[0, 0, h]), NEG) for h in heads]
    s_win = [jnp.where(vis_w, _dot_t(qs, wkv_ref[0, pl.ds(h, n_win, stride=KV_SLOTS), :]), NEG) for h in heads]

    o_c = jnp.zeros((HEAD_ROWS, HEAD_DIM), F32)
    scores = []
    for h in heads:
        mine = _head_mask(h)
        p = _softmax_rows(s_cmp[h]) * float(PAST_LEN >= CMP_BLOCK - 1)
        o_c = jnp.where(mine, _dot(p, ckv_ref[0, 1, h]), o_c)
        p_sum = jnp.sum(jnp.where(mine, p, 0.0), axis=0, keepdims=True)
        score = jnp.dot(jnp.broadcast_to(p_sum, (8, n_pad)), cover, preferred_element_type=F32,
                        precision=lax.Precision.HIGHEST)
        score = jnp.where(forced, jnp.inf, jnp.where(j_row <= qblk, score, -jnp.inf))
        scores.append(jnp.where(j_row < n_slc, score, -jnp.inf))
    oc_ref[0] = o_c

    o_w = jnp.zeros((HEAD_ROWS, HEAD_DIM), F32)
    for h in heads:
        hs = slice(h * HEAD_DIM, (h + 1) * HEAD_DIM)
        vsl = slice((N_KV + h) * HEAD_DIM, (N_KV + h + 1) * HEAD_DIM)
        v_win = wkv_ref[0, pl.ds(N_KV + h, n_win, stride=KV_SLOTS), :]
        s = s_win[h]
        s_new = jnp.sum(qs * wnew_ref[0, :, hs], axis=-1, keepdims=True)
        m = jnp.maximum(jnp.max(s, axis=-1, keepdims=True), s_new)
        e = jnp.exp(s - m)
        e_new = jnp.exp(s_new - m)
        o = (_dot(e, v_win) + e_new * wnew_ref[0, :, vsl]) / (jnp.sum(e, axis=-1, keepdims=True) + e_new)
        o_w = jnp.where(_head_mask(h), o, o_w)
    ow_ref[0] = o_w

    sel_out = jnp.zeros((N_SELECT, HEAD_DIM), jnp.int32)
    for h in heads:
        score = scores[h]
        s_col = jnp.transpose(score)[:, 0:1]
        beats = ((s_col > score[0:1]) | ((s_col == score[0:1]) & (j_col < j_row))) & (j_col < n_slc)
        rank = jnp.sum(beats.astype(F32), axis=0, keepdims=True)
        hit = (rank == r_col.astype(F32)) & (j_row < n_slc)
        idx = jnp.sum(jnp.where(hit, j_row.astype(F32), 0.0), axis=1, keepdims=True)
        sel_out = jnp.where(lane == h, idx.astype(jnp.int32), sel_out)
    sel_ref[0] = sel_out


def nsa_sample_a(q16, cos2, sin2, ckv, win_cache, win_new, n_cmp, n_slc):
    db = q16.shape[0]
    n_pad = ckv.shape[3]
    win_rows = win_cache.shape[1]
    j_pad = -(-n_slc // 128) * 128
    hspec = pl.BlockSpec((1, HEAD_ROWS, HEAD_DIM), lambda b: (b, 0, 0))
    hout = jax.ShapeDtypeStruct((db, HEAD_ROWS, HEAD_DIM), F32)
    return pl.pallas_call(
        functools.partial(_nsa_sample_a_kernel, n_cmp=n_cmp, n_slc=n_slc, j_pad=j_pad),
        grid=(db,),
        in_specs=[hspec,
                  pl.BlockSpec((1, HEAD_DIM), lambda b: (0, 0)),
                  pl.BlockSpec((1, HEAD_DIM), lambda b: (0, 0)),
                  pl.BlockSpec((1, 2, N_KV, n_pad, HEAD_DIM), lambda b: (b, 0, 0, 0, 0)),
                  pl.BlockSpec((1, win_rows, HEAD_DIM), lambda b: (b, 0, 0)),
                  pl.BlockSpec((1, 1, KV_W), lambda b: (b, 0, 0))],
        out_specs=[hspec, hspec, hspec, pl.BlockSpec((1, N_SELECT, HEAD_DIM), lambda b: (b, 0, 0))],
        out_shape=[hout, hout, hout, jax.ShapeDtypeStruct((db, N_SELECT, HEAD_DIM), jnp.int32)],
        compiler_params=_cparams("arbitrary"),
        name="nsa_sample_a",
    )(q16, cos2, sin2, ckv, win_cache, win_new)


def _nsa_sample_b_kernel(sel_ref, tbl_ref, q_ref, *refs):
    blocks = refs[:N_SELECT]
    new_ref, oc_ref, ow_ref, gl_ref, bg_ref, o_ref, os_ref = refs[N_SELECT:]
    b = pl.program_id(0)
    h = pl.program_id(1)
    n_keys = N_SELECT * SLC_BLOCK
    lane_blk = jnp.right_shift(lax.broadcasted_iota(jnp.int32, (1, n_keys), 1), SLC_SHIFT)
    new_k = new_ref[0, pl.ds(h, 1), :]
    new_v = new_ref[0, pl.ds(h + N_KV, 1), :]
    ks, vs = [], []
    blk_of = jnp.zeros((1, n_keys), jnp.int32)
    for r in range(N_SELECT):
        blk = sel_ref[b, h, r]
        in_past = blk * SLC_BLOCK < PAST_LEN
        ks.append(jnp.where(in_past, blocks[r][0, pl.ds(h, SLC_BLOCK, stride=KV_SLOTS), :], new_k))
        vs.append(jnp.where(in_past, blocks[r][0, pl.ds(h + N_KV, SLC_BLOCK, stride=KV_SLOTS), :], new_v))
        blk_of = jnp.where(lane_blk == r, blk, blk_of)
    kpos = blk_of * SLC_BLOCK + (lax.broadcasted_iota(jnp.int32, (1, n_keys), 1) & (SLC_BLOCK - 1))
    s = jnp.where(kpos <= PAST_LEN, _dot_t(q_ref[0] * SCALE, jnp.concatenate(ks, axis=0)), NEG)
    e = jnp.exp(s - jnp.max(s, axis=-1, keepdims=True))
    o_s = _dot(e, jnp.concatenate(vs, axis=0)) / jnp.sum(e, axis=-1, keepdims=True)

    @pl.when(h == 0)
    def _():
        os_ref[...] = jnp.zeros(os_ref.shape, F32)

    row = lax.broadcasted_iota(jnp.int32, (HEAD_ROWS, 1), 0)
    mine = (row >= h * GQA) & (row < (h + 1) * GQA)
    os_ref[...] = jnp.where(mine, o_s, os_ref[...])

    @pl.when(h == pl.num_programs(1) - 1)
    def _():
        gates = jax.nn.sigmoid(gl_ref[0] + bg_ref[...])
        o_ref[0] = gates[:, 0:1] * oc_ref[0] + gates[:, 1:2] * os_ref[...] + gates[:, 2:3] * ow_ref[0]


def nsa_sample_b(sel, page_table, q_rope, slc_halves, slc_new, o_c, o_w, gate_logits, b_gate16):
    db = q_rope.shape[0]
    last_blk = PAST_LEN // SLC_BLOCK - 1
    halves = PAGE_SIZE // SLC_BLOCK

    def cache_map(r):
        def index(b, h, sel_ref, tbl_ref):
            j = jnp.minimum(sel_ref[b, h, r], last_blk)
            return (tbl_ref[b, j // halves] * halves + j % halves, 0, 0)
        return index

    hspec = pl.BlockSpec((1, HEAD_ROWS, HEAD_DIM), lambda b, h, s, t: (b, 0, 0))
    return pl.pallas_call(
        _nsa_sample_b_kernel,
        grid_spec=pltpu.PrefetchScalarGridSpec(
            num_scalar_prefetch=2,
            grid=(db, N_KV),
            in_specs=[hspec]
            + [pl.BlockSpec((1, SLC_BLOCK * KV_SLOTS, HEAD_DIM), cache_map(r)) for r in range(N_SELECT)]
            + [pl.BlockSpec((1, KV_SLOTS, HEAD_DIM), lambda b, h, s, t: (b, 0, 0)),
               hspec, hspec, hspec,
               pl.BlockSpec((HEAD_ROWS, HEAD_DIM), lambda b, h, s, t: (0, 0))],
            out_specs=hspec,
            scratch_shapes=[pltpu.VMEM((HEAD_ROWS, HEAD_DIM), F32)],
        ),
        out_shape=jax.ShapeDtypeStruct((db, HEAD_ROWS, HEAD_DIM), F32),
        compiler_params=_cparams("arbitrary", "arbitrary"),
        name="nsa_sample_b",
    )(sel, page_table, q_rope, *([slc_halves] * N_SELECT), slc_new, o_c, o_w, gate_logits, b_gate16)


def _rope_tables(pos):
    half = HEAD_DIM // 2
    inv = ROPE_THETA ** (-jnp.arange(half, dtype=F32) / half)
    ang = pos.astype(F32)[:, None] * inv[None, :]
    cos, sin = jnp.cos(ang), jnp.sin(ang)
    return jnp.concatenate([cos, cos], -1), jnp.concatenate([-sin, sin], -1)


def _pad_w_in_b(w):
    return jnp.concatenate([w[..., :MAIN_W], w[..., MAIN_W + GATE_W:],
                            jnp.pad(w[..., MAIN_W:MAIN_W + GATE_W], ((0, 0), (0, 0), (0, GATE_PAD - GATE_W)))], axis=-1)


def kernel(x_prompt, x_sample, mem_prompt, state_pool, cache_cmp_kv, cache_slc_kv, cache_win_kv, cache_mem_kv, page_table, g_mix, w_in_a, pool_grp_w, pool_scale, w_in_b, b_gate, g_kv, w_kv, cmp_pe, cmp_w1, cmp_w2, g_mem, w_mem_kv, w_out, g_ffn, w_gu, w_down, g_final):
    batch, seq, _ = x_prompt.shape
    db = x_sample.shape[0]
    m = batch * seq
    tm = PROJ_TM
    n_phys = cache_cmp_kv.shape[0]
    w_in_b_pad = _pad_w_in_b(w_in_b)
    xp = x_prompt.reshape(m, D_MODEL)
    xs = x_sample.reshape(db, D_MODEL)
    mem2 = mem_prompt.reshape(batch * MEM_TOKENS, D_MODEL)
    mem_kv = [rms_matmul(mem2, g_mem[l], w_mem_kv, l, 512, 512) for l in range(DEPTH)]
    mem_cache = cache_mem_kv.reshape(-1, MEM_HEAD_DIM)
    cos_p, sin_p = _rope_tables(jnp.arange(seq, dtype=jnp.int32))
    cos_s, sin_s = _rope_tables(jnp.full((db,), PAST_LEN, jnp.int32))
    pools_p, pools_s = [], []
    for l in range(DEPTH):
        if l == N_A:
            cmp_kv, slc_kv, win_kv, slc_k, slc_vt, win_k, win_v = kv_proj(xp, g_kv, w_kv, cos_p, sin_p, SLC_KT, True)
            table = jnp.arange(batch * seq // PAGE_SIZE, dtype=jnp.int32).reshape(batch, seq // PAGE_SIZE)
            ckv_p = compress(cmp_kv.reshape(-1, PAGE_SIZE, KV_SLOTS, HEAD_DIM), table,
                             jnp.zeros((batch, PAGE_SIZE, KV_SLOTS, HEAD_DIM), F32), cmp_pe, cmp_w1, cmp_w2)
            cmp_new, slc_new, win_new = kv_proj(xs, g_kv, w_kv, cos_s, sin_s, db, False)
            extra = jnp.pad(cmp_new.reshape(db, 1, KV_SLOTS, HEAD_DIM), ((0, 0), (0, PAGE_SIZE - 1), (0, 0), (0, 0)))
            ckv_s = compress(cache_cmp_kv.reshape(n_phys, PAGE_SIZE, KV_SLOTS, HEAD_DIM), page_table, extra,
                             cmp_pe, cmp_w1, cmp_w2)
            t_full = -(-(PAST_LEN + 1) // SLC_BLOCK) * SLC_BLOCK
            n_cmp = t_full // CMP_STRIDE - 1
            n_slc = t_full // SLC_BLOCK
        if l < N_A:
            up, us = rms_matmul(xp, g_mix[l], w_in_a, l, tm, 512, xs=xs)
            u3 = up.reshape(batch, seq, -1)
            zp = pool_prompt(u3, pool_grp_w[l], pool_scale[l]).reshape(m, MAIN_W)
            pools_p.append(u3[:, seq - POOL_BUF:, :MAIN_W])
            zs = pool_sample(us, jnp.transpose(state_pool[l], (1, 0, 2)), pool_grp_w[l], pool_scale[l])
            pools_s.append(jnp.concatenate([state_pool[l][:, 1:], us[:, None, :MAIN_W]], axis=1))
        else:
            j = l - N_A
            up, us = rms_matmul(xp, g_mix[l], w_in_b_pad, j, tm, 768, xs=xs)
            u3 = up.reshape(batch, seq, -1)
            bg = jnp.pad(b_gate[j], (0, GATE_PAD - GATE_W)).reshape(1, GATE_PAD)
            zp = nsa_prompt(up, bg, cos_p, sin_p, ckv_p, slc_k, slc_vt, win_k, win_v, batch, seq)
            q16 = jnp.pad(us[:, :MAIN_W].reshape(db, N_HEADS, HEAD_DIM), ((0, 0), (0, HEAD_ROWS - N_HEADS), (0, 0)))
            q_rope, o_c, o_w, sel = nsa_sample_a(
                q16, cos_s[:1], sin_s[:1], ckv_s, cache_win_kv.reshape(db, -1, HEAD_DIM),
                win_new.reshape(db, 1, KV_W), n_cmp, n_slc)
            sel = jnp.transpose(sel[:, :, :N_KV], (0, 2, 1))
            gl = us[:, MAIN_W + MEM_W:MAIN_W + MEM_W + GATE_W].reshape(db, N_HEADS, N_BRANCH)
            gl = jnp.pad(gl, ((0, 0), (0, HEAD_ROWS - N_HEADS), (0, HEAD_DIM - N_BRANCH)))
            bg16 = jnp.pad(b_gate[j].reshape(N_HEADS, N_BRANCH), ((0, HEAD_ROWS - N_HEADS), (0, HEAD_DIM - N_BRANCH)))
            z16 = nsa_sample_b(sel, page_table, q_rope,
                               cache_slc_kv.reshape(n_phys * 2, SLC_BLOCK * KV_SLOTS, HEAD_DIM),
                               slc_new.reshape(db, KV_SLOTS, HEAD_DIM), o_c, o_w, gl, bg16)
            zs = z16[:, :N_HEADS].reshape(db, MAIN_W)
        mo_p = mem_attend(u3, mem_kv[l], 512).reshape(m, MEM_W)
        mo_s = mem_attend(us.reshape(db, 1, -1), mem_cache, 1, layer=l).reshape(db, MEM_W)
        xp, xs = out_proj(zp, mo_p, w_out, l, xp, tm, 512, sample=(zs, mo_s, xs))
        xp, xs = ffn(xp, g_ffn[l], w_gu, w_down, l, g_final, l == DEPTH - 1, FFN_TM, FFN_TF, xs=xs)
    kv5 = (batch, seq, 2, N_KV, HEAD_DIM)
    kv5s = (db, 1, 2, N_KV, HEAD_DIM)
    n_keep = min(WINDOW, seq)
    win_s = jnp.concatenate([cache_win_kv, win_new.reshape(kv5s)], axis=1)
    win_s = win_s[:, win_s.shape[1] - min(WINDOW, win_s.shape[1]):]
    return (xp.reshape(batch, seq, D_MODEL), xs.reshape(db, 1, D_MODEL), jnp.stack(pools_p),
            cmp_kv.reshape(kv5), slc_kv.reshape(kv5), win_kv.reshape(kv5)[:, seq - n_keep:],
            jnp.stack(mem_kv).reshape(DEPTH, batch, MEM_TOKENS, 2, N_MEM_HEADS, MEM_HEAD_DIM),
            jnp.stack(pools_s), cmp_new.reshape(kv5s), slc_new.reshape(kv5s), win_s)
```

```python
import functools

import jax
import jax.numpy as jnp
from jax import lax
from jax.experimental import pallas as pl
from jax.experimental.pallas import tpu as pltpu

F32 = jnp.float32
BF16 = jnp.bfloat16

D_MODEL = 2048
DEPTH = 4
N_A = 2
PAST_LEN = 16384
PAGE_SIZE = 128
MEM_TOKENS = 256
N_MEM_HEADS = 4
MEM_HEAD_DIM = 128
MEM_W = 512
MAIN_W = 1536
POOL_WINDOWS = (2, 4, 8, 16)
POOL_GC = 384
POOL_BUF = 15
HEAD_DIM = 128
N_HEADS = 12
N_KV = 4
GQA = 3
N_BRANCH = 3
GATE_W = 36
GATE_PAD = 256
CMP_BLOCK = 32
CMP_STRIDE = 16
SLC_BLOCK = 64
SLC_SHIFT = 6
N_SELECT = 16
WINDOW = 512
Q_BLOCK = 128
D_FF = 5632
ROPE_THETA = 10000.0
EPS = 1e-6
NEG = -1e30
SCALE = HEAD_DIM ** -0.5
MEM_SCALE = MEM_HEAD_DIM ** -0.5
KV_SLOTS = 2 * N_KV
KV_W = KV_SLOTS * HEAD_DIM
W_IN_B = MAIN_W + MEM_W + GATE_PAD

VMEM_LIMIT = 56 * 1024 * 1024
SLC_KT = 512
LOG2E = 1.4426950408889634
PROJ_TM = 1024
FFN_TM = 1024
FFN_TF = 256
PAGES_PER_STEP = 16
CH_PER_PAGE = PAGE_SIZE // CMP_STRIDE
CMP_STEP = PAGES_PER_STEP * CH_PER_PAGE


def _cparams(*sem):
    return pltpu.CompilerParams(dimension_semantics=sem, vmem_limit_bytes=VMEM_LIMIT)


def _rms(x, g):
    return x * lax.rsqrt(jnp.mean(x * x, axis=-1, keepdims=True) + EPS) * g


def _dot(a, b):
    return jnp.dot(a.astype(BF16), b.astype(BF16), preferred_element_type=F32)


def _dot_t(a, b):
    return lax.dot_general(a.astype(BF16), b.astype(BF16), (((1,), (1,)), ((), ())),
                           preferred_element_type=F32)


def _lhs_dtype(rows):
    return BF16 if rows % 16 == 0 else F32


def _rope(x, cos2, sin2):
    return x * cos2 + pltpu.roll(x, HEAD_DIM // 2, axis=1) * sin2


def _rms_matmul_kernel(x_ref, g_ref, w_ref, o_ref, xn_ref, wb_ref):
    i = pl.program_id(0)
    j = pl.program_id(1)

    @pl.when(j == 0)
    def _():
        xn_ref[...] = _rms(x_ref[...], g_ref[...]).astype(xn_ref.dtype)

    @pl.when(i == 0)
    def _():
        wb_ref[j] = w_ref[...].astype(BF16)

    o_ref[...] = _dot(xn_ref[...], wb_ref[j])


def _rms_matmul2_kernel(x_ref, xs_ref, g_ref, w_ref, o_ref, os_ref, xn_ref, xsn_ref, wb_ref):
    _rms_matmul_kernel(x_ref, g_ref, w_ref, o_ref, xn_ref, wb_ref)
    i = pl.program_id(0)
    j = pl.program_id(1)

    @pl.when((i == 0) & (j == 0))
    def _():
        xsn_ref[...] = _rms(xs_ref[...], g_ref[...])

    @pl.when(i == 0)
    def _():
        os_ref[...] = _dot(xsn_ref[...], wb_ref[j])


def _once_per_column(nj):
    return lambda i, j: jnp.where(i == 0, j, nj - 1)


def rms_matmul(x, g, w, layer, tm, tn, xs=None):
    m, k = x.shape
    n = w.shape[2]
    nj = n // tn
    col = _once_per_column(nj)
    x_spec = pl.BlockSpec((tm, k), lambda i, j: (i, 0))
    g_spec = pl.BlockSpec((1, k), lambda i, j: (0, 0))
    w_spec = pl.BlockSpec((None, k, tn), lambda i, j: (layer, 0, col(i, j)))
    o_spec = pl.BlockSpec((tm, tn), lambda i, j: (i, j))
    scratch = [pltpu.VMEM((tm, k), _lhs_dtype(tm)), pltpu.VMEM((nj, k, tn), BF16)]
    if xs is None:
        return pl.pallas_call(
            _rms_matmul_kernel,
            grid=(m // tm, nj),
            in_specs=[x_spec, g_spec, w_spec],
            out_specs=o_spec,
            out_shape=jax.ShapeDtypeStruct((m, n), F32),
            scratch_shapes=scratch,
            compiler_params=_cparams("arbitrary", "arbitrary"),
            name="rms_matmul",
        )(x, g.reshape(1, k), w)
    ms = xs.shape[0]
    return pl.pallas_call(
        _rms_matmul2_kernel,
        grid=(m // tm, nj),
        in_specs=[x_spec, pl.BlockSpec((ms, k), lambda i, j: (0, 0)), g_spec, w_spec],
        out_specs=[o_spec, pl.BlockSpec((ms, tn), lambda i, j: (0, col(i, j)))],
        out_shape=[jax.ShapeDtypeStruct((m, n), F32), jax.ShapeDtypeStruct((ms, n), F32)],
        scratch_shapes=[scratch[0], pltpu.VMEM((ms, k), F32), scratch[1]],
        compiler_params=_cparams("arbitrary", "arbitrary"),
        name="rms_matmul2",
    )(x, xs, g.reshape(1, k), w)


def _kv_proj_kernel(x_ref, g_ref, w_ref, cos_ref, sin_ref, cmp_ref, slc_ref, win_ref, *rest):
    xn_ref, wb_ref = rest[-2:]
    attn = rest[:-2]
    i = pl.program_id(0)
    j = pl.program_id(1)
    half = N_KV * HEAD_DIM

    @pl.when(j == 0)
    def _():
        xn_ref[...] = _rms(x_ref[...], g_ref[...]).astype(xn_ref.dtype)

    @pl.when(i == 0)
    def _():
        wb_ref[j] = w_ref[...].astype(BF16)

    for br, ref in enumerate((cmp_ref, slc_ref, win_ref)):
        for c in range(2):
            @pl.when(j == 2 * br + c)
            def _(ref=ref, br=br, c=c):
                y = _dot(xn_ref[...], wb_ref[2 * br + c])
                val = y
                if c == 0:
                    cos2 = cos_ref[...]
                    sin2 = sin_ref[...]
                    val = jnp.concatenate(
                        [_rope(y[:, h * HEAD_DIM:(h + 1) * HEAD_DIM], cos2, sin2) for h in range(N_KV)], axis=1)
                for h in range(N_KV):
                    ref[:, c * N_KV + h, :] = val[:, h * HEAD_DIM:(h + 1) * HEAD_DIM]
                if attn and br == 1 and c == 1:
                    attn[1][0] = jnp.transpose(val).astype(BF16)
                elif attn and br > 0:
                    attn[2 * (br - 1) + c][...] = val.astype(BF16)


def kv_proj(x, g, w, cos2, sin2, tm, attn_copies):
    m, k = x.shape
    half = N_KV * HEAD_DIM
    nt = cos2.shape[0] // tm
    nj = 2 * N_BRANCH
    col = _once_per_column(nj)
    out = jax.ShapeDtypeStruct((m, KV_SLOTS, HEAD_DIM), F32)
    ospec = pl.BlockSpec((tm, KV_SLOTS, HEAD_DIM), lambda i, j: (i, 0, 0))
    out_shape, out_specs = [out, out, out], [ospec, ospec, ospec]
    if attn_copies:
        out16 = jax.ShapeDtypeStruct((m, half), BF16)
        ospec16 = pl.BlockSpec((tm, half), lambda i, j: (i, 0))
        out_shape += [out16, jax.ShapeDtypeStruct((m // tm, half, tm), BF16), out16, out16]
        out_specs += [ospec16, pl.BlockSpec((1, half, tm), lambda i, j: (i, 0, 0)), ospec16, ospec16]
    return pl.pallas_call(
        _kv_proj_kernel,
        grid=(m // tm, nj),
        in_specs=[pl.BlockSpec((tm, k), lambda i, j: (i, 0)),
                  pl.BlockSpec((1, k), lambda i, j: (0, 0)),
                  pl.BlockSpec((k, half), lambda i, j: (0, col(i, j))),
                  pl.BlockSpec((tm, HEAD_DIM), lambda i, j: (i % nt, 0)),
                  pl.BlockSpec((tm, HEAD_DIM), lambda i, j: (i % nt, 0))],
        out_specs=out_specs,
        out_shape=out_shape,
        scratch_shapes=[pltpu.VMEM((tm, k), _lhs_dtype(tm)), pltpu.VMEM((nj, k, half), BF16)],
        compiler_params=_cparams("arbitrary", "arbitrary"),
        name="kv_proj",
    )(x, g.reshape(1, k), w, cos2, sin2)


def _pool_prompt_kernel(cur_ref, prev_ref, w_ref, sc_ref, o_ref, *, ts):
    i = pl.program_id(1)
    keep = (i > 0).astype(F32)
    pos = (i * ts + lax.broadcasted_iota(jnp.int32, (ts, 1), 0) + 1).astype(F32)
    diffs = []
    for g, w in enumerate(POOL_WINDOWS):
        sl = slice(g * POOL_GC, (g + 1) * POOL_GC)
        x = cur_ref[0, :, sl]
        ext = jnp.concatenate([prev_ref[0, :, sl] * keep, x], axis=0)
        span = 1
        while span < w:
            ext = ext[span:] + ext[:-span]
            span *= 2
        win = ext[17 - w:17 - w + ts]
        diffs.append((win / jnp.minimum(pos, float(w)) - x).astype(BF16))
    for g in range(len(POOL_WINDOWS)):
        sl = slice(g * POOL_GC, (g + 1) * POOL_GC)
        o_ref[0, :, sl] = _dot(diffs[g], w_ref[g]) * sc_ref[:, sl]


def pool_prompt(u3, w_grp, scale, ts=512):
    b, s, n = u3.shape
    r = ts // 16
    return pl.pallas_call(
        functools.partial(_pool_prompt_kernel, ts=ts),
        grid=(b, s // ts),
        in_specs=[pl.BlockSpec((1, ts, MAIN_W), lambda bi, i: (bi, i, 0)),
                  pl.BlockSpec((1, 16, MAIN_W), lambda bi, i: (bi, jnp.maximum(i * r - 1, 0), 0)),
                  pl.BlockSpec((4, POOL_GC, POOL_GC), lambda bi, i: (0, 0, 0)),
                  pl.BlockSpec((1, MAIN_W), lambda bi, i: (0, 0))],
        out_specs=pl.BlockSpec((1, ts, MAIN_W), lambda bi, i: (bi, i, 0)),
        out_shape=jax.ShapeDtypeStruct((b, s, MAIN_W), F32),
        compiler_params=_cparams("parallel", "arbitrary"),
        name="pool_prompt",
    )(u3, u3, w_grp, scale.reshape(1, MAIN_W))


def _pool_sample_kernel(cur_ref, buf_ref, w_ref, sc_ref, o_ref):
    for g, w in enumerate(POOL_WINDOWS):
        sl = slice(g * POOL_GC, (g + 1) * POOL_GC)
        x = cur_ref[:, sl]
        win = x
        for r in range(POOL_BUF - (w - 1), POOL_BUF):
            win = win + buf_ref[r, :, sl]
        d = win / float(w) - x
        o_ref[:, sl] = _dot(d, w_ref[g]) * sc_ref[:, sl]


def pool_sample(u, buf_t, w_grp, scale):
    db = u.shape[0]
    return pl.pallas_call(
        _pool_sample_kernel,
        grid=(1,),
        in_specs=[pl.BlockSpec((db, MAIN_W), lambda i: (0, 0)),
                  pl.BlockSpec((POOL_BUF, db, MAIN_W), lambda i: (0, 0, 0)),
                  pl.BlockSpec((4, POOL_GC, POOL_GC), lambda i: (0, 0, 0)),
                  pl.BlockSpec((1, MAIN_W), lambda i: (0, 0))],
        out_specs=pl.BlockSpec((db, MAIN_W), lambda i: (0, 0)),
        out_shape=jax.ShapeDtypeStruct((db, MAIN_W), F32),
        compiler_params=_cparams("arbitrary"),
        name="pool_sample",
    )(u, buf_t, w_grp, scale.reshape(1, MAIN_W))


def _mem_attend_kernel(q_ref, kv_ref, o_ref, *, tq, slots):
    rows = max(tq, 8)
    logits, values = [], []
    for h in range(N_MEM_HEADS):
        sl = slice(h * MEM_HEAD_DIM, (h + 1) * MEM_HEAD_DIM)
        q = q_ref[0, :, sl]
        if tq < rows:
            q = jnp.broadcast_to(q[0:1], (rows, MEM_HEAD_DIM))
        if slots:
            k = kv_ref[pl.ds(h, MEM_TOKENS, stride=2 * N_MEM_HEADS), :]
            v = kv_ref[pl.ds(N_MEM_HEADS + h, MEM_TOKENS, stride=2 * N_MEM_HEADS), :]
        else:
            k = kv_ref[:, sl]
            v = kv_ref[:, MEM_W + h * MEM_HEAD_DIM:MEM_W + (h + 1) * MEM_HEAD_DIM]
        logits.append(_dot_t(q, k) * MEM_SCALE)
        values.append(v)
    for h in range(N_MEM_HEADS):
        sl = slice(h * MEM_HEAD_DIM, (h + 1) * MEM_HEAD_DIM)
        s, v = logits[h], values[h]
        e = jnp.exp(s - jnp.max(s, axis=-1, keepdims=True))
        o = _dot(e, v) / jnp.sum(e, axis=-1, keepdims=True)
        o_ref[0, :, sl] = o[0:tq]


def mem_attend(u3, mkv, tq, layer=None):
    b, s, _ = u3.shape
    if layer is None:
        kv_spec = pl.BlockSpec((MEM_TOKENS, 2 * MEM_W), lambda bi, i: (bi, 0))
    else:
        kv_spec = pl.BlockSpec((MEM_TOKENS * 2 * N_MEM_HEADS, MEM_HEAD_DIM), lambda bi, i: (layer * b + bi, 0))
    return pl.pallas_call(
        functools.partial(_mem_attend_kernel, tq=tq, slots=layer is not None),
        grid=(b, s // tq),
        in_specs=[pl.BlockSpec((1, tq, MEM_W), lambda bi, i: (bi, i, MAIN_W // MEM_W)), kv_spec],
        out_specs=pl.BlockSpec((1, tq, MEM_W), lambda bi, i: (bi, i, 0)),
        out_shape=jax.ShapeDtypeStruct((b, s, MEM_W), F32),
        compiler_params=_cparams("parallel", "arbitrary"),
        name="mem_attend",
    )(u3, mkv)


def _out_proj_kernel(z_ref, mo_ref, w_ref, x_ref, o_ref, lhs_ref, wb_ref):
    i = pl.program_id(0)
    j = pl.program_id(1)

    @pl.when(j == 0)
    def _():
        lhs_ref[:, :MAIN_W] = z_ref[...].astype(lhs_ref.dtype)
        lhs_ref[:, MAIN_W:] = mo_ref[...].astype(lhs_ref.dtype)

    @pl.when(i == 0)
    def _():
        wb_ref[j] = w_ref[...].astype(BF16)

    o_ref[...] = x_ref[...] + _dot(lhs_ref[...], wb_ref[j])


def _out_proj2_kernel(z_ref, mo_ref, w_ref, x_ref, zs_ref, mos_ref, xs_ref, o_ref, os_ref, lhs_ref, wb_ref):
    _out_proj_kernel(z_ref, mo_ref, w_ref, x_ref, o_ref, lhs_ref, wb_ref)
    j = pl.program_id(1)

    @pl.when(pl.program_id(0) == 0)
    def _():
        lhs = jnp.concatenate([zs_ref[...], mos_ref[...]], axis=1)
        os_ref[...] = xs_ref[...] + _dot(lhs, wb_ref[j])


def out_proj(z, mo, w, layer, x, tm, tn, sample=None):
    m = x.shape[0]
    k = MAIN_W + MEM_W
    nj = D_MODEL // tn
    col = _once_per_column(nj)
    in_specs = [pl.BlockSpec((tm, MAIN_W), lambda i, j: (i, 0)),
                pl.BlockSpec((tm, MEM_W), lambda i, j: (i, 0)),
                pl.BlockSpec((None, k, tn), lambda i, j: (layer, 0, col(i, j))),
                pl.BlockSpec((tm, tn), lambda i, j: (i, j))]
    o_spec = pl.BlockSpec((tm, tn), lambda i, j: (i, j))
    scratch = [pltpu.VMEM((tm, k), _lhs_dtype(tm)), pltpu.VMEM((nj, k, tn), BF16)]
    if sample is None:
        return pl.pallas_call(
            _out_proj_kernel,
            grid=(m // tm, nj),
            in_specs=in_specs,
            out_specs=o_spec,
            out_shape=jax.ShapeDtypeStruct((m, D_MODEL), F32),
            scratch_shapes=scratch,
            compiler_params=_cparams("arbitrary", "arbitrary"),
            name="out_proj",
        )(z, mo, w, x)
    zs, mos, xs = sample
    ms = xs.shape[0]
    s_spec = pl.BlockSpec((ms, tn), lambda i, j: (0, col(i, j)))
    return pl.pallas_call(
        _out_proj2_kernel,
        grid=(m // tm, nj),
        in_specs=in_specs + [pl.BlockSpec((ms, MAIN_W), lambda i, j: (0, 0)),
                             pl.BlockSpec((ms, MEM_W), lambda i, j: (0, 0)), s_spec],
        out_specs=[o_spec, s_spec],
        out_shape=[jax.ShapeDtypeStruct((m, D_MODEL), F32), jax.ShapeDtypeStruct((ms, D_MODEL), F32)],
        scratch_shapes=scratch,
        compiler_params=_cparams("arbitrary", "arbitrary"),
        name="out_proj2",
    )(z, mo, w, x, zs, mos, xs)


def _ffn_kernel(x_ref, g_ref, wa_ref, wb_ref, wd_ref, gf_ref, o_ref, xn_ref, *, final):
    j = pl.program_id(1)

    @pl.when(j == 0)
    def _():
        x = x_ref[...]
        xn_ref[...] = _rms(x, g_ref[...]).astype(xn_ref.dtype)
        o_ref[...] = x

    xn = xn_ref[...]
    a = _dot(xn, wa_ref[...])
    b = _dot(xn, wb_ref[...])
    o_ref[...] += _dot(jax.nn.silu(a) * b, wd_ref[...])

    if final:
        @pl.when(j == pl.num_programs(1) - 1)
        def _():
            o_ref[...] = _rms(o_ref[...], gf_ref[...])


def _ffn2_kernel(x_ref, g_ref, wa_ref, wb_ref, wd_ref, gf_ref, xs_ref, o_ref, os_ref, xn_ref, xsn_ref, *, final):
    _ffn_kernel(x_ref, g_ref, wa_ref, wb_ref, wd_ref, gf_ref, o_ref, xn_ref, final=final)
    j = pl.program_id(1)

    @pl.when(pl.program_id(0) == 0)
    def _():
        @pl.when(j == 0)
        def _():
            xs = xs_ref[...]
            xsn_ref[...] = _rms(xs, g_ref[...])
            os_ref[...] = xs

        xsn = xsn_ref[...]
        a = _dot(xsn, wa_ref[...])
        b = _dot(xsn, wb_ref[...])
        os_ref[...] += _dot(jax.nn.silu(a) * b, wd_ref[...])

        if final:
            @pl.when(j == pl.num_programs(1) - 1)
            def _():
                os_ref[...] = _rms(os_ref[...], gf_ref[...])


def ffn(x, g, w_gu, w_down, layer, g_final, final, tm, tf, xs=None):
    m, k = x.shape
    nf = D_FF // tf
    in_specs = [pl.BlockSpec((tm, k), lambda i, j: (i, 0), pipeline_mode=pl.Buffered(1)),
                pl.BlockSpec((1, k), lambda i, j: (0, 0)),
                pl.BlockSpec((None, k, tf), lambda i, j: (layer, 0, j)),
                pl.BlockSpec((None, k, tf), lambda i, j: (layer, 0, nf + j)),
                pl.BlockSpec((None, tf, k), lambda i, j: (layer, j, 0)),
                pl.BlockSpec((1, k), lambda i, j: (0, 0))]
    o_spec = pl.BlockSpec((tm, k), lambda i, j: (i, 0))
    scratch = [pltpu.VMEM((tm, k), _lhs_dtype(tm))]
    args = (x, g.reshape(1, k), w_gu, w_gu, w_down, g_final.reshape(1, k))
    if xs is None:
        return pl.pallas_call(
            functools.partial(_ffn_kernel, final=final),
            grid=(m // tm, nf),
            in_specs=in_specs,
            out_specs=o_spec,
            out_shape=jax.ShapeDtypeStruct((m, k), F32),
            scratch_shapes=scratch,
            compiler_params=_cparams("parallel", "arbitrary"),
            name="ffn",
        )(*args)
    ms = xs.shape[0]
    s_spec = pl.BlockSpec((ms, k), lambda i, j: (0, 0))
    return pl.pallas_call(
        functools.partial(_ffn2_kernel, final=final),
        grid=(m // tm, nf),
        in_specs=in_specs + [s_spec],
        out_specs=[o_spec, s_spec],
        out_shape=[jax.ShapeDtypeStruct((m, k), F32), jax.ShapeDtypeStruct((ms, k), F32)],
        scratch_shapes=scratch + [pltpu.VMEM((ms, k), F32)],
        compiler_params=_cparams("arbitrary", "arbitrary"),
        name="ffn2",
    )(*args, xs)


def _compress_kernel(tbl_ref, *refs, n_steps):
    pages = refs[:PAGES_PER_STEP]
    nxt_ref, extra_ref, w1_ref, pe_ref, w2_ref, o_ref, w1b_ref, lhs_ref = refs[PAGES_PER_STEP:]
    p = pl.program_id(1)
    on_extra = p == n_steps
    nxt_extra = p == n_steps - 1
    nh = CMP_STEP * N_KV
    width = CMP_STRIDE * HEAD_DIM
    d = HEAD_DIM
    group = 4
    n_groups = CMP_STEP // group + 1
    n_rows = n_groups * group * N_KV

    @pl.when((pl.program_id(0) == 0) & (p == 0))
    def _():
        for c in range(2):
            w1b_ref[c] = jnp.concatenate([w1_ref[c, 0], w1_ref[c, 1]], axis=1).astype(BF16)

    def slots(n, r):
        if n < CMP_STEP:
            k, i = divmod(n, CH_PER_PAGE)
            x = pages[k][0, i * CMP_STRIDE + r]
            return jnp.where(on_extra, extra_ref[0, i * CMP_STRIDE + r] if k == 0 else 0.0, x)
        if n == CMP_STEP:
            return jnp.where(on_extra, 0.0, jnp.where(nxt_extra, extra_ref[0, r], nxt_ref[0, r]))
        return jnp.zeros((KV_SLOTS, HEAD_DIM), F32)

    for q in range(n_groups):
        for r in range(CMP_STRIDE):
            tiles = [slots(group * q + i, r) for i in range(group)]
            for c in range(2):
                piece = jnp.concatenate([t[c * N_KV:(c + 1) * N_KV] for t in tiles], axis=0)
                lhs_ref[c, group * N_KV * q:group * N_KV * (q + 1), r * HEAD_DIM:(r + 1) * HEAD_DIM] = piece.astype(BF16)
    for c in range(2):
        tail = jnp.concatenate([pe_ref[c, 0], pe_ref[c, 1], jnp.zeros((14, width), F32)], axis=0)
        lhs_ref[c, n_rows:n_rows + 16, :] = tail.astype(BF16)
    hc = [_dot(lhs_ref[c], w1b_ref[c]) for c in range(2)]
    for c in range(2):
        pe_term = hc[c][n_rows:n_rows + 1, :d] + hc[c][n_rows + 1:n_rows + 2, d:]
        hid = jax.nn.gelu(hc[c][:nh, :d] + hc[c][N_KV:nh + N_KV, d:] + pe_term)
        o_ref[0, c] = _dot(hid, w2_ref[c])


def compress(pages, table, extra, cmp_pe, cmp_w1, cmp_w2):
    nb, n_pages = table.shape
    n_steps = n_pages // PAGES_PER_STEP
    w1 = cmp_w1.reshape(2, 2, CMP_STRIDE * HEAD_DIM, HEAD_DIM)
    pe = cmp_pe.reshape(2, 2, 1, CMP_STRIDE * HEAD_DIM)

    def page_map(k):
        return lambda b, p, tbl: (tbl[b, jnp.minimum(p * PAGES_PER_STEP + k, n_pages - 1)], 0, 0, 0)

    in_specs = [pl.BlockSpec((1, PAGE_SIZE, KV_SLOTS, HEAD_DIM), page_map(k)) for k in range(PAGES_PER_STEP)]
    in_specs += [pl.BlockSpec((1, CMP_STRIDE, KV_SLOTS, HEAD_DIM), page_map(PAGES_PER_STEP)),
                 pl.BlockSpec((1, PAGE_SIZE, KV_SLOTS, HEAD_DIM), lambda b, p, tbl: (b, 0, 0, 0)),
                 pl.BlockSpec(w1.shape, lambda b, p, tbl: (0, 0, 0, 0)),
                 pl.BlockSpec(pe.shape, lambda b, p, tbl: (0, 0, 0, 0)),
                 pl.BlockSpec(cmp_w2.shape, lambda b, p, tbl: (0, 0, 0))]
    n_out = CMP_STEP * (n_steps + 1)
    out = pl.pallas_call(
        functools.partial(_compress_kernel, n_steps=n_steps),
        grid_spec=pltpu.PrefetchScalarGridSpec(
            num_scalar_prefetch=1,
            grid=(nb, n_steps + 1),
            in_specs=in_specs,
            out_specs=pl.BlockSpec((1, 2, CMP_STEP * N_KV, HEAD_DIM), lambda b, p, tbl: (b, 0, p, 0)),
            scratch_shapes=[pltpu.VMEM((2, CMP_STRIDE * HEAD_DIM, 2 * HEAD_DIM), BF16),
                            pltpu.VMEM((2, (CMP_STEP + 8) * N_KV, CMP_STRIDE * HEAD_DIM), BF16)],
        ),
        out_shape=jax.ShapeDtypeStruct((nb, 2, n_out * N_KV, HEAD_DIM), F32),
        compiler_params=_cparams("arbitrary", "arbitrary"),
        name="compress",
    )(table, *([pages] * (PAGES_PER_STEP + 1)), extra, w1, pe, cmp_w2)
    return jnp.transpose(out.reshape(nb, 2, n_out, N_KV, HEAD_DIM), (0, 1, 3, 2, 4))


def _cover(n_pad, j_pad, n_cmp):
    n = lax.broadcasted_iota(jnp.int32, (n_pad, j_pad), 0)
    j = lax.broadcasted_iota(jnp.int32, (n_pad, j_pad), 1)
    hit = (n * CMP_STRIDE < j * SLC_BLOCK + SLC_BLOCK) & (n * CMP_STRIDE + CMP_BLOCK - 1 >= j * SLC_BLOCK)
    return (hit & (n < n_cmp)).astype(F32)


def _cover_t(j_pad, n_pad, n_cmp):
    j = lax.broadcasted_iota(jnp.int32, (j_pad, n_pad), 0)
    n = lax.broadcasted_iota(jnp.int32, (j_pad, n_pad), 1)
    hit = (n * CMP_STRIDE < j * SLC_BLOCK + SLC_BLOCK) & (n * CMP_STRIDE + CMP_BLOCK - 1 >= j * SLC_BLOCK)
    return (hit & (n < n_cmp)).astype(F32)


def _softmax_rows(s):
    e = jnp.exp(s - jnp.max(s, axis=-1, keepdims=True))
    return e / jnp.sum(e, axis=-1, keepdims=True)


def _nsa_prompt_kernel(q_ref, gl_ref, bg_ref, cos_ref, sin_ref, ckv_ref, ks_ref, vt_ref, kw_ref, vw_ref,
                       o_ref, qt_ref, drop_ref, *, n_cmp, n_slc, kt):
    qb = pl.program_id(1)
    tq = Q_BLOCK
    cos2 = cos_ref[...]
    sin2 = sin_ref[...]
    n_pad = ckv_ref.shape[3]
    qpos1 = qb * tq + lax.broadcasted_iota(jnp.int32, (tq, 1), 0)
    qpos = jnp.concatenate([qpos1] * GQA, axis=0)
    gates = jax.nn.sigmoid(gl_ref[...] + bg_ref[...])

    n_i = lax.broadcasted_iota(jnp.int32, (1, n_pad), 1)
    vis_c = (n_i * CMP_STRIDE + CMP_BLOCK - 1 <= qpos) & (n_i < n_cmp)
    any_c = (qpos >= CMP_BLOCK - 1).astype(F32)
    cover_t = _cover_t(n_slc, n_pad, n_cmp)
    j_col = lax.broadcasted_iota(jnp.int32, (n_slc, 1), 0)
    qblk = jnp.right_shift(qb * tq + lax.broadcasted_iota(jnp.int32, (1, tq), 1), SLC_SHIFT)
    forced = (j_col == 0) | (j_col == qblk) | (j_col == qblk - 1)
    valid = j_col <= qblk
    n_sel = min(N_SELECT, n_slc)
    qpos_row = qb * tq + lax.broadcasted_iota(jnp.int32, (1, tq), 1)

    w_lo = jnp.maximum(qb * tq - WINDOW, 0)
    w_lo = pl.multiple_of(w_lo, tq)
    kpos_w = w_lo + lax.broadcasted_iota(jnp.int32, (1, WINDOW + tq), 1)
    vis_w = (kpos_w <= qpos) & (qpos - kpos_w < WINDOW)

    def gate_col(hd, br):
        return gates[:, hd * N_BRANCH + br:hd * N_BRANCH + br + 1]

    heads = range(N_KV)
    s_cmp, s_win = [], []
    for h in heads:
        q = jnp.concatenate(
            [_rope(q_ref[:, (h * GQA + g) * HEAD_DIM:(h * GQA + g + 1) * HEAD_DIM], cos2, sin2)
             for g in range(GQA)], axis=0)
        q = q * (SCALE * LOG2E)
        qs = q.astype(BF16)
        qt_ref[h] = jnp.transpose(q).astype(BF16)
        hs = slice(h * HEAD_DIM, (h + 1) * HEAD_DIM)
        s_cmp.append(jnp.where(vis_c, _dot_t(qs, ckv_ref[0, 0, h]), NEG))
        s_win.append(jnp.where(vis_w, _dot_t(qs, kw_ref[pl.ds(w_lo, WINDOW + tq), hs]), NEG))

    o_cmp, scores = [], []
    for h in heads:
        e = jnp.exp2(s_cmp[h] - jnp.max(s_cmp[h], axis=-1, keepdims=True))
        p = e / jnp.sum(e, axis=-1, keepdims=True) * any_c
        o_cmp.append(_dot(p, ckv_ref[0, 1, h]))
        p_sum = p[0:tq] + p[tq:2 * tq] + p[2 * tq:3 * tq]
        score = lax.dot_general(cover_t, p_sum, (((1,), (1,)), ((), ())), preferred_element_type=F32,
                                precision=lax.Precision.HIGHEST)
        scores.append(jnp.where(forced, jnp.inf, jnp.where(valid, score, -jnp.inf)))

    for h in heads:
        hs = slice(h * HEAD_DIM, (h + 1) * HEAD_DIM)
        e = jnp.exp2(s_win[h] - jnp.max(s_win[h], axis=-1, keepdims=True))
        o_w = _dot(e, vw_ref[pl.ds(w_lo, WINDOW + tq), hs]) / jnp.sum(e, axis=-1, keepdims=True)
        for g in range(GQA):
            hd = h * GQA + g
            r = slice(g * tq, (g + 1) * tq)
            o_ref[:, hd * HEAD_DIM:(hd + 1) * HEAD_DIM] = gate_col(hd, 0) * o_cmp[h][r] + gate_col(hd, 2) * o_w[r]

    for h in heads:
        score = scores[h]
        rank = jnp.zeros((n_slc, tq), F32)
        for i in range(n_slc):
            row = score[i:i + 1, :]
            beats = (row > score) | ((row == score) & (i < j_col))
            rank = rank + jnp.where(beats, 1.0, 0.0)
        drop_ref[h] = jnp.where(rank >= n_sel, NEG, 0.0)

    blocks_per_tile = kt // SLC_BLOCK

    def slc_step(t, carry):
        k0 = pl.multiple_of(t * kt, kt)
        kpos = k0 + lax.broadcasted_iota(jnp.int32, (kt, 1), 0)
        causal = jnp.where(kpos <= qpos_row, 0.0, NEG)
        def logits(h):
            hs = slice(h * HEAD_DIM, (h + 1) * HEAD_DIM)
            dropped = jnp.concatenate(
                [jnp.broadcast_to(drop_ref[h, pl.ds(t * blocks_per_tile + jb, 1), :], (SLC_BLOCK, tq))
                 for jb in range(blocks_per_tile)], axis=0)
            bias = dropped + causal
            return _dot(ks_ref[pl.ds(k0, kt), hs], qt_ref[h]) + jnp.concatenate([bias] * GQA, axis=1)

        scores = [logits(h) for h in range(N_KV)]
        out = []
        for h in range(N_KV):
            m, l, acc = carry[h]
            hs = slice(h * HEAD_DIM, (h + 1) * HEAD_DIM)
            s = scores[h]
            m_new = jnp.maximum(m, jnp.max(s, axis=0, keepdims=True))
            a = jnp.exp2(m - m_new)
            e = jnp.exp2(s - m_new)
            l = l * a + jnp.sum(e, axis=0, keepdims=True)
            acc = acc * a + _dot(vt_ref[t, hs, :], e)
            out.append((m_new, l, acc))
        return tuple(out)

    n_t = (qb * tq + tq + kt - 1) // kt
    init = (jnp.full((1, GQA * tq), NEG, F32), jnp.zeros((1, GQA * tq), F32), jnp.zeros((HEAD_DIM, GQA * tq), F32))
    done = lax.fori_loop(0, n_t, slc_step, (init,) * N_KV)
    for h in range(N_KV):
        _, l, acc = done[h]
        o_s = jnp.transpose(acc / l)
        for g in range(GQA):
            hd = h * GQA + g
            r = slice(g * tq, (g + 1) * tq)
            o_ref[:, hd * HEAD_DIM:(hd + 1) * HEAD_DIM] += gate_col(hd, 1) * o_s[r]


def nsa_prompt(u, b_gate_pad, cos2, sin2, ckv, slc_k, slc_vt, win_k, win_v, batch, seq):
    nqb = seq // Q_BLOCK
    n_pad = ckv.shape[3]
    half = N_KV * HEAD_DIM
    kspec = pl.BlockSpec((seq, half), lambda b, i: (b, 0))
    vt_spec = pl.BlockSpec((seq // SLC_KT, half, SLC_KT), lambda b, i: (b, 0, 0))
    return pl.pallas_call(
        functools.partial(_nsa_prompt_kernel, n_cmp=seq // CMP_STRIDE - 1, n_slc=seq // SLC_BLOCK, kt=SLC_KT),
        grid=(batch, nqb),
        in_specs=[pl.BlockSpec((Q_BLOCK, MAIN_W), lambda b, i: (b * nqb + i, 0)),
                  pl.BlockSpec((Q_BLOCK, GATE_PAD), lambda b, i: (b * nqb + i, (MAIN_W + MEM_W) // GATE_PAD)),
                  pl.BlockSpec((1, GATE_PAD), lambda b, i: (0, 0)),
                  pl.BlockSpec((Q_BLOCK, HEAD_DIM), lambda b, i: (i, 0)),
                  pl.BlockSpec((Q_BLOCK, HEAD_DIM), lambda b, i: (i, 0)),
                  pl.BlockSpec((1, 2, N_KV, n_pad, HEAD_DIM), lambda b, i: (b, 0, 0, 0, 0)),
                  kspec, vt_spec, kspec, kspec],
        out_specs=pl.BlockSpec((Q_BLOCK, MAIN_W), lambda b, i: (b * nqb + i, 0)),
        out_shape=jax.ShapeDtypeStruct((batch * seq, MAIN_W), F32),
        scratch_shapes=[pltpu.VMEM((N_KV, HEAD_DIM, GQA * Q_BLOCK), BF16),
                        pltpu.VMEM((N_KV, seq // SLC_BLOCK, Q_BLOCK), F32)],
        compiler_params=_cparams("parallel", "arbitrary"),
        name="nsa_prompt",
    )(u, u, b_gate_pad, cos2, sin2, ckv, slc_k, slc_vt, win_k, win_v)


HEAD_ROWS = 16


def _head_mask(h):
    r = lax.broadcasted_iota(jnp.int32, (HEAD_ROWS, 1), 0)
    return (r >= h * GQA) & (r < (h + 1) * GQA)


def _nsa_sample_a_kernel(q_ref, cos_ref, sin_ref, ckv_ref, wkv_ref, wnew_ref, qr_ref, oc_ref, ow_ref, sel_ref,
                         *, n_cmp, n_slc, j_pad):
    n_pad = ckv_ref.shape[3]
    q = _rope(q_ref[0], cos_ref[...], sin_ref[...])
    qr_ref[0] = q
    qs = q * SCALE
    n_i = lax.broadcasted_iota(jnp.int32, (1, n_pad), 1)
    vis_c = (n_i * CMP_STRIDE + CMP_BLOCK - 1 <= PAST_LEN) & (n_i < n_cmp)
    cover = _cover(n_pad, j_pad, n_cmp)
    j_row = lax.broadcasted_iota(jnp.int32, (1, j_pad), 1)
    j_col = lax.broadcasted_iota(jnp.int32, (j_pad, 1), 0)
    qblk = PAST_LEN // SLC_BLOCK
    lane = lax.broadcasted_iota(jnp.int32, (N_SELECT, HEAD_DIM), 1)
    r_col = lax.broadcasted_iota(jnp.int32, (N_SELECT, 1), 0)
    n_win = wkv_ref.shape[1] // KV_SLOTS
    kpos_w = PAST_LEN - n_win + lax.broadcasted_iota(jnp.int32, (1, n_win), 1)
    vis_w = (kpos_w <= PAST_LEN) & (PAST_LEN - kpos_w < WINDOW)

    heads = range(N_KV)
    forced = (j_row == 0) | (j_row == qblk) | (j_row == qblk - 1)
    s_cmp = [jnp.where(vis_c, _dot_t(qs, ckv_ref[0, 0, h]), NEG) for h in heads]
    s_win = [jnp.where(vis_w, _dot_t(qs, wkv_ref[0, pl.ds(h, n_win, stride=KV_SLOTS), :]), NEG) for h in heads]

    o_c = jnp.zeros((HEAD_ROWS, HEAD_DIM), F32)
    scores = []
    for h in heads:
        mine = _head_mask(h)
        p = _softmax_rows(s_cmp[h]) * float(PAST_LEN >= CMP_BLOCK - 1)
        o_c = jnp.where(mine, _dot(p, ckv_ref[0, 1, h]), o_c)
        p_sum = jnp.sum(jnp.where(mine, p, 0.0), axis=0, keepdims=True)
        score = jnp.dot(jnp.broadcast_to(p_sum, (8, n_pad)), cover, preferred_element_type=F32,
                        precision=lax.Precision.HIGHEST)
        score = jnp.where(forced, jnp.inf, jnp.where(j_row <= qblk, score, -jnp.inf))
        scores.append(jnp.where(j_row < n_slc, score, -jnp.inf))
    oc_ref[0] = o_c

    o_w = jnp.zeros((HEAD_ROWS, HEAD_DIM), F32)
    for h in heads:
        hs = slice(h * HEAD_DIM, (h + 1) * HEAD_DIM)
        vsl = slice((N_KV + h) * HEAD_DIM, (N_KV + h + 1) * HEAD_DIM)
        v_win = wkv_ref[0, pl.ds(N_KV + h, n_win, stride=KV_SLOTS), :]
        s = s_win[h]
        s_new = jnp.sum(qs * wnew_ref[0, :, hs], axis=-1, keepdims=True)
        m = jnp.maximum(jnp.max(s, axis=-1, keepdims=True), s_new)
        e = jnp.exp(s - m)
        e_new = jnp.exp(s_new - m)
        o = (_dot(e, v_win) + e_new * wnew_ref[0, :, vsl]) / (jnp.sum(e, axis=-1, keepdims=True) + e_new)
        o_w = jnp.where(_head_mask(h), o, o_w)
    ow_ref[0] = o_w

    sel_out = jnp.zeros((N_SELECT, HEAD_DIM), jnp.int32)
    for h in heads:
        score = scores[h]
        s_col = jnp.transpose(score)[:, 0:1]
        beats = ((s_col > score[0:1]) | ((s_col == score[0:1]) & (j_col < j_row))) & (j_col < n_slc)
        rank = jnp.sum(beats.astype(F32), axis=0, keepdims=True)
        hit = (rank == r_col.astype(F32)) & (j_row < n_slc)
        idx = jnp.sum(jnp.where(hit, j_row.astype(F32), 0.0), axis=1, keepdims=True)
        sel_out = jnp.where(lane == h, idx.astype(jnp.int32), sel_out)
    sel_ref[0] = sel_out


def nsa_sample_a(q16, cos2, sin2, ckv, win_cache, win_new, n_cmp, n_slc):
    db = q16.shape[0]
    n_pad = ckv.shape[3]
    win_rows = win_cache.shape[1]
    j_pad = -(-n_slc // 128) * 128
    hspec = pl.BlockSpec((1, HEAD_ROWS, HEAD_DIM), lambda b: (b, 0, 0))
    hout = jax.ShapeDtypeStruct((db, HEAD_ROWS, HEAD_DIM), F32)
    return pl.pallas_call(
        functools.partial(_nsa_sample_a_kernel, n_cmp=n_cmp, n_slc=n_slc, j_pad=j_pad),
        grid=(db,),
        in_specs=[hspec,
                  pl.BlockSpec((1, HEAD_DIM), lambda b: (0, 0)),
                  pl.BlockSpec((1, HEAD_DIM), lambda b: (0, 0)),
                  pl.BlockSpec((1, 2, N_KV, n_pad, HEAD_DIM), lambda b: (b, 0, 0, 0, 0)),
                  pl.BlockSpec((1, win_rows, HEAD_DIM), lambda b: (b, 0, 0)),
                  pl.BlockSpec((1, 1, KV_W), lambda b: (b, 0, 0))],
        out_specs=[hspec, hspec, hspec, pl.BlockSpec((1, N_SELECT, HEAD_DIM), lambda b: (b, 0, 0))],
        out_shape=[hout, hout, hout, jax.ShapeDtypeStruct((db, N_SELECT, HEAD_DIM), jnp.int32)],
        compiler_params=_cparams("arbitrary"),
        name="nsa_sample_a",
    )(q16, cos2, sin2, ckv, win_cache, win_new)


def _nsa_sample_b_kernel(sel_ref, tbl_ref, q_ref, *refs):
    blocks = refs[:N_SELECT]
    new_ref, oc_ref, ow_ref, gl_ref, bg_ref, o_ref, os_ref = refs[N_SELECT:]
    b = pl.program_id(0)
    h = pl.program_id(1)
    n_keys = N_SELECT * SLC_BLOCK
    lane_blk = jnp.right_shift(lax.broadcasted_iota(jnp.int32, (1, n_keys), 1), SLC_SHIFT)
    new_k = new_ref[0, pl.ds(h, 1), :]
    new_v = new_ref[0, pl.ds(h + N_KV, 1), :]
    ks, vs = [], []
    blk_of = jnp.zeros((1, n_keys), jnp.int32)
    for r in range(N_SELECT):
        blk = sel_ref[b, h, r]
        in_past = blk * SLC_BLOCK < PAST_LEN
        ks.append(jnp.where(in_past, blocks[r][0, pl.ds(h, SLC_BLOCK, stride=KV_SLOTS), :], new_k))
        vs.append(jnp.where(in_past, blocks[r][0, pl.ds(h + N_KV, SLC_BLOCK, stride=KV_SLOTS), :], new_v))
        blk_of = jnp.where(lane_blk == r, blk, blk_of)
    kpos = blk_of * SLC_BLOCK + (lax.broadcasted_iota(jnp.int32, (1, n_keys), 1) & (SLC_BLOCK - 1))
    s = jnp.where(kpos <= PAST_LEN, _dot_t(q_ref[0] * SCALE, jnp.concatenate(ks, axis=0)), NEG)
    e = jnp.exp(s - jnp.max(s, axis=-1, keepdims=True))
    o_s = _dot(e, jnp.concatenate(vs, axis=0)) / jnp.sum(e, axis=-1, keepdims=True)

    @pl.when(h == 0)
    def _():
        os_ref[...] = jnp.zeros(os_ref.shape, F32)

    row = lax.broadcasted_iota(jnp.int32, (HEAD_ROWS, 1), 0)
    mine = (row >= h * GQA) & (row < (h + 1) * GQA)
    os_ref[...] = jnp.where(mine, o_s, os_ref[...])

    @pl.when(h == pl.num_programs(1) - 1)
    def _():
        gates = jax.nn.sigmoid(gl_ref[0] + bg_ref[...])
        o_ref[0] = gates[:, 0:1] * oc_ref[0] + gates[:, 1:2] * os_ref[...] + gates[:, 2:3] * ow_ref[0]


def nsa_sample_b(sel, page_table, q_rope, slc_halves, slc_new, o_c, o_w, gate_logits, b_gate16):
    db = q_rope.shape[0]
    last_blk = PAST_LEN // SLC_BLOCK - 1
    halves = PAGE_SIZE // SLC_BLOCK

    def cache_map(r):
        def index(b, h, sel_ref, tbl_ref):
            j = jnp.minimum(sel_ref[b, h, r], last_blk)
            return (tbl_ref[b, j // halves] * halves + j % halves, 0, 0)
        return index

    hspec = pl.BlockSpec((1, HEAD_ROWS, HEAD_DIM), lambda b, h, s, t: (b, 0, 0))
    return pl.pallas_call(
        _nsa_sample_b_kernel,
        grid_spec=pltpu.PrefetchScalarGridSpec(
            num_scalar_prefetch=2,
            grid=(db, N_KV),
            in_specs=[hspec]
            + [pl.BlockSpec((1, SLC_BLOCK * KV_SLOTS, HEAD_DIM), cache_map(r)) for r in range(N_SELECT)]
            + [pl.BlockSpec((1, KV_SLOTS, HEAD_DIM), lambda b, h, s, t: (b, 0, 0)),
               hspec, hspec, hspec,
               pl.BlockSpec((HEAD_ROWS, HEAD_DIM), lambda b, h, s, t: (0, 0))],
            out_specs=hspec,
            scratch_shapes=[pltpu.VMEM((HEAD_ROWS, HEAD_DIM), F32)],
        ),
        out_shape=jax.ShapeDtypeStruct((db, HEAD_ROWS, HEAD_DIM), F32),
        compiler_params=_cparams("arbitrary", "arbitrary"),
        name="nsa_sample_b",
    )(sel, page_table, q_rope, *([slc_halves] * N_SELECT), slc_new, o_c, o_w, gate_logits, b_gate16)


def _rope_tables(pos):
    half = HEAD_DIM // 2
    inv = ROPE_THETA ** (-jnp.arange(half, dtype=F32) / half)
    ang = pos.astype(F32)[:, None] * inv[None, :]
    cos, sin = jnp.cos(ang), jnp.sin(ang)
    return jnp.concatenate([cos, cos], -1), jnp.concatenate([-sin, sin], -1)


def _pad_w_in_b(w):
    return jnp.concatenate([w[..., :MAIN_W], w[..., MAIN_W + GATE_W:],
                            jnp.pad(w[..., MAIN_W:MAIN_W + GATE_W], ((0, 0), (0, 0), (0, GATE_PAD - GATE_W)))], axis=-1)


def kernel(x_prompt, x_sample, mem_prompt, state_pool, cache_cmp_kv, cache_slc_kv, cache_win_kv, cache_mem_kv, page_table, g_mix, w_in_a, pool_grp_w, pool_scale, w_in_b, b_gate, g_kv, w_kv, cmp_pe, cmp_w1, cmp_w2, g_mem, w_mem_kv, w_out, g_ffn, w_gu, w_down, g_final):
    batch, seq, _ = x_prompt.shape
    db = x_sample.shape[0]
    m = batch * seq
    tm = PROJ_TM
    n_phys = cache_cmp_kv.shape[0]
    w_in_b_pad = _pad_w_in_b(w_in_b)
    xp = x_prompt.reshape(m, D_MODEL)
    xs = x_sample.reshape(db, D_MODEL)
    mem2 = mem_prompt.reshape(batch * MEM_TOKENS, D_MODEL)
    mem_kv = [rms_matmul(mem2, g_mem[l], w_mem_kv, l, 512, 512) for l in range(DEPTH)]
    mem_cache = cache_mem_kv.reshape(-1, MEM_HEAD_DIM)
    cos_p, sin_p = _rope_tables(jnp.arange(seq, dtype=jnp.int32))
    cos_s, sin_s = _rope_tables(jnp.full((db,), PAST_LEN, jnp.int32))
    pools_p, pools_s = [], []
    for l in range(DEPTH):
        if l == N_A:
            cmp_kv, slc_kv, win_kv, slc_k, slc_vt, win_k, win_v = kv_proj(xp, g_kv, w_kv, cos_p, sin_p, SLC_KT, True)
            table = jnp.arange(batch * seq // PAGE_SIZE, dtype=jnp.int32).reshape(batch, seq // PAGE_SIZE)
            ckv_p = compress(cmp_kv.reshape(-1, PAGE_SIZE, KV_SLOTS, HEAD_DIM), table,
                             jnp.zeros((batch, PAGE_SIZE, KV_SLOTS, HEAD_DIM), F32), cmp_pe, cmp_w1, cmp_w2)
            cmp_new, slc_new, win_new = kv_proj(xs, g_kv, w_kv, cos_s, sin_s, db, False)
            extra = jnp.pad(cmp_new.reshape(db, 1, KV_SLOTS, HEAD_DIM), ((0, 0), (0, PAGE_SIZE - 1), (0, 0), (0, 0)))
            ckv_s = compress(cache_cmp_kv.reshape(n_phys, PAGE_SIZE, KV_SLOTS, HEAD_DIM), page_table, extra,
                             cmp_pe, cmp_w1, cmp_w2)
            t_full = -(-(PAST_LEN + 1) // SLC_BLOCK) * SLC_BLOCK
            n_cmp = t_full // CMP_STRIDE - 1
            n_slc = t_full // SLC_BLOCK
        if l < N_A:
            up, us = rms_matmul(xp, g_mix[l], w_in_a, l, tm, 512, xs=xs)
            u3 = up.reshape(batch, seq, -1)
            zp = pool_prompt(u3, pool_grp_w[l], pool_scale[l]).reshape(m, MAIN_W)
            pools_p.append(u3[:, seq - POOL_BUF:, :MAIN_W])
            zs = pool_sample(us, jnp.transpose(state_pool[l], (1, 0, 2)), pool_grp_w[l], pool_scale[l])
            pools_s.append(jnp.concatenate([state_pool[l][:, 1:], us[:, None, :MAIN_W]], axis=1))
        else:
            j = l - N_A
            up, us = rms_matmul(xp, g_mix[l], w_in_b_pad, j, tm, 768, xs=xs)
            u3 = up.reshape(batch, seq, -1)
            bg = jnp.pad(b_gate[j], (0, GATE_PAD - GATE_W)).reshape(1, GATE_PAD)
            zp = nsa_prompt(up, bg, cos_p, sin_p, ckv_p, slc_k, slc_vt, win_k, win_v, batch, seq)
            q16 = jnp.pad(us[:, :MAIN_W].reshape(db, N_HEADS, HEAD_DIM), ((0, 0), (0, HEAD_ROWS - N_HEADS), (0, 0)))
            q_rope, o_c, o_w, sel = nsa_sample_a(
                q16, cos_s[:1], sin_s[:1], ckv_s, cache_win_kv.reshape(db, -1, HEAD_DIM),
                win_new.reshape(db, 1, KV_W), n_cmp, n_slc)
            sel = jnp.transpose(sel[:, :, :N_KV], (0, 2, 1))
            gl = us[:, MAIN_W + MEM_W:MAIN_W + MEM_W + GATE_W].reshape(db, N_HEADS, N_BRANCH)
            gl = jnp.pad(gl, ((0, 0), (0, HEAD_ROWS - N_HEADS), (0, HEAD_DIM - N_BRANCH)))
            bg16 = jnp.pad(b_gate[j].reshape(N_HEADS, N_BRANCH), ((0, HEAD_ROWS - N_HEADS), (0, HEAD_DIM - N_BRANCH)))
            z16 = nsa_sample_b(sel, page_table, q_rope,
                               cache_slc_kv.reshape(n_phys * 2, SLC_BLOCK * KV_SLOTS, HEAD_DIM),
                               slc_new.reshape(db, KV_SLOTS, HEAD_DIM), o_c, o_w, gl, bg16)
            zs = z16[:, :N_HEADS].reshape(db, MAIN_W)
        mo_p = mem_attend(u3, mem_kv[l], 1024).reshape(m, MEM_W)
        mo_s = mem_attend(us.reshape(db, 1, -1), mem_cache, 1, layer=l).reshape(db, MEM_W)
        xp, xs = out_proj(zp, mo_p, w_out, l, xp, tm, 512, sample=(zs, mo_s, xs))
        xp, xs = ffn(xp, g_ffn[l], w_gu, w_down, l, g_final, l == DEPTH - 1, FFN_TM, FFN_TF, xs=xs)
    kv5 = (batch, seq, 2, N_KV, HEAD_DIM)
    kv5s = (db, 1, 2, N_KV, HEAD_DIM)
    n_keep = min(WINDOW, seq)
    win_s = jnp.concatenate([cache_win_kv, win_new.reshape(kv5s)], axis=1)
    win_s = win_s[:, win_s.shape[1] - min(WINDOW, win_s.shape[1]):]
    return (xp.reshape(batch, seq, D_MODEL), xs.reshape(db, 1, D_MODEL), jnp.stack(pools_p),
            cmp_kv.reshape(kv5), slc_kv.reshape(kv5), win_kv.reshape(kv5)[:, seq - n_keep:],
            jnp.stack(mem_kv).reshape(DEPTH, batch, MEM_TOKENS, 2, N_MEM_HEADS, MEM_HEAD_DIM),
            jnp.stack(pools_s), cmp_new.reshape(kv5s), slc_new.reshape(kv5s), win_s)
```

```python
import functools

import jax
import jax.numpy as jnp
from jax import lax
from jax.experimental import pallas as pl
from jax.experimental.pallas import tpu as pltpu

F32 = jnp.float32
BF16 = jnp.bfloat16

D_MODEL = 2048
DEPTH = 4
N_A = 2
PAST_LEN = 16384
PAGE_SIZE = 128
MEM_TOKENS = 256
N_MEM_HEADS = 4
MEM_HEAD_DIM = 128
MEM_W = 512
MAIN_W = 1536
POOL_WINDOWS = (2, 4, 8, 16)
POOL_GC = 384
POOL_BUF = 15
HEAD_DIM = 128
N_HEADS = 12
N_KV = 4
GQA = 3
N_BRANCH = 3
GATE_W = 36
GATE_PAD = 256
CMP_BLOCK = 32
CMP_STRIDE = 16
SLC_BLOCK = 64
SLC_SHIFT = 6
N_SELECT = 16
WINDOW = 512
Q_BLOCK = 128
D_FF = 5632
ROPE_THETA = 10000.0
EPS = 1e-6
NEG = -1e30
SCALE = HEAD_DIM ** -0.5
MEM_SCALE = MEM_HEAD_DIM ** -0.5
KV_SLOTS = 2 * N_KV
KV_W = KV_SLOTS * HEAD_DIM
W_IN_B = MAIN_W + MEM_W + GATE_PAD

VMEM_LIMIT = 56 * 1024 * 1024
SLC_KT = 512
LOG2E = 1.4426950408889634
PROJ_TM = 1024
FFN_TM = 1024
FFN_TF = 256
PAGES_PER_STEP = 16
CH_PER_PAGE = PAGE_SIZE // CMP_STRIDE
CMP_STEP = PAGES_PER_STEP * CH_PER_PAGE


def _cparams(*sem):
    return pltpu.CompilerParams(dimension_semantics=sem, vmem_limit_bytes=VMEM_LIMIT)


def _rms(x, g):
    return x * lax.rsqrt(jnp.mean(x * x, axis=-1, keepdims=True) + EPS) * g


def _dot(a, b):
    return jnp.dot(a.astype(BF16), b.astype(BF16), preferred_element_type=F32)


def _dot_t(a, b):
    return lax.dot_general(a.astype(BF16), b.astype(BF16), (((1,), (1,)), ((), ())),
                           preferred_element_type=F32)


def _lhs_dtype(rows):
    return BF16 if rows % 16 == 0 else F32


def _rope(x, cos2, sin2):
    return x * cos2 + pltpu.roll(x, HEAD_DIM // 2, axis=1) * sin2


def _rms_matmul_kernel(x_ref, g_ref, w_ref, o_ref, xn_ref, wb_ref):
    i = pl.program_id(0)
    j = pl.program_id(1)

    @pl.when(j == 0)
    def _():
        xn_ref[...] = _rms(x_ref[...], g_ref[...]).astype(xn_ref.dtype)

    @pl.when(i == 0)
    def _():
        wb_ref[j] = w_ref[...].astype(BF16)

    o_ref[...] = _dot(xn_ref[...], wb_ref[j])


def _rms_matmul2_kernel(x_ref, xs_ref, g_ref, w_ref, o_ref, os_ref, xn_ref, xsn_ref, wb_ref):
    _rms_matmul_kernel(x_ref, g_ref, w_ref, o_ref, xn_ref, wb_ref)
    i = pl.program_id(0)
    j = pl.program_id(1)

    @pl.when((i == 0) & (j == 0))
    def _():
        xsn_ref[...] = _rms(xs_ref[...], g_ref[...])

    @pl.when(i == 0)
    def _():
        os_ref[...] = _dot(xsn_ref[...], wb_ref[j])


def _once_per_column(nj):
    return lambda i, j: jnp.where(i == 0, j, nj - 1)


def rms_matmul(x, g, w, layer, tm, tn, xs=None):
    m, k = x.shape
    n = w.shape[2]
    nj = n // tn
    col = _once_per_column(nj)
    x_spec = pl.BlockSpec((tm, k), lambda i, j: (i, 0))
    g_spec = pl.BlockSpec((1, k), lambda i, j: (0, 0))
    w_spec = pl.BlockSpec((None, k, tn), lambda i, j: (layer, 0, col(i, j)))
    o_spec = pl.BlockSpec((tm, tn), lambda i, j: (i, j))
    scratch = [pltpu.VMEM((tm, k), _lhs_dtype(tm)), pltpu.VMEM((nj, k, tn), BF16)]
    if xs is None:
        return pl.pallas_call(
            _rms_matmul_kernel,
            grid=(m // tm, nj),
            in_specs=[x_spec, g_spec, w_spec],
            out_specs=o_spec,
            out_shape=jax.ShapeDtypeStruct((m, n), F32),
            scratch_shapes=scratch,
            compiler_params=_cparams("arbitrary", "arbitrary"),
            name="rms_matmul",
        )(x, g.reshape(1, k), w)
    ms = xs.shape[0]
    return pl.pallas_call(
        _rms_matmul2_kernel,
        grid=(m // tm, nj),
        in_specs=[x_spec, pl.BlockSpec((ms, k), lambda i, j: (0, 0)), g_spec, w_spec],
        out_specs=[o_spec, pl.BlockSpec((ms, tn), lambda i, j: (0, col(i, j)))],
        out_shape=[jax.ShapeDtypeStruct((m, n), F32), jax.ShapeDtypeStruct((ms, n), F32)],
        scratch_shapes=[scratch[0], pltpu.VMEM((ms, k), F32), scratch[1]],
        compiler_params=_cparams("arbitrary", "arbitrary"),
        name="rms_matmul2",
    )(x, xs, g.reshape(1, k), w)


def _kv_proj_kernel(x_ref, g_ref, w_ref, cos_ref, sin_ref, cmp_ref, slc_ref, win_ref, *rest):
    xn_ref, wb_ref = rest[-2:]
    attn = rest[:-2]
    i = pl.program_id(0)
    j = pl.program_id(1)
    half = N_KV * HEAD_DIM

    @pl.when(j == 0)
    def _():
        xn_ref[...] = _rms(x_ref[...], g_ref[...]).astype(xn_ref.dtype)

    @pl.when(i == 0)
    def _():
        wb_ref[j] = w_ref[...].astype(BF16)

    for br, ref in enumerate((cmp_ref, slc_ref, win_ref)):
        for c in range(2):
            @pl.when(j == 2 * br + c)
            def _(ref=ref, br=br, c=c):
                y = _dot(xn_ref[...], wb_ref[2 * br + c])
                val = y
                if c == 0:
                    cos2 = cos_ref[...]
                    sin2 = sin_ref[...]
                    val = jnp.concatenate(
                        [_rope(y[:, h * HEAD_DIM:(h + 1) * HEAD_DIM], cos2, sin2) for h in range(N_KV)], axis=1)
                for h in range(N_KV):
                    ref[:, c * N_KV + h, :] = val[:, h * HEAD_DIM:(h + 1) * HEAD_DIM]
                if attn and br == 1 and c == 1:
                    attn[1][0] = jnp.transpose(val).astype(BF16)
                elif attn and br == 2 and c == 1:
                    for t in range(val.shape[0] // Q_BLOCK):
                        attn[3][t] = jnp.transpose(val[t * Q_BLOCK:(t + 1) * Q_BLOCK]).astype(BF16)
                elif attn and br > 0:
                    attn[2 * (br - 1) + c][...] = val.astype(BF16)


def kv_proj(x, g, w, cos2, sin2, tm, attn_copies):
    m, k = x.shape
    half = N_KV * HEAD_DIM
    nt = cos2.shape[0] // tm
    nj = 2 * N_BRANCH
    col = _once_per_column(nj)
    out = jax.ShapeDtypeStruct((m, KV_SLOTS, HEAD_DIM), F32)
    ospec = pl.BlockSpec((tm, KV_SLOTS, HEAD_DIM), lambda i, j: (i, 0, 0))
    out_shape, out_specs = [out, out, out], [ospec, ospec, ospec]
    if attn_copies:
        out16 = jax.ShapeDtypeStruct((m, half), BF16)
        ospec16 = pl.BlockSpec((tm, half), lambda i, j: (i, 0))
        out_shape += [out16, jax.ShapeDtypeStruct((m // tm, half, tm), BF16), out16,
                      jax.ShapeDtypeStruct((m // Q_BLOCK, half, Q_BLOCK), BF16)]
        out_specs += [ospec16, pl.BlockSpec((1, half, tm), lambda i, j: (i, 0, 0)), ospec16,
                      pl.BlockSpec((tm // Q_BLOCK, half, Q_BLOCK), lambda i, j: (i, 0, 0))]
    return pl.pallas_call(
        _kv_proj_kernel,
        grid=(m // tm, nj),
        in_specs=[pl.BlockSpec((tm, k), lambda i, j: (i, 0)),
                  pl.BlockSpec((1, k), lambda i, j: (0, 0)),
                  pl.BlockSpec((k, half), lambda i, j: (0, col(i, j))),
                  pl.BlockSpec((tm, HEAD_DIM), lambda i, j: (i % nt, 0)),
                  pl.BlockSpec((tm, HEAD_DIM), lambda i, j: (i % nt, 0))],
        out_specs=out_specs,
        out_shape=out_shape,
        scratch_shapes=[pltpu.VMEM((tm, k), _lhs_dtype(tm)), pltpu.VMEM((nj, k, half), BF16)],
        compiler_params=_cparams("arbitrary", "arbitrary"),
        name="kv_proj",
    )(x, g.reshape(1, k), w, cos2, sin2)


def _pool_prompt_kernel(cur_ref, prev_ref, w_ref, sc_ref, o_ref, *, ts):
    i = pl.program_id(1)
    keep = (i > 0).astype(F32)
    pos = (i * ts + lax.broadcasted_iota(jnp.int32, (ts, 1), 0) + 1).astype(F32)
    diffs = []
    for g, w in enumerate(POOL_WINDOWS):
        sl = slice(g * POOL_GC, (g + 1) * POOL_GC)
        x = cur_ref[0, :, sl]
        ext = jnp.concatenate([prev_ref[0, :, sl] * keep, x], axis=0)
        span = 1
        while span < w:
            ext = ext[span:] + ext[:-span]
            span *= 2
        win = ext[17 - w:17 - w + ts]
        diffs.append((win / jnp.minimum(pos, float(w)) - x).astype(BF16))
    for g in range(len(POOL_WINDOWS)):
        sl = slice(g * POOL_GC, (g + 1) * POOL_GC)
        o_ref[0, :, sl] = _dot(diffs[g], w_ref[g]) * sc_ref[:, sl]


def pool_prompt(u3, w_grp, scale, ts=512):
    b, s, n = u3.shape
    r = ts // 16
    return pl.pallas_call(
        functools.partial(_pool_prompt_kernel, ts=ts),
        grid=(b, s // ts),
        in_specs=[pl.BlockSpec((1, ts, MAIN_W), lambda bi, i: (bi, i, 0)),
                  pl.BlockSpec((1, 16, MAIN_W), lambda bi, i: (bi, jnp.maximum(i * r - 1, 0), 0)),
                  pl.BlockSpec((4, POOL_GC, POOL_GC), lambda bi, i: (0, 0, 0)),
                  pl.BlockSpec((1, MAIN_W), lambda bi, i: (0, 0))],
        out_specs=pl.BlockSpec((1, ts, MAIN_W), lambda bi, i: (bi, i, 0)),
        out_shape=jax.ShapeDtypeStruct((b, s, MAIN_W), F32),
        compiler_params=_cparams("parallel", "arbitrary"),
        name="pool_prompt",
    )(u3, u3, w_grp, scale.reshape(1, MAIN_W))


def _pool_sample_kernel(cur_ref, buf_ref, w_ref, sc_ref, o_ref):
    for g, w in enumerate(POOL_WINDOWS):
        sl = slice(g * POOL_GC, (g + 1) * POOL_GC)
        x = cur_ref[:, sl]
        win = x
        for r in range(POOL_BUF - (w - 1), POOL_BUF):
            win = win + buf_ref[r, :, sl]
        d = win / float(w) - x
        o_ref[:, sl] = _dot(d, w_ref[g]) * sc_ref[:, sl]


def pool_sample(u, buf_t, w_grp, scale):
    db = u.shape[0]
    return pl.pallas_call(
        _pool_sample_kernel,
        grid=(1,),
        in_specs=[pl.BlockSpec((db, MAIN_W), lambda i: (0, 0)),
                  pl.BlockSpec((POOL_BUF, db, MAIN_W), lambda i: (0, 0, 0)),
                  pl.BlockSpec((4, POOL_GC, POOL_GC), lambda i: (0, 0, 0)),
                  pl.BlockSpec((1, MAIN_W), lambda i: (0, 0))],
        out_specs=pl.BlockSpec((db, MAIN_W), lambda i: (0, 0)),
        out_shape=jax.ShapeDtypeStruct((db, MAIN_W), F32),
        compiler_params=_cparams("arbitrary"),
        name="pool_sample",
    )(u, buf_t, w_grp, scale.reshape(1, MAIN_W))


def _mem_attend_kernel(q_ref, kv_ref, o_ref, *, tq, slots):
    rows = max(tq, 8)
    logits, values = [], []
    for h in range(N_MEM_HEADS):
        sl = slice(h * MEM_HEAD_DIM, (h + 1) * MEM_HEAD_DIM)
        q = q_ref[0, :, sl]
        if tq < rows:
            q = jnp.broadcast_to(q[0:1], (rows, MEM_HEAD_DIM))
        if slots:
            k = kv_ref[pl.ds(h, MEM_TOKENS, stride=2 * N_MEM_HEADS), :]
            v = kv_ref[pl.ds(N_MEM_HEADS + h, MEM_TOKENS, stride=2 * N_MEM_HEADS), :]
        else:
            k = kv_ref[:, sl]
            v = kv_ref[:, MEM_W + h * MEM_HEAD_DIM:MEM_W + (h + 1) * MEM_HEAD_DIM]
        logits.append(_dot_t(q, k) * MEM_SCALE)
        values.append(v)
    for h in range(N_MEM_HEADS):
        sl = slice(h * MEM_HEAD_DIM, (h + 1) * MEM_HEAD_DIM)
        s, v = logits[h], values[h]
        e = jnp.exp(s - jnp.max(s, axis=-1, keepdims=True))
        o = _dot(e, v) / jnp.sum(e, axis=-1, keepdims=True)
        o_ref[0, :, sl] = o[0:tq]


def mem_attend(u3, mkv, tq, layer=None):
    b, s, _ = u3.shape
    if layer is None:
        kv_spec = pl.BlockSpec((MEM_TOKENS, 2 * MEM_W), lambda bi, i: (bi, 0))
    else:
        kv_spec = pl.BlockSpec((MEM_TOKENS * 2 * N_MEM_HEADS, MEM_HEAD_DIM), lambda bi, i: (layer * b + bi, 0))
    return pl.pallas_call(
        functools.partial(_mem_attend_kernel, tq=tq, slots=layer is not None),
        grid=(b, s // tq),
        in_specs=[pl.BlockSpec((1, tq, MEM_W), lambda bi, i: (bi, i, MAIN_W // MEM_W)), kv_spec],
        out_specs=pl.BlockSpec((1, tq, MEM_W), lambda bi, i: (bi, i, 0)),
        out_shape=jax.ShapeDtypeStruct((b, s, MEM_W), F32),
        compiler_params=_cparams("parallel", "arbitrary"),
        name="mem_attend",
    )(u3, mkv)


def _out_proj_kernel(z_ref, mo_ref, w_ref, x_ref, o_ref, lhs_ref, wb_ref):
    i = pl.program_id(0)
    j = pl.program_id(1)

    @pl.when(j == 0)
    def _():
        lhs_ref[:, :MAIN_W] = z_ref[...].astype(lhs_ref.dtype)
        lhs_ref[:, MAIN_W:] = mo_ref[...].astype(lhs_ref.dtype)

    @pl.when(i == 0)
    def _():
        wb_ref[j] = w_ref[...].astype(BF16)

    o_ref[...] = x_ref[...] + _dot(lhs_ref[...], wb_ref[j])


def _out_proj2_kernel(z_ref, mo_ref, w_ref, x_ref, zs_ref, mos_ref, xs_ref, o_ref, os_ref, lhs_ref, wb_ref):
    _out_proj_kernel(z_ref, mo_ref, w_ref, x_ref, o_ref, lhs_ref, wb_ref)
    j = pl.program_id(1)

    @pl.when(pl.program_id(0) == 0)
    def _():
        lhs = jnp.concatenate([zs_ref[...], mos_ref[...]], axis=1)
        os_ref[...] = xs_ref[...] + _dot(lhs, wb_ref[j])


def out_proj(z, mo, w, layer, x, tm, tn, sample=None):
    m = x.shape[0]
    k = MAIN_W + MEM_W
    nj = D_MODEL // tn
    col = _once_per_column(nj)
    in_specs = [pl.BlockSpec((tm, MAIN_W), lambda i, j: (i, 0)),
                pl.BlockSpec((tm, MEM_W), lambda i, j: (i, 0)),
                pl.BlockSpec((None, k, tn), lambda i, j: (layer, 0, col(i, j))),
                pl.BlockSpec((tm, tn), lambda i, j: (i, j))]
    o_spec = pl.BlockSpec((tm, tn), lambda i, j: (i, j))
    scratch = [pltpu.VMEM((tm, k), _lhs_dtype(tm)), pltpu.VMEM((nj, k, tn), BF16)]
    if sample is None:
        return pl.pallas_call(
            _out_proj_kernel,
            grid=(m // tm, nj),
            in_specs=in_specs,
            out_specs=o_spec,
            out_shape=jax.ShapeDtypeStruct((m, D_MODEL), F32),
            scratch_shapes=scratch,
            compiler_params=_cparams("arbitrary", "arbitrary"),
            name="out_proj",
        )(z, mo, w, x)
    zs, mos, xs = sample
    ms = xs.shape[0]
    s_spec = pl.BlockSpec((ms, tn), lambda i, j: (0, col(i, j)))
    return pl.pallas_call(
        _out_proj2_kernel,
        grid=(m // tm, nj),
        in_specs=in_specs + [pl.BlockSpec((ms, MAIN_W), lambda i, j: (0, 0)),
                             pl.BlockSpec((ms, MEM_W), lambda i, j: (0, 0)), s_spec],
        out_specs=[o_spec, s_spec],
        out_shape=[jax.ShapeDtypeStruct((m, D_MODEL), F32), jax.ShapeDtypeStruct((ms, D_MODEL), F32)],
        scratch_shapes=scratch,
        compiler_params=_cparams("arbitrary", "arbitrary"),
        name="out_proj2",
    )(z, mo, w, x, zs, mos, xs)


def _ffn_kernel(x_ref, g_ref, wa_ref, wb_ref, wd_ref, gf_ref, o_ref, xn_ref, *, final):
    j = pl.program_id(1)

    @pl.when(j == 0)
    def _():
        x = x_ref[...]
        xn_ref[...] = _rms(x, g_ref[...]).astype(xn_ref.dtype)
        o_ref[...] = x

    xn = xn_ref[...]
    a = _dot(xn, wa_ref[...])
    b = _dot(xn, wb_ref[...])
    o_ref[...] += _dot(jax.nn.silu(a) * b, wd_ref[...])

    if final:
        @pl.when(j == pl.num_programs(1) - 1)
        def _():
            o_ref[...] = _rms(o_ref[...], gf_ref[...])


def _ffn2_kernel(x_ref, g_ref, wa_ref, wb_ref, wd_ref, gf_ref, xs_ref, o_ref, os_ref, xn_ref, xsn_ref, *, final):
    _ffn_kernel(x_ref, g_ref, wa_ref, wb_ref, wd_ref, gf_ref, o_ref, xn_ref, final=final)
    j = pl.program_id(1)

    @pl.when(pl.program_id(0) == 0)
    def _():
        @pl.when(j == 0)
        def _():
            xs = xs_ref[...]
            xsn_ref[...] = _rms(xs, g_ref[...])
            os_ref[...] = xs

        xsn = xsn_ref[...]
        a = _dot(xsn, wa_ref[...])
        b = _dot(xsn, wb_ref[...])
        os_ref[...] += _dot(jax.nn.silu(a) * b, wd_ref[...])

        if final:
            @pl.when(j == pl.num_programs(1) - 1)
            def _():
                os_ref[...] = _rms(os_ref[...], gf_ref[...])


def ffn(x, g, w_gu, w_down, layer, g_final, final, tm, tf, xs=None):
    m, k = x.shape
    nf = D_FF // tf
    in_specs = [pl.BlockSpec((tm, k), lambda i, j: (i, 0), pipeline_mode=pl.Buffered(1)),
                pl.BlockSpec((1, k), lambda i, j: (0, 0)),
                pl.BlockSpec((None, k, tf), lambda i, j: (layer, 0, j)),
                pl.BlockSpec((None, k, tf), lambda i, j: (layer, 0, nf + j)),
                pl.BlockSpec((None, tf, k), lambda i, j: (layer, j, 0)),
                pl.BlockSpec((1, k), lambda i, j: (0, 0))]
    o_spec = pl.BlockSpec((tm, k), lambda i, j: (i, 0))
    scratch = [pltpu.VMEM((tm, k), _lhs_dtype(tm))]
    args = (x, g.reshape(1, k), w_gu, w_gu, w_down, g_final.reshape(1, k))
    if xs is None:
        return pl.pallas_call(
            functools.partial(_ffn_kernel, final=final),
            grid=(m // tm, nf),
            in_specs=in_specs,
            out_specs=o_spec,
            out_shape=jax.ShapeDtypeStruct((m, k), F32),
            scratch_shapes=scratch,
            compiler_params=_cparams("parallel", "arbitrary"),
            name="ffn",
        )(*args)
    ms = xs.shape[0]
    s_spec = pl.BlockSpec((ms, k), lambda i, j: (0, 0))
    return pl.pallas_call(
        functools.partial(_ffn2_kernel, final=final),
        grid=(m // tm, nf),
        in_specs=in_specs + [s_spec],
        out_specs=[o_spec, s_spec],
        out_shape=[jax.ShapeDtypeStruct((m, k), F32), jax.ShapeDtypeStruct((ms, k), F32)],
        scratch_shapes=scratch + [pltpu.VMEM((ms, k), F32)],
        compiler_params=_cparams("arbitrary", "arbitrary"),
        name="ffn2",
    )(*args, xs)


def _compress_kernel(tbl_ref, *refs, n_steps):
    pages = refs[:PAGES_PER_STEP]
    nxt_ref, extra_ref, w1_ref, pe_ref, w2_ref, o_ref, w1b_ref, lhs_ref = refs[PAGES_PER_STEP:]
    p = pl.program_id(1)
    on_extra = p == n_steps
    nxt_extra = p == n_steps - 1
    nh = CMP_STEP * N_KV
    width = CMP_STRIDE * HEAD_DIM
    d = HEAD_DIM
    group = 4
    n_groups = CMP_STEP // group + 1
    n_rows = n_groups * group * N_KV

    @pl.when((pl.program_id(0) == 0) & (p == 0))
    def _():
        for c in range(2):
            w1b_ref[c] = jnp.concatenate([w1_ref[c, 0], w1_ref[c, 1]], axis=1).astype(BF16)

    def slots(n, r):
        if n < CMP_STEP:
            k, i = divmod(n, CH_PER_PAGE)
            x = pages[k][0, i * CMP_STRIDE + r]
            return jnp.where(on_extra, extra_ref[0, i * CMP_STRIDE + r] if k == 0 else 0.0, x)
        if n == CMP_STEP:
            return jnp.where(on_extra, 0.0, jnp.where(nxt_extra, extra_ref[0, r], nxt_ref[0, r]))
        return jnp.zeros((KV_SLOTS, HEAD_DIM), F32)

    for q in range(n_groups):
        for r in range(CMP_STRIDE):
            tiles = [slots(group * q + i, r) for i in range(group)]
            for c in range(2):
                piece = jnp.concatenate([t[c * N_KV:(c + 1) * N_KV] for t in tiles], axis=0)
                lhs_ref[c, group * N_KV * q:group * N_KV * (q + 1), r * HEAD_DIM:(r + 1) * HEAD_DIM] = piece.astype(BF16)
    for c in range(2):
        tail = jnp.concatenate([pe_ref[c, 0], pe_ref[c, 1], jnp.zeros((14, width), F32)], axis=0)
        lhs_ref[c, n_rows:n_rows + 16, :] = tail.astype(BF16)
    hc = [_dot(lhs_ref[c], w1b_ref[c]) for c in range(2)]
    for c in range(2):
        pe_term = hc[c][n_rows:n_rows + 1, :d] + hc[c][n_rows + 1:n_rows + 2, d:]
        hid = jax.nn.gelu(hc[c][:nh, :d] + hc[c][N_KV:nh + N_KV, d:] + pe_term)
        o_ref[0, c] = _dot(hid, w2_ref[c])


def compress(pages, table, extra, cmp_pe, cmp_w1, cmp_w2):
    nb, n_pages = table.shape
    n_steps = n_pages // PAGES_PER_STEP
    w1 = cmp_w1.reshape(2, 2, CMP_STRIDE * HEAD_DIM, HEAD_DIM)
    pe = cmp_pe.reshape(2, 2, 1, CMP_STRIDE * HEAD_DIM)

    def page_map(k):
        return lambda b, p, tbl: (tbl[b, jnp.minimum(p * PAGES_PER_STEP + k, n_pages - 1)], 0, 0, 0)

    in_specs = [pl.BlockSpec((1, PAGE_SIZE, KV_SLOTS, HEAD_DIM), page_map(k)) for k in range(PAGES_PER_STEP)]
    in_specs += [pl.BlockSpec((1, CMP_STRIDE, KV_SLOTS, HEAD_DIM), page_map(PAGES_PER_STEP)),
                 pl.BlockSpec((1, PAGE_SIZE, KV_SLOTS, HEAD_DIM), lambda b, p, tbl: (b, 0, 0, 0)),
                 pl.BlockSpec(w1.shape, lambda b, p, tbl: (0, 0, 0, 0)),
                 pl.BlockSpec(pe.shape, lambda b, p, tbl: (0, 0, 0, 0)),
                 pl.BlockSpec(cmp_w2.shape, lambda b, p, tbl: (0, 0, 0))]
    n_out = CMP_STEP * (n_steps + 1)
    out = pl.pallas_call(
        functools.partial(_compress_kernel, n_steps=n_steps),
        grid_spec=pltpu.PrefetchScalarGridSpec(
            num_scalar_prefetch=1,
            grid=(nb, n_steps + 1),
            in_specs=in_specs,
            out_specs=pl.BlockSpec((1, 2, CMP_STEP * N_KV, HEAD_DIM), lambda b, p, tbl: (b, 0, p, 0)),
            scratch_shapes=[pltpu.VMEM((2, CMP_STRIDE * HEAD_DIM, 2 * HEAD_DIM), BF16),
                            pltpu.VMEM((2, (CMP_STEP + 8) * N_KV, CMP_STRIDE * HEAD_DIM), BF16)],
        ),
        out_shape=jax.ShapeDtypeStruct((nb, 2, n_out * N_KV, HEAD_DIM), F32),
        compiler_params=_cparams("arbitrary", "arbitrary"),
        name="compress",
    )(table, *([pages] * (PAGES_PER_STEP + 1)), extra, w1, pe, cmp_w2)
    return jnp.transpose(out.reshape(nb, 2, n_out, N_KV, HEAD_DIM), (0, 1, 3, 2, 4))


def _cover(n_pad, j_pad, n_cmp):
    n = lax.broadcasted_iota(jnp.int32, (n_pad, j_pad), 0)
    j = lax.broadcasted_iota(jnp.int32, (n_pad, j_pad), 1)
    hit = (n * CMP_STRIDE < j * SLC_BLOCK + SLC_BLOCK) & (n * CMP_STRIDE + CMP_BLOCK - 1 >= j * SLC_BLOCK)
    return (hit & (n < n_cmp)).astype(F32)


def _cover_t(j_pad, n_pad, n_cmp):
    j = lax.broadcasted_iota(jnp.int32, (j_pad, n_pad), 0)
    n = lax.broadcasted_iota(jnp.int32, (j_pad, n_pad), 1)
    hit = (n * CMP_STRIDE < j * SLC_BLOCK + SLC_BLOCK) & (n * CMP_STRIDE + CMP_BLOCK - 1 >= j * SLC_BLOCK)
    return (hit & (n < n_cmp)).astype(F32)


def _softmax_rows(s):
    e = jnp.exp(s - jnp.max(s, axis=-1, keepdims=True))
    return e / jnp.sum(e, axis=-1, keepdims=True)


def _nsa_prompt_kernel(q_ref, gl_ref, bg_ref, cos_ref, sin_ref, ck_ref, cvt_ref, ks_ref, vt_ref, kw_ref, vwt_ref,
                       o_ref, qt_ref, drop_ref, z_ref, *, n_cmp, n_slc, kt):
    qb = pl.program_id(1)
    tq = Q_BLOCK
    cos2 = cos_ref[...]
    sin2 = sin_ref[...]
    n_pad = ck_ref.shape[2]
    span = WINDOW + tq
    qpos_row = qb * tq + lax.broadcasted_iota(jnp.int32, (1, tq), 1)
    qpos = jnp.concatenate([qpos_row] * GQA, axis=1)
    gates_t = jnp.transpose(jax.nn.sigmoid(gl_ref[...] + bg_ref[...]))

    def gate_row(hd, br):
        return gates_t[hd * N_BRANCH + br:hd * N_BRANCH + br + 1, :]

    n_col = lax.broadcasted_iota(jnp.int32, (n_pad, 1), 0)
    bias_c = jnp.where((n_col * CMP_STRIDE + CMP_BLOCK - 1 <= qpos) & (n_col < n_cmp), 0.0, NEG)
    any_c = (qpos >= CMP_BLOCK - 1).astype(F32)
    cover_t = _cover_t(n_slc, n_pad, n_cmp)
    j_col = lax.broadcasted_iota(jnp.int32, (n_slc, 1), 0)
    qblk = jnp.right_shift(qpos_row, SLC_SHIFT)
    forced = (j_col == 0) | (j_col == qblk) | (j_col == qblk - 1)
    valid = j_col <= qblk
    n_sel = min(N_SELECT, n_slc)

    w_lo = jnp.maximum(qb * tq - WINDOW, 0)
    w_lo = pl.multiple_of(w_lo, tq)
    kpos_w = w_lo + lax.broadcasted_iota(jnp.int32, (span, 1), 0)
    bias_w = jnp.where((kpos_w <= qpos) & (qpos - kpos_w < WINDOW), 0.0, NEG)

    heads = range(N_KV)
    s_cmp, s_win = [], []
    for h in heads:
        q = jnp.concatenate(
            [_rope(q_ref[:, (h * GQA + g) * HEAD_DIM:(h * GQA + g + 1) * HEAD_DIM], cos2, sin2)
             for g in range(GQA)], axis=0)
        qt = jnp.transpose(q * (SCALE * LOG2E)).astype(BF16)
        qt_ref[h] = qt
        hs = slice(h * HEAD_DIM, (h + 1) * HEAD_DIM)
        s_cmp.append(_dot(ck_ref[0, h], qt) + bias_c)
        s_win.append(_dot(kw_ref[pl.ds(w_lo, span), hs], qt) + bias_w)

    o_cmp, scores = [], []
    for h in heads:
        e = jnp.exp2(s_cmp[h] - jnp.max(s_cmp[h], axis=0, keepdims=True))
        p = e / jnp.sum(e, axis=0, keepdims=True) * any_c
        o_cmp.append(_dot(cvt_ref[0, h], p))
        p_sum = p[:, 0:tq] + p[:, tq:2 * tq] + p[:, 2 * tq:3 * tq]
        score = jnp.dot(cover_t, p_sum, preferred_element_type=F32,
                        precision=lax.Precision.HIGHEST)
        scores.append(jnp.where(forced, jnp.inf, jnp.where(valid, score, -jnp.inf)))

    for h in heads:
        hs = slice(h * HEAD_DIM, (h + 1) * HEAD_DIM)
        e = jnp.exp2(s_win[h] - jnp.max(s_win[h], axis=0, keepdims=True))
        o_w = _dot(vwt_ref[w_lo // tq, hs, :], e[0:tq])
        for i in range(1, span // tq):
            o_w = o_w + _dot(vwt_ref[w_lo // tq + i, hs, :], e[i * tq:(i + 1) * tq])
        o_w = o_w / jnp.sum(e, axis=0, keepdims=True)
        for g in range(GQA):
            hd = h * GQA + g
            r = slice(g * tq, (g + 1) * tq)
            z_ref[hd] = gate_row(hd, 0) * o_cmp[h][:, r] + gate_row(hd, 2) * o_w[:, r]

    for h in heads:
        score = scores[h]
        rank = jnp.zeros((n_slc, tq), F32)
        for i in range(n_slc):
            row = score[i:i + 1, :]
            beats = (row > score) | ((row == score) & (i < j_col))
            rank = rank + jnp.where(beats, 1.0, 0.0)
        drop_ref[h] = jnp.where(rank >= n_sel, NEG, 0.0)

    blocks_per_tile = kt // SLC_BLOCK

    def slc_step(t, carry):
        k0 = pl.multiple_of(t * kt, kt)
        kpos = k0 + lax.broadcasted_iota(jnp.int32, (kt, 1), 0)
        causal = jnp.where(kpos <= qpos_row, 0.0, NEG)
        def logits(h):
            hs = slice(h * HEAD_DIM, (h + 1) * HEAD_DIM)
            dropped = jnp.concatenate(
                [jnp.broadcast_to(drop_ref[h, pl.ds(t * blocks_per_tile + jb, 1), :], (SLC_BLOCK, tq))
                 for jb in range(blocks_per_tile)], axis=0)
            bias = dropped + causal
            return _dot(ks_ref[pl.ds(k0, kt), hs], qt_ref[h]) + jnp.concatenate([bias] * GQA, axis=1)

        scores = [logits(h) for h in range(N_KV)]
        out = []
        for h in range(N_KV):
            m, l, acc = carry[h]
            hs = slice(h * HEAD_DIM, (h + 1) * HEAD_DIM)
            s = scores[h]
            m_new = jnp.maximum(m, jnp.max(s, axis=0, keepdims=True))
            a = jnp.exp2(m - m_new)
            e = jnp.exp2(s - m_new)
            l = l * a + jnp.sum(e, axis=0, keepdims=True)
            acc = acc * a + _dot(vt_ref[t, hs, :], e)
            out.append((m_new, l, acc))
        return tuple(out)

    n_t = (qb * tq + tq + kt - 1) // kt
    init = (jnp.full((1, GQA * tq), NEG, F32), jnp.zeros((1, GQA * tq), F32), jnp.zeros((HEAD_DIM, GQA * tq), F32))
    done = lax.fori_loop(0, n_t, slc_step, (init,) * N_KV)
    for h in range(N_KV):
        _, l, acc = done[h]
        o_s = acc / l
        for g in range(GQA):
            hd = h * GQA + g
            r = slice(g * tq, (g + 1) * tq)
            o_ref[:, hd * HEAD_DIM:(hd + 1) * HEAD_DIM] = jnp.transpose(z_ref[hd] + gate_row(hd, 1) * o_s[:, r])


def nsa_prompt(u, b_gate_pad, cos2, sin2, ckv, slc_k, slc_vt, win_k, win_vt, batch, seq):
    nqb = seq // Q_BLOCK
    n_pad = ckv.shape[3]
    half = N_KV * HEAD_DIM
    cmp_k = ckv[:, 0]
    cmp_vt = jnp.transpose(ckv[:, 1], (0, 1, 3, 2))
    kspec = pl.BlockSpec((seq, half), lambda b, i: (b, 0))
    vt_spec = pl.BlockSpec((seq // SLC_KT, half, SLC_KT), lambda b, i: (b, 0, 0))
    wvt_spec = pl.BlockSpec((seq // Q_BLOCK, half, Q_BLOCK), lambda b, i: (b, 0, 0))
    return pl.pallas_call(
        functools.partial(_nsa_prompt_kernel, n_cmp=seq // CMP_STRIDE - 1, n_slc=seq // SLC_BLOCK, kt=SLC_KT),
        grid=(batch, nqb),
        in_specs=[pl.BlockSpec((Q_BLOCK, MAIN_W), lambda b, i: (b * nqb + i, 0)),
                  pl.BlockSpec((Q_BLOCK, GATE_PAD), lambda b, i: (b * nqb + i, (MAIN_W + MEM_W) // GATE_PAD)),
                  pl.BlockSpec((1, GATE_PAD), lambda b, i: (0, 0)),
                  pl.BlockSpec((Q_BLOCK, HEAD_DIM), lambda b, i: (i, 0)),
                  pl.BlockSpec((Q_BLOCK, HEAD_DIM), lambda b, i: (i, 0)),
                  pl.BlockSpec((1, N_KV, n_pad, HEAD_DIM), lambda b, i: (b, 0, 0, 0)),
                  pl.BlockSpec((1, N_KV, HEAD_DIM, n_pad), lambda b, i: (b, 0, 0, 0)),
                  kspec, vt_spec, kspec, wvt_spec],
        out_specs=pl.BlockSpec((Q_BLOCK, MAIN_W), lambda b, i: (b * nqb + i, 0)),
        out_shape=jax.ShapeDtypeStruct((batch * seq, MAIN_W), F32),
        scratch_shapes=[pltpu.VMEM((N_KV, HEAD_DIM, GQA * Q_BLOCK), BF16),
                        pltpu.VMEM((N_KV, seq // SLC_BLOCK, Q_BLOCK), F32),
                        pltpu.VMEM((N_HEADS, HEAD_DIM, Q_BLOCK), F32)],
        compiler_params=_cparams("parallel", "arbitrary"),
        name="nsa_prompt",
    )(u, u, b_gate_pad, cos2, sin2, cmp_k, cmp_vt, slc_k, slc_vt, win_k, win_vt)


HEAD_ROWS = 16


def _head_mask(h):
    r = lax.broadcasted_iota(jnp.int32, (HEAD_ROWS, 1), 0)
    return (r >= h * GQA) & (r < (h + 1) * GQA)


def _nsa_sample_a_kernel(q_ref, cos_ref, sin_ref, ckv_ref, wkv_ref, wnew_ref, qr_ref, oc_ref, ow_ref, sel_ref,
                         *, n_cmp, n_slc, j_pad):
    n_pad = ckv_ref.shape[3]
    q = _rope(q_ref[0], cos_ref[...], sin_ref[...])
    qr_ref[0] = q
    qs = q * SCALE
    n_i = lax.broadcasted_iota(jnp.int32, (1, n_pad), 1)
    vis_c = (n_i * CMP_STRIDE + CMP_BLOCK - 1 <= PAST_LEN) & (n_i < n_cmp)
    cover = _cover(n_pad, j_pad, n_cmp)
    j_row = lax.broadcasted_iota(jnp.int32, (1, j_pad), 1)
    j_col = lax.broadcasted_iota(jnp.int32, (j_pad, 1), 0)
    qblk = PAST_LEN // SLC_BLOCK
    lane = lax.broadcasted_iota(jnp.int32, (N_SELECT, HEAD_DIM), 1)
    r_col = lax.broadcasted_iota(jnp.int32, (N_SELECT, 1), 0)
    n_win = wkv_ref.shape[1] // KV_SLOTS
    kpos_w = PAST_LEN - n_win + lax.broadcasted_iota(jnp.int32, (1, n_win), 1)
    vis_w = (kpos_w <= PAST_LEN) & (PAST_LEN - kpos_w < WINDOW)

    heads = range(N_KV)
    forced = (j_row == 0) | (j_row == qblk) | (j_row == qblk - 1)
    s_cmp = [jnp.where(vis_c, _dot_t(qs, ckv_ref[0, 0, h]), NEG) for h in heads]
    s_win = [jnp.where(vis_w, _dot_t(qs, wkv_ref[0, pl.ds(h, n_win, stride=KV_SLOTS), :]), NEG) for h in heads]

    o_c = jnp.zeros((HEAD_ROWS, HEAD_DIM), F32)
    scores = []
    for h in heads:
        mine = _head_mask(h)
        p = _softmax_rows(s_cmp[h]) * float(PAST_LEN >= CMP_BLOCK - 1)
        o_c = jnp.where(mine, _dot(p, ckv_ref[0, 1, h]), o_c)
        p_sum = jnp.sum(jnp.where(mine, p, 0.0), axis=0, keepdims=True)
        score = jnp.dot(jnp.broadcast_to(p_sum, (8, n_pad)), cover, preferred_element_type=F32,
                        precision=lax.Precision.HIGHEST)
        score = jnp.where(forced, jnp.inf, jnp.where(j_row <= qblk, score, -jnp.inf))
        scores.append(jnp.where(j_row < n_slc, score, -jnp.inf))
    oc_ref[0] = o_c

    o_w = jnp.zeros((HEAD_ROWS, HEAD_DIM), F32)
    for h in heads:
        hs = slice(h * HEAD_DIM, (h + 1) * HEAD_DIM)
        vsl = slice((N_KV + h) * HEAD_DIM, (N_KV + h + 1) * HEAD_DIM)
        v_win = wkv_ref[0, pl.ds(N_KV + h, n_win, stride=KV_SLOTS), :]
        s = s_win[h]
        s_new = jnp.sum(qs * wnew_ref[0, :, hs], axis=-1, keepdims=True)
        m = jnp.maximum(jnp.max(s, axis=-1, keepdims=True), s_new)
        e = jnp.exp(s - m)
        e_new = jnp.exp(s_new - m)
        o = (_dot(e, v_win) + e_new * wnew_ref[0, :, vsl]) / (jnp.sum(e, axis=-1, keepdims=True) + e_new)
        o_w = jnp.where(_head_mask(h), o, o_w)
    ow_ref[0] = o_w

    sel_out = jnp.zeros((N_SELECT, HEAD_DIM), jnp.int32)
    for h in heads:
        score = scores[h]
        s_col = jnp.transpose(score)[:, 0:1]
        beats = ((s_col > score[0:1]) | ((s_col == score[0:1]) & (j_col < j_row))) & (j_col < n_slc)
        rank = jnp.sum(beats.astype(F32), axis=0, keepdims=True)
        hit = (rank == r_col.astype(F32)) & (j_row < n_slc)
        idx = jnp.sum(jnp.where(hit, j_row.astype(F32), 0.0), axis=1, keepdims=True)
        sel_out = jnp.where(lane == h, idx.astype(jnp.int32), sel_out)
    sel_ref[0] = sel_out


def nsa_sample_a(q16, cos2, sin2, ckv, win_cache, win_new, n_cmp, n_slc):
    db = q16.shape[0]
    n_pad = ckv.shape[3]
    win_rows = win_cache.shape[1]
    j_pad = -(-n_slc // 128) * 128
    hspec = pl.BlockSpec((1, HEAD_ROWS, HEAD_DIM), lambda b: (b, 0, 0))
    hout = jax.ShapeDtypeStruct((db, HEAD_ROWS, HEAD_DIM), F32)
    return pl.pallas_call(
        functools.partial(_nsa_sample_a_kernel, n_cmp=n_cmp, n_slc=n_slc, j_pad=j_pad),
        grid=(db,),
        in_specs=[hspec,
                  pl.BlockSpec((1, HEAD_DIM), lambda b: (0, 0)),
                  pl.BlockSpec((1, HEAD_DIM), lambda b: (0, 0)),
                  pl.BlockSpec((1, 2, N_KV, n_pad, HEAD_DIM), lambda b: (b, 0, 0, 0, 0)),
                  pl.BlockSpec((1, win_rows, HEAD_DIM), lambda b: (b, 0, 0)),
                  pl.BlockSpec((1, 1, KV_W), lambda b: (b, 0, 0))],
        out_specs=[hspec, hspec, hspec, pl.BlockSpec((1, N_SELECT, HEAD_DIM), lambda b: (b, 0, 0))],
        out_shape=[hout, hout, hout, jax.ShapeDtypeStruct((db, N_SELECT, HEAD_DIM), jnp.int32)],
        compiler_params=_cparams("arbitrary"),
        name="nsa_sample_a",
    )(q16, cos2, sin2, ckv, win_cache, win_new)


def _nsa_sample_b_kernel(sel_ref, tbl_ref, q_ref, *refs):
    blocks = refs[:N_SELECT]
    new_ref, oc_ref, ow_ref, gl_ref, bg_ref, o_ref, os_ref = refs[N_SELECT:]
    b = pl.program_id(0)
    h = pl.program_id(1)
    n_keys = N_SELECT * SLC_BLOCK
    lane_blk = jnp.right_shift(lax.broadcasted_iota(jnp.int32, (1, n_keys), 1), SLC_SHIFT)
    new_k = new_ref[0, pl.ds(h, 1), :]
    new_v = new_ref[0, pl.ds(h + N_KV, 1), :]
    ks, vs = [], []
    blk_of = jnp.zeros((1, n_keys), jnp.int32)
    for r in range(N_SELECT):
        blk = sel_ref[b, h, r]
        in_past = blk * SLC_BLOCK < PAST_LEN
        ks.append(jnp.where(in_past, blocks[r][0, pl.ds(h, SLC_BLOCK, stride=KV_SLOTS), :], new_k))
        vs.append(jnp.where(in_past, blocks[r][0, pl.ds(h + N_KV, SLC_BLOCK, stride=KV_SLOTS), :], new_v))
        blk_of = jnp.where(lane_blk == r, blk, blk_of)
    kpos = blk_of * SLC_BLOCK + (lax.broadcasted_iota(jnp.int32, (1, n_keys), 1) & (SLC_BLOCK - 1))
    s = jnp.where(kpos <= PAST_LEN, _dot_t(q_ref[0] * SCALE, jnp.concatenate(ks, axis=0)), NEG)
    e = jnp.exp(s - jnp.max(s, axis=-1, keepdims=True))
    o_s = _dot(e, jnp.concatenate(vs, axis=0)) / jnp.sum(e, axis=-1, keepdims=True)

    @pl.when(h == 0)
    def _():
        os_ref[...] = jnp.zeros(os_ref.shape, F32)

    row = lax.broadcasted_iota(jnp.int32, (HEAD_ROWS, 1), 0)
    mine = (row >= h * GQA) & (row < (h + 1) * GQA)
    os_ref[...] = jnp.where(mine, o_s, os_ref[...])

    @pl.when(h == pl.num_programs(1) - 1)
    def _():
        gates = jax.nn.sigmoid(gl_ref[0] + bg_ref[...])
        o_ref[0] = gates[:, 0:1] * oc_ref[0] + gates[:, 1:2] * os_ref[...] + gates[:, 2:3] * ow_ref[0]


def nsa_sample_b(sel, page_table, q_rope, slc_halves, slc_new, o_c, o_w, gate_logits, b_gate16):
    db = q_rope.shape[0]
    last_blk = PAST_LEN // SLC_BLOCK - 1
    halves = PAGE_SIZE // SLC_BLOCK

    def cache_map(r):
        def index(b, h, sel_ref, tbl_ref):
            j = jnp.minimum(sel_ref[b, h, r], last_blk)
            return (tbl_ref[b, j // halves] * halves + j % halves, 0, 0)
        return index

    hspec = pl.BlockSpec((1, HEAD_ROWS, HEAD_DIM), lambda b, h, s, t: (b, 0, 0))
    return pl.pallas_call(
        _nsa_sample_b_kernel,
        grid_spec=pltpu.PrefetchScalarGridSpec(
            num_scalar_prefetch=2,
            grid=(db, N_KV),
            in_specs=[hspec]
            + [pl.BlockSpec((1, SLC_BLOCK * KV_SLOTS, HEAD_DIM), cache_map(r)) for r in range(N_SELECT)]
            + [pl.BlockSpec((1, KV_SLOTS, HEAD_DIM), lambda b, h, s, t: (b, 0, 0)),
               hspec, hspec, hspec,
               pl.BlockSpec((HEAD_ROWS, HEAD_DIM), lambda b, h, s, t: (0, 0))],
            out_specs=hspec,
            scratch_shapes=[pltpu.VMEM((HEAD_ROWS, HEAD_DIM), F32)],
        ),
        out_shape=jax.ShapeDtypeStruct((db, HEAD_ROWS, HEAD_DIM), F32),
        compiler_params=_cparams("arbitrary", "arbitrary"),
        name="nsa_sample_b",
    )(sel, page_table, q_rope, *([slc_halves] * N_SELECT), slc_new, o_c, o_w, gate_logits, b_gate16)


def _rope_tables(pos):
    half = HEAD_DIM // 2
    inv = ROPE_THETA ** (-jnp.arange(half, dtype=F32) / half)
    ang = pos.astype(F32)[:, None] * inv[None, :]
    cos, sin = jnp.cos(ang), jnp.sin(ang)
    return jnp.concatenate([cos, cos], -1), jnp.concatenate([-sin, sin], -1)


def _pad_w_in_b(w):
    return jnp.concatenate([w[..., :MAIN_W], w[..., MAIN_W + GATE_W:],
                            jnp.pad(w[..., MAIN_W:MAIN_W + GATE_W], ((0, 0), (0, 0), (0, GATE_PAD - GATE_W)))], axis=-1)


def kernel(x_prompt, x_sample, mem_prompt, state_pool, cache_cmp_kv, cache_slc_kv, cache_win_kv, cache_mem_kv, page_table, g_mix, w_in_a, pool_grp_w, pool_scale, w_in_b, b_gate, g_kv, w_kv, cmp_pe, cmp_w1, cmp_w2, g_mem, w_mem_kv, w_out, g_ffn, w_gu, w_down, g_final):
    batch, seq, _ = x_prompt.shape
    db = x_sample.shape[0]
    m = batch * seq
    tm = PROJ_TM
    n_phys = cache_cmp_kv.shape[0]
    w_in_b_pad = _pad_w_in_b(w_in_b)
    xp = x_prompt.reshape(m, D_MODEL)
    xs = x_sample.reshape(db, D_MODEL)
    mem2 = mem_prompt.reshape(batch * MEM_TOKENS, D_MODEL)
    mem_kv = [rms_matmul(mem2, g_mem[l], w_mem_kv, l, 512, 512) for l in range(DEPTH)]
    mem_cache = cache_mem_kv.reshape(-1, MEM_HEAD_DIM)
    cos_p, sin_p = _rope_tables(jnp.arange(seq, dtype=jnp.int32))
    cos_s, sin_s = _rope_tables(jnp.full((db,), PAST_LEN, jnp.int32))
    pools_p, pools_s = [], []
    for l in range(DEPTH):
        if l == N_A:
            cmp_kv, slc_kv, win_kv, slc_k, slc_vt, win_k, win_vt = kv_proj(xp, g_kv, w_kv, cos_p, sin_p, SLC_KT, True)
            table = jnp.arange(batch * seq // PAGE_SIZE, dtype=jnp.int32).reshape(batch, seq // PAGE_SIZE)
            ckv_p = compress(cmp_kv.reshape(-1, PAGE_SIZE, KV_SLOTS, HEAD_DIM), table,
                             jnp.zeros((batch, PAGE_SIZE, KV_SLOTS, HEAD_DIM), F32), cmp_pe, cmp_w1, cmp_w2)
            cmp_new, slc_new, win_new = kv_proj(xs, g_kv, w_kv, cos_s, sin_s, db, False)
            extra = jnp.pad(cmp_new.reshape(db, 1, KV_SLOTS, HEAD_DIM), ((0, 0), (0, PAGE_SIZE - 1), (0, 0), (0, 0)))
            ckv_s = compress(cache_cmp_kv.reshape(n_phys, PAGE_SIZE, KV_SLOTS, HEAD_DIM), page_table, extra,
                             cmp_pe, cmp_w1, cmp_w2)
            t_full = -(-(PAST_LEN + 1) // SLC_BLOCK) * SLC_BLOCK
            n_cmp = t_full // CMP_STRIDE - 1
            n_slc = t_full // SLC_BLOCK
        if l < N_A:
            up, us = rms_matmul(xp, g_mix[l], w_in_a, l, tm, 512, xs=xs)
            u3 = up.reshape(batch, seq, -1)
            zp = pool_prompt(u3, pool_grp_w[l], pool_scale[l]).reshape(m, MAIN_W)
            pools_p.append(u3[:, seq - POOL_BUF:, :MAIN_W])
            zs = pool_sample(us, jnp.transpose(state_pool[l], (1, 0, 2)), pool_grp_w[l], pool_scale[l])
            pools_s.append(jnp.concatenate([state_pool[l][:, 1:], us[:, None, :MAIN_W]], axis=1))
        else:
            j = l - N_A
            up, us = rms_matmul(xp, g_mix[l], w_in_b_pad, j, tm, 768, xs=xs)
            u3 = up.reshape(batch, seq, -1)
            bg = jnp.pad(b_gate[j], (0, GATE_PAD - GATE_W)).reshape(1, GATE_PAD)
            zp = nsa_prompt(up, bg, cos_p, sin_p, ckv_p, slc_k, slc_vt, win_k, win_vt, batch, seq)
            q16 = jnp.pad(us[:, :MAIN_W].reshape(db, N_HEADS, HEAD_DIM), ((0, 0), (0, HEAD_ROWS - N_HEADS), (0, 0)))
            q_rope, o_c, o_w, sel = nsa_sample_a(
                q16, cos_s[:1], sin_s[:1], ckv_s, cache_win_kv.reshape(db, -1, HEAD_DIM),
                win_new.reshape(db, 1, KV_W), n_cmp, n_slc)
            sel = jnp.transpose(sel[:, :, :N_KV], (0, 2, 1))
            gl = us[:, MAIN_W + MEM_W:MAIN_W + MEM_W + GATE_W].reshape(db, N_HEADS, N_BRANCH)
            gl = jnp.pad(gl, ((0, 0), (0, HEAD_ROWS - N_HEADS), (0, HEAD_DIM - N_BRANCH)))
            bg16 = jnp.pad(b_gate[j].reshape(N_HEADS, N_BRANCH), ((0, HEAD_ROWS - N_HEADS), (0, HEAD_DIM - N_BRANCH)))
            z16 = nsa_sample_b(sel, page_table, q_rope,
                               cache_slc_kv.reshape(n_phys * 2, SLC_BLOCK * KV_SLOTS, HEAD_DIM),
                               slc_new.reshape(db, KV_SLOTS, HEAD_DIM), o_c, o_w, gl, bg16)
            zs = z16[:, :N_HEADS].reshape(db, MAIN_W)
        mo_p = mem_attend(u3, mem_kv[l], 1024).reshape(m, MEM_W)
        mo_s = mem_attend(us.reshape(db, 1, -1), mem_cache, 1, layer=l).reshape(db, MEM_W)
        xp, xs = out_proj(zp, mo_p, w_out, l, xp, tm, 512, sample=(zs, mo_s, xs))
        xp, xs = ffn(xp, g_ffn[l], w_gu, w_down, l, g_final, l == DEPTH - 1, FFN_TM, FFN_TF, xs=xs)
    kv5 = (batch, seq, 2, N_KV, HEAD_DIM)
    kv5s = (db, 1, 2, N_KV, HEAD_DIM)
    n_keep = min(WINDOW, seq)
    win_s = jnp.concatenate([cache_win_kv, win_new.reshape(kv5s)], axis=1)
    win_s = win_s[:, win_s.shape[1] - min(WINDOW, win_s.shape[1]):]
    return (xp.reshape(batch, seq, D_MODEL), xs.reshape(db, 1, D_MODEL), jnp.stack(pools_p),
            cmp_kv.reshape(kv5), slc_kv.reshape(kv5), win_kv.reshape(kv5)[:, seq - n_keep:],
            jnp.stack(mem_kv).reshape(DEPTH, batch, MEM_TOKENS, 2, N_MEM_HEADS, MEM_HEAD_DIM),
            jnp.stack(pools_s), cmp_new.reshape(kv5s), slc_new.reshape(kv5s), win_s)
```

```python
import functools

import jax
import jax.numpy as jnp
from jax import lax
from jax.experimental import pallas as pl
from jax.experimental.pallas import tpu as pltpu

F32 = jnp.float32
BF16 = jnp.bfloat16

D_MODEL = 2048
DEPTH = 4
N_A = 2
PAST_LEN = 16384
PAGE_SIZE = 128
MEM_TOKENS = 256
N_MEM_HEADS = 4
MEM_HEAD_DIM = 128
MEM_W = 512
MAIN_W = 1536
POOL_WINDOWS = (2, 4, 8, 16)
POOL_GC = 384
POOL_BUF = 15
HEAD_DIM = 128
N_HEADS = 12
N_KV = 4
GQA = 3
N_BRANCH = 3
GATE_W = 36
GATE_PAD = 256
CMP_BLOCK = 32
CMP_STRIDE = 16
SLC_BLOCK = 64
SLC_SHIFT = 6
N_SELECT = 16
WINDOW = 512
Q_BLOCK = 128
D_FF = 5632
ROPE_THETA = 10000.0
EPS = 1e-6
NEG = -1e30
SCALE = HEAD_DIM ** -0.5
MEM_SCALE = MEM_HEAD_DIM ** -0.5
KV_SLOTS = 2 * N_KV
KV_W = KV_SLOTS * HEAD_DIM
W_IN_B = MAIN_W + MEM_W + GATE_PAD

VMEM_LIMIT = 56 * 1024 * 1024
SLC_KT = 512
LOG2E = 1.4426950408889634
PROJ_TM = 1024
FFN_TM = 1024
FFN_TF = 256
PAGES_PER_STEP = 16
CH_PER_PAGE = PAGE_SIZE // CMP_STRIDE
CMP_STEP = PAGES_PER_STEP * CH_PER_PAGE


def _cparams(*sem):
    return pltpu.CompilerParams(dimension_semantics=sem, vmem_limit_bytes=VMEM_LIMIT)


def _rms(x, g):
    return x * lax.rsqrt(jnp.mean(x * x, axis=-1, keepdims=True) + EPS) * g


def _dot(a, b):
    return jnp.dot(a.astype(BF16), b.astype(BF16), preferred_element_type=F32)


def _dot_t(a, b):
    return lax.dot_general(a.astype(BF16), b.astype(BF16), (((1,), (1,)), ((), ())),
                           preferred_element_type=F32)


def _lhs_dtype(rows):
    return BF16 if rows % 16 == 0 else F32


def _rope(x, cos2, sin2):
    return x * cos2 + pltpu.roll(x, HEAD_DIM // 2, axis=1) * sin2


def _rms_matmul_kernel(x_ref, g_ref, w_ref, o_ref, xn_ref, wb_ref):
    i = pl.program_id(0)
    j = pl.program_id(1)

    @pl.when(j == 0)
    def _():
        xn_ref[...] = _rms(x_ref[...], g_ref[...]).astype(xn_ref.dtype)

    @pl.when(i == 0)
    def _():
        wb_ref[j] = w_ref[...].astype(BF16)

    o_ref[...] = _dot(xn_ref[...], wb_ref[j])


def _rms_matmul2_kernel(x_ref, xs_ref, g_ref, w_ref, o_ref, os_ref, xn_ref, xsn_ref, wb_ref):
    _rms_matmul_kernel(x_ref, g_ref, w_ref, o_ref, xn_ref, wb_ref)
    i = pl.program_id(0)
    j = pl.program_id(1)

    @pl.when((i == 0) & (j == 0))
    def _():
        xsn_ref[...] = _rms(xs_ref[...], g_ref[...])

    @pl.when(i == 0)
    def _():
        os_ref[...] = _dot(xsn_ref[...], wb_ref[j])


def _once_per_column(nj):
    return lambda i, j: jnp.where(i == 0, j, nj - 1)


def rms_matmul(x, g, w, layer, tm, tn, xs=None):
    m, k = x.shape
    n = w.shape[2]
    nj = n // tn
    col = _once_per_column(nj)
    x_spec = pl.BlockSpec((tm, k), lambda i, j: (i, 0))
    g_spec = pl.BlockSpec((1, k), lambda i, j: (0, 0))
    w_spec = pl.BlockSpec((None, k, tn), lambda i, j: (layer, 0, col(i, j)))
    o_spec = pl.BlockSpec((tm, tn), lambda i, j: (i, j))
    scratch = [pltpu.VMEM((tm, k), _lhs_dtype(tm)), pltpu.VMEM((nj, k, tn), BF16)]
    if xs is None:
        return pl.pallas_call(
            _rms_matmul_kernel,
            grid=(m // tm, nj),
            in_specs=[x_spec, g_spec, w_spec],
            out_specs=o_spec,
            out_shape=jax.ShapeDtypeStruct((m, n), F32),
            scratch_shapes=scratch,
            compiler_params=_cparams("arbitrary", "arbitrary"),
            name="rms_matmul",
        )(x, g.reshape(1, k), w)
    ms = xs.shape[0]
    return pl.pallas_call(
        _rms_matmul2_kernel,
        grid=(m // tm, nj),
        in_specs=[x_spec, pl.BlockSpec((ms, k), lambda i, j: (0, 0)), g_spec, w_spec],
        out_specs=[o_spec, pl.BlockSpec((ms, tn), lambda i, j: (0, col(i, j)))],
        out_shape=[jax.ShapeDtypeStruct((m, n), F32), jax.ShapeDtypeStruct((ms, n), F32)],
        scratch_shapes=[scratch[0], pltpu.VMEM((ms, k), F32), scratch[1]],
        compiler_params=_cparams("arbitrary", "arbitrary"),
        name="rms_matmul2",
    )(x, xs, g.reshape(1, k), w)


def _kv_proj_kernel(x_ref, g_ref, w_ref, cos_ref, sin_ref, cmp_ref, slc_ref, win_ref, *rest):
    xn_ref, wb_ref = rest[-2:]
    attn = rest[:-2]
    i = pl.program_id(0)
    j = pl.program_id(1)
    half = N_KV * HEAD_DIM

    @pl.when(j == 0)
    def _():
        xn_ref[...] = _rms(x_ref[...], g_ref[...]).astype(xn_ref.dtype)

    @pl.when(i == 0)
    def _():
        wb_ref[j] = w_ref[...].astype(BF16)

    for br, ref in enumerate((cmp_ref, slc_ref, win_ref)):
        for c in range(2):
            @pl.when(j == 2 * br + c)
            def _(ref=ref, br=br, c=c):
                y = _dot(xn_ref[...], wb_ref[2 * br + c])
                val = y
                if c == 0:
                    cos2 = cos_ref[...]
                    sin2 = sin_ref[...]
                    val = jnp.concatenate(
                        [_rope(y[:, h * HEAD_DIM:(h + 1) * HEAD_DIM], cos2, sin2) for h in range(N_KV)], axis=1)
                for h in range(N_KV):
                    ref[:, c * N_KV + h, :] = val[:, h * HEAD_DIM:(h + 1) * HEAD_DIM]
                if attn and br == 1 and c == 1:
                    attn[1][0] = jnp.transpose(val).astype(BF16)
                elif attn and br == 2 and c == 1:
                    for t in range(val.shape[0] // Q_BLOCK):
                        attn[3][t] = jnp.transpose(val[t * Q_BLOCK:(t + 1) * Q_BLOCK]).astype(BF16)
                elif attn and br > 0:
                    attn[2 * (br - 1) + c][...] = val.astype(BF16)


def kv_proj(x, g, w, cos2, sin2, tm, attn_copies):
    m, k = x.shape
    half = N_KV * HEAD_DIM
    nt = cos2.shape[0] // tm
    nj = 2 * N_BRANCH
    col = _once_per_column(nj)
    out = jax.ShapeDtypeStruct((m, KV_SLOTS, HEAD_DIM), F32)
    ospec = pl.BlockSpec((tm, KV_SLOTS, HEAD_DIM), lambda i, j: (i, 0, 0))
    out_shape, out_specs = [out, out, out], [ospec, ospec, ospec]
    if attn_copies:
        out16 = jax.ShapeDtypeStruct((m, half), BF16)
        ospec16 = pl.BlockSpec((tm, half), lambda i, j: (i, 0))
        out_shape += [out16, jax.ShapeDtypeStruct((m // tm, half, tm), BF16), out16,
                      jax.ShapeDtypeStruct((m // Q_BLOCK, half, Q_BLOCK), BF16)]
        out_specs += [ospec16, pl.BlockSpec((1, half, tm), lambda i, j: (i, 0, 0)), ospec16,
                      pl.BlockSpec((tm // Q_BLOCK, half, Q_BLOCK), lambda i, j: (i, 0, 0))]
    return pl.pallas_call(
        _kv_proj_kernel,
        grid=(m // tm, nj),
        in_specs=[pl.BlockSpec((tm, k), lambda i, j: (i, 0)),
                  pl.BlockSpec((1, k), lambda i, j: (0, 0)),
                  pl.BlockSpec((k, half), lambda i, j: (0, col(i, j))),
                  pl.BlockSpec((tm, HEAD_DIM), lambda i, j: (i % nt, 0)),
                  pl.BlockSpec((tm, HEAD_DIM), lambda i, j: (i % nt, 0))],
        out_specs=out_specs,
        out_shape=out_shape,
        scratch_shapes=[pltpu.VMEM((tm, k), _lhs_dtype(tm)), pltpu.VMEM((nj, k, half), BF16)],
        compiler_params=_cparams("arbitrary", "arbitrary"),
        name="kv_proj",
    )(x, g.reshape(1, k), w, cos2, sin2)


def _pool_prompt_kernel(cur_ref, prev_ref, w_ref, sc_ref, o_ref, *, ts):
    i = pl.program_id(1)
    keep = (i > 0).astype(F32)
    pos = (i * ts + lax.broadcasted_iota(jnp.int32, (ts, 1), 0) + 1).astype(F32)
    diffs = []
    for g, w in enumerate(POOL_WINDOWS):
        sl = slice(g * POOL_GC, (g + 1) * POOL_GC)
        x = cur_ref[0, :, sl]
        ext = jnp.concatenate([prev_ref[0, :, sl] * keep, x], axis=0)
        span = 1
        while span < w:
            ext = ext[span:] + ext[:-span]
            span *= 2
        win = ext[17 - w:17 - w + ts]
        diffs.append((win / jnp.minimum(pos, float(w)) - x).astype(BF16))
    for g in range(len(POOL_WINDOWS)):
        sl = slice(g * POOL_GC, (g + 1) * POOL_GC)
        o_ref[0, :, sl] = _dot(diffs[g], w_ref[g]) * sc_ref[:, sl]


def pool_prompt(u3, w_grp, scale, ts=512):
    b, s, n = u3.shape
    r = ts // 16
    return pl.pallas_call(
        functools.partial(_pool_prompt_kernel, ts=ts),
        grid=(b, s // ts),
        in_specs=[pl.BlockSpec((1, ts, MAIN_W), lambda bi, i: (bi, i, 0)),
                  pl.BlockSpec((1, 16, MAIN_W), lambda bi, i: (bi, jnp.maximum(i * r - 1, 0), 0)),
                  pl.BlockSpec((4, POOL_GC, POOL_GC), lambda bi, i: (0, 0, 0)),
                  pl.BlockSpec((1, MAIN_W), lambda bi, i: (0, 0))],
        out_specs=pl.BlockSpec((1, ts, MAIN_W), lambda bi, i: (bi, i, 0)),
        out_shape=jax.ShapeDtypeStruct((b, s, MAIN_W), F32),
        compiler_params=_cparams("parallel", "arbitrary"),
        name="pool_prompt",
    )(u3, u3, w_grp, scale.reshape(1, MAIN_W))


def _pool_sample_kernel(cur_ref, buf_ref, w_ref, sc_ref, o_ref):
    for g, w in enumerate(POOL_WINDOWS):
        sl = slice(g * POOL_GC, (g + 1) * POOL_GC)
        x = cur_ref[:, sl]
        win = x
        for r in range(POOL_BUF - (w - 1), POOL_BUF):
            win = win + buf_ref[r, :, sl]
        d = win / float(w) - x
        o_ref[:, sl] = _dot(d, w_ref[g]) * sc_ref[:, sl]


def pool_sample(u, buf_t, w_grp, scale):
    db = u.shape[0]
    return pl.pallas_call(
        _pool_sample_kernel,
        grid=(1,),
        in_specs=[pl.BlockSpec((db, MAIN_W), lambda i: (0, 0)),
                  pl.BlockSpec((POOL_BUF, db, MAIN_W), lambda i: (0, 0, 0)),
                  pl.BlockSpec((4, POOL_GC, POOL_GC), lambda i: (0, 0, 0)),
                  pl.BlockSpec((1, MAIN_W), lambda i: (0, 0))],
        out_specs=pl.BlockSpec((db, MAIN_W), lambda i: (0, 0)),
        out_shape=jax.ShapeDtypeStruct((db, MAIN_W), F32),
        compiler_params=_cparams("arbitrary"),
        name="pool_sample",
    )(u, buf_t, w_grp, scale.reshape(1, MAIN_W))


def _mem_attend_kernel(q_ref, kv_ref, o_ref, *, tq, slots):
    rows = max(tq, 8)
    logits, values = [], []
    for h in range(N_MEM_HEADS):
        sl = slice(h * MEM_HEAD_DIM, (h + 1) * MEM_HEAD_DIM)
        q = q_ref[0, :, sl]
        if tq < rows:
            q = jnp.broadcast_to(q[0:1], (rows, MEM_HEAD_DIM))
        if slots:
            k = kv_ref[pl.ds(h, MEM_TOKENS, stride=2 * N_MEM_HEADS), :]
            v = kv_ref[pl.ds(N_MEM_HEADS + h, MEM_TOKENS, stride=2 * N_MEM_HEADS), :]
        else:
            k = kv_ref[:, sl]
            v = kv_ref[:, MEM_W + h * MEM_HEAD_DIM:MEM_W + (h + 1) * MEM_HEAD_DIM]
        logits.append(_dot_t(q, k) * MEM_SCALE)
        values.append(v)
    for h in range(N_MEM_HEADS):
        sl = slice(h * MEM_HEAD_DIM, (h + 1) * MEM_HEAD_DIM)
        s, v = logits[h], values[h]
        e = jnp.exp(s - jnp.max(s, axis=-1, keepdims=True))
        o = _dot(e, v) / jnp.sum(e, axis=-1, keepdims=True)
        o_ref[0, :, sl] = o[0:tq]


def mem_attend(u3, mkv, tq, layer=None):
    b, s, _ = u3.shape
    if layer is None:
        kv_spec = pl.BlockSpec((MEM_TOKENS, 2 * MEM_W), lambda bi, i: (bi, 0))
    else:
        kv_spec = pl.BlockSpec((MEM_TOKENS * 2 * N_MEM_HEADS, MEM_HEAD_DIM), lambda bi, i: (layer * b + bi, 0))
    return pl.pallas_call(
        functools.partial(_mem_attend_kernel, tq=tq, slots=layer is not None),
        grid=(b, s // tq),
        in_specs=[pl.BlockSpec((1, tq, MEM_W), lambda bi, i: (bi, i, MAIN_W // MEM_W)), kv_spec],
        out_specs=pl.BlockSpec((1, tq, MEM_W), lambda bi, i: (bi, i, 0)),
        out_shape=jax.ShapeDtypeStruct((b, s, MEM_W), F32),
        compiler_params=_cparams("parallel", "arbitrary"),
        name="mem_attend",
    )(u3, mkv)


def _out_proj_kernel(z_ref, mo_ref, w_ref, x_ref, o_ref, lhs_ref, wb_ref):
    i = pl.program_id(0)
    j = pl.program_id(1)

    @pl.when(j == 0)
    def _():
        lhs_ref[:, :MAIN_W] = z_ref[...].astype(lhs_ref.dtype)
        lhs_ref[:, MAIN_W:] = mo_ref[...].astype(lhs_ref.dtype)

    @pl.when(i == 0)
    def _():
        wb_ref[j] = w_ref[...].astype(BF16)

    o_ref[...] = x_ref[...] + _dot(lhs_ref[...], wb_ref[j])


def _out_proj2_kernel(z_ref, mo_ref, w_ref, x_ref, zs_ref, mos_ref, xs_ref, o_ref, os_ref, lhs_ref, wb_ref):
    _out_proj_kernel(z_ref, mo_ref, w_ref, x_ref, o_ref, lhs_ref, wb_ref)
    j = pl.program_id(1)

    @pl.when(pl.program_id(0) == 0)
    def _():
        lhs = jnp.concatenate([zs_ref[...], mos_ref[...]], axis=1)
        os_ref[...] = xs_ref[...] + _dot(lhs, wb_ref[j])


def out_proj(z, mo, w, layer, x, tm, tn, sample=None):
    m = x.shape[0]
    k = MAIN_W + MEM_W
    nj = D_MODEL // tn
    col = _once_per_column(nj)
    in_specs = [pl.BlockSpec((tm, MAIN_W), lambda i, j: (i, 0)),
                pl.BlockSpec((tm, MEM_W), lambda i, j: (i, 0)),
                pl.BlockSpec((None, k, tn), lambda i, j: (layer, 0, col(i, j))),
                pl.BlockSpec((tm, tn), lambda i, j: (i, j))]
    o_spec = pl.BlockSpec((tm, tn), lambda i, j: (i, j))
    scratch = [pltpu.VMEM((tm, k), _lhs_dtype(tm)), pltpu.VMEM((nj, k, tn), BF16)]
    if sample is None:
        return pl.pallas_call(
            _out_proj_kernel,
            grid=(m // tm, nj),
            in_specs=in_specs,
            out_specs=o_spec,
            out_shape=jax.ShapeDtypeStruct((m, D_MODEL), F32),
            scratch_shapes=scratch,
            compiler_params=_cparams("arbitrary", "arbitrary"),
            name="out_proj",
        )(z, mo, w, x)
    zs, mos, xs = sample
    ms = xs.shape[0]
    s_spec = pl.BlockSpec((ms, tn), lambda i, j: (0, col(i, j)))
    return pl.pallas_call(
        _out_proj2_kernel,
        grid=(m // tm, nj),
        in_specs=in_specs + [pl.BlockSpec((ms, MAIN_W), lambda i, j: (0, 0)),
                             pl.BlockSpec((ms, MEM_W), lambda i, j: (0, 0)), s_spec],
        out_specs=[o_spec, s_spec],
        out_shape=[jax.ShapeDtypeStruct((m, D_MODEL), F32), jax.ShapeDtypeStruct((ms, D_MODEL), F32)],
        scratch_shapes=scratch,
        compiler_params=_cparams("arbitrary", "arbitrary"),
        name="out_proj2",
    )(z, mo, w, x, zs, mos, xs)


def _ffn_kernel(x_ref, g_ref, wa_ref, wb_ref, wd_ref, gf_ref, o_ref, xn_ref, *, final):
    j = pl.program_id(1)

    @pl.when(j == 0)
    def _():
        x = x_ref[...]
        xn_ref[...] = _rms(x, g_ref[...]).astype(xn_ref.dtype)
        o_ref[...] = x

    xn = xn_ref[...]
    a = _dot(xn, wa_ref[...])
    b = _dot(xn, wb_ref[...])
    o_ref[...] += _dot(jax.nn.silu(a) * b, wd_ref[...])

    if final:
        @pl.when(j == pl.num_programs(1) - 1)
        def _():
            o_ref[...] = _rms(o_ref[...], gf_ref[...])


def _ffn2_kernel(x_ref, g_ref, wa_ref, wb_ref, wd_ref, gf_ref, xs_ref, o_ref, os_ref, xn_ref, xsn_ref, *, final):
    _ffn_kernel(x_ref, g_ref, wa_ref, wb_ref, wd_ref, gf_ref, o_ref, xn_ref, final=final)
    j = pl.program_id(1)

    @pl.when(pl.program_id(0) == 0)
    def _():
        @pl.when(j == 0)
        def _():
            xs = xs_ref[...]
            xsn_ref[...] = _rms(xs, g_ref[...])
            os_ref[...] = xs

        xsn = xsn_ref[...]
        a = _dot(xsn, wa_ref[...])
        b = _dot(xsn, wb_ref[...])
        os_ref[...] += _dot(jax.nn.silu(a) * b, wd_ref[...])

        if final:
            @pl.when(j == pl.num_programs(1) - 1)
            def _():
                os_ref[...] = _rms(os_ref[...], gf_ref[...])


def ffn(x, g, w_gu, w_down, layer, g_final, final, tm, tf, xs=None):
    m, k = x.shape
    nf = D_FF // tf
    in_specs = [pl.BlockSpec((tm, k), lambda i, j: (i, 0), pipeline_mode=pl.Buffered(1)),
                pl.BlockSpec((1, k), lambda i, j: (0, 0)),
                pl.BlockSpec((None, k, tf), lambda i, j: (layer, 0, j)),
                pl.BlockSpec((None, k, tf), lambda i, j: (layer, 0, nf + j)),
                pl.BlockSpec((None, tf, k), lambda i, j: (layer, j, 0)),
                pl.BlockSpec((1, k), lambda i, j: (0, 0))]
    o_spec = pl.BlockSpec((tm, k), lambda i, j: (i, 0))
    scratch = [pltpu.VMEM((tm, k), _lhs_dtype(tm))]
    args = (x, g.reshape(1, k), w_gu, w_gu, w_down, g_final.reshape(1, k))
    if xs is None:
        return pl.pallas_call(
            functools.partial(_ffn_kernel, final=final),
            grid=(m // tm, nf),
            in_specs=in_specs,
            out_specs=o_spec,
            out_shape=jax.ShapeDtypeStruct((m, k), F32),
            scratch_shapes=scratch,
            compiler_params=_cparams("parallel", "arbitrary"),
            name="ffn",
        )(*args)
    ms = xs.shape[0]
    s_spec = pl.BlockSpec((ms, k), lambda i, j: (0, 0))
    return pl.pallas_call(
        functools.partial(_ffn2_kernel, final=final),
        grid=(m // tm, nf),
        in_specs=in_specs + [s_spec],
        out_specs=[o_spec, s_spec],
        out_shape=[jax.ShapeDtypeStruct((m, k), F32), jax.ShapeDtypeStruct((ms, k), F32)],
        scratch_shapes=scratch + [pltpu.VMEM((ms, k), F32)],
        compiler_params=_cparams("arbitrary", "arbitrary"),
        name="ffn2",
    )(*args, xs)


def _compress_kernel(tbl_ref, *refs, n_steps):
    pages = refs[:PAGES_PER_STEP]
    nxt_ref, extra_ref, w1_ref, pe_ref, w2_ref, o_ref, w1b_ref, lhs_ref = refs[PAGES_PER_STEP:]
    p = pl.program_id(1)
    on_extra = p == n_steps
    nxt_extra = p == n_steps - 1
    nh = CMP_STEP * N_KV
    width = CMP_STRIDE * HEAD_DIM
    d = HEAD_DIM
    group = 4
    n_groups = CMP_STEP // group + 1
    n_rows = n_groups * group * N_KV

    @pl.when((pl.program_id(0) == 0) & (p == 0))
    def _():
        for c in range(2):
            w1b_ref[c] = jnp.concatenate([w1_ref[c, 0], w1_ref[c, 1]], axis=1).astype(BF16)

    def slots(n, r):
        if n < CMP_STEP:
            k, i = divmod(n, CH_PER_PAGE)
            x = pages[k][0, i * CMP_STRIDE + r]
            return jnp.where(on_extra, extra_ref[0, i * CMP_STRIDE + r] if k == 0 else 0.0, x)
        if n == CMP_STEP:
            return jnp.where(on_extra, 0.0, jnp.where(nxt_extra, extra_ref[0, r], nxt_ref[0, r]))
        return jnp.zeros((KV_SLOTS, HEAD_DIM), F32)

    for q in range(n_groups):
        for r in range(CMP_STRIDE):
            tiles = [slots(group * q + i, r) for i in range(group)]
            for c in range(2):
                piece = jnp.concatenate([t[c * N_KV:(c + 1) * N_KV] for t in tiles], axis=0)
                lhs_ref[c, group * N_KV * q:group * N_KV * (q + 1), r * HEAD_DIM:(r + 1) * HEAD_DIM] = piece.astype(BF16)
    for c in range(2):
        tail = jnp.concatenate([pe_ref[c, 0], pe_ref[c, 1], jnp.zeros((14, width), F32)], axis=0)
        lhs_ref[c, n_rows:n_rows + 16, :] = tail.astype(BF16)
    hc = [_dot(lhs_ref[c], w1b_ref[c]) for c in range(2)]
    for c in range(2):
        pe_term = hc[c][n_rows:n_rows + 1, :d] + hc[c][n_rows + 1:n_rows + 2, d:]
        hid = jax.nn.gelu(hc[c][:nh, :d] + hc[c][N_KV:nh + N_KV, d:] + pe_term)
        o_ref[0, c] = _dot(hid, w2_ref[c])


def compress(pages, table, extra, cmp_pe, cmp_w1, cmp_w2):
    nb, n_pages = table.shape
    n_steps = n_pages // PAGES_PER_STEP
    w1 = cmp_w1.reshape(2, 2, CMP_STRIDE * HEAD_DIM, HEAD_DIM)
    pe = cmp_pe.reshape(2, 2, 1, CMP_STRIDE * HEAD_DIM)

    def page_map(k):
        return lambda b, p, tbl: (tbl[b, jnp.minimum(p * PAGES_PER_STEP + k, n_pages - 1)], 0, 0, 0)

    in_specs = [pl.BlockSpec((1, PAGE_SIZE, KV_SLOTS, HEAD_DIM), page_map(k)) for k in range(PAGES_PER_STEP)]
    in_specs += [pl.BlockSpec((1, CMP_STRIDE, KV_SLOTS, HEAD_DIM), page_map(PAGES_PER_STEP)),
                 pl.BlockSpec((1, PAGE_SIZE, KV_SLOTS, HEAD_DIM), lambda b, p, tbl: (b, 0, 0, 0)),
                 pl.BlockSpec(w1.shape, lambda b, p, tbl: (0, 0, 0, 0)),
                 pl.BlockSpec(pe.shape, lambda b, p, tbl: (0, 0, 0, 0)),
                 pl.BlockSpec(cmp_w2.shape, lambda b, p, tbl: (0, 0, 0))]
    n_out = CMP_STEP * (n_steps + 1)
    out = pl.pallas_call(
        functools.partial(_compress_kernel, n_steps=n_steps),
        grid_spec=pltpu.PrefetchScalarGridSpec(
            num_scalar_prefetch=1,
            grid=(nb, n_steps + 1),
            in_specs=in_specs,
            out_specs=pl.BlockSpec((1, 2, CMP_STEP * N_KV, HEAD_DIM), lambda b, p, tbl: (b, 0, p, 0)),
            scratch_shapes=[pltpu.VMEM((2, CMP_STRIDE * HEAD_DIM, 2 * HEAD_DIM), BF16),
                            pltpu.VMEM((2, (CMP_STEP + 8) * N_KV, CMP_STRIDE * HEAD_DIM), BF16)],
        ),
        out_shape=jax.ShapeDtypeStruct((nb, 2, n_out * N_KV, HEAD_DIM), F32),
        compiler_params=_cparams("arbitrary", "arbitrary"),
        name="compress",
    )(table, *([pages] * (PAGES_PER_STEP + 1)), extra, w1, pe, cmp_w2)
    return jnp.transpose(out.reshape(nb, 2, n_out, N_KV, HEAD_DIM), (0, 1, 3, 2, 4))


def _cover(n_pad, j_pad, n_cmp):
    n = lax.broadcasted_iota(jnp.int32, (n_pad, j_pad), 0)
    j = lax.broadcasted_iota(jnp.int32, (n_pad, j_pad), 1)
    hit = (n * CMP_STRIDE < j * SLC_BLOCK + SLC_BLOCK) & (n * CMP_STRIDE + CMP_BLOCK - 1 >= j * SLC_BLOCK)
    return (hit & (n < n_cmp)).astype(F32)


def _cover_t(j_pad, n_pad, n_cmp):
    j = lax.broadcasted_iota(jnp.int32, (j_pad, n_pad), 0)
    n = lax.broadcasted_iota(jnp.int32, (j_pad, n_pad), 1)
    hit = (n * CMP_STRIDE < j * SLC_BLOCK + SLC_BLOCK) & (n * CMP_STRIDE + CMP_BLOCK - 1 >= j * SLC_BLOCK)
    return (hit & (n < n_cmp)).astype(F32)


def _softmax_rows(s):
    e = jnp.exp(s - jnp.max(s, axis=-1, keepdims=True))
    return e / jnp.sum(e, axis=-1, keepdims=True)


def _nsa_prompt_kernel(q_ref, gl_ref, bg_ref, cos_ref, sin_ref, ck_ref, cvt_ref, ks_ref, vt_ref, kw_ref, vwt_ref,
                       o_ref, qt_ref, drop_ref, z_ref, *, n_cmp, n_slc, kt):
    qb = pl.program_id(1)
    tq = Q_BLOCK
    cos2 = cos_ref[...]
    sin2 = sin_ref[...]
    n_pad = ck_ref.shape[2]
    span = WINDOW + tq
    qpos_row = qb * tq + lax.broadcasted_iota(jnp.int32, (1, tq), 1)
    qpos = jnp.concatenate([qpos_row] * GQA, axis=1)
    gates_t = jnp.transpose(jax.nn.sigmoid(gl_ref[...] + bg_ref[...]))

    def gate_row(hd, br):
        return gates_t[hd * N_BRANCH + br:hd * N_BRANCH + br + 1, :]

    n_col = lax.broadcasted_iota(jnp.int32, (n_pad, 1), 0)
    bias_c = jnp.where((n_col * CMP_STRIDE + CMP_BLOCK - 1 <= qpos) & (n_col < n_cmp), 0.0, NEG)
    any_c = (qpos >= CMP_BLOCK - 1).astype(F32)
    cover_t = _cover_t(n_slc, n_pad, n_cmp)
    j_col = lax.broadcasted_iota(jnp.int32, (n_slc, 1), 0)
    qblk = jnp.right_shift(qpos_row, SLC_SHIFT)
    forced = (j_col == 0) | (j_col == qblk) | (j_col == qblk - 1)
    valid = j_col <= qblk
    n_sel = min(N_SELECT, n_slc)

    w_lo = jnp.maximum(qb * tq - WINDOW, 0)
    w_lo = pl.multiple_of(w_lo, tq)
    kpos_w = w_lo + lax.broadcasted_iota(jnp.int32, (span, 1), 0)
    bias_w = jnp.where((kpos_w <= qpos) & (qpos - kpos_w < WINDOW), 0.0, NEG)

    heads = range(N_KV)
    s_cmp, s_win = [], []
    for h in heads:
        q = jnp.concatenate(
            [_rope(q_ref[:, (h * GQA + g) * HEAD_DIM:(h * GQA + g + 1) * HEAD_DIM], cos2, sin2)
             for g in range(GQA)], axis=0)
        qt = jnp.transpose(q * (SCALE * LOG2E)).astype(BF16)
        qt_ref[h] = qt
        hs = slice(h * HEAD_DIM, (h + 1) * HEAD_DIM)
        s_cmp.append(_dot(ck_ref[0, h], qt) + bias_c)
        s_win.append(_dot(kw_ref[pl.ds(w_lo, span), hs], qt) + bias_w)

    o_cmp, scores = [], []
    for h in heads:
        e = jnp.exp2(s_cmp[h] - jnp.max(s_cmp[h], axis=0, keepdims=True))
        p = e / jnp.sum(e, axis=0, keepdims=True) * any_c
        o_cmp.append(_dot(cvt_ref[0, h], p))
        p_sum = p[:, 0:tq] + p[:, tq:2 * tq] + p[:, 2 * tq:3 * tq]
        score = jnp.dot(cover_t, p_sum, preferred_element_type=F32,
                        precision=lax.Precision.HIGHEST)
        scores.append(jnp.where(forced, jnp.inf, jnp.where(valid, score, -jnp.inf)))

    for h in heads:
        hs = slice(h * HEAD_DIM, (h + 1) * HEAD_DIM)
        e = jnp.exp2(s_win[h] - jnp.max(s_win[h], axis=0, keepdims=True))
        o_w = _dot(vwt_ref[w_lo // tq, hs, :], e[0:tq])
        for i in range(1, span // tq):
            o_w = o_w + _dot(vwt_ref[w_lo // tq + i, hs, :], e[i * tq:(i + 1) * tq])
        o_w = o_w / jnp.sum(e, axis=0, keepdims=True)
        for g in range(GQA):
            hd = h * GQA + g
            r = slice(g * tq, (g + 1) * tq)
            z_ref[hd] = gate_row(hd, 0) * o_cmp[h][:, r] + gate_row(hd, 2) * o_w[:, r]

    for h in heads:
        score = scores[h]
        rank = jnp.zeros((n_slc, tq), F32)
        for i in range(n_slc):
            row = score[i:i + 1, :]
            beats = (row > score) | ((row == score) & (i < j_col))
            rank = rank + jnp.where(beats, 1.0, 0.0)
        drop_ref[h] = jnp.where(rank >= n_sel, NEG, 0.0)

    blocks_per_tile = kt // SLC_BLOCK

    def slc_step(t, carry, nk=kt):
        k0 = pl.multiple_of(t * kt, kt)
        kpos = k0 + lax.broadcasted_iota(jnp.int32, (nk, 1), 0)
        causal = jnp.where(kpos <= qpos_row, 0.0, NEG)
        def logits(h):
            hs = slice(h * HEAD_DIM, (h + 1) * HEAD_DIM)
            dropped = jnp.concatenate(
                [jnp.broadcast_to(drop_ref[h, pl.ds(t * blocks_per_tile + jb, 1), :], (SLC_BLOCK, tq))
                 for jb in range(nk // SLC_BLOCK)], axis=0)
            bias = dropped + causal
            return _dot(ks_ref[pl.ds(k0, nk), hs], qt_ref[h]) + jnp.concatenate([bias] * GQA, axis=1)

        scores = [logits(h) for h in range(N_KV)]
        out = []
        for h in range(N_KV):
            m, l, acc = carry[h]
            hs = slice(h * HEAD_DIM, (h + 1) * HEAD_DIM)
            s = scores[h]
            m_new = jnp.maximum(m, jnp.max(s, axis=0, keepdims=True))
            a = jnp.exp2(m - m_new)
            e = jnp.exp2(s - m_new)
            l = l * a + jnp.sum(e, axis=0, keepdims=True)
            acc = acc * a + _dot(vt_ref[t, hs, :nk], e)
            out.append((m_new, l, acc))
        return tuple(out)

    q_end = qb * tq + tq
    n_full = q_end // kt
    init = (jnp.full((1, GQA * tq), NEG, F32), jnp.zeros((1, GQA * tq), F32), jnp.zeros((HEAD_DIM, GQA * tq), F32))
    done = lax.fori_loop(0, n_full, slc_step, (init,) * N_KV)
    tails = [lambda c: c] + [functools.partial(lambda c, nk: slc_step(n_full, c, nk), nk=i * tq)
                             for i in range(1, kt // tq)]
    done = lax.switch((q_end % kt) // tq, tails, done)
    for h in range(N_KV):
        _, l, acc = done[h]
        o_s = acc / l
        for g in range(GQA):
            hd = h * GQA + g
            r = slice(g * tq, (g + 1) * tq)
            o_ref[:, hd * HEAD_DIM:(hd + 1) * HEAD_DIM] = jnp.transpose(z_ref[hd] + gate_row(hd, 1) * o_s[:, r])


def nsa_prompt(u, b_gate_pad, cos2, sin2, ckv, slc_k, slc_vt, win_k, win_vt, batch, seq):
    nqb = seq // Q_BLOCK
    n_pad = ckv.shape[3]
    half = N_KV * HEAD_DIM
    cmp_k = ckv[:, 0]
    cmp_vt = jnp.transpose(ckv[:, 1], (0, 1, 3, 2))
    kspec = pl.BlockSpec((seq, half), lambda b, i: (b, 0))
    vt_spec = pl.BlockSpec((seq // SLC_KT, half, SLC_KT), lambda b, i: (b, 0, 0))
    wvt_spec = pl.BlockSpec((seq // Q_BLOCK, half, Q_BLOCK), lambda b, i: (b, 0, 0))
    return pl.pallas_call(
        functools.partial(_nsa_prompt_kernel, n_cmp=seq // CMP_STRIDE - 1, n_slc=seq // SLC_BLOCK, kt=SLC_KT),
        grid=(batch, nqb),
        in_specs=[pl.BlockSpec((Q_BLOCK, MAIN_W), lambda b, i: (b * nqb + i, 0)),
                  pl.BlockSpec((Q_BLOCK, GATE_PAD), lambda b, i: (b * nqb + i, (MAIN_W + MEM_W) // GATE_PAD)),
                  pl.BlockSpec((1, GATE_PAD), lambda b, i: (0, 0)),
                  pl.BlockSpec((Q_BLOCK, HEAD_DIM), lambda b, i: (i, 0)),
                  pl.BlockSpec((Q_BLOCK, HEAD_DIM), lambda b, i: (i, 0)),
                  pl.BlockSpec((1, N_KV, n_pad, HEAD_DIM), lambda b, i: (b, 0, 0, 0)),
                  pl.BlockSpec((1, N_KV, HEAD_DIM, n_pad), lambda b, i: (b, 0, 0, 0)),
                  kspec, vt_spec, kspec, wvt_spec],
        out_specs=pl.BlockSpec((Q_BLOCK, MAIN_W), lambda b, i: (b * nqb + i, 0)),
        out_shape=jax.ShapeDtypeStruct((batch * seq, MAIN_W), F32),
        scratch_shapes=[pltpu.VMEM((N_KV, HEAD_DIM, GQA * Q_BLOCK), BF16),
                        pltpu.VMEM((N_KV, seq // SLC_BLOCK, Q_BLOCK), F32),
                        pltpu.VMEM((N_HEADS, HEAD_DIM, Q_BLOCK), F32)],
        compiler_params=_cparams("parallel", "arbitrary"),
        name="nsa_prompt",
    )(u, u, b_gate_pad, cos2, sin2, cmp_k, cmp_vt, slc_k, slc_vt, win_k, win_vt)


HEAD_ROWS = 16


def _head_mask(h):
    r = lax.broadcasted_iota(jnp.int32, (HEAD_ROWS, 1), 0)
    return (r >= h * GQA) & (r < (h + 1) * GQA)


def _nsa_sample_a_kernel(q_ref, cos_ref, sin_ref, ckv_ref, wkv_ref, wnew_ref, qr_ref, oc_ref, ow_ref, sel_ref,
                         *, n_cmp, n_slc, j_pad):
    n_pad = ckv_ref.shape[3]
    q = _rope(q_ref[0], cos_ref[...], sin_ref[...])
    qr_ref[0] = q
    qs = q * SCALE
    n_i = lax.broadcasted_iota(jnp.int32, (1, n_pad), 1)
    vis_c = (n_i * CMP_STRIDE + CMP_BLOCK - 1 <= PAST_LEN) & (n_i < n_cmp)
    cover = _cover(n_pad, j_pad, n_cmp)
    j_row = lax.broadcasted_iota(jnp.int32, (1, j_pad), 1)
    j_col = lax.broadcasted_iota(jnp.int32, (j_pad, 1), 0)
    qblk = PAST_LEN // SLC_BLOCK
    lane = lax.broadcasted_iota(jnp.int32, (N_SELECT, HEAD_DIM), 1)
    r_col = lax.broadcasted_iota(jnp.int32, (N_SELECT, 1), 0)
    n_win = wkv_ref.shape[1] // KV_SLOTS
    kpos_w = PAST_LEN - n_win + lax.broadcasted_iota(jnp.int32, (1, n_win), 1)
    vis_w = (kpos_w <= PAST_LEN) & (PAST_LEN - kpos_w < WINDOW)

    heads = range(N_KV)
    forced = (j_row == 0) | (j_row == qblk) | (j_row == qblk - 1)
    s_cmp = [jnp.where(vis_c, _dot_t(qs, ckv_ref[0, 0, h]), NEG) for h in heads]
    s_win = [jnp.where(vis_w, _dot_t(qs, wkv_ref[0, pl.ds(h, n_win, stride=KV_SLOTS), :]), NEG) for h in heads]

    o_c = jnp.zeros((HEAD_ROWS, HEAD_DIM), F32)
    scores = []
    for h in heads:
        mine = _head_mask(h)
        p = _softmax_rows(s_cmp[h]) * float(PAST_LEN >= CMP_BLOCK - 1)
        o_c = jnp.where(mine, _dot(p, ckv_ref[0, 1, h]), o_c)
        p_sum = jnp.sum(jnp.where(mine, p, 0.0), axis=0, keepdims=True)
        score = jnp.dot(jnp.broadcast_to(p_sum, (8, n_pad)), cover, preferred_element_type=F32,
                        precision=lax.Precision.HIGHEST)
        score = jnp.where(forced, jnp.inf, jnp.where(j_row <= qblk, score, -jnp.inf))
        scores.append(jnp.where(j_row < n_slc, score, -jnp.inf))
    oc_ref[0] = o_c

    o_w = jnp.zeros((HEAD_ROWS, HEAD_DIM), F32)
    for h in heads:
        hs = slice(h * HEAD_DIM, (h + 1) * HEAD_DIM)
        vsl = slice((N_KV + h) * HEAD_DIM, (N_KV + h + 1) * HEAD_DIM)
        v_win = wkv_ref[0, pl.ds(N_KV + h, n_win, stride=KV_SLOTS), :]
        s = s_win[h]
        s_new = jnp.sum(qs * wnew_ref[0, :, hs], axis=-1, keepdims=True)
        m = jnp.maximum(jnp.max(s, axis=-1, keepdims=True), s_new)
        e = jnp.exp(s - m)
        e_new = jnp.exp(s_new - m)
        o = (_dot(e, v_win) + e_new * wnew_ref[0, :, vsl]) / (jnp.sum(e, axis=-1, keepdims=True) + e_new)
        o_w = jnp.where(_head_mask(h), o, o_w)
    ow_ref[0] = o_w

    sel_out = jnp.zeros((N_SELECT, HEAD_DIM), jnp.int32)
    for h in heads:
        score = scores[h]
        s_col = jnp.transpose(score)[:, 0:1]
        beats = ((s_col > score[0:1]) | ((s_col == score[0:1]) & (j_col < j_row))) & (j_col < n_slc)
        rank = jnp.sum(beats.astype(F32), axis=0, keepdims=True)
        hit = (rank == r_col.astype(F32)) & (j_row < n_slc)
        idx = jnp.sum(jnp.where(hit, j_row.astype(F32), 0.0), axis=1, keepdims=True)
        sel_out = jnp.where(lane == h, idx.astype(jnp.int32), sel_out)
    sel_ref[0] = sel_out


def nsa_sample_a(q16, cos2, sin2, ckv, win_cache, win_new, n_cmp, n_slc):
    db = q16.shape[0]
    n_pad = ckv.shape[3]
    win_rows = win_cache.shape[1]
    j_pad = -(-n_slc // 128) * 128
    hspec = pl.BlockSpec((1, HEAD_ROWS, HEAD_DIM), lambda b: (b, 0, 0))
    hout = jax.ShapeDtypeStruct((db, HEAD_ROWS, HEAD_DIM), F32)
    return pl.pallas_call(
        functools.partial(_nsa_sample_a_kernel, n_cmp=n_cmp, n_slc=n_slc, j_pad=j_pad),
        grid=(db,),
        in_specs=[hspec,
                  pl.BlockSpec((1, HEAD_DIM), lambda b: (0, 0)),
                  pl.BlockSpec((1, HEAD_DIM), lambda b: (0, 0)),
                  pl.BlockSpec((1, 2, N_KV, n_pad, HEAD_DIM), lambda b: (b, 0, 0, 0, 0)),
                  pl.BlockSpec((1, win_rows, HEAD_DIM), lambda b: (b, 0, 0)),
                  pl.BlockSpec((1, 1, KV_W), lambda b: (b, 0, 0))],
        out_specs=[hspec, hspec, hspec, pl.BlockSpec((1, N_SELECT, HEAD_DIM), lambda b: (b, 0, 0))],
        out_shape=[hout, hout, hout, jax.ShapeDtypeStruct((db, N_SELECT, HEAD_DIM), jnp.int32)],
        compiler_params=_cparams("arbitrary"),
        name="nsa_sample_a",
    )(q16, cos2, sin2, ckv, win_cache, win_new)


def _nsa_sample_b_kernel(sel_ref, tbl_ref, q_ref, *refs):
    blocks = refs[:N_SELECT]
    new_ref, oc_ref, ow_ref, gl_ref, bg_ref, o_ref, os_ref = refs[N_SELECT:]
    b = pl.program_id(0)
    h = pl.program_id(1)
    n_keys = N_SELECT * SLC_BLOCK
    lane_blk = jnp.right_shift(lax.broadcasted_iota(jnp.int32, (1, n_keys), 1), SLC_SHIFT)
    new_k = new_ref[0, pl.ds(h, 1), :]
    new_v = new_ref[0, pl.ds(h + N_KV, 1), :]
    ks, vs = [], []
    blk_of = jnp.zeros((1, n_keys), jnp.int32)
    for r in range(N_SELECT):
        blk = sel_ref[b, h, r]
        in_past = blk * SLC_BLOCK < PAST_LEN
        ks.append(jnp.where(in_past, blocks[r][0, pl.ds(h, SLC_BLOCK, stride=KV_SLOTS), :], new_k))
        vs.append(jnp.where(in_past, blocks[r][0, pl.ds(h + N_KV, SLC_BLOCK, stride=KV_SLOTS), :], new_v))
        blk_of = jnp.where(lane_blk == r, blk, blk_of)
    kpos = blk_of * SLC_BLOCK + (lax.broadcasted_iota(jnp.int32, (1, n_keys), 1) & (SLC_BLOCK - 1))
    s = jnp.where(kpos <= PAST_LEN, _dot_t(q_ref[0] * SCALE, jnp.concatenate(ks, axis=0)), NEG)
    e = jnp.exp(s - jnp.max(s, axis=-1, keepdims=True))
    o_s = _dot(e, jnp.concatenate(vs, axis=0)) / jnp.sum(e, axis=-1, keepdims=True)

    @pl.when(h == 0)
    def _():
        os_ref[...] = jnp.zeros(os_ref.shape, F32)

    row = lax.broadcasted_iota(jnp.int32, (HEAD_ROWS, 1), 0)
    mine = (row >= h * GQA) & (row < (h + 1) * GQA)
    os_ref[...] = jnp.where(mine, o_s, os_ref[...])

    @pl.when(h == pl.num_programs(1) - 1)
    def _():
        gates = jax.nn.sigmoid(gl_ref[0] + bg_ref[...])
        o_ref[0] = gates[:, 0:1] * oc_ref[0] + gates[:, 1:2] * os_ref[...] + gates[:, 2:3] * ow_ref[0]


def nsa_sample_b(sel, page_table, q_rope, slc_halves, slc_new, o_c, o_w, gate_logits, b_gate16):
    db = q_rope.shape[0]
    last_blk = PAST_LEN // SLC_BLOCK - 1
    halves = PAGE_SIZE // SLC_BLOCK

    def cache_map(r):
        def index(b, h, sel_ref, tbl_ref):
            j = jnp.minimum(sel_ref[b, h, r], last_blk)
            return (tbl_ref[b, j // halves] * halves + j % halves, 0, 0)
        return index

    hspec = pl.BlockSpec((1, HEAD_ROWS, HEAD_DIM), lambda b, h, s, t: (b, 0, 0))
    return pl.pallas_call(
        _nsa_sample_b_kernel,
        grid_spec=pltpu.PrefetchScalarGridSpec(
            num_scalar_prefetch=2,
            grid=(db, N_KV),
            in_specs=[hspec]
            + [pl.BlockSpec((1, SLC_BLOCK * KV_SLOTS, HEAD_DIM), cache_map(r)) for r in range(N_SELECT)]
            + [pl.BlockSpec((1, KV_SLOTS, HEAD_DIM), lambda b, h, s, t: (b, 0, 0)),
               hspec, hspec, hspec,
               pl.BlockSpec((HEAD_ROWS, HEAD_DIM), lambda b, h, s, t: (0, 0))],
            out_specs=hspec,
            scratch_shapes=[pltpu.VMEM((HEAD_ROWS, HEAD_DIM), F32)],
        ),
        out_shape=jax.ShapeDtypeStruct((db, HEAD_ROWS, HEAD_DIM), F32),
        compiler_params=_cparams("arbitrary", "arbitrary"),
        name="nsa_sample_b",
    )(sel, page_table, q_rope, *([slc_halves] * N_SELECT), slc_new, o_c, o_w, gate_logits, b_gate16)


def _rope_tables(pos):
    half = HEAD_DIM // 2
    inv = ROPE_THETA ** (-jnp.arange(half, dtype=F32) / half)
    ang = pos.astype(F32)[:, None] * inv[None, :]
    cos, sin = jnp.cos(ang), jnp.sin(ang)
    return jnp.concatenate([cos, cos], -1), jnp.concatenate([-sin, sin], -1)


def _pad_w_in_b(w):
    return jnp.concatenate([w[..., :MAIN_W], w[..., MAIN_W + GATE_W:],
                            jnp.pad(w[..., MAIN_W:MAIN_W + GATE_W], ((0, 0), (0, 0), (0, GATE_PAD - GATE_W)))], axis=-1)


def kernel(x_prompt, x_sample, mem_prompt, state_pool, cache_cmp_kv, cache_slc_kv, cache_win_kv, cache_mem_kv, page_table, g_mix, w_in_a, pool_grp_w, pool_scale, w_in_b, b_gate, g_kv, w_kv, cmp_pe, cmp_w1, cmp_w2, g_mem, w_mem_kv, w_out, g_ffn, w_gu, w_down, g_final):
    batch, seq, _ = x_prompt.shape
    db = x_sample.shape[0]
    m = batch * seq
    tm = PROJ_TM
    n_phys = cache_cmp_kv.shape[0]
    w_in_b_pad = _pad_w_in_b(w_in_b)
    xp = x_prompt.reshape(m, D_MODEL)
    xs = x_sample.reshape(db, D_MODEL)
    mem2 = mem_prompt.reshape(batch * MEM_TOKENS, D_MODEL)
    mem_kv = [rms_matmul(mem2, g_mem[l], w_mem_kv, l, 512, 512) for l in range(DEPTH)]
    mem_cache = cache_mem_kv.reshape(-1, MEM_HEAD_DIM)
    cos_p, sin_p = _rope_tables(jnp.arange(seq, dtype=jnp.int32))
    cos_s, sin_s = _rope_tables(jnp.full((db,), PAST_LEN, jnp.int32))
    pools_p, pools_s = [], []
    for l in range(DEPTH):
        if l == N_A:
            cmp_kv, slc_kv, win_kv, slc_k, slc_vt, win_k, win_vt = kv_proj(xp, g_kv, w_kv, cos_p, sin_p, SLC_KT, True)
            table = jnp.arange(batch * seq // PAGE_SIZE, dtype=jnp.int32).reshape(batch, seq // PAGE_SIZE)
            ckv_p = compress(cmp_kv.reshape(-1, PAGE_SIZE, KV_SLOTS, HEAD_DIM), table,
                             jnp.zeros((batch, PAGE_SIZE, KV_SLOTS, HEAD_DIM), F32), cmp_pe, cmp_w1, cmp_w2)
            cmp_new, slc_new, win_new = kv_proj(xs, g_kv, w_kv, cos_s, sin_s, db, False)
            extra = jnp.pad(cmp_new.reshape(db, 1, KV_SLOTS, HEAD_DIM), ((0, 0), (0, PAGE_SIZE - 1), (0, 0), (0, 0)))
            ckv_s = compress(cache_cmp_kv.reshape(n_phys, PAGE_SIZE, KV_SLOTS, HEAD_DIM), page_table, extra,
                             cmp_pe, cmp_w1, cmp_w2)
            t_full = -(-(PAST_LEN + 1) // SLC_BLOCK) * SLC_BLOCK
            n_cmp = t_full // CMP_STRIDE - 1
            n_slc = t_full // SLC_BLOCK
        if l < N_A:
            up, us = rms_matmul(xp, g_mix[l], w_in_a, l, tm, 512, xs=xs)
            u3 = up.reshape(batch, seq, -1)
            zp = pool_prompt(u3, pool_grp_w[l], pool_scale[l]).reshape(m, MAIN_W)
            pools_p.append(u3[:, seq - POOL_BUF:, :MAIN_W])
            zs = pool_sample(us, jnp.transpose(state_pool[l], (1, 0, 2)), pool_grp_w[l], pool_scale[l])
            pools_s.append(jnp.concatenate([state_pool[l][:, 1:], us[:, None, :MAIN_W]], axis=1))
        else:
            j = l - N_A
            up, us = rms_matmul(xp, g_mix[l], w_in_b_pad, j, tm, 768, xs=xs)
            u3 = up.reshape(batch, seq, -1)
            bg = jnp.pad(b_gate[j], (0, GATE_PAD - GATE_W)).reshape(1, GATE_PAD)
            zp = nsa_prompt(up, bg, cos_p, sin_p, ckv_p, slc_k, slc_vt, win_k, win_vt, batch, seq)
            q16 = jnp.pad(us[:, :MAIN_W].reshape(db, N_HEADS, HEAD_DIM), ((0, 0), (0, HEAD_ROWS - N_HEADS), (0, 0)))
            q_rope, o_c, o_w, sel = nsa_sample_a(
                q16, cos_s[:1], sin_s[:1], ckv_s, cache_win_kv.reshape(db, -1, HEAD_DIM),
                win_new.reshape(db, 1, KV_W), n_cmp, n_slc)
            sel = jnp.transpose(sel[:, :, :N_KV], (0, 2, 1))
            gl = us[:, MAIN_W + MEM_W:MAIN_W + MEM_W + GATE_W].reshape(db, N_HEADS, N_BRANCH)
            gl = jnp.pad(gl, ((0, 0), (0, HEAD_ROWS - N_HEADS), (0, HEAD_DIM - N_BRANCH)))
            bg16 = jnp.pad(b_gate[j].reshape(N_HEADS, N_BRANCH), ((0, HEAD_ROWS - N_HEADS), (0, HEAD_DIM - N_BRANCH)))
            z16 = nsa_sample_b(sel, page_table, q_rope,
                               cache_slc_kv.reshape(n_phys * 2, SLC_BLOCK * KV_SLOTS, HEAD_DIM),
                               slc_new.reshape(db, KV_SLOTS, HEAD_DIM), o_c, o_w, gl, bg16)
            zs = z16[:, :N_HEADS].reshape(db, MAIN_W)
        mo_p = mem_attend(u3, mem_kv[l], 1024).reshape(m, MEM_W)
        mo_s = mem_attend(us.reshape(db, 1, -1), mem_cache, 1, layer=l).reshape(db, MEM_W)
        xp, xs = out_proj(zp, mo_p, w_out, l, xp, tm, 512, sample=(zs, mo_s, xs))
        xp, xs = ffn(xp, g_ffn[l], w_gu, w_down, l, g_final, l == DEPTH - 1, FFN_TM, FFN_TF, xs=xs)
    kv5 = (batch, seq, 2, N_KV, HEAD_DIM)
    kv5s = (db, 1, 2, N_KV, HEAD_DIM)
    n_keep = min(WINDOW, seq)
    win_s = jnp.concatenate([cache_win_kv, win_new.reshape(kv5s)], axis=1)
    win_s = win_s[:, win_s.shape[1] - min(WINDOW, win_s.shape[1]):]
    return (xp.reshape(batch, seq, D_MODEL), xs.reshape(db, 1, D_MODEL), jnp.stack(pools_p),
            cmp_kv.reshape(kv5), slc_kv.reshape(kv5), win_kv.reshape(kv5)[:, seq - n_keep:],
            jnp.stack(mem_kv).reshape(DEPTH, batch, MEM_TOKENS, 2, N_MEM_HEADS, MEM_HEAD_DIM),
            jnp.stack(pools_s), cmp_new.reshape(kv5s), slc_new.reshape(kv5s), win_s)
```

```python
import functools

import jax
import jax.numpy as jnp
from jax import lax
from jax.experimental import pallas as pl
from jax.experimental.pallas import tpu as pltpu

F32 = jnp.float32
BF16 = jnp.bfloat16

D_MODEL = 2048
DEPTH = 4
N_A = 2
PAST_LEN = 16384
PAGE_SIZE = 128
MEM_TOKENS = 256
N_MEM_HEADS = 4
MEM_HEAD_DIM = 128
MEM_W = 512
MAIN_W = 1536
POOL_WINDOWS = (2, 4, 8, 16)
POOL_GC = 384
POOL_BUF = 15
HEAD_DIM = 128
N_HEADS = 12
N_KV = 4
GQA = 3
N_BRANCH = 3
GATE_W = 36
GATE_PAD = 256
CMP_BLOCK = 32
CMP_STRIDE = 16
SLC_BLOCK = 64
SLC_SHIFT = 6
N_SELECT = 16
WINDOW = 512
Q_BLOCK = 128
D_FF = 5632
ROPE_THETA = 10000.0
EPS = 1e-6
NEG = -1e30
SCALE = HEAD_DIM ** -0.5
MEM_SCALE = MEM_HEAD_DIM ** -0.5
KV_SLOTS = 2 * N_KV
KV_W = KV_SLOTS * HEAD_DIM
W_IN_B = MAIN_W + MEM_W + GATE_PAD

VMEM_LIMIT = 56 * 1024 * 1024
SLC_KT = 512
LOG2E = 1.4426950408889634
PROJ_TM = 1024
FFN_TM = 1024
FFN_TF = 256
PAGES_PER_STEP = 16
CH_PER_PAGE = PAGE_SIZE // CMP_STRIDE
CMP_STEP = PAGES_PER_STEP * CH_PER_PAGE


def _cparams(*sem):
    return pltpu.CompilerParams(dimension_semantics=sem, vmem_limit_bytes=VMEM_LIMIT)


def _rms(x, g):
    return x * lax.rsqrt(jnp.mean(x * x, axis=-1, keepdims=True) + EPS) * g


def _dot(a, b):
    return jnp.dot(a.astype(BF16), b.astype(BF16), preferred_element_type=F32)


def _dot_t(a, b):
    return lax.dot_general(a.astype(BF16), b.astype(BF16), (((1,), (1,)), ((), ())),
                           preferred_element_type=F32)


def _lhs_dtype(rows):
    return BF16 if rows % 16 == 0 else F32


def _rope(x, cos2, sin2):
    return x * cos2 + pltpu.roll(x, HEAD_DIM // 2, axis=1) * sin2


def _rms_matmul_kernel(x_ref, g_ref, w_ref, o_ref, xn_ref, wb_ref, *, lead=0):
    i = pl.program_id(lead)
    j = pl.program_id(lead + 1)

    @pl.when(j == 0)
    def _():
        xn_ref[...] = _rms(x_ref[...], g_ref[...]).astype(xn_ref.dtype)

    @pl.when(i == 0)
    def _():
        wb_ref[j] = w_ref[...].astype(BF16)

    o_ref[...] = _dot(xn_ref[...], wb_ref[j])


def _rms_matmul2_kernel(x_ref, xs_ref, g_ref, w_ref, o_ref, os_ref, xn_ref, xsn_ref, wb_ref):
    _rms_matmul_kernel(x_ref, g_ref, w_ref, o_ref, xn_ref, wb_ref)
    i = pl.program_id(0)
    j = pl.program_id(1)

    @pl.when((i == 0) & (j == 0))
    def _():
        xsn_ref[...] = _rms(xs_ref[...], g_ref[...])

    @pl.when(i == 0)
    def _():
        os_ref[...] = _dot(xsn_ref[...], wb_ref[j])


def _once_per_column(nj):
    return lambda i, j: jnp.where(i == 0, j, nj - 1)


def rms_matmul(x, g, w, layer, tm, tn, xs=None):
    m, k = x.shape
    n = w.shape[2]
    nj = n // tn
    col = _once_per_column(nj)
    x_spec = pl.BlockSpec((tm, k), lambda i, j: (i, 0))
    g_spec = pl.BlockSpec((1, k), lambda i, j: (0, 0))
    w_spec = pl.BlockSpec((None, k, tn), lambda i, j: (layer, 0, col(i, j)))
    o_spec = pl.BlockSpec((tm, tn), lambda i, j: (i, j))
    scratch = [pltpu.VMEM((tm, k), _lhs_dtype(tm)), pltpu.VMEM((nj, k, tn), BF16)]
    if xs is None:
        return pl.pallas_call(
            _rms_matmul_kernel,
            grid=(m // tm, nj),
            in_specs=[x_spec, g_spec, w_spec],
            out_specs=o_spec,
            out_shape=jax.ShapeDtypeStruct((m, n), F32),
            scratch_shapes=scratch,
            compiler_params=_cparams("arbitrary", "arbitrary"),
            name="rms_matmul",
        )(x, g.reshape(1, k), w)
    ms = xs.shape[0]
    return pl.pallas_call(
        _rms_matmul2_kernel,
        grid=(m // tm, nj),
        in_specs=[x_spec, pl.BlockSpec((ms, k), lambda i, j: (0, 0)), g_spec, w_spec],
        out_specs=[o_spec, pl.BlockSpec((ms, tn), lambda i, j: (0, col(i, j)))],
        out_shape=[jax.ShapeDtypeStruct((m, n), F32), jax.ShapeDtypeStruct((ms, n), F32)],
        scratch_shapes=[scratch[0], pltpu.VMEM((ms, k), F32), scratch[1]],
        compiler_params=_cparams("arbitrary", "arbitrary"),
        name="rms_matmul2",
    )(x, xs, g.reshape(1, k), w)


def rms_matmul_layers(x, g, w, tm, tn):
    m, k = x.shape
    layers, _, n = w.shape
    nj = n // tn
    col = _once_per_column(nj)
    return pl.pallas_call(
        functools.partial(_rms_matmul_kernel, lead=1),
        grid=(layers, m // tm, nj),
        in_specs=[pl.BlockSpec((tm, k), lambda l, i, j: (i, 0)),
                  pl.BlockSpec((None, 1, k), lambda l, i, j: (l, 0, 0)),
                  pl.BlockSpec((None, k, tn), lambda l, i, j: (l, 0, col(i, j)))],
        out_specs=pl.BlockSpec((None, tm, tn), lambda l, i, j: (l, i, j)),
        out_shape=jax.ShapeDtypeStruct((layers, m, n), F32),
        scratch_shapes=[pltpu.VMEM((tm, k), _lhs_dtype(tm)), pltpu.VMEM((nj, k, tn), BF16)],
        compiler_params=_cparams("arbitrary", "arbitrary", "arbitrary"),
        name="rms_matmul_layers",
    )(x, g.reshape(layers, 1, k), w)


def _kv_proj_kernel(x_ref, g_ref, w_ref, cos_ref, sin_ref, cmp_ref, slc_ref, win_ref, *rest):
    xn_ref, wb_ref = rest[-2:]
    attn = rest[:-2]
    i = pl.program_id(0)
    j = pl.program_id(1)
    half = N_KV * HEAD_DIM

    @pl.when(j == 0)
    def _():
        xn_ref[...] = _rms(x_ref[...], g_ref[...]).astype(xn_ref.dtype)

    @pl.when(i == 0)
    def _():
        wb_ref[j] = w_ref[...].astype(BF16)

    for br, ref in enumerate((cmp_ref, slc_ref, win_ref)):
        for c in range(2):
            @pl.when(j == 2 * br + c)
            def _(ref=ref, br=br, c=c):
                y = _dot(xn_ref[...], wb_ref[2 * br + c])
                val = y
                if c == 0:
                    cos2 = cos_ref[...]
                    sin2 = sin_ref[...]
                    val = jnp.concatenate(
                        [_rope(y[:, h * HEAD_DIM:(h + 1) * HEAD_DIM], cos2, sin2) for h in range(N_KV)], axis=1)
                for h in range(N_KV):
                    ref[:, c * N_KV + h, :] = val[:, h * HEAD_DIM:(h + 1) * HEAD_DIM]
                if attn and br == 1 and c == 1:
                    attn[1][0] = jnp.transpose(val).astype(BF16)
                elif attn and br == 2 and c == 1:
                    for t in range(val.shape[0] // Q_BLOCK):
                        attn[3][t] = jnp.transpose(val[t * Q_BLOCK:(t + 1) * Q_BLOCK]).astype(BF16)
                elif attn and br > 0:
                    attn[2 * (br - 1) + c][...] = val.astype(BF16)


def kv_proj(x, g, w, cos2, sin2, tm, attn_copies):
    m, k = x.shape
    half = N_KV * HEAD_DIM
    nt = cos2.shape[0] // tm
    nj = 2 * N_BRANCH
    col = _once_per_column(nj)
    out = jax.ShapeDtypeStruct((m, KV_SLOTS, HEAD_DIM), F32)
    ospec = pl.BlockSpec((tm, KV_SLOTS, HEAD_DIM), lambda i, j: (i, 0, 0))
    out_shape, out_specs = [out, out, out], [ospec, ospec, ospec]
    if attn_copies:
        out16 = jax.ShapeDtypeStruct((m, half), BF16)
        ospec16 = pl.BlockSpec((tm, half), lambda i, j: (i, 0))
        out_shape += [out16, jax.ShapeDtypeStruct((m // tm, half, tm), BF16), out16,
                      jax.ShapeDtypeStruct((m // Q_BLOCK, half, Q_BLOCK), BF16)]
        out_specs += [ospec16, pl.BlockSpec((1, half, tm), lambda i, j: (i, 0, 0)), ospec16,
                      pl.BlockSpec((tm // Q_BLOCK, half, Q_BLOCK), lambda i, j: (i, 0, 0))]
    return pl.pallas_call(
        _kv_proj_kernel,
        grid=(m // tm, nj),
        in_specs=[pl.BlockSpec((tm, k), lambda i, j: (i, 0)),
                  pl.BlockSpec((1, k), lambda i, j: (0, 0)),
                  pl.BlockSpec((k, half), lambda i, j: (0, col(i, j))),
                  pl.BlockSpec((tm, HEAD_DIM), lambda i, j: (i % nt, 0)),
                  pl.BlockSpec((tm, HEAD_DIM), lambda i, j: (i % nt, 0))],
        out_specs=out_specs,
        out_shape=out_shape,
        scratch_shapes=[pltpu.VMEM((tm, k), _lhs_dtype(tm)), pltpu.VMEM((nj, k, half), BF16)],
        compiler_params=_cparams("arbitrary", "arbitrary"),
        name="kv_proj",
    )(x, g.reshape(1, k), w, cos2, sin2)


def _pool_prompt_kernel(cur_ref, prev_ref, w_ref, sc_ref, o_ref, *, ts):
    i = pl.program_id(1)
    keep = (i > 0).astype(F32)
    pos = (i * ts + lax.broadcasted_iota(jnp.int32, (ts, 1), 0) + 1).astype(F32)
    diffs = []
    for g, w in enumerate(POOL_WINDOWS):
        sl = slice(g * POOL_GC, (g + 1) * POOL_GC)
        x = cur_ref[0, :, sl]
        ext = jnp.concatenate([prev_ref[0, :, sl] * keep, x], axis=0)
        span = 1
        while span < w:
            ext = ext[span:] + ext[:-span]
            span *= 2
        win = ext[17 - w:17 - w + ts]
        diffs.append((win / jnp.minimum(pos, float(w)) - x).astype(BF16))
    for g in range(len(POOL_WINDOWS)):
        sl = slice(g * POOL_GC, (g + 1) * POOL_GC)
        o_ref[0, :, sl] = _dot(diffs[g], w_ref[g]) * sc_ref[:, sl]


def pool_prompt(u3, w_grp, scale, ts=512):
    b, s, n = u3.shape
    r = ts // 16
    return pl.pallas_call(
        functools.partial(_pool_prompt_kernel, ts=ts),
        grid=(b, s // ts),
        in_specs=[pl.BlockSpec((1, ts, MAIN_W), lambda bi, i: (bi, i, 0)),
                  pl.BlockSpec((1, 16, MAIN_W), lambda bi, i: (bi, jnp.maximum(i * r - 1, 0), 0)),
                  pl.BlockSpec((4, POOL_GC, POOL_GC), lambda bi, i: (0, 0, 0)),
                  pl.BlockSpec((1, MAIN_W), lambda bi, i: (0, 0))],
        out_specs=pl.BlockSpec((1, ts, MAIN_W), lambda bi, i: (bi, i, 0)),
        out_shape=jax.ShapeDtypeStruct((b, s, MAIN_W), F32),
        compiler_params=_cparams("parallel", "arbitrary"),
        name="pool_prompt",
    )(u3, u3, w_grp, scale.reshape(1, MAIN_W))


def _pool_sample_kernel(cur_ref, buf_ref, w_ref, sc_ref, o_ref):
    for g, w in enumerate(POOL_WINDOWS):
        sl = slice(g * POOL_GC, (g + 1) * POOL_GC)
        x = cur_ref[:, sl]
        win = x
        for r in range(POOL_BUF - (w - 1), POOL_BUF):
            win = win + buf_ref[r, :, sl]
        d = win / float(w) - x
        o_ref[:, sl] = _dot(d, w_ref[g]) * sc_ref[:, sl]


def pool_sample(u, buf_t, w_grp, scale):
    db = u.shape[0]
    return pl.pallas_call(
        _pool_sample_kernel,
        grid=(1,),
        in_specs=[pl.BlockSpec((db, MAIN_W), lambda i: (0, 0)),
                  pl.BlockSpec((POOL_BUF, db, MAIN_W), lambda i: (0, 0, 0)),
                  pl.BlockSpec((4, POOL_GC, POOL_GC), lambda i: (0, 0, 0)),
                  pl.BlockSpec((1, MAIN_W), lambda i: (0, 0))],
        out_specs=pl.BlockSpec((db, MAIN_W), lambda i: (0, 0)),
        out_shape=jax.ShapeDtypeStruct((db, MAIN_W), F32),
        compiler_params=_cparams("arbitrary"),
        name="pool_sample",
    )(u, buf_t, w_grp, scale.reshape(1, MAIN_W))


def _mem_attend_kernel(q_ref, kv_ref, o_ref, *, tq, slots):
    rows = max(tq, 8)
    logits, values = [], []
    for h in range(N_MEM_HEADS):
        sl = slice(h * MEM_HEAD_DIM, (h + 1) * MEM_HEAD_DIM)
        q = q_ref[0, :, sl]
        if tq < rows:
            q = jnp.broadcast_to(q[0:1], (rows, MEM_HEAD_DIM))
        if slots:
            k = kv_ref[pl.ds(h, MEM_TOKENS, stride=2 * N_MEM_HEADS), :]
            v = kv_ref[pl.ds(N_MEM_HEADS + h, MEM_TOKENS, stride=2 * N_MEM_HEADS), :]
        else:
            k = kv_ref[:, sl]
            v = kv_ref[:, MEM_W + h * MEM_HEAD_DIM:MEM_W + (h + 1) * MEM_HEAD_DIM]
        logits.append(_dot_t(q, k) * MEM_SCALE)
        values.append(v)
    for h in range(N_MEM_HEADS):
        sl = slice(h * MEM_HEAD_DIM, (h + 1) * MEM_HEAD_DIM)
        s, v = logits[h], values[h]
        e = jnp.exp(s - jnp.max(s, axis=-1, keepdims=True))
        o = _dot(e, v) / jnp.sum(e, axis=-1, keepdims=True)
        o_ref[0, :, sl] = o[0:tq]


def mem_attend(u3, mkv, tq, layer, slots):
    b, s, _ = u3.shape
    if slots:
        kv_spec = pl.BlockSpec((MEM_TOKENS * 2 * N_MEM_HEADS, MEM_HEAD_DIM), lambda bi, i: (layer * b + bi, 0))
    else:
        kv_spec = pl.BlockSpec((MEM_TOKENS, 2 * MEM_W), lambda bi, i: (layer * b + bi, 0))
    return pl.pallas_call(
        functools.partial(_mem_attend_kernel, tq=tq, slots=slots),
        grid=(b, s // tq),
        in_specs=[pl.BlockSpec((1, tq, MEM_W), lambda bi, i: (bi, i, MAIN_W // MEM_W)), kv_spec],
        out_specs=pl.BlockSpec((1, tq, MEM_W), lambda bi, i: (bi, i, 0)),
        out_shape=jax.ShapeDtypeStruct((b, s, MEM_W), F32),
        compiler_params=_cparams("parallel", "arbitrary"),
        name="mem_attend",
    )(u3, mkv)


def _out_proj_kernel(z_ref, mo_ref, w_ref, x_ref, o_ref, lhs_ref, wb_ref):
    i = pl.program_id(0)
    j = pl.program_id(1)

    @pl.when(j == 0)
    def _():
        lhs_ref[:, :MAIN_W] = z_ref[...].astype(lhs_ref.dtype)
        lhs_ref[:, MAIN_W:] = mo_ref[...].astype(lhs_ref.dtype)

    @pl.when(i == 0)
    def _():
        wb_ref[j] = w_ref[...].astype(BF16)

    o_ref[...] = x_ref[...] + _dot(lhs_ref[...], wb_ref[j])


def _out_proj2_kernel(z_ref, mo_ref, w_ref, x_ref, zs_ref, mos_ref, xs_ref, o_ref, os_ref, lhs_ref, wb_ref):
    _out_proj_kernel(z_ref, mo_ref, w_ref, x_ref, o_ref, lhs_ref, wb_ref)
    j = pl.program_id(1)

    @pl.when(pl.program_id(0) == 0)
    def _():
        lhs = jnp.concatenate([zs_ref[...], mos_ref[...]], axis=1)
        os_ref[...] = xs_ref[...] + _dot(lhs, wb_ref[j])


def out_proj(z, mo, w, layer, x, tm, tn, sample=None):
    m = x.shape[0]
    k = MAIN_W + MEM_W
    nj = D_MODEL // tn
    col = _once_per_column(nj)
    in_specs = [pl.BlockSpec((tm, MAIN_W), lambda i, j: (i, 0)),
                pl.BlockSpec((tm, MEM_W), lambda i, j: (i, 0)),
                pl.BlockSpec((None, k, tn), lambda i, j: (layer, 0, col(i, j))),
                pl.BlockSpec((tm, tn), lambda i, j: (i, j))]
    o_spec = pl.BlockSpec((tm, tn), lambda i, j: (i, j))
    scratch = [pltpu.VMEM((tm, k), _lhs_dtype(tm)), pltpu.VMEM((nj, k, tn), BF16)]
    if sample is None:
        return pl.pallas_call(
            _out_proj_kernel,
            grid=(m // tm, nj),
            in_specs=in_specs,
            out_specs=o_spec,
            out_shape=jax.ShapeDtypeStruct((m, D_MODEL), F32),
            scratch_shapes=scratch,
            compiler_params=_cparams("arbitrary", "arbitrary"),
            name="out_proj",
        )(z, mo, w, x)
    zs, mos, xs = sample
    ms = xs.shape[0]
    s_spec = pl.BlockSpec((ms, tn), lambda i, j: (0, col(i, j)))
    return pl.pallas_call(
        _out_proj2_kernel,
        grid=(m // tm, nj),
        in_specs=in_specs + [pl.BlockSpec((ms, MAIN_W), lambda i, j: (0, 0)),
                             pl.BlockSpec((ms, MEM_W), lambda i, j: (0, 0)), s_spec],
        out_specs=[o_spec, s_spec],
        out_shape=[jax.ShapeDtypeStruct((m, D_MODEL), F32), jax.ShapeDtypeStruct((ms, D_MODEL), F32)],
        scratch_shapes=scratch,
        compiler_params=_cparams("arbitrary", "arbitrary"),
        name="out_proj2",
    )(z, mo, w, x, zs, mos, xs)


def _ffn_kernel(x_ref, g_ref, wa_ref, wb_ref, wd_ref, gf_ref, o_ref, xn_ref, *, final):
    j = pl.program_id(1)

    @pl.when(j == 0)
    def _():
        x = x_ref[...]
        xn_ref[...] = _rms(x, g_ref[...]).astype(xn_ref.dtype)
        o_ref[...] = x

    xn = xn_ref[...]
    a = _dot(xn, wa_ref[...])
    b = _dot(xn, wb_ref[...])
    o_ref[...] += _dot(jax.nn.silu(a) * b, wd_ref[...])

    if final:
        @pl.when(j == pl.num_programs(1) - 1)
        def _():
            o_ref[...] = _rms(o_ref[...], gf_ref[...])


def _ffn2_kernel(x_ref, g_ref, wa_ref, wb_ref, wd_ref, gf_ref, xs_ref, o_ref, os_ref, xn_ref, xsn_ref, *, final):
    _ffn_kernel(x_ref, g_ref, wa_ref, wb_ref, wd_ref, gf_ref, o_ref, xn_ref, final=final)
    j = pl.program_id(1)

    @pl.when(pl.program_id(0) == 0)
    def _():
        @pl.when(j == 0)
        def _():
            xs = xs_ref[...]
            xsn_ref[...] = _rms(xs, g_ref[...])
            os_ref[...] = xs

        xsn = xsn_ref[...]
        a = _dot(xsn, wa_ref[...])
        b = _dot(xsn, wb_ref[...])
        os_ref[...] += _dot(jax.nn.silu(a) * b, wd_ref[...])

        if final:
            @pl.when(j == pl.num_programs(1) - 1)
            def _():
                os_ref[...] = _rms(os_ref[...], gf_ref[...])


def ffn(x, g, w_gu, w_down, layer, g_final, final, tm, tf, xs=None):
    m, k = x.shape
    nf = D_FF // tf
    in_specs = [pl.BlockSpec((tm, k), lambda i, j: (i, 0), pipeline_mode=pl.Buffered(1)),
                pl.BlockSpec((1, k), lambda i, j: (0, 0)),
                pl.BlockSpec((None, k, tf), lambda i, j: (layer, 0, j)),
                pl.BlockSpec((None, k, tf), lambda i, j: (layer, 0, nf + j)),
                pl.BlockSpec((None, tf, k), lambda i, j: (layer, j, 0)),
                pl.BlockSpec((1, k), lambda i, j: (0, 0))]
    o_spec = pl.BlockSpec((tm, k), lambda i, j: (i, 0))
    scratch = [pltpu.VMEM((tm, k), _lhs_dtype(tm))]
    args = (x, g.reshape(1, k), w_gu, w_gu, w_down, g_final.reshape(1, k))
    if xs is None:
        return pl.pallas_call(
            functools.partial(_ffn_kernel, final=final),
            grid=(m // tm, nf),
            in_specs=in_specs,
            out_specs=o_spec,
            out_shape=jax.ShapeDtypeStruct((m, k), F32),
            scratch_shapes=scratch,
            compiler_params=_cparams("parallel", "arbitrary"),
            name="ffn",
        )(*args)
    ms = xs.shape[0]
    s_spec = pl.BlockSpec((ms, k), lambda i, j: (0, 0))
    return pl.pallas_call(
        functools.partial(_ffn2_kernel, final=final),
        grid=(m // tm, nf),
        in_specs=in_specs + [s_spec],
        out_specs=[o_spec, s_spec],
        out_shape=[jax.ShapeDtypeStruct((m, k), F32), jax.ShapeDtypeStruct((ms, k), F32)],
        scratch_shapes=scratch + [pltpu.VMEM((ms, k), F32)],
        compiler_params=_cparams("arbitrary", "arbitrary"),
        name="ffn2",
    )(*args, xs)


def _compress_kernel(tbl_ref, *refs, n_steps):
    pages = refs[:PAGES_PER_STEP]
    nxt_ref, extra_ref, w1_ref, pe_ref, w2_ref, o_ref, w1b_ref, lhs_ref = refs[PAGES_PER_STEP:]
    p = pl.program_id(1)
    on_extra = p == n_steps
    nxt_extra = p == n_steps - 1
    nh = CMP_STEP * N_KV
    width = CMP_STRIDE * HEAD_DIM
    d = HEAD_DIM
    group = 4
    n_groups = CMP_STEP // group + 1
    n_rows = n_groups * group * N_KV

    @pl.when((pl.program_id(0) == 0) & (p == 0))
    def _():
        for c in range(2):
            w1b_ref[c] = jnp.concatenate([w1_ref[c, 0], w1_ref[c, 1]], axis=1).astype(BF16)

    def slots(n, r):
        if n < CMP_STEP:
            k, i = divmod(n, CH_PER_PAGE)
            x = pages[k][0, i * CMP_STRIDE + r]
            return jnp.where(on_extra, extra_ref[0, i * CMP_STRIDE + r] if k == 0 else 0.0, x)
        if n == CMP_STEP:
            return jnp.where(on_extra, 0.0, jnp.where(nxt_extra, extra_ref[0, r], nxt_ref[0, r]))
        return jnp.zeros((KV_SLOTS, HEAD_DIM), F32)

    for q in range(n_groups):
        for r in range(CMP_STRIDE):
            tiles = [slots(group * q + i, r) for i in range(group)]
            for c in range(2):
                piece = jnp.concatenate([t[c * N_KV:(c + 1) * N_KV] for t in tiles], axis=0)
                lhs_ref[c, group * N_KV * q:group * N_KV * (q + 1), r * HEAD_DIM:(r + 1) * HEAD_DIM] = piece.astype(BF16)
    for c in range(2):
        tail = jnp.concatenate([pe_ref[c, 0], pe_ref[c, 1], jnp.zeros((14, width), F32)], axis=0)
        lhs_ref[c, n_rows:n_rows + 16, :] = tail.astype(BF16)
    hc = [_dot(lhs_ref[c], w1b_ref[c]) for c in range(2)]
    for c in range(2):
        pe_term = hc[c][n_rows:n_rows + 1, :d] + hc[c][n_rows + 1:n_rows + 2, d:]
        hid = jax.nn.gelu(hc[c][:nh, :d] + hc[c][N_KV:nh + N_KV, d:] + pe_term)
        o_ref[0, c] = _dot(hid, w2_ref[c])


def compress(pages, table, extra, cmp_pe, cmp_w1, cmp_w2):
    nb, n_pages = table.shape
    n_steps = n_pages // PAGES_PER_STEP
    w1 = cmp_w1.reshape(2, 2, CMP_STRIDE * HEAD_DIM, HEAD_DIM)
    pe = cmp_pe.reshape(2, 2, 1, CMP_STRIDE * HEAD_DIM)

    def page_map(k):
        return lambda b, p, tbl: (tbl[b, jnp.minimum(p * PAGES_PER_STEP + k, n_pages - 1)], 0, 0, 0)

    in_specs = [pl.BlockSpec((1, PAGE_SIZE, KV_SLOTS, HEAD_DIM), page_map(k)) for k in range(PAGES_PER_STEP)]
    in_specs += [pl.BlockSpec((1, CMP_STRIDE, KV_SLOTS, HEAD_DIM), page_map(PAGES_PER_STEP)),
                 pl.BlockSpec((1, PAGE_SIZE, KV_SLOTS, HEAD_DIM), lambda b, p, tbl: (b, 0, 0, 0)),
                 pl.BlockSpec(w1.shape, lambda b, p, tbl: (0, 0, 0, 0)),
                 pl.BlockSpec(pe.shape, lambda b, p, tbl: (0, 0, 0, 0)),
                 pl.BlockSpec(cmp_w2.shape, lambda b, p, tbl: (0, 0, 0))]
    n_out = CMP_STEP * (n_steps + 1)
    out = pl.pallas_call(
        functools.partial(_compress_kernel, n_steps=n_steps),
        grid_spec=pltpu.PrefetchScalarGridSpec(
            num_scalar_prefetch=1,
            grid=(nb, n_steps + 1),
            in_specs=in_specs,
            out_specs=pl.BlockSpec((1, 2, CMP_STEP * N_KV, HEAD_DIM), lambda b, p, tbl: (b, 0, p, 0)),
            scratch_shapes=[pltpu.VMEM((2, CMP_STRIDE * HEAD_DIM, 2 * HEAD_DIM), BF16),
                            pltpu.VMEM((2, (CMP_STEP + 8) * N_KV, CMP_STRIDE * HEAD_DIM), BF16)],
        ),
        out_shape=jax.ShapeDtypeStruct((nb, 2, n_out * N_KV, HEAD_DIM), F32),
        compiler_params=_cparams("arbitrary", "arbitrary"),
        name="compress",
    )(table, *([pages] * (PAGES_PER_STEP + 1)), extra, w1, pe, cmp_w2)
    return jnp.transpose(out.reshape(nb, 2, n_out, N_KV, HEAD_DIM), (0, 1, 3, 2, 4))


def _cover(n_pad, j_pad, n_cmp):
    n = lax.broadcasted_iota(jnp.int32, (n_pad, j_pad), 0)
    j = lax.broadcasted_iota(jnp.int32, (n_pad, j_pad), 1)
    hit = (n * CMP_STRIDE < j * SLC_BLOCK + SLC_BLOCK) & (n * CMP_STRIDE + CMP_BLOCK - 1 >= j * SLC_BLOCK)
    return (hit & (n < n_cmp)).astype(F32)


def _cover_t(j_pad, n_pad, n_cmp):
    j = lax.broadcasted_iota(jnp.int32, (j_pad, n_pad), 0)
    n = lax.broadcasted_iota(jnp.int32, (j_pad, n_pad), 1)
    hit = (n * CMP_STRIDE < j * SLC_BLOCK + SLC_BLOCK) & (n * CMP_STRIDE + CMP_BLOCK - 1 >= j * SLC_BLOCK)
    return (hit & (n < n_cmp)).astype(F32)


def _softmax_rows(s):
    e = jnp.exp(s - jnp.max(s, axis=-1, keepdims=True))
    return e / jnp.sum(e, axis=-1, keepdims=True)


def _nsa_prompt_kernel(q_ref, gl_ref, bg_ref, cos_ref, sin_ref, ck_ref, cvt_ref, ks_ref, vt_ref, kw_ref, vwt_ref,
                       o_ref, qt_ref, drop_ref, z_ref, *, n_cmp, n_slc, kt):
    qb = pl.program_id(1)
    tq = Q_BLOCK
    cos2 = cos_ref[...]
    sin2 = sin_ref[...]
    n_pad = ck_ref.shape[2]
    span = WINDOW + tq
    qpos_row = qb * tq + lax.broadcasted_iota(jnp.int32, (1, tq), 1)
    qpos = jnp.concatenate([qpos_row] * GQA, axis=1)
    gates_t = jnp.transpose(jax.nn.sigmoid(gl_ref[...] + bg_ref[...]))

    def gate_row(hd, br):
        return gates_t[hd * N_BRANCH + br:hd * N_BRANCH + br + 1, :]

    n_col = lax.broadcasted_iota(jnp.int32, (n_pad, 1), 0)
    bias_c = jnp.where((n_col * CMP_STRIDE + CMP_BLOCK - 1 <= qpos) & (n_col < n_cmp), 0.0, NEG)
    any_c = (qpos >= CMP_BLOCK - 1).astype(F32)
    cover_t = _cover_t(n_slc, n_pad, n_cmp)
    j_col = lax.broadcasted_iota(jnp.int32, (n_slc, 1), 0)
    qblk = jnp.right_shift(qpos_row, SLC_SHIFT)
    forced = (j_col == 0) | (j_col == qblk) | (j_col == qblk - 1)
    valid = j_col <= qblk
    n_sel = min(N_SELECT, n_slc)

    w_lo = jnp.maximum(qb * tq - WINDOW, 0)
    w_lo = pl.multiple_of(w_lo, tq)
    kpos_w = w_lo + lax.broadcasted_iota(jnp.int32, (span, 1), 0)
    bias_w = jnp.where((kpos_w <= qpos) & (qpos - kpos_w < WINDOW), 0.0, NEG)

    heads = range(N_KV)
    s_cmp, s_win = [], []
    for h in heads:
        q = jnp.concatenate(
            [_rope(q_ref[:, (h * GQA + g) * HEAD_DIM:(h * GQA + g + 1) * HEAD_DIM], cos2, sin2)
             for g in range(GQA)], axis=0)
        qt = jnp.transpose(q * (SCALE * LOG2E)).astype(BF16)
        qt_ref[h] = qt
        hs = slice(h * HEAD_DIM, (h + 1) * HEAD_DIM)
        s_cmp.append(_dot(ck_ref[0, h], qt) + bias_c)
        s_win.append(_dot(kw_ref[pl.ds(w_lo, span), hs], qt) + bias_w)

    o_cmp, scores = [], []
    for h in heads:
        e = jnp.exp2(s_cmp[h] - jnp.max(s_cmp[h], axis=0, keepdims=True))
        p = e / jnp.sum(e, axis=0, keepdims=True) * any_c
        o_cmp.append(_dot(cvt_ref[0, h], p))
        p_sum = p[:, 0:tq] + p[:, tq:2 * tq] + p[:, 2 * tq:3 * tq]
        score = jnp.dot(cover_t, p_sum, preferred_element_type=F32,
                        precision=lax.Precision.HIGHEST)
        scores.append(jnp.where(forced, jnp.inf, jnp.where(valid, score, -jnp.inf)))

    for h in heads:
        hs = slice(h * HEAD_DIM, (h + 1) * HEAD_DIM)
        e = jnp.exp2(s_win[h] - jnp.max(s_win[h], axis=0, keepdims=True))
        o_w = _dot(vwt_ref[w_lo // tq, hs, :], e[0:tq])
        for i in range(1, span // tq):
            o_w = o_w + _dot(vwt_ref[w_lo // tq + i, hs, :], e[i * tq:(i + 1) * tq])
        o_w = o_w / jnp.sum(e, axis=0, keepdims=True)
        for g in range(GQA):
            hd = h * GQA + g
            r = slice(g * tq, (g + 1) * tq)
            z_ref[hd] = gate_row(hd, 0) * o_cmp[h][:, r] + gate_row(hd, 2) * o_w[:, r]

    for h in heads:
        score = scores[h]
        rank = jnp.zeros((n_slc, tq), F32)
        for i in range(n_slc):
            row = score[i:i + 1, :]
            beats = (row > score) | ((row == score) & (i < j_col))
            rank = rank + jnp.where(beats, 1.0, 0.0)
        drop_ref[h] = jnp.where(rank >= n_sel, NEG, 0.0)

    blocks_per_tile = kt // SLC_BLOCK

    def slc_step(t, carry, nk=kt):
        k0 = pl.multiple_of(t * kt, kt)
        kpos = k0 + lax.broadcasted_iota(jnp.int32, (nk, 1), 0)
        causal = jnp.where(kpos <= qpos_row, 0.0, NEG)
        def logits(h):
            hs = slice(h * HEAD_DIM, (h + 1) * HEAD_DIM)
            dropped = jnp.concatenate(
                [jnp.broadcast_to(drop_ref[h, pl.ds(t * blocks_per_tile + jb, 1), :], (SLC_BLOCK, tq))
                 for jb in range(nk // SLC_BLOCK)], axis=0)
            bias = dropped + causal
            return _dot(ks_ref[pl.ds(k0, nk), hs], qt_ref[h]) + jnp.concatenate([bias] * GQA, axis=1)

        scores = [logits(h) for h in range(N_KV)]
        out = []
        for h in range(N_KV):
            m, l, acc = carry[h]
            hs = slice(h * HEAD_DIM, (h + 1) * HEAD_DIM)
            s = scores[h]
            m_new = jnp.maximum(m, jnp.max(s, axis=0, keepdims=True))
            a = jnp.exp2(m - m_new)
            e = jnp.exp2(s - m_new)
            l = l * a + jnp.sum(e, axis=0, keepdims=True)
            acc = acc * a + _dot(vt_ref[t, hs, :nk], e)
            out.append((m_new, l, acc))
        return tuple(out)

    q_end = qb * tq + tq
    n_full = q_end // kt
    init = (jnp.full((1, GQA * tq), NEG, F32), jnp.zeros((1, GQA * tq), F32), jnp.zeros((HEAD_DIM, GQA * tq), F32))
    done = lax.fori_loop(0, n_full, slc_step, (init,) * N_KV)
    tails = [lambda c: c] + [functools.partial(lambda c, nk: slc_step(n_full, c, nk), nk=i * tq)
                             for i in range(1, kt // tq)]
    done = lax.switch((q_end % kt) // tq, tails, done)
    for h in range(N_KV):
        _, l, acc = done[h]
        o_s = acc / l
        for g in range(GQA):
            hd = h * GQA + g
            r = slice(g * tq, (g + 1) * tq)
            o_ref[:, hd * HEAD_DIM:(hd + 1) * HEAD_DIM] = jnp.transpose(z_ref[hd] + gate_row(hd, 1) * o_s[:, r])


def nsa_prompt(u, b_gate_pad, cos2, sin2, ckv, slc_k, slc_vt, win_k, win_vt, batch, seq):
    nqb = seq // Q_BLOCK
    n_pad = ckv.shape[3]
    half = N_KV * HEAD_DIM
    cmp_k = ckv[:, 0]
    cmp_vt = jnp.transpose(ckv[:, 1], (0, 1, 3, 2))
    kspec = pl.BlockSpec((seq, half), lambda b, i: (b, 0))
    vt_spec = pl.BlockSpec((seq // SLC_KT, half, SLC_KT), lambda b, i: (b, 0, 0))
    wvt_spec = pl.BlockSpec((seq // Q_BLOCK, half, Q_BLOCK), lambda b, i: (b, 0, 0))
    return pl.pallas_call(
        functools.partial(_nsa_prompt_kernel, n_cmp=seq // CMP_STRIDE - 1, n_slc=seq // SLC_BLOCK, kt=SLC_KT),
        grid=(batch, nqb),
        in_specs=[pl.BlockSpec((Q_BLOCK, MAIN_W), lambda b, i: (b * nqb + i, 0)),
                  pl.BlockSpec((Q_BLOCK, GATE_PAD), lambda b, i: (b * nqb + i, (MAIN_W + MEM_W) // GATE_PAD)),
                  pl.BlockSpec((1, GATE_PAD), lambda b, i: (0, 0)),
                  pl.BlockSpec((Q_BLOCK, HEAD_DIM), lambda b, i: (i, 0)),
                  pl.BlockSpec((Q_BLOCK, HEAD_DIM), lambda b, i: (i, 0)),
                  pl.BlockSpec((1, N_KV, n_pad, HEAD_DIM), lambda b, i: (b, 0, 0, 0)),
                  pl.BlockSpec((1, N_KV, HEAD_DIM, n_pad), lambda b, i: (b, 0, 0, 0)),
                  kspec, vt_spec, kspec, wvt_spec],
        out_specs=pl.BlockSpec((Q_BLOCK, MAIN_W), lambda b, i: (b * nqb + i, 0)),
        out_shape=jax.ShapeDtypeStruct((batch * seq, MAIN_W), F32),
        scratch_shapes=[pltpu.VMEM((N_KV, HEAD_DIM, GQA * Q_BLOCK), BF16),
                        pltpu.VMEM((N_KV, seq // SLC_BLOCK, Q_BLOCK), F32),
                        pltpu.VMEM((N_HEADS, HEAD_DIM, Q_BLOCK), F32)],
        compiler_params=_cparams("parallel", "arbitrary"),
        name="nsa_prompt",
    )(u, u, b_gate_pad, cos2, sin2, cmp_k, cmp_vt, slc_k, slc_vt, win_k, win_vt)


HEAD_ROWS = 16


def _head_mask(h):
    r = lax.broadcasted_iota(jnp.int32, (HEAD_ROWS, 1), 0)
    return (r >= h * GQA) & (r < (h + 1) * GQA)


def _nsa_sample_a_kernel(q_ref, cos_ref, sin_ref, ckv_ref, wkv_ref, wnew_ref, qr_ref, oc_ref, ow_ref, sel_ref,
                         *, n_cmp, n_slc, j_pad):
    n_pad = ckv_ref.shape[3]
    q = _rope(q_ref[0], cos_ref[...], sin_ref[...])
    qr_ref[0] = q
    qs = q * SCALE
    n_i = lax.broadcasted_iota(jnp.int32, (1, n_pad), 1)
    vis_c = (n_i * CMP_STRIDE + CMP_BLOCK - 1 <= PAST_LEN) & (n_i < n_cmp)
    cover = _cover(n_pad, j_pad, n_cmp)
    j_row = lax.broadcasted_iota(jnp.int32, (1, j_pad), 1)
    j_col = lax.broadcasted_iota(jnp.int32, (j_pad, 1), 0)
    qblk = PAST_LEN // SLC_BLOCK
    lane = lax.broadcasted_iota(jnp.int32, (N_SELECT, HEAD_DIM), 1)
    r_col = lax.broadcasted_iota(jnp.int32, (N_SELECT, 1), 0)
    n_win = wkv_ref.shape[1] // KV_SLOTS
    kpos_w = PAST_LEN - n_win + lax.broadcasted_iota(jnp.int32, (1, n_win), 1)
    vis_w = (kpos_w <= PAST_LEN) & (PAST_LEN - kpos_w < WINDOW)

    heads = range(N_KV)
    forced = (j_row == 0) | (j_row == qblk) | (j_row == qblk - 1)
    s_cmp = [jnp.where(vis_c, _dot_t(qs, ckv_ref[0, 0, h]), NEG) for h in heads]
    s_win = [jnp.where(vis_w, _dot_t(qs, wkv_ref[0, pl.ds(h, n_win, stride=KV_SLOTS), :]), NEG) for h in heads]

    o_c = jnp.zeros((HEAD_ROWS, HEAD_DIM), F32)
    scores = []
    for h in heads:
        mine = _head_mask(h)
        p = _softmax_rows(s_cmp[h]) * float(PAST_LEN >= CMP_BLOCK - 1)
        o_c = jnp.where(mine, _dot(p, ckv_ref[0, 1, h]), o_c)
        p_sum = jnp.sum(jnp.where(mine, p, 0.0), axis=0, keepdims=True)
        score = jnp.dot(jnp.broadcast_to(p_sum, (8, n_pad)), cover, preferred_element_type=F32,
                        precision=lax.Precision.HIGHEST)
        score = jnp.where(forced, jnp.inf, jnp.where(j_row <= qblk, score, -jnp.inf))
        scores.append(jnp.where(j_row < n_slc, score, -jnp.inf))
    oc_ref[0] = o_c

    o_w = jnp.zeros((HEAD_ROWS, HEAD_DIM), F32)
    for h in heads:
        hs = slice(h * HEAD_DIM, (h + 1) * HEAD_DIM)
        vsl = slice((N_KV + h) * HEAD_DIM, (N_KV + h + 1) * HEAD_DIM)
        v_win = wkv_ref[0, pl.ds(N_KV + h, n_win, stride=KV_SLOTS), :]
        s = s_win[h]
        s_new = jnp.sum(qs * wnew_ref[0, :, hs], axis=-1, keepdims=True)
        m = jnp.maximum(jnp.max(s, axis=-1, keepdims=True), s_new)
        e = jnp.exp(s - m)
        e_new = jnp.exp(s_new - m)
        o = (_dot(e, v_win) + e_new * wnew_ref[0, :, vsl]) / (jnp.sum(e, axis=-1, keepdims=True) + e_new)
        o_w = jnp.where(_head_mask(h), o, o_w)
    ow_ref[0] = o_w

    sel_out = jnp.zeros((N_SELECT, HEAD_DIM), jnp.int32)
    for h in heads:
        score = scores[h]
        s_col = jnp.transpose(score)[:, 0:1]
        beats = ((s_col > score[0:1]) | ((s_col == score[0:1]) & (j_col < j_row))) & (j_col < n_slc)
        rank = jnp.sum(beats.astype(F32), axis=0, keepdims=True)
        hit = (rank == r_col.astype(F32)) & (j_row < n_slc)
        idx = jnp.sum(jnp.where(hit, j_row.astype(F32), 0.0), axis=1, keepdims=True)
        sel_out = jnp.where(lane == h, idx.astype(jnp.int32), sel_out)
    sel_ref[0] = sel_out


def nsa_sample_a(q16, cos2, sin2, ckv, win_cache, win_new, n_cmp, n_slc):
    db = q16.shape[0]
    n_pad = ckv.shape[3]
    win_rows = win_cache.shape[1]
    j_pad = -(-n_slc // 128) * 128
    hspec = pl.BlockSpec((1, HEAD_ROWS, HEAD_DIM), lambda b: (b, 0, 0))
    hout = jax.ShapeDtypeStruct((db, HEAD_ROWS, HEAD_DIM), F32)
    return pl.pallas_call(
        functools.partial(_nsa_sample_a_kernel, n_cmp=n_cmp, n_slc=n_slc, j_pad=j_pad),
        grid=(db,),
        in_specs=[hspec,
                  pl.BlockSpec((1, HEAD_DIM), lambda b: (0, 0)),
                  pl.BlockSpec((1, HEAD_DIM), lambda b: (0, 0)),
                  pl.BlockSpec((1, 2, N_KV, n_pad, HEAD_DIM), lambda b: (b, 0, 0, 0, 0)),
                  pl.BlockSpec((1, win_rows, HEAD_DIM), lambda b: (b, 0, 0)),
                  pl.BlockSpec((1, 1, KV_W), lambda b: (b, 0, 0))],
        out_specs=[hspec, hspec, hspec, pl.BlockSpec((1, N_SELECT, HEAD_DIM), lambda b: (b, 0, 0))],
        out_shape=[hout, hout, hout, jax.ShapeDtypeStruct((db, N_SELECT, HEAD_DIM), jnp.int32)],
        compiler_params=_cparams("arbitrary"),
        name="nsa_sample_a",
    )(q16, cos2, sin2, ckv, win_cache, win_new)


def _nsa_sample_b_kernel(sel_ref, tbl_ref, q_ref, *refs):
    blocks = refs[:N_SELECT]
    new_ref, oc_ref, ow_ref, gl_ref, bg_ref, o_ref, os_ref = refs[N_SELECT:]
    b = pl.program_id(0)
    h = pl.program_id(1)
    n_keys = N_SELECT * SLC_BLOCK
    lane_blk = jnp.right_shift(lax.broadcasted_iota(jnp.int32, (1, n_keys), 1), SLC_SHIFT)
    new_k = new_ref[0, pl.ds(h, 1), :]
    new_v = new_ref[0, pl.ds(h + N_KV, 1), :]
    ks, vs = [], []
    blk_of = jnp.zeros((1, n_keys), jnp.int32)
    for r in range(N_SELECT):
        blk = sel_ref[b, h, r]
        in_past = blk * SLC_BLOCK < PAST_LEN
        ks.append(jnp.where(in_past, blocks[r][0, pl.ds(h, SLC_BLOCK, stride=KV_SLOTS), :], new_k))
        vs.append(jnp.where(in_past, blocks[r][0, pl.ds(h + N_KV, SLC_BLOCK, stride=KV_SLOTS), :], new_v))
        blk_of = jnp.where(lane_blk == r, blk, blk_of)
    kpos = blk_of * SLC_BLOCK + (lax.broadcasted_iota(jnp.int32, (1, n_keys), 1) & (SLC_BLOCK - 1))
    s = jnp.where(kpos <= PAST_LEN, _dot_t(q_ref[0] * SCALE, jnp.concatenate(ks, axis=0)), NEG)
    e = jnp.exp(s - jnp.max(s, axis=-1, keepdims=True))
    o_s = _dot(e, jnp.concatenate(vs, axis=0)) / jnp.sum(e, axis=-1, keepdims=True)

    @pl.when(h == 0)
    def _():
        os_ref[...] = jnp.zeros(os_ref.shape, F32)

    row = lax.broadcasted_iota(jnp.int32, (HEAD_ROWS, 1), 0)
    mine = (row >= h * GQA) & (row < (h + 1) * GQA)
    os_ref[...] = jnp.where(mine, o_s, os_ref[...])

    @pl.when(h == pl.num_programs(1) - 1)
    def _():
        gates = jax.nn.sigmoid(gl_ref[0] + bg_ref[...])
        o_ref[0] = gates[:, 0:1] * oc_ref[0] + gates[:, 1:2] * os_ref[...] + gates[:, 2:3] * ow_ref[0]


def nsa_sample_b(sel, page_table, q_rope, slc_halves, slc_new, o_c, o_w, gate_logits, b_gate16):
    db = q_rope.shape[0]
    last_blk = PAST_LEN // SLC_BLOCK - 1
    halves = PAGE_SIZE // SLC_BLOCK

    def cache_map(r):
        def index(b, h, sel_ref, tbl_ref):
            j = jnp.minimum(sel_ref[b, h, r], last_blk)
            return (tbl_ref[b, j // halves] * halves + j % halves, 0, 0)
        return index

    hspec = pl.BlockSpec((1, HEAD_ROWS, HEAD_DIM), lambda b, h, s, t: (b, 0, 0))
    return pl.pallas_call(
        _nsa_sample_b_kernel,
        grid_spec=pltpu.PrefetchScalarGridSpec(
            num_scalar_prefetch=2,
            grid=(db, N_KV),
            in_specs=[hspec]
            + [pl.BlockSpec((1, SLC_BLOCK * KV_SLOTS, HEAD_DIM), cache_map(r)) for r in range(N_SELECT)]
            + [pl.BlockSpec((1, KV_SLOTS, HEAD_DIM), lambda b, h, s, t: (b, 0, 0)),
               hspec, hspec, hspec,
               pl.BlockSpec((HEAD_ROWS, HEAD_DIM), lambda b, h, s, t: (0, 0))],
            out_specs=hspec,
            scratch_shapes=[pltpu.VMEM((HEAD_ROWS, HEAD_DIM), F32)],
        ),
        out_shape=jax.ShapeDtypeStruct((db, HEAD_ROWS, HEAD_DIM), F32),
        compiler_params=_cparams("arbitrary", "arbitrary"),
        name="nsa_sample_b",
    )(sel, page_table, q_rope, *([slc_halves] * N_SELECT), slc_new, o_c, o_w, gate_logits, b_gate16)


def _rope_tables(pos):
    half = HEAD_DIM // 2
    inv = ROPE_THETA ** (-jnp.arange(half, dtype=F32) / half)
    ang = pos.astype(F32)[:, None] * inv[None, :]
    cos, sin = jnp.cos(ang), jnp.sin(ang)
    return jnp.concatenate([cos, cos], -1), jnp.concatenate([-sin, sin], -1)


def _pad_w_in_b(w):
    return jnp.concatenate([w[..., :MAIN_W], w[..., MAIN_W + GATE_W:],
                            jnp.pad(w[..., MAIN_W:MAIN_W + GATE_W], ((0, 0), (0, 0), (0, GATE_PAD - GATE_W)))], axis=-1)


def kernel(x_prompt, x_sample, mem_prompt, state_pool, cache_cmp_kv, cache_slc_kv, cache_win_kv, cache_mem_kv, page_table, g_mix, w_in_a, pool_grp_w, pool_scale, w_in_b, b_gate, g_kv, w_kv, cmp_pe, cmp_w1, cmp_w2, g_mem, w_mem_kv, w_out, g_ffn, w_gu, w_down, g_final):
    batch, seq, _ = x_prompt.shape
    db = x_sample.shape[0]
    m = batch * seq
    tm = PROJ_TM
    n_phys = cache_cmp_kv.shape[0]
    w_in_b_pad = _pad_w_in_b(w_in_b)
    xp = x_prompt.reshape(m, D_MODEL)
    xs = x_sample.reshape(db, D_MODEL)
    mem2 = mem_prompt.reshape(batch * MEM_TOKENS, D_MODEL)
    mem_kv = rms_matmul_layers(mem2, g_mem, w_mem_kv, 512, 512)
    mem_rows = mem_kv.reshape(-1, 2 * MEM_W)
    mem_cache = cache_mem_kv.reshape(-1, MEM_HEAD_DIM)
    cos_p, sin_p = _rope_tables(jnp.arange(seq, dtype=jnp.int32))
    cos_s, sin_s = _rope_tables(jnp.full((db,), PAST_LEN, jnp.int32))
    pools_p, pools_s = [], []
    for l in range(DEPTH):
        if l == N_A:
            cmp_kv, slc_kv, win_kv, slc_k, slc_vt, win_k, win_vt = kv_proj(xp, g_kv, w_kv, cos_p, sin_p, SLC_KT, True)
            table = jnp.arange(batch * seq // PAGE_SIZE, dtype=jnp.int32).reshape(batch, seq // PAGE_SIZE)
            ckv_p = compress(cmp_kv.reshape(-1, PAGE_SIZE, KV_SLOTS, HEAD_DIM), table,
                             jnp.zeros((batch, PAGE_SIZE, KV_SLOTS, HEAD_DIM), F32), cmp_pe, cmp_w1, cmp_w2)
            cmp_new, slc_new, win_new = kv_proj(xs, g_kv, w_kv, cos_s, sin_s, db, False)
            extra = jnp.pad(cmp_new.reshape(db, 1, KV_SLOTS, HEAD_DIM), ((0, 0), (0, PAGE_SIZE - 1), (0, 0), (0, 0)))
            ckv_s = compress(cache_cmp_kv.reshape(n_phys, PAGE_SIZE, KV_SLOTS, HEAD_DIM), page_table, extra,
                             cmp_pe, cmp_w1, cmp_w2)
            t_full = -(-(PAST_LEN + 1) // SLC_BLOCK) * SLC_BLOCK
            n_cmp = t_full // CMP_STRIDE - 1
            n_slc = t_full // SLC_BLOCK
        if l < N_A:
            up, us = rms_matmul(xp, g_mix[l], w_in_a, l, tm, 512, xs=xs)
            u3 = up.reshape(batch, seq, -1)
            zp = pool_prompt(u3, pool_grp_w[l], pool_scale[l]).reshape(m, MAIN_W)
            pools_p.append(u3[:, seq - POOL_BUF:, :MAIN_W])
            zs = pool_sample(us, jnp.transpose(state_pool[l], (1, 0, 2)), pool_grp_w[l], pool_scale[l])
            pools_s.append(jnp.concatenate([state_pool[l][:, 1:], us[:, None, :MAIN_W]], axis=1))
        else:
            j = l - N_A
            up, us = rms_matmul(xp, g_mix[l], w_in_b_pad, j, tm, 768, xs=xs)
            u3 = up.reshape(batch, seq, -1)
            bg = jnp.pad(b_gate[j], (0, GATE_PAD - GATE_W)).reshape(1, GATE_PAD)
            zp = nsa_prompt(up, bg, cos_p, sin_p, ckv_p, slc_k, slc_vt, win_k, win_vt, batch, seq)
            q16 = jnp.pad(us[:, :MAIN_W].reshape(db, N_HEADS, HEAD_DIM), ((0, 0), (0, HEAD_ROWS - N_HEADS), (0, 0)))
            q_rope, o_c, o_w, sel = nsa_sample_a(
                q16, cos_s[:1], sin_s[:1], ckv_s, cache_win_kv.reshape(db, -1, HEAD_DIM),
                win_new.reshape(db, 1, KV_W), n_cmp, n_slc)
            sel = jnp.transpose(sel[:, :, :N_KV], (0, 2, 1))
            gl = us[:, MAIN_W + MEM_W:MAIN_W + MEM_W + GATE_W].reshape(db, N_HEADS, N_BRANCH)
            gl = jnp.pad(gl, ((0, 0), (0, HEAD_ROWS - N_HEADS), (0, HEAD_DIM - N_BRANCH)))
            bg16 = jnp.pad(b_gate[j].reshape(N_HEADS, N_BRANCH), ((0, HEAD_ROWS - N_HEADS), (0, HEAD_DIM - N_BRANCH)))
            z16 = nsa_sample_b(sel, page_table, q_rope,
                               cache_slc_kv.reshape(n_phys * 2, SLC_BLOCK * KV_SLOTS, HEAD_DIM),
                               slc_new.reshape(db, KV_SLOTS, HEAD_DIM), o_c, o_w, gl, bg16)
            zs = z16[:, :N_HEADS].reshape(db, MAIN_W)
        mo_p = mem_attend(u3, mem_rows, 1024, l, False).reshape(m, MEM_W)
        mo_s = mem_attend(us.reshape(db, 1, -1), mem_cache, 1, l, True).reshape(db, MEM_W)
        xp, xs = out_proj(zp, mo_p, w_out, l, xp, tm, 512, sample=(zs, mo_s, xs))
        xp, xs = ffn(xp, g_ffn[l], w_gu, w_down, l, g_final, l == DEPTH - 1, FFN_TM, FFN_TF, xs=xs)
    kv5 = (batch, seq, 2, N_KV, HEAD_DIM)
    kv5s = (db, 1, 2, N_KV, HEAD_DIM)
    n_keep = min(WINDOW, seq)
    win_s = jnp.concatenate([cache_win_kv, win_new.reshape(kv5s)], axis=1)
    win_s = win_s[:, win_s.shape[1] - min(WINDOW, win_s.shape[1]):]
    return (xp.reshape(batch, seq, D_MODEL), xs.reshape(db, 1, D_MODEL), jnp.stack(pools_p),
            cmp_kv.reshape(kv5), slc_kv.reshape(kv5), win_kv.reshape(kv5)[:, seq - n_keep:],
            mem_kv.reshape(DEPTH, batch, MEM_TOKENS, 2, N_MEM_HEADS, MEM_HEAD_DIM),
            jnp.stack(pools_s), cmp_new.reshape(kv5s), slc_new.reshape(kv5s), win_s)
```

```python
import functools

import jax
import jax.numpy as jnp
from jax import lax
from jax.experimental import pallas as pl
from jax.experimental.pallas import tpu as pltpu

F32 = jnp.float32
BF16 = jnp.bfloat16

D_MODEL = 2048
DEPTH = 4
N_A = 2
PAST_LEN = 16384
PAGE_SIZE = 128
MEM_TOKENS = 256
N_MEM_HEADS = 4
MEM_HEAD_DIM = 128
MEM_W = 512
MAIN_W = 1536
POOL_WINDOWS = (2, 4, 8, 16)
POOL_GC = 384
POOL_BUF = 15
HEAD_DIM = 128
N_HEADS = 12
N_KV = 4
GQA = 3
N_BRANCH = 3
GATE_W = 36
GATE_PAD = 256
CMP_BLOCK = 32
CMP_STRIDE = 16
SLC_BLOCK = 64
SLC_SHIFT = 6
N_SELECT = 16
WINDOW = 512
Q_BLOCK = 128
D_FF = 5632
ROPE_THETA = 10000.0
EPS = 1e-6
NEG = -1e30
SCALE = HEAD_DIM ** -0.5
MEM_SCALE = MEM_HEAD_DIM ** -0.5
KV_SLOTS = 2 * N_KV
KV_W = KV_SLOTS * HEAD_DIM
W_IN_B = MAIN_W + MEM_W + GATE_PAD

VMEM_LIMIT = 56 * 1024 * 1024
SLC_KT = 512
LOG2E = 1.4426950408889634
PROJ_TM = 1024
FFN_TM = 1024
FFN_TF = 256
PAGES_PER_STEP = 16
PAGE_RING = 3
CH_PER_PAGE = PAGE_SIZE // CMP_STRIDE
CMP_STEP = PAGES_PER_STEP * CH_PER_PAGE


def _cparams(*sem):
    return pltpu.CompilerParams(dimension_semantics=sem, vmem_limit_bytes=VMEM_LIMIT)


def _rms(x, g):
    return x * lax.rsqrt(jnp.mean(x * x, axis=-1, keepdims=True) + EPS) * g


def _dot(a, b):
    return jnp.dot(a.astype(BF16), b.astype(BF16), preferred_element_type=F32)


def _dot_t(a, b):
    return lax.dot_general(a.astype(BF16), b.astype(BF16), (((1,), (1,)), ((), ())),
                           preferred_element_type=F32)


def _lhs_dtype(rows):
    return BF16 if rows % 16 == 0 else F32


def _rope(x, cos2, sin2):
    return x * cos2 + pltpu.roll(x, HEAD_DIM // 2, axis=1) * sin2


def _rms_matmul_kernel(x_ref, g_ref, w_ref, o_ref, xn_ref, wb_ref, *, lead=0):
    i = pl.program_id(lead)
    j = pl.program_id(lead + 1)

    @pl.when(j == 0)
    def _():
        xn_ref[...] = _rms(x_ref[...], g_ref[...]).astype(xn_ref.dtype)

    @pl.when(i == 0)
    def _():
        wb_ref[j] = w_ref[...].astype(BF16)

    o_ref[...] = _dot(xn_ref[...], wb_ref[j])


def _rms_matmul2_kernel(x_ref, xs_ref, g_ref, w_ref, o_ref, os_ref, xn_ref, xsn_ref, wb_ref):
    _rms_matmul_kernel(x_ref, g_ref, w_ref, o_ref, xn_ref, wb_ref)
    i = pl.program_id(0)
    j = pl.program_id(1)

    @pl.when((i == 0) & (j == 0))
    def _():
        xsn_ref[...] = _rms(xs_ref[...], g_ref[...])

    @pl.when(i == 0)
    def _():
        os_ref[...] = _dot(xsn_ref[...], wb_ref[j])


def _once_per_column(nj):
    return lambda i, j: jnp.where(i == 0, j, nj - 1)


def rms_matmul(x, g, w, layer, tm, tn, xs=None):
    m, k = x.shape
    n = w.shape[2]
    nj = n // tn
    col = _once_per_column(nj)
    x_spec = pl.BlockSpec((tm, k), lambda i, j: (i, 0))
    g_spec = pl.BlockSpec((1, k), lambda i, j: (0, 0))
    w_spec = pl.BlockSpec((None, k, tn), lambda i, j: (layer, 0, col(i, j)))
    o_spec = pl.BlockSpec((tm, tn), lambda i, j: (i, j))
    scratch = [pltpu.VMEM((tm, k), _lhs_dtype(tm)), pltpu.VMEM((nj, k, tn), BF16)]
    if xs is None:
        return pl.pallas_call(
            _rms_matmul_kernel,
            grid=(m // tm, nj),
            in_specs=[x_spec, g_spec, w_spec],
            out_specs=o_spec,
            out_shape=jax.ShapeDtypeStruct((m, n), F32),
            scratch_shapes=scratch,
            compiler_params=_cparams("arbitrary", "arbitrary"),
            name="rms_matmul",
        )(x, g.reshape(1, k), w)
    ms = xs.shape[0]
    return pl.pallas_call(
        _rms_matmul2_kernel,
        grid=(m // tm, nj),
        in_specs=[x_spec, pl.BlockSpec((ms, k), lambda i, j: (0, 0)), g_spec, w_spec],
        out_specs=[o_spec, pl.BlockSpec((ms, tn), lambda i, j: (0, col(i, j)))],
        out_shape=[jax.ShapeDtypeStruct((m, n), F32), jax.ShapeDtypeStruct((ms, n), F32)],
        scratch_shapes=[scratch[0], pltpu.VMEM((ms, k), F32), scratch[1]],
        compiler_params=_cparams("arbitrary", "arbitrary"),
        name="rms_matmul2",
    )(x, xs, g.reshape(1, k), w)


def rms_matmul_layers(x, g, w, tm, tn):
    m, k = x.shape
    layers, _, n = w.shape
    nj = n // tn
    col = _once_per_column(nj)
    return pl.pallas_call(
        functools.partial(_rms_matmul_kernel, lead=1),
        grid=(layers, m // tm, nj),
        in_specs=[pl.BlockSpec((tm, k), lambda l, i, j: (i, 0)),
                  pl.BlockSpec((None, 1, k), lambda l, i, j: (l, 0, 0)),
                  pl.BlockSpec((None, k, tn), lambda l, i, j: (l, 0, col(i, j)))],
        out_specs=pl.BlockSpec((None, tm, tn), lambda l, i, j: (l, i, j)),
        out_shape=jax.ShapeDtypeStruct((layers, m, n), F32),
        scratch_shapes=[pltpu.VMEM((tm, k), _lhs_dtype(tm)), pltpu.VMEM((nj, k, tn), BF16)],
        compiler_params=_cparams("arbitrary", "arbitrary", "arbitrary"),
        name="rms_matmul_layers",
    )(x, g.reshape(layers, 1, k), w)


def _kv_proj_kernel(x_ref, g_ref, w_ref, cos_ref, sin_ref, cmp_ref, slc_ref, win_ref, *rest):
    xn_ref, wb_ref = rest[-2:]
    attn = rest[:-2]
    i = pl.program_id(0)
    j = pl.program_id(1)
    half = N_KV * HEAD_DIM

    @pl.when(j == 0)
    def _():
        xn_ref[...] = _rms(x_ref[...], g_ref[...]).astype(xn_ref.dtype)

    @pl.when(i == 0)
    def _():
        wb_ref[j] = w_ref[...].astype(BF16)

    for br, ref in enumerate((cmp_ref, slc_ref, win_ref)):
        for c in range(2):
            @pl.when(j == 2 * br + c)
            def _(ref=ref, br=br, c=c):
                y = _dot(xn_ref[...], wb_ref[2 * br + c])
                val = y
                if c == 0:
                    cos2 = cos_ref[...]
                    sin2 = sin_ref[...]
                    val = jnp.concatenate(
                        [_rope(y[:, h * HEAD_DIM:(h + 1) * HEAD_DIM], cos2, sin2) for h in range(N_KV)], axis=1)
                for h in range(N_KV):
                    ref[:, c * N_KV + h, :] = val[:, h * HEAD_DIM:(h + 1) * HEAD_DIM]
                if attn and br == 1 and c == 1:
                    attn[1][0] = jnp.transpose(val).astype(BF16)
                elif attn and br == 2 and c == 1:
                    for t in range(val.shape[0] // Q_BLOCK):
                        attn[3][t] = jnp.transpose(val[t * Q_BLOCK:(t + 1) * Q_BLOCK]).astype(BF16)
                elif attn and br > 0:
                    attn[2 * (br - 1) + c][...] = val.astype(BF16)


def kv_proj(x, g, w, cos2, sin2, tm, attn_copies):
    m, k = x.shape
    half = N_KV * HEAD_DIM
    nt = cos2.shape[0] // tm
    nj = 2 * N_BRANCH
    col = _once_per_column(nj)
    out = jax.ShapeDtypeStruct((m, KV_SLOTS, HEAD_DIM), F32)
    ospec = pl.BlockSpec((tm, KV_SLOTS, HEAD_DIM), lambda i, j: (i, 0, 0))
    out_shape, out_specs = [out, out, out], [ospec, ospec, ospec]
    if attn_copies:
        out16 = jax.ShapeDtypeStruct((m, half), BF16)
        ospec16 = pl.BlockSpec((tm, half), lambda i, j: (i, 0))
        out_shape += [out16, jax.ShapeDtypeStruct((m // tm, half, tm), BF16), out16,
                      jax.ShapeDtypeStruct((m // Q_BLOCK, half, Q_BLOCK), BF16)]
        out_specs += [ospec16, pl.BlockSpec((1, half, tm), lambda i, j: (i, 0, 0)), ospec16,
                      pl.BlockSpec((tm // Q_BLOCK, half, Q_BLOCK), lambda i, j: (i, 0, 0))]
    return pl.pallas_call(
        _kv_proj_kernel,
        grid=(m // tm, nj),
        in_specs=[pl.BlockSpec((tm, k), lambda i, j: (i, 0)),
                  pl.BlockSpec((1, k), lambda i, j: (0, 0)),
                  pl.BlockSpec((k, half), lambda i, j: (0, col(i, j))),
                  pl.BlockSpec((tm, HEAD_DIM), lambda i, j: (i % nt, 0)),
                  pl.BlockSpec((tm, HEAD_DIM), lambda i, j: (i % nt, 0))],
        out_specs=out_specs,
        out_shape=out_shape,
        scratch_shapes=[pltpu.VMEM((tm, k), _lhs_dtype(tm)), pltpu.VMEM((nj, k, half), BF16)],
        compiler_params=_cparams("arbitrary", "arbitrary"),
        name="kv_proj",
    )(x, g.reshape(1, k), w, cos2, sin2)


def _pool_prompt_kernel(cur_ref, prev_ref, w_ref, sc_ref, o_ref, *, ts):
    i = pl.program_id(1)
    keep = (i > 0).astype(F32)
    pos = (i * ts + lax.broadcasted_iota(jnp.int32, (ts, 1), 0) + 1).astype(F32)
    diffs = []
    for g, w in enumerate(POOL_WINDOWS):
        sl = slice(g * POOL_GC, (g + 1) * POOL_GC)
        x = cur_ref[0, :, sl]
        ext = jnp.concatenate([prev_ref[0, :, sl] * keep, x], axis=0)
        span = 1
        while span < w:
            ext = ext[span:] + ext[:-span]
            span *= 2
        win = ext[17 - w:17 - w + ts]
        diffs.append((win / jnp.minimum(pos, float(w)) - x).astype(BF16))
    for g in range(len(POOL_WINDOWS)):
        sl = slice(g * POOL_GC, (g + 1) * POOL_GC)
        o_ref[0, :, sl] = _dot(diffs[g], w_ref[g]) * sc_ref[:, sl]


def pool_prompt(u3, w_grp, scale, ts=512):
    b, s, n = u3.shape
    r = ts // 16
    return pl.pallas_call(
        functools.partial(_pool_prompt_kernel, ts=ts),
        grid=(b, s // ts),
        in_specs=[pl.BlockSpec((1, ts, MAIN_W), lambda bi, i: (bi, i, 0)),
                  pl.BlockSpec((1, 16, MAIN_W), lambda bi, i: (bi, jnp.maximum(i * r - 1, 0), 0)),
                  pl.BlockSpec((4, POOL_GC, POOL_GC), lambda bi, i: (0, 0, 0)),
                  pl.BlockSpec((1, MAIN_W), lambda bi, i: (0, 0))],
        out_specs=pl.BlockSpec((1, ts, MAIN_W), lambda bi, i: (bi, i, 0)),
        out_shape=jax.ShapeDtypeStruct((b, s, MAIN_W), F32),
        compiler_params=_cparams("parallel", "arbitrary"),
        name="pool_prompt",
    )(u3, u3, w_grp, scale.reshape(1, MAIN_W))


def _pool_sample_kernel(cur_ref, buf_ref, w_ref, sc_ref, o_ref):
    for g, w in enumerate(POOL_WINDOWS):
        sl = slice(g * POOL_GC, (g + 1) * POOL_GC)
        x = cur_ref[:, sl]
        win = x
        for r in range(POOL_BUF - (w - 1), POOL_BUF):
            win = win + buf_ref[r, :, sl]
        d = win / float(w) - x
        o_ref[:, sl] = _dot(d, w_ref[g]) * sc_ref[:, sl]


def pool_sample(u, buf_t, w_grp, scale):
    db = u.shape[0]
    return pl.pallas_call(
        _pool_sample_kernel,
        grid=(1,),
        in_specs=[pl.BlockSpec((db, MAIN_W), lambda i: (0, 0)),
                  pl.BlockSpec((POOL_BUF, db, MAIN_W), lambda i: (0, 0, 0)),
                  pl.BlockSpec((4, POOL_GC, POOL_GC), lambda i: (0, 0, 0)),
                  pl.BlockSpec((1, MAIN_W), lambda i: (0, 0))],
        out_specs=pl.BlockSpec((db, MAIN_W), lambda i: (0, 0)),
        out_shape=jax.ShapeDtypeStruct((db, MAIN_W), F32),
        compiler_params=_cparams("arbitrary"),
        name="pool_sample",
    )(u, buf_t, w_grp, scale.reshape(1, MAIN_W))


def _mem_attend_kernel(q_ref, kv_ref, o_ref, *, tq, slots):
    rows = max(tq, 8)
    logits, values = [], []
    for h in range(N_MEM_HEADS):
        sl = slice(h * MEM_HEAD_DIM, (h + 1) * MEM_HEAD_DIM)
        q = q_ref[0, :, sl]
        if tq < rows:
            q = jnp.broadcast_to(q[0:1], (rows, MEM_HEAD_DIM))
        if slots:
            k = kv_ref[pl.ds(h, MEM_TOKENS, stride=2 * N_MEM_HEADS), :]
            v = kv_ref[pl.ds(N_MEM_HEADS + h, MEM_TOKENS, stride=2 * N_MEM_HEADS), :]
        else:
            k = kv_ref[:, sl]
            v = kv_ref[:, MEM_W + h * MEM_HEAD_DIM:MEM_W + (h + 1) * MEM_HEAD_DIM]
        logits.append(_dot_t(q, k) * MEM_SCALE)
        values.append(v)
    for h in range(N_MEM_HEADS):
        sl = slice(h * MEM_HEAD_DIM, (h + 1) * MEM_HEAD_DIM)
        s, v = logits[h], values[h]
        e = jnp.exp(s - jnp.max(s, axis=-1, keepdims=True))
        o = _dot(e, v) / jnp.sum(e, axis=-1, keepdims=True)
        o_ref[0, :, sl] = o[0:tq]


def mem_attend(u3, mkv, tq, layer, slots):
    b, s, _ = u3.shape
    if slots:
        kv_spec = pl.BlockSpec((MEM_TOKENS * 2 * N_MEM_HEADS, MEM_HEAD_DIM), lambda bi, i: (layer * b + bi, 0))
    else:
        kv_spec = pl.BlockSpec((MEM_TOKENS, 2 * MEM_W), lambda bi, i: (layer * b + bi, 0))
    return pl.pallas_call(
        functools.partial(_mem_attend_kernel, tq=tq, slots=slots),
        grid=(b, s // tq),
        in_specs=[pl.BlockSpec((1, tq, MEM_W), lambda bi, i: (bi, i, MAIN_W // MEM_W)), kv_spec],
        out_specs=pl.BlockSpec((1, tq, MEM_W), lambda bi, i: (bi, i, 0)),
        out_shape=jax.ShapeDtypeStruct((b, s, MEM_W), F32),
        compiler_params=_cparams("parallel", "arbitrary"),
        name="mem_attend",
    )(u3, mkv)


def _out_proj_kernel(z_ref, mo_ref, w_ref, x_ref, o_ref, lhs_ref, wb_ref):
    i = pl.program_id(0)
    j = pl.program_id(1)

    @pl.when(j == 0)
    def _():
        lhs_ref[:, :MAIN_W] = z_ref[...].astype(lhs_ref.dtype)
        lhs_ref[:, MAIN_W:] = mo_ref[...].astype(lhs_ref.dtype)

    @pl.when(i == 0)
    def _():
        wb_ref[j] = w_ref[...].astype(BF16)

    o_ref[...] = x_ref[...] + _dot(lhs_ref[...], wb_ref[j])


def _out_proj2_kernel(z_ref, mo_ref, w_ref, x_ref, zs_ref, mos_ref, xs_ref, o_ref, os_ref, lhs_ref, wb_ref):
    _out_proj_kernel(z_ref, mo_ref, w_ref, x_ref, o_ref, lhs_ref, wb_ref)
    j = pl.program_id(1)

    @pl.when(pl.program_id(0) == 0)
    def _():
        lhs = jnp.concatenate([zs_ref[...], mos_ref[...]], axis=1)
        os_ref[...] = xs_ref[...] + _dot(lhs, wb_ref[j])


def out_proj(z, mo, w, layer, x, tm, tn, sample=None):
    m = x.shape[0]
    k = MAIN_W + MEM_W
    nj = D_MODEL // tn
    col = _once_per_column(nj)
    in_specs = [pl.BlockSpec((tm, MAIN_W), lambda i, j: (i, 0)),
                pl.BlockSpec((tm, MEM_W), lambda i, j: (i, 0)),
                pl.BlockSpec((None, k, tn), lambda i, j: (layer, 0, col(i, j))),
                pl.BlockSpec((tm, tn), lambda i, j: (i, j))]
    o_spec = pl.BlockSpec((tm, tn), lambda i, j: (i, j))
    scratch = [pltpu.VMEM((tm, k), _lhs_dtype(tm)), pltpu.VMEM((nj, k, tn), BF16)]
    if sample is None:
        return pl.pallas_call(
            _out_proj_kernel,
            grid=(m // tm, nj),
            in_specs=in_specs,
            out_specs=o_spec,
            out_shape=jax.ShapeDtypeStruct((m, D_MODEL), F32),
            scratch_shapes=scratch,
            compiler_params=_cparams("arbitrary", "arbitrary"),
            name="out_proj",
        )(z, mo, w, x)
    zs, mos, xs = sample
    ms = xs.shape[0]
    s_spec = pl.BlockSpec((ms, tn), lambda i, j: (0, col(i, j)))
    return pl.pallas_call(
        _out_proj2_kernel,
        grid=(m // tm, nj),
        in_specs=in_specs + [pl.BlockSpec((ms, MAIN_W), lambda i, j: (0, 0)),
                             pl.BlockSpec((ms, MEM_W), lambda i, j: (0, 0)), s_spec],
        out_specs=[o_spec, s_spec],
        out_shape=[jax.ShapeDtypeStruct((m, D_MODEL), F32), jax.ShapeDtypeStruct((ms, D_MODEL), F32)],
        scratch_shapes=scratch,
        compiler_params=_cparams("arbitrary", "arbitrary"),
        name="out_proj2",
    )(z, mo, w, x, zs, mos, xs)


def _ffn_kernel(x_ref, g_ref, wa_ref, wb_ref, wd_ref, gf_ref, o_ref, xn_ref, *, final):
    j = pl.program_id(1)

    @pl.when(j == 0)
    def _():
        x = x_ref[...]
        xn_ref[...] = _rms(x, g_ref[...]).astype(xn_ref.dtype)
        o_ref[...] = x

    xn = xn_ref[...]
    a = _dot(xn, wa_ref[...])
    b = _dot(xn, wb_ref[...])
    o_ref[...] += _dot(jax.nn.silu(a) * b, wd_ref[...])

    if final:
        @pl.when(j == pl.num_programs(1) - 1)
        def _():
            o_ref[...] = _rms(o_ref[...], gf_ref[...])


def _ffn2_kernel(x_ref, g_ref, wa_ref, wb_ref, wd_ref, gf_ref, xs_ref, o_ref, os_ref, xn_ref, xsn_ref, *, final):
    _ffn_kernel(x_ref, g_ref, wa_ref, wb_ref, wd_ref, gf_ref, o_ref, xn_ref, final=final)
    j = pl.program_id(1)

    @pl.when(pl.program_id(0) == 0)
    def _():
        @pl.when(j == 0)
        def _():
            xs = xs_ref[...]
            xsn_ref[...] = _rms(xs, g_ref[...])
            os_ref[...] = xs

        xsn = xsn_ref[...]
        a = _dot(xsn, wa_ref[...])
        b = _dot(xsn, wb_ref[...])
        os_ref[...] += _dot(jax.nn.silu(a) * b, wd_ref[...])

        if final:
            @pl.when(j == pl.num_programs(1) - 1)
            def _():
                os_ref[...] = _rms(os_ref[...], gf_ref[...])


def ffn(x, g, w_gu, w_down, layer, g_final, final, tm, tf, xs=None):
    m, k = x.shape
    nf = D_FF // tf
    in_specs = [pl.BlockSpec((tm, k), lambda i, j: (i, 0), pipeline_mode=pl.Buffered(1)),
                pl.BlockSpec((1, k), lambda i, j: (0, 0)),
                pl.BlockSpec((None, k, tf), lambda i, j: (layer, 0, j)),
                pl.BlockSpec((None, k, tf), lambda i, j: (layer, 0, nf + j)),
                pl.BlockSpec((None, tf, k), lambda i, j: (layer, j, 0)),
                pl.BlockSpec((1, k), lambda i, j: (0, 0))]
    o_spec = pl.BlockSpec((tm, k), lambda i, j: (i, 0))
    scratch = [pltpu.VMEM((tm, k), _lhs_dtype(tm))]
    args = (x, g.reshape(1, k), w_gu, w_gu, w_down, g_final.reshape(1, k))
    if xs is None:
        return pl.pallas_call(
            functools.partial(_ffn_kernel, final=final),
            grid=(m // tm, nf),
            in_specs=in_specs,
            out_specs=o_spec,
            out_shape=jax.ShapeDtypeStruct((m, k), F32),
            scratch_shapes=scratch,
            compiler_params=_cparams("parallel", "arbitrary"),
            name="ffn",
        )(*args)
    ms = xs.shape[0]
    s_spec = pl.BlockSpec((ms, k), lambda i, j: (0, 0))
    return pl.pallas_call(
        functools.partial(_ffn2_kernel, final=final),
        grid=(m // tm, nf),
        in_specs=in_specs + [s_spec],
        out_specs=[o_spec, s_spec],
        out_shape=[jax.ShapeDtypeStruct((m, k), F32), jax.ShapeDtypeStruct((ms, k), F32)],
        scratch_shapes=scratch + [pltpu.VMEM((ms, k), F32)],
        compiler_params=_cparams("arbitrary", "arbitrary"),
        name="ffn2",
    )(*args, xs)


def _compress_kernel(tbl_ref, pages_hbm, nxt_ref, extra_ref, w1_ref, pe_ref, w2_ref, o_ref, w1b_ref, lhs_ref,
                     ring_ref, sem_ref, *, n_steps, n_pages):
    p = pl.program_id(1)
    per_seq = n_steps + 1
    g = pl.program_id(0) * per_seq + p
    n_total = pl.num_programs(0) * per_seq

    def page_copy(step, k):
        sb = step // per_seq
        sp = step % per_seq
        page = tbl_ref[sb, jnp.minimum(sp * PAGES_PER_STEP + k, n_pages - 1)]
        slot = step % PAGE_RING
        return pltpu.make_async_copy(pages_hbm.at[page], ring_ref.at[slot, k], sem_ref.at[slot])

    @pl.when(g == 0)
    def _():
        for s in range(PAGE_RING - 1):
            for k in range(PAGES_PER_STEP):
                page_copy(s, k).start()

    @pl.when(g + PAGE_RING - 1 < n_total)
    def _():
        for k in range(PAGES_PER_STEP):
            page_copy(g + PAGE_RING - 1, k).start()

    for k in range(PAGES_PER_STEP):
        page_copy(g, k).wait()
    slot = g % PAGE_RING
    on_extra = p == n_steps
    nxt_extra = p == n_steps - 1
    nh = CMP_STEP * N_KV
    width = CMP_STRIDE * HEAD_DIM
    d = HEAD_DIM
    group = 4
    n_groups = CMP_STEP // group + 1
    n_rows = n_groups * group * N_KV

    @pl.when((pl.program_id(0) == 0) & (p == 0))
    def _():
        for c in range(2):
            w1b_ref[c] = jnp.concatenate([w1_ref[c, 0], w1_ref[c, 1]], axis=1).astype(BF16)

    def slots(n, r):
        if n < CMP_STEP:
            k, i = divmod(n, CH_PER_PAGE)
            x = ring_ref[slot, k, i * CMP_STRIDE + r]
            return jnp.where(on_extra, extra_ref[0, i * CMP_STRIDE + r] if k == 0 else 0.0, x)
        if n == CMP_STEP:
            return jnp.where(on_extra, 0.0, jnp.where(nxt_extra, extra_ref[0, r], nxt_ref[0, r]))
        return jnp.zeros((KV_SLOTS, HEAD_DIM), F32)

    for q in range(n_groups):
        for r in range(CMP_STRIDE):
            tiles = [slots(group * q + i, r) for i in range(group)]
            for c in range(2):
                piece = jnp.concatenate([t[c * N_KV:(c + 1) * N_KV] for t in tiles], axis=0)
                lhs_ref[c, group * N_KV * q:group * N_KV * (q + 1), r * HEAD_DIM:(r + 1) * HEAD_DIM] = piece.astype(BF16)
    for c in range(2):
        tail = jnp.concatenate([pe_ref[c, 0], pe_ref[c, 1], jnp.zeros((14, width), F32)], axis=0)
        lhs_ref[c, n_rows:n_rows + 16, :] = tail.astype(BF16)
    hc = [_dot(lhs_ref[c], w1b_ref[c]) for c in range(2)]
    for c in range(2):
        pe_term = hc[c][n_rows:n_rows + 1, :d] + hc[c][n_rows + 1:n_rows + 2, d:]
        hid = jax.nn.gelu(hc[c][:nh, :d] + hc[c][N_KV:nh + N_KV, d:] + pe_term)
        o_ref[0, c] = _dot(hid, w2_ref[c])


def compress(pages, table, extra, cmp_pe, cmp_w1, cmp_w2):
    nb, n_pages = table.shape
    n_steps = n_pages // PAGES_PER_STEP
    w1 = cmp_w1.reshape(2, 2, CMP_STRIDE * HEAD_DIM, HEAD_DIM)
    pe = cmp_pe.reshape(2, 2, 1, CMP_STRIDE * HEAD_DIM)

    def page_map(k):
        return lambda b, p, tbl: (tbl[b, jnp.minimum(p * PAGES_PER_STEP + k, n_pages - 1)], 0, 0, 0)

    in_specs = [pl.BlockSpec(memory_space=pl.ANY)]
    in_specs += [pl.BlockSpec((1, CMP_STRIDE, KV_SLOTS, HEAD_DIM), page_map(PAGES_PER_STEP)),
                 pl.BlockSpec((1, PAGE_SIZE, KV_SLOTS, HEAD_DIM), lambda b, p, tbl: (b, 0, 0, 0)),
                 pl.BlockSpec(w1.shape, lambda b, p, tbl: (0, 0, 0, 0)),
                 pl.BlockSpec(pe.shape, lambda b, p, tbl: (0, 0, 0, 0)),
                 pl.BlockSpec(cmp_w2.shape, lambda b, p, tbl: (0, 0, 0))]
    n_out = CMP_STEP * (n_steps + 1)
    out = pl.pallas_call(
        functools.partial(_compress_kernel, n_steps=n_steps, n_pages=n_pages),
        grid_spec=pltpu.PrefetchScalarGridSpec(
            num_scalar_prefetch=1,
            grid=(nb, n_steps + 1),
            in_specs=in_specs,
            out_specs=pl.BlockSpec((1, 2, CMP_STEP * N_KV, HEAD_DIM), lambda b, p, tbl: (b, 0, p, 0)),
            scratch_shapes=[pltpu.VMEM((2, CMP_STRIDE * HEAD_DIM, 2 * HEAD_DIM), BF16),
                            pltpu.VMEM((2, (CMP_STEP + 8) * N_KV, CMP_STRIDE * HEAD_DIM), BF16),
                            pltpu.VMEM((PAGE_RING, PAGES_PER_STEP, PAGE_SIZE, KV_SLOTS, HEAD_DIM), F32),
                            pltpu.SemaphoreType.DMA((PAGE_RING,))],
        ),
        out_shape=jax.ShapeDtypeStruct((nb, 2, n_out * N_KV, HEAD_DIM), F32),
        compiler_params=_cparams("arbitrary", "arbitrary"),
        name="compress",
    )(table, pages, pages, extra, w1, pe, cmp_w2)
    return jnp.transpose(out.reshape(nb, 2, n_out, N_KV, HEAD_DIM), (0, 1, 3, 2, 4))


def _cover(n_pad, j_pad, n_cmp):
    n = lax.broadcasted_iota(jnp.int32, (n_pad, j_pad), 0)
    j = lax.broadcasted_iota(jnp.int32, (n_pad, j_pad), 1)
    hit = (n * CMP_STRIDE < j * SLC_BLOCK + SLC_BLOCK) & (n * CMP_STRIDE + CMP_BLOCK - 1 >= j * SLC_BLOCK)
    return (hit & (n < n_cmp)).astype(F32)


def _cover_t(j_pad, n_pad, n_cmp):
    j = lax.broadcasted_iota(jnp.int32, (j_pad, n_pad), 0)
    n = lax.broadcasted_iota(jnp.int32, (j_pad, n_pad), 1)
    hit = (n * CMP_STRIDE < j * SLC_BLOCK + SLC_BLOCK) & (n * CMP_STRIDE + CMP_BLOCK - 1 >= j * SLC_BLOCK)
    return (hit & (n < n_cmp)).astype(F32)


def _softmax_rows(s):
    e = jnp.exp(s - jnp.max(s, axis=-1, keepdims=True))
    return e / jnp.sum(e, axis=-1, keepdims=True)


def _nsa_prompt_kernel(q_ref, gl_ref, bg_ref, cos_ref, sin_ref, ck_ref, cvt_ref, ks_ref, vt_ref, kw_ref, vwt_ref,
                       o_ref, qt_ref, drop_ref, z_ref, *, n_cmp, n_slc, kt):
    qb = pl.program_id(1)
    tq = Q_BLOCK
    cos2 = cos_ref[...]
    sin2 = sin_ref[...]
    n_pad = ck_ref.shape[2]
    span = WINDOW + tq
    qpos_row = qb * tq + lax.broadcasted_iota(jnp.int32, (1, tq), 1)
    qpos = jnp.concatenate([qpos_row] * GQA, axis=1)
    gates_t = jnp.transpose(jax.nn.sigmoid(gl_ref[...] + bg_ref[...]))

    def gate_row(hd, br):
        return gates_t[hd * N_BRANCH + br:hd * N_BRANCH + br + 1, :]

    n_col = lax.broadcasted_iota(jnp.int32, (n_pad, 1), 0)
    bias_c = jnp.where((n_col * CMP_STRIDE + CMP_BLOCK - 1 <= qpos) & (n_col < n_cmp), 0.0, NEG)
    any_c = (qpos >= CMP_BLOCK - 1).astype(F32)
    cover_t = _cover_t(n_slc, n_pad, n_cmp)
    j_col = lax.broadcasted_iota(jnp.int32, (n_slc, 1), 0)
    qblk = jnp.right_shift(qpos_row, SLC_SHIFT)
    forced = (j_col == 0) | (j_col == qblk) | (j_col == qblk - 1)
    valid = j_col <= qblk
    n_sel = min(N_SELECT, n_slc)

    w_lo = jnp.maximum(qb * tq - WINDOW, 0)
    w_lo = pl.multiple_of(w_lo, tq)
    kpos_w = w_lo + lax.broadcasted_iota(jnp.int32, (span, 1), 0)
    bias_w = jnp.where((kpos_w <= qpos) & (qpos - kpos_w < WINDOW), 0.0, NEG)

    heads = range(N_KV)
    s_cmp, s_win = [], []
    for h in heads:
        q = jnp.concatenate(
            [_rope(q_ref[:, (h * GQA + g) * HEAD_DIM:(h * GQA + g + 1) * HEAD_DIM], cos2, sin2)
             for g in range(GQA)], axis=0)
        qt = jnp.transpose(q * (SCALE * LOG2E)).astype(BF16)
        qt_ref[h] = qt
        hs = slice(h * HEAD_DIM, (h + 1) * HEAD_DIM)
        s_cmp.append(_dot(ck_ref[0, h], qt) + bias_c)
        s_win.append(_dot(kw_ref[pl.ds(w_lo, span), hs], qt) + bias_w)

    o_cmp, scores = [], []
    for h in heads:
        e = jnp.exp2(s_cmp[h] - jnp.max(s_cmp[h], axis=0, keepdims=True))
        p = e / jnp.sum(e, axis=0, keepdims=True) * any_c
        o_cmp.append(_dot(cvt_ref[0, h], p))
        p_sum = p[:, 0:tq] + p[:, tq:2 * tq] + p[:, 2 * tq:3 * tq]
        score = jnp.dot(cover_t, p_sum, preferred_element_type=F32,
                        precision=lax.Precision.HIGHEST)
        scores.append(jnp.where(forced, jnp.inf, jnp.where(valid, score, -jnp.inf)))

    for h in heads:
        hs = slice(h * HEAD_DIM, (h + 1) * HEAD_DIM)
        e = jnp.exp2(s_win[h] - jnp.max(s_win[h], axis=0, keepdims=True))
        o_w = _dot(vwt_ref[w_lo // tq, hs, :], e[0:tq])
        for i in range(1, span // tq):
            o_w = o_w + _dot(vwt_ref[w_lo // tq + i, hs, :], e[i * tq:(i + 1) * tq])
        o_w = o_w / jnp.sum(e, axis=0, keepdims=True)
        for g in range(GQA):
            hd = h * GQA + g
            r = slice(g * tq, (g + 1) * tq)
            z_ref[hd] = gate_row(hd, 0) * o_cmp[h][:, r] + gate_row(hd, 2) * o_w[:, r]

    for h in heads:
        score = scores[h]
        rank = jnp.zeros((n_slc, tq), F32)
        for i in range(n_slc):
            row = score[i:i + 1, :]
            beats = (row > score) | ((row == score) & (i < j_col))
            rank = rank + jnp.where(beats, 1.0, 0.0)
        drop_ref[h] = jnp.where(rank >= n_sel, NEG, 0.0)

    blocks_per_tile = kt // SLC_BLOCK

    def slc_step(t, carry, nk=kt):
        k0 = pl.multiple_of(t * kt, kt)
        kpos = k0 + lax.broadcasted_iota(jnp.int32, (nk, 1), 0)
        causal = jnp.where(kpos <= qpos_row, 0.0, NEG)
        def logits(h):
            hs = slice(h * HEAD_DIM, (h + 1) * HEAD_DIM)
            dropped = jnp.concatenate(
                [jnp.broadcast_to(drop_ref[h, pl.ds(t * blocks_per_tile + jb, 1), :], (SLC_BLOCK, tq))
                 for jb in range(nk // SLC_BLOCK)], axis=0)
            bias = dropped + causal
            return _dot(ks_ref[pl.ds(k0, nk), hs], qt_ref[h]) + jnp.concatenate([bias] * GQA, axis=1)

        scores = [logits(h) for h in range(N_KV)]
        out = []
        for h in range(N_KV):
            m, l, acc = carry[h]
            hs = slice(h * HEAD_DIM, (h + 1) * HEAD_DIM)
            s = scores[h]
            m_new = jnp.maximum(m, jnp.max(s, axis=0, keepdims=True))
            a = jnp.exp2(m - m_new)
            e = jnp.exp2(s - m_new)
            l = l * a + jnp.sum(e, axis=0, keepdims=True)
            acc = acc * a + _dot(vt_ref[t, hs, :nk], e)
            out.append((m_new, l, acc))
        return tuple(out)

    q_end = qb * tq + tq
    n_full = q_end // kt
    init = (jnp.full((1, GQA * tq), NEG, F32), jnp.zeros((1, GQA * tq), F32), jnp.zeros((HEAD_DIM, GQA * tq), F32))
    done = lax.fori_loop(0, n_full, slc_step, (init,) * N_KV)
    tails = [lambda c: c] + [functools.partial(lambda c, nk: slc_step(n_full, c, nk), nk=i * tq)
                             for i in range(1, kt // tq)]
    done = lax.switch((q_end % kt) // tq, tails, done)
    for h in range(N_KV):
        _, l, acc = done[h]
        o_s = acc / l
        for g in range(GQA):
            hd = h * GQA + g
            r = slice(g * tq, (g + 1) * tq)
            o_ref[:, hd * HEAD_DIM:(hd + 1) * HEAD_DIM] = jnp.transpose(z_ref[hd] + gate_row(hd, 1) * o_s[:, r])


def nsa_prompt(u, b_gate_pad, cos2, sin2, ckv, slc_k, slc_vt, win_k, win_vt, batch, seq):
    nqb = seq // Q_BLOCK
    n_pad = ckv.shape[3]
    half = N_KV * HEAD_DIM
    cmp_k = ckv[:, 0]
    cmp_vt = jnp.transpose(ckv[:, 1], (0, 1, 3, 2))
    kspec = pl.BlockSpec((seq, half), lambda b, i: (b, 0))
    vt_spec = pl.BlockSpec((seq // SLC_KT, half, SLC_KT), lambda b, i: (b, 0, 0))
    wvt_spec = pl.BlockSpec((seq // Q_BLOCK, half, Q_BLOCK), lambda b, i: (b, 0, 0))
    return pl.pallas_call(
        functools.partial(_nsa_prompt_kernel, n_cmp=seq // CMP_STRIDE - 1, n_slc=seq // SLC_BLOCK, kt=SLC_KT),
        grid=(batch, nqb),
        in_specs=[pl.BlockSpec((Q_BLOCK, MAIN_W), lambda b, i: (b * nqb + i, 0)),
                  pl.BlockSpec((Q_BLOCK, GATE_PAD), lambda b, i: (b * nqb + i, (MAIN_W + MEM_W) // GATE_PAD)),
                  pl.BlockSpec((1, GATE_PAD), lambda b, i: (0, 0)),
                  pl.BlockSpec((Q_BLOCK, HEAD_DIM), lambda b, i: (i, 0)),
                  pl.BlockSpec((Q_BLOCK, HEAD_DIM), lambda b, i: (i, 0)),
                  pl.BlockSpec((1, N_KV, n_pad, HEAD_DIM), lambda b, i: (b, 0, 0, 0)),
                  pl.BlockSpec((1, N_KV, HEAD_DIM, n_pad), lambda b, i: (b, 0, 0, 0)),
                  kspec, vt_spec, kspec, wvt_spec],
        out_specs=pl.BlockSpec((Q_BLOCK, MAIN_W), lambda b, i: (b * nqb + i, 0)),
        out_shape=jax.ShapeDtypeStruct((batch * seq, MAIN_W), F32),
        scratch_shapes=[pltpu.VMEM((N_KV, HEAD_DIM, GQA * Q_BLOCK), BF16),
                        pltpu.VMEM((N_KV, seq // SLC_BLOCK, Q_BLOCK), F32),
                        pltpu.VMEM((N_HEADS, HEAD_DIM, Q_BLOCK), F32)],
        compiler_params=_cparams("parallel", "arbitrary"),
        name="nsa_prompt",
    )(u, u, b_gate_pad, cos2, sin2, cmp_k, cmp_vt, slc_k, slc_vt, win_k, win_vt)


HEAD_ROWS = 16


def _head_mask(h):
    r = lax.broadcasted_iota(jnp.int32, (HEAD_ROWS, 1), 0)
    return (r >= h * GQA) & (r < (h + 1) * GQA)


def _nsa_sample_a_kernel(q_ref, cos_ref, sin_ref, ckv_ref, wkv_ref, wnew_ref, qr_ref, oc_ref, ow_ref, sel_ref,
                         *, n_cmp, n_slc, j_pad):
    n_pad = ckv_ref.shape[3]
    q = _rope(q_ref[0], cos_ref[...], sin_ref[...])
    qr_ref[0] = q
    qs = q * SCALE
    n_i = lax.broadcasted_iota(jnp.int32, (1, n_pad), 1)
    vis_c = (n_i * CMP_STRIDE + CMP_BLOCK - 1 <= PAST_LEN) & (n_i < n_cmp)
    cover = _cover(n_pad, j_pad, n_cmp)
    j_row = lax.broadcasted_iota(jnp.int32, (1, j_pad), 1)
    j_col = lax.broadcasted_iota(jnp.int32, (j_pad, 1), 0)
    qblk = PAST_LEN // SLC_BLOCK
    lane = lax.broadcasted_iota(jnp.int32, (N_SELECT, HEAD_DIM), 1)
    r_col = lax.broadcasted_iota(jnp.int32, (N_SELECT, 1), 0)
    n_win = wkv_ref.shape[1] // KV_SLOTS
    kpos_w = PAST_LEN - n_win + lax.broadcasted_iota(jnp.int32, (1, n_win), 1)
    vis_w = (kpos_w <= PAST_LEN) & (PAST_LEN - kpos_w < WINDOW)

    heads = range(N_KV)
    forced = (j_row == 0) | (j_row == qblk) | (j_row == qblk - 1)
    s_cmp = [jnp.where(vis_c, _dot_t(qs, ckv_ref[0, 0, h]), NEG) for h in heads]
    s_win = [jnp.where(vis_w, _dot_t(qs, wkv_ref[0, pl.ds(h, n_win, stride=KV_SLOTS), :]), NEG) for h in heads]

    o_c = jnp.zeros((HEAD_ROWS, HEAD_DIM), F32)
    scores = []
    for h in heads:
        mine = _head_mask(h)
        p = _softmax_rows(s_cmp[h]) * float(PAST_LEN >= CMP_BLOCK - 1)
        o_c = jnp.where(mine, _dot(p, ckv_ref[0, 1, h]), o_c)
        p_sum = jnp.sum(jnp.where(mine, p, 0.0), axis=0, keepdims=True)
        score = jnp.dot(jnp.broadcast_to(p_sum, (8, n_pad)), cover, preferred_element_type=F32,
                        precision=lax.Precision.HIGHEST)
        score = jnp.where(forced, jnp.inf, jnp.where(j_row <= qblk, score, -jnp.inf))
        scores.append(jnp.where(j_row < n_slc, score, -jnp.inf))
    oc_ref[0] = o_c

    o_w = jnp.zeros((HEAD_ROWS, HEAD_DIM), F32)
    for h in heads:
        hs = slice(h * HEAD_DIM, (h + 1) * HEAD_DIM)
        vsl = slice((N_KV + h) * HEAD_DIM, (N_KV + h + 1) * HEAD_DIM)
        v_win = wkv_ref[0, pl.ds(N_KV + h, n_win, stride=KV_SLOTS), :]
        s = s_win[h]
        s_new = jnp.sum(qs * wnew_ref[0, :, hs], axis=-1, keepdims=True)
        m = jnp.maximum(jnp.max(s, axis=-1, keepdims=True), s_new)
        e = jnp.exp(s - m)
        e_new = jnp.exp(s_new - m)
        o = (_dot(e, v_win) + e_new * wnew_ref[0, :, vsl]) / (jnp.sum(e, axis=-1, keepdims=True) + e_new)
        o_w = jnp.where(_head_mask(h), o, o_w)
    ow_ref[0] = o_w

    sel_out = jnp.zeros((N_SELECT, HEAD_DIM), jnp.int32)
    for h in heads:
        score = scores[h]
        s_col = jnp.transpose(score)[:, 0:1]
        beats = ((s_col > score[0:1]) | ((s_col == score[0:1]) & (j_col < j_row))) & (j_col < n_slc)
        rank = jnp.sum(beats.astype(F32), axis=0, keepdims=True)
        hit = (rank == r_col.astype(F32)) & (j_row < n_slc)
        idx = jnp.sum(jnp.where(hit, j_row.astype(F32), 0.0), axis=1, keepdims=True)
        sel_out = jnp.where(lane == h, idx.astype(jnp.int32), sel_out)
    sel_ref[0] = sel_out


def nsa_sample_a(q16, cos2, sin2, ckv, win_cache, win_new, n_cmp, n_slc):
    db = q16.shape[0]
    n_pad = ckv.shape[3]
    win_rows = win_cache.shape[1]
    j_pad = -(-n_slc // 128) * 128
    hspec = pl.BlockSpec((1, HEAD_ROWS, HEAD_DIM), lambda b: (b, 0, 0))
    hout = jax.ShapeDtypeStruct((db, HEAD_ROWS, HEAD_DIM), F32)
    return pl.pallas_call(
        functools.partial(_nsa_sample_a_kernel, n_cmp=n_cmp, n_slc=n_slc, j_pad=j_pad),
        grid=(db,),
        in_specs=[hspec,
                  pl.BlockSpec((1, HEAD_DIM), lambda b: (0, 0)),
                  pl.BlockSpec((1, HEAD_DIM), lambda b: (0, 0)),
                  pl.BlockSpec((1, 2, N_KV, n_pad, HEAD_DIM), lambda b: (b, 0, 0, 0, 0)),
                  pl.BlockSpec((1, win_rows, HEAD_DIM), lambda b: (b, 0, 0)),
                  pl.BlockSpec((1, 1, KV_W), lambda b: (b, 0, 0))],
        out_specs=[hspec, hspec, hspec, pl.BlockSpec((1, N_SELECT, HEAD_DIM), lambda b: (b, 0, 0))],
        out_shape=[hout, hout, hout, jax.ShapeDtypeStruct((db, N_SELECT, HEAD_DIM), jnp.int32)],
        compiler_params=_cparams("arbitrary"),
        name="nsa_sample_a",
    )(q16, cos2, sin2, ckv, win_cache, win_new)


def _nsa_sample_b_kernel(sel_ref, tbl_ref, q_ref, *refs):
    blocks = refs[:N_SELECT]
    new_ref, oc_ref, ow_ref, gl_ref, bg_ref, o_ref, os_ref = refs[N_SELECT:]
    b = pl.program_id(0)
    h = pl.program_id(1)
    n_keys = N_SELECT * SLC_BLOCK
    lane_blk = jnp.right_shift(lax.broadcasted_iota(jnp.int32, (1, n_keys), 1), SLC_SHIFT)
    new_k = new_ref[0, pl.ds(h, 1), :]
    new_v = new_ref[0, pl.ds(h + N_KV, 1), :]
    ks, vs = [], []
    blk_of = jnp.zeros((1, n_keys), jnp.int32)
    for r in range(N_SELECT):
        blk = sel_ref[b, h, r]
        in_past = blk * SLC_BLOCK < PAST_LEN
        ks.append(jnp.where(in_past, blocks[r][0, pl.ds(h, SLC_BLOCK, stride=KV_SLOTS), :], new_k))
        vs.append(jnp.where(in_past, blocks[r][0, pl.ds(h + N_KV, SLC_BLOCK, stride=KV_SLOTS), :], new_v))
        blk_of = jnp.where(lane_blk == r, blk, blk_of)
    kpos = blk_of * SLC_BLOCK + (lax.broadcasted_iota(jnp.int32, (1, n_keys), 1) & (SLC_BLOCK - 1))
    s = jnp.where(kpos <= PAST_LEN, _dot_t(q_ref[0] * SCALE, jnp.concatenate(ks, axis=0)), NEG)
    e = jnp.exp(s - jnp.max(s, axis=-1, keepdims=True))
    o_s = _dot(e, jnp.concatenate(vs, axis=0)) / jnp.sum(e, axis=-1, keepdims=True)

    @pl.when(h == 0)
    def _():
        os_ref[...] = jnp.zeros(os_ref.shape, F32)

    row = lax.broadcasted_iota(jnp.int32, (HEAD_ROWS, 1), 0)
    mine = (row >= h * GQA) & (row < (h + 1) * GQA)
    os_ref[...] = jnp.where(mine, o_s, os_ref[...])

    @pl.when(h == pl.num_programs(1) - 1)
    def _():
        gates = jax.nn.sigmoid(gl_ref[0] + bg_ref[...])
        o_ref[0] = gates[:, 0:1] * oc_ref[0] + gates[:, 1:2] * os_ref[...] + gates[:, 2:3] * ow_ref[0]


def nsa_sample_b(sel, page_table, q_rope, slc_halves, slc_new, o_c, o_w, gate_logits, b_gate16):
    db = q_rope.shape[0]
    last_blk = PAST_LEN // SLC_BLOCK - 1
    halves = PAGE_SIZE // SLC_BLOCK

    def cache_map(r):
        def index(b, h, sel_ref, tbl_ref):
            j = jnp.minimum(sel_ref[b, h, r], last_blk)
            return (tbl_ref[b, j // halves] * halves + j % halves, 0, 0)
        return index

    hspec = pl.BlockSpec((1, HEAD_ROWS, HEAD_DIM), lambda b, h, s, t: (b, 0, 0))
    return pl.pallas_call(
        _nsa_sample_b_kernel,
        grid_spec=pltpu.PrefetchScalarGridSpec(
            num_scalar_prefetch=2,
            grid=(db, N_KV),
            in_specs=[hspec]
            + [pl.BlockSpec((1, SLC_BLOCK * KV_SLOTS, HEAD_DIM), cache_map(r)) for r in range(N_SELECT)]
            + [pl.BlockSpec((1, KV_SLOTS, HEAD_DIM), lambda b, h, s, t: (b, 0, 0)),
               hspec, hspec, hspec,
               pl.BlockSpec((HEAD_ROWS, HEAD_DIM), lambda b, h, s, t: (0, 0))],
            out_specs=hspec,
            scratch_shapes=[pltpu.VMEM((HEAD_ROWS, HEAD_DIM), F32)],
        ),
        out_shape=jax.ShapeDtypeStruct((db, HEAD_ROWS, HEAD_DIM), F32),
        compiler_params=_cparams("arbitrary", "arbitrary"),
        name="nsa_sample_b",
    )(sel, page_table, q_rope, *([slc_halves] * N_SELECT), slc_new, o_c, o_w, gate_logits, b_gate16)


def _rope_tables(pos):
    half = HEAD_DIM // 2
    inv = ROPE_THETA ** (-jnp.arange(half, dtype=F32) / half)
    ang = pos.astype(F32)[:, None] * inv[None, :]
    cos, sin = jnp.cos(ang), jnp.sin(ang)
    return jnp.concatenate([cos, cos], -1), jnp.concatenate([-sin, sin], -1)


def _pad_w_in_b(w):
    return jnp.concatenate([w[..., :MAIN_W], w[..., MAIN_W + GATE_W:],
                            jnp.pad(w[..., MAIN_W:MAIN_W + GATE_W], ((0, 0), (0, 0), (0, GATE_PAD - GATE_W)))], axis=-1)


def kernel(x_prompt, x_sample, mem_prompt, state_pool, cache_cmp_kv, cache_slc_kv, cache_win_kv, cache_mem_kv, page_table, g_mix, w_in_a, pool_grp_w, pool_scale, w_in_b, b_gate, g_kv, w_kv, cmp_pe, cmp_w1, cmp_w2, g_mem, w_mem_kv, w_out, g_ffn, w_gu, w_down, g_final):
    batch, seq, _ = x_prompt.shape
    db = x_sample.shape[0]
    m = batch * seq
    tm = PROJ_TM
    n_phys = cache_cmp_kv.shape[0]
    w_in_b_pad = _pad_w_in_b(w_in_b)
    xp = x_prompt.reshape(m, D_MODEL)
    xs = x_sample.reshape(db, D_MODEL)
    mem2 = mem_prompt.reshape(batch * MEM_TOKENS, D_MODEL)
    mem_kv = rms_matmul_layers(mem2, g_mem, w_mem_kv, 512, 512)
    mem_rows = mem_kv.reshape(-1, 2 * MEM_W)
    mem_cache = cache_mem_kv.reshape(-1, MEM_HEAD_DIM)
    cos_p, sin_p = _rope_tables(jnp.arange(seq, dtype=jnp.int32))
    cos_s, sin_s = _rope_tables(jnp.full((db,), PAST_LEN, jnp.int32))
    pools_p, pools_s = [], []
    for l in range(DEPTH):
        if l == N_A:
            cmp_kv, slc_kv, win_kv, slc_k, slc_vt, win_k, win_vt = kv_proj(xp, g_kv, w_kv, cos_p, sin_p, SLC_KT, True)
            table = jnp.arange(batch * seq // PAGE_SIZE, dtype=jnp.int32).reshape(batch, seq // PAGE_SIZE)
            ckv_p = compress(cmp_kv.reshape(-1, PAGE_SIZE, KV_SLOTS, HEAD_DIM), table,
                             jnp.zeros((batch, PAGE_SIZE, KV_SLOTS, HEAD_DIM), F32), cmp_pe, cmp_w1, cmp_w2)
            cmp_new, slc_new, win_new = kv_proj(xs, g_kv, w_kv, cos_s, sin_s, db, False)
            extra = jnp.pad(cmp_new.reshape(db, 1, KV_SLOTS, HEAD_DIM), ((0, 0), (0, PAGE_SIZE - 1), (0, 0), (0, 0)))
            ckv_s = compress(cache_cmp_kv.reshape(n_phys, PAGE_SIZE, KV_SLOTS, HEAD_DIM), page_table, extra,
                             cmp_pe, cmp_w1, cmp_w2)
            t_full = -(-(PAST_LEN + 1) // SLC_BLOCK) * SLC_BLOCK
            n_cmp = t_full // CMP_STRIDE - 1
            n_slc = t_full // SLC_BLOCK
        if l < N_A:
            up, us = rms_matmul(xp, g_mix[l], w_in_a, l, tm, 512, xs=xs)
            u3 = up.reshape(batch, seq, -1)
            zp = pool_prompt(u3, pool_grp_w[l], pool_scale[l]).reshape(m, MAIN_W)
            pools_p.append(u3[:, seq - POOL_BUF:, :MAIN_W])
            zs = pool_sample(us, jnp.transpose(state_pool[l], (1, 0, 2)), pool_grp_w[l], pool_scale[l])
            pools_s.append(jnp.concatenate([state_pool[l][:, 1:], us[:, None, :MAIN_W]], axis=1))
        else:
            j = l - N_A
            up, us = rms_matmul(xp, g_mix[l], w_in_b_pad, j, tm, 768, xs=xs)
            u3 = up.reshape(batch, seq, -1)
            bg = jnp.pad(b_gate[j], (0, GATE_PAD - GATE_W)).reshape(1, GATE_PAD)
            zp = nsa_prompt(up, bg, cos_p, sin_p, ckv_p, slc_k, slc_vt, win_k, win_vt, batch, seq)
            q16 = jnp.pad(us[:, :MAIN_W].reshape(db, N_HEADS, HEAD_DIM), ((0, 0), (0, HEAD_ROWS - N_HEADS), (0, 0)))
            q_rope, o_c, o_w, sel = nsa_sample_a(
                q16, cos_s[:1], sin_s[:1], ckv_s, cache_win_kv.reshape(db, -1, HEAD_DIM),
                win_new.reshape(db, 1, KV_W), n_cmp, n_slc)
            sel = jnp.transpose(sel[:, :, :N_KV], (0, 2, 1))
            gl = us[:, MAIN_W + MEM_W:MAIN_W + MEM_W + GATE_W].reshape(db, N_HEADS, N_BRANCH)
            gl = jnp.pad(gl, ((0, 0), (0, HEAD_ROWS - N_HEADS), (0, HEAD_DIM - N_BRANCH)))
            bg16 = jnp.pad(b_gate[j].reshape(N_HEADS, N_BRANCH), ((0, HEAD_ROWS - N_HEADS), (0, HEAD_DIM - N_BRANCH)))
            z16 = nsa_sample_b(sel, page_table, q_rope,
                               cache_slc_kv.reshape(n_phys * 2, SLC_BLOCK * KV_SLOTS, HEAD_DIM),
                               slc_new.reshape(db, KV_SLOTS, HEAD_DIM), o_c, o_w, gl, bg16)
            zs = z16[:, :N_HEADS].reshape(db, MAIN_W)
        mo_p = mem_attend(u3, mem_rows, 1024, l, False).reshape(m, MEM_W)
        mo_s = mem_attend(us.reshape(db, 1, -1), mem_cache, 1, l, True).reshape(db, MEM_W)
        xp, xs = out_proj(zp, mo_p, w_out, l, xp, tm, 512, sample=(zs, mo_s, xs))
        xp, xs = ffn(xp, g_ffn[l], w_gu, w_down, l, g_final, l == DEPTH - 1, FFN_TM, FFN_TF, xs=xs)
    kv5 = (batch, seq, 2, N_KV, HEAD_DIM)
    kv5s = (db, 1, 2, N_KV, HEAD_DIM)
    n_keep = min(WINDOW, seq)
    win_s = jnp.concatenate([cache_win_kv, win_new.reshape(kv5s)], axis=1)
    win_s = win_s[:, win_s.shape[1] - min(WINDOW, win_s.shape[1]):]
    return (xp.reshape(batch, seq, D_MODEL), xs.reshape(db, 1, D_MODEL), jnp.stack(pools_p),
            cmp_kv.reshape(kv5), slc_kv.reshape(kv5), win_kv.reshape(kv5)[:, seq - n_keep:],
            mem_kv.reshape(DEPTH, batch, MEM_TOKENS, 2, N_MEM_HEADS, MEM_HEAD_DIM),
            jnp.stack(pools_s), cmp_new.reshape(kv5s), slc_new.reshape(kv5s), win_s)
```
